```python
import jax, jax.numpy as jnp
from jax import lax
import numpy as np

D_MODEL = 1024
BATCH = 4
SEQ = 8192
DEPTH = 2

GLA_HEADS = 4
GLA_DK = D_MODEL // 2 // GLA_HEADS
GLA_DV = D_MODEL // GLA_HEADS
GLA_GATE_RANK = 16
GLA_TAU = 16.0
GLA_CHUNK = 64
SWA_HEADS = 16
SWA_KV_HEADS = 2
SWA_HD = 64
WINDOW = 128
SWA_BLOCK = 128
N_BUCKETS = 32
MAX_DISTANCE = 128
PEER_HEADS = 8
PEER_NKEYS = 128
PEER_NEXPERTS = PEER_NKEYS * PEER_NKEYS
PEER_QDIM = 256
PEER_TOPK = 16
PEER_TOKEN_BLOCK = 128
EPS = 1e-6

GLA_QK_W = GLA_HEADS * GLA_DK
GLA_V_W = GLA_HEADS * GLA_DV
SWA_Q_W = SWA_HEADS * SWA_HD
SWA_KV_W = SWA_KV_HEADS * SWA_HD
SPLIT_SIZES = (GLA_QK_W, GLA_QK_W, GLA_V_W, GLA_GATE_RANK, GLA_V_W, SWA_Q_W, SWA_KV_W, SWA_KV_W, D_MODEL, D_MODEL)
IN_W = GLA_QK_W * 2 + GLA_V_W * 2 + GLA_GATE_RANK + SWA_Q_W + SWA_KV_W * 2 + D_MODEL * 2

kernel_name = 'hybrid_gla_swa_peer_block'


def _split_points():
    return [int(s) for s in np.cumsum(SPLIT_SIZES)[:-1]]


def _rms_f32(x, g):
    xf = x.astype(jnp.float32)
    y = xf * lax.rsqrt(jnp.mean(xf * xf, axis=-1, keepdims=True) + EPS)
    return y * g.astype(jnp.float32)


def rms_norm(x, g):
    return _rms_f32(x, g).astype(x.dtype)


def t5_causal_bucket(dist):
    max_exact = N_BUCKETS // 2
    d = np.maximum(dist, 1).astype(np.float32)
    large = max_exact + (np.log(d / max_exact) / np.log(MAX_DISTANCE / max_exact)
                         * (N_BUCKETS - max_exact)).astype(np.int32)
    large = np.minimum(large, N_BUCKETS - 1)
    return np.where(dist < max_exact, dist, large).astype(np.int32)


def gla_branch(q, k, v, gate_lr, r, w2, b2, norm_g):
    B, T, _ = q.shape
    dt = q.dtype
    f32 = jnp.float32
    q = q.reshape(B, T, GLA_HEADS, GLA_DK).astype(f32) * (GLA_DK ** -0.5)
    k = k.reshape(B, T, GLA_HEADS, GLA_DK).astype(f32)
    v = v.reshape(B, T, GLA_HEADS, GLA_DV).astype(f32)
    log_a = jax.nn.log_sigmoid((gate_lr @ w2 + b2).astype(f32)) / GLA_TAU
    log_a = log_a.reshape(B, T, GLA_HEADS, GLA_DK)
    n_chunks = T // GLA_CHUNK

    def to_chunks(a):
        return a.reshape(B, n_chunks, GLA_CHUNK, a.shape[2], a.shape[3]).transpose(1, 0, 2, 3, 4)

    qc, kc, vc, gc = to_chunks(q), to_chunks(k), to_chunks(v), to_chunks(log_a)
    causal = jnp.tril(jnp.ones((GLA_CHUNK, GLA_CHUNK), dtype=bool))

    def step(S, inp):
        qn, kn, vn, gn = inp
        b = jnp.cumsum(gn, axis=1)
        b_last = b[:, -1]
        q_dec = qn * jnp.exp(b)
        k_inv = kn * jnp.exp(-b)
        attn = jnp.where(causal, jnp.einsum('bthd,bshd->bhts', q_dec, k_inv), 0.0)
        o = (jnp.einsum('bhts,bshv->bthv', attn, vn)
             + jnp.einsum('bthd,bhdv->bthv', q_dec, S))
        k_tail = kn * jnp.exp(b_last[:, None] - b)
        S = S * jnp.exp(b_last)[..., None] + jnp.einsum('bshd,bshv->bhdv', k_tail, vn)
        return S, o

    S0 = jnp.zeros((B, GLA_HEADS, GLA_DK, GLA_DV), f32)
    _, o = lax.scan(step, S0, (qc, kc, vc, gc))
    o = o.transpose(1, 0, 2, 3, 4).reshape(B, T, GLA_HEADS, GLA_DV)
    o = _rms_f32(o, norm_g).reshape(B, T, GLA_V_W)
    o = o * jax.nn.silu(r.astype(f32))
    return o.astype(dt)


def swa_branch(q, k, v, qn_g, kn_g, sinks, rel_bias):
    B, T, _ = q.shape
    dt = q.dtype
    f32 = jnp.float32
    G = SWA_HEADS // SWA_KV_HEADS
    BLK = SWA_BLOCK
    NB = T // BLK
    q = _rms_f32(q.reshape(B, T, SWA_KV_HEADS, G, SWA_HD), qn_g) * (SWA_HD ** -0.5)
    k = _rms_f32(k.reshape(B, T, SWA_KV_HEADS, SWA_HD), kn_g)
    v = v.reshape(B, T, SWA_KV_HEADS, SWA_HD).astype(f32)

    pad = jnp.zeros((B, BLK, SWA_KV_HEADS, SWA_HD), f32)

    def windows(a):
        ap = jnp.concatenate([pad, a], axis=1).reshape(B, NB + 1, BLK, SWA_KV_HEADS, SWA_HD)
        w = jnp.concatenate([ap[:, :-1], ap[:, 1:]], axis=2)
        return w.transpose(1, 0, 2, 3, 4)

    kw, vw = windows(k), windows(v)
    qb = q.reshape(B, NB, BLK, SWA_KV_HEADS, G, SWA_HD).transpose(1, 0, 2, 3, 4, 5)

    qi = np.arange(BLK)[:, None]
    sj = np.arange(2 * BLK)[None, :]
    dist = BLK + qi - sj
    band = (dist >= 0) & (dist < WINDOW)
    key_pos = np.arange(NB)[:, None, None] * BLK + sj[None] - BLK
    valid = jnp.asarray(band[None] & (key_pos >= 0))
    bucket = t5_causal_bucket(np.clip(dist, 0, None))
    bias = rel_bias[bucket].astype(f32)
    bias = bias.transpose(2, 0, 1).reshape(SWA_KV_HEADS, G, BLK, 2 * BLK)
    sink = sinks.astype(f32).reshape(SWA_KV_HEADS, G, 1, 1)

    def block(args):
        qn, kn, vn, ok = args
        logits = jnp.einsum('bqkgd,bskd->bkgqs', qn, kn) + bias
        logits = jnp.where(ok, logits, -jnp.inf)
        m = jnp.maximum(jnp.max(logits, axis=-1, keepdims=True), sink)
        p = jnp.exp(logits - m)
        denom = jnp.sum(p, axis=-1, keepdims=True) + jnp.exp(sink - m)
        return jnp.einsum('bkgqs,bskd->bqkgd', p / denom, vn)

    out = lax.map(block, (qb, kw, vw, valid))
    out = out.transpose(1, 0, 2, 3, 4, 5).reshape(B, T, SWA_Q_W)
    return out.astype(dt)


def peer_ffn(h, wq, subkeys, u, v):
    B, T, D = h.shape
    dt = h.dtype
    f32 = jnp.float32
    K = PEER_TOPK
    q = (h @ wq).reshape(B, T, PEER_HEADS, 2, PEER_QDIM // 2).astype(f32)
    s = jnp.einsum('bthpd,pnd->bthpn', q, subkeys.astype(f32))
    s_top, i_top = lax.top_k(s, K)
    cand = s_top[..., 0, :, None] + s_top[..., 1, None, :]
    cand_idx = i_top[..., 0, :, None] * PEER_NKEYS + i_top[..., 1, None, :]
    best, pos = lax.top_k(cand.reshape(B, T, PEER_HEADS, K * K), K)
    experts = jnp.take_along_axis(cand_idx.reshape(B, T, PEER_HEADS, K * K), pos, axis=-1)
    gates = jax.nn.softmax(best, axis=-1)

    n_blocks = (B * T) // PEER_TOKEN_BLOCK
    hb = h.reshape(n_blocks, PEER_TOKEN_BLOCK, D)
    eb = experts.reshape(n_blocks, PEER_TOKEN_BLOCK, PEER_HEADS * K)
    gb = gates.reshape(n_blocks, PEER_TOKEN_BLOCK, PEER_HEADS * K).astype(dt)

    def block(args):
        hx, e, g = args
        a = jnp.einsum('td,tkd->tk', hx, u[e])
        return jnp.einsum('tk,tkd->td', g * jax.nn.gelu(a), v[e])

    y = lax.map(block, (hb, eb, gb))
    return y.reshape(B, T, D)


def setup_inputs(seed: int = 0) -> dict:
    key = jax.random.key(seed)
    ks = jax.random.split(key, 21)
    f32 = jnp.float32
    nrm = lambda k, shape, s: jax.random.normal(k, shape, f32) * s
    L, D = DEPTH, D_MODEL
    return {
        'x': nrm(ks[0], (BATCH, SEQ, D), 1.0),
        'c': nrm(ks[1], (BATCH, D), 1.0),
        'w_ada': nrm(ks[2], (L, D, 6 * D), 0.5 * D ** -0.5),
        'b_ada': nrm(ks[3], (L, 6 * D), 0.02),
        'norm1_g': 1.0 + nrm(ks[4], (L, D), 0.02),
        'w_in': nrm(ks[5], (L, D, IN_W), D ** -0.5),
        'gla_gate_w2': nrm(ks[6], (L, GLA_GATE_RANK, GLA_QK_W), GLA_GATE_RANK ** -0.5),
        'gla_gate_b': nrm(ks[7], (L, GLA_QK_W), 0.1),
        'gla_norm_g': 1.0 + nrm(ks[8], (L, GLA_DV), 0.02),
        'swa_qnorm_g': 1.0 + nrm(ks[9], (L, SWA_HD), 0.02),
        'swa_knorm_g': 1.0 + nrm(ks[10], (L, SWA_HD), 0.02),
        'swa_sinks': nrm(ks[11], (L, SWA_HEADS), 0.5),
        'rel_bias': nrm(ks[12], (N_BUCKETS, SWA_HEADS), 0.2),
        'w_up_a': nrm(ks[13], (L, GLA_V_W, D), GLA_V_W ** -0.5),
        'w_up_b': nrm(ks[14], (L, SWA_Q_W, D), SWA_Q_W ** -0.5),
        'w_out': nrm(ks[15], (L, D, D), D ** -0.5),
        'norm2_g': 1.0 + nrm(ks[16], (L, D), 0.02),
        'peer_wq': nrm(ks[17], (L, D, PEER_HEADS * PEER_QDIM), D ** -0.5),
        'peer_subkeys': nrm(ks[18], (L, 2, PEER_NKEYS, PEER_QDIM // 2), (PEER_QDIM // 2) ** -0.5),
        'peer_u': nrm(ks[19], (L, PEER_NEXPERTS, D), D ** -0.5),
        'peer_v': nrm(ks[20], (L, PEER_NEXPERTS, D), 1.0),
    }


def reference(x, c, w_ada, b_ada, norm1_g, w_in, gla_gate_w2, gla_gate_b, gla_norm_g,
              swa_qnorm_g, swa_knorm_g, swa_sinks, rel_bias, w_up_a, w_up_b, w_out,
              norm2_g, peer_wq, peer_subkeys, peer_u, peer_v):
    splits = _split_points()
    for l in range(DEPTH):
        mod = jax.nn.silu(c) @ w_ada[l] + b_ada[l]
        sh1, sc1, g1, sh2, sc2, g2 = jnp.split(mod, 6, axis=-1)

        h = rms_norm(x, norm1_g[l]) * (1.0 + sc1[:, None]) + sh1[:, None]
        proj = h @ w_in[l]
        gq, gk, gv, glr, gr, sq, sk, sv, ga, gb = jnp.split(proj, splits, axis=-1)
        ya = gla_branch(gq, gk, gv, glr, gr, gla_gate_w2[l], gla_gate_b[l], gla_norm_g[l]) @ w_up_a[l]
        yb = swa_branch(sq, sk, sv, swa_qnorm_g[l], swa_knorm_g[l], swa_sinks[l], rel_bias) @ w_up_b[l]
        mixed = (jax.nn.sigmoid(ga) * ya + jax.nn.sigmoid(gb) * yb) @ w_out[l]
        x = x + g1[:, None] * mixed

        h2 = rms_norm(x, norm2_g[l]) * (1.0 + sc2[:, None]) + sh2[:, None]
        x = x + g2[:, None] * peer_ffn(h2, peer_wq[l], peer_subkeys[l], peer_u[l], peer_v[l])
    return x
```

```python
import functools

import numpy as np
import jax
import jax.numpy as jnp
from jax import lax
from jax.experimental import pallas as pl
from jax.experimental.pallas import tpu as pltpu

F32 = jnp.float32
BF16 = jnp.bfloat16
I32 = jnp.int32
HIGHEST = lax.Precision.HIGHEST
EPS = 1e-6

GLA_HEADS = 4
GLA_DK = 128
GLA_DV = 256
GLA_RANK = 16
GLA_TAU = 16.0
GLA_CHUNK = 64
SWA_HEADS = 16
SWA_KV_HEADS = 2
SWA_HD = 64
SWA_BLOCK = 128
N_BUCKETS = 32
MAX_DISTANCE = 128
PEER_HEADS = 8
PEER_NKEYS = 128
PEER_TOPK = 16
PEER_SLOTS = PEER_HEADS * PEER_TOPK

SUBLANES = 8
LANES = 128
VMEM_LIMIT_BYTES = 56 * 1024 * 1024

NT_DIMS = (((1,), (1,)), ((), ()))
TN_DIMS = (((0,), (0,)), ((), ()))

COL_Q, COL_K, COL_V, COL_GR, COL_SQ, COL_GA, COL_GB = 0, 512, 1024, 2048, 3072, 4096, 5120
COL_SK, COL_SV, COL_GLR = 6144, 6272, 6400
PROJ_W = 6528


def _params(sem, vmem=None):
    return pltpu.CompilerParams(dimension_semantics=sem, vmem_limit_bytes=vmem)


def _adaln_kernel(c_ref, w_ref, b_ref, o_ref):
    c = c_ref[...]
    a = c * jax.nn.sigmoid(c)
    o_ref[0] = jnp.dot(a, w_ref[0], preferred_element_type=F32, precision=HIGHEST) + b_ref[0]


def _adaln(c, w_ada, b_ada):
    L, D, W = w_ada.shape
    B = c.shape[0]
    rows = -(-B // SUBLANES) * SUBLANES
    cp = jnp.zeros((rows, D), F32).at[:B].set(c)
    tn = W // 4
    out = pl.pallas_call(
        _adaln_kernel,
        grid=(L, W // tn),
        in_specs=[
            pl.BlockSpec((rows, D), lambda l, j: (0, 0)),
            pl.BlockSpec((1, D, tn), lambda l, j: (l, 0, j)),
            pl.BlockSpec((1, 1, tn), lambda l, j: (l, 0, j)),
        ],
        out_specs=pl.BlockSpec((1, rows, tn), lambda l, j: (l, 0, j)),
        out_shape=jax.ShapeDtypeStruct((L, rows, W), F32),
        compiler_params=_params(("arbitrary", "arbitrary")),
        name="adaln",
    )(cp, w_ada, b_ada.reshape(L, 1, W))
    return out[:, :B]


def _fold_kernel(sk_ref, wq_ref, o_ref):
    o_ref[0] = lax.dot_general(sk_ref[0, 0], wq_ref[0], NT_DIMS, precision=HIGHEST,
                               preferred_element_type=F32).astype(BF16)


def _fold_peer_keys(peer_wq, peer_subkeys):
    L, D, QW = peer_wq.shape
    half = peer_subkeys.shape[-1]
    ngroups = QW // half
    return pl.pallas_call(
        _fold_kernel,
        grid=(L, ngroups),
        in_specs=[
            pl.BlockSpec((1, 1, PEER_NKEYS, half), lambda l, g: (l, g % 2, 0, 0)),
            pl.BlockSpec((1, D, half), lambda l, g: (l, 0, g)),
        ],
        out_specs=pl.BlockSpec((1, PEER_NKEYS, D), lambda l, g: (l, g, 0)),
        out_shape=jax.ShapeDtypeStruct((L, ngroups * PEER_NKEYS, D), BF16),
        compiler_params=_params(("arbitrary", "arbitrary")),
        name="peer_fold",
    )(peer_subkeys, peer_wq)


def _inproj_kernel(x_ref, g_ref, sc_ref, sh_ref, w_ref, o_ref):
    x = x_ref[...]
    ms = jnp.mean(x * x, axis=-1, keepdims=True)
    y = x * lax.rsqrt(ms + EPS) * g_ref[...]
    h = y * (1.0 + sc_ref[0]) + sh_ref[0]
    o_ref[...] = jnp.dot(h.astype(BF16), w_ref[...], preferred_element_type=F32).astype(BF16)


def _in_proj(xf, g, sc, sh, wp, T, tm=512):
    N, D = xf.shape
    B = sc.shape[0]
    ncol = 3
    tn = PROJ_W // ncol
    return pl.pallas_call(
        _inproj_kernel,
        grid=(ncol, N // tm),
        in_specs=[
            pl.BlockSpec((tm, D), lambda j, i: (i, 0)),
            pl.BlockSpec((1, D), lambda j, i: (0, 0)),
            pl.BlockSpec((1, 1, D), lambda j, i: ((i * tm) // T, 0, 0)),
            pl.BlockSpec((1, 1, D), lambda j, i: ((i * tm) // T, 0, 0)),
            pl.BlockSpec((D, tn), lambda j, i: (0, j)),
        ],
        out_specs=pl.BlockSpec((tm, tn), lambda j, i: (i, j)),
        out_shape=jax.ShapeDtypeStruct((N, PROJ_W), BF16),
        compiler_params=_params(("arbitrary", "arbitrary")),
        name="in_proj",
    )(xf, g.reshape(1, D), sc.reshape(B, 1, D), sh.reshape(B, 1, D), wp)


def _pack_w_in(w):
    o = np.cumsum([0, 512, 512, 1024, GLA_RANK, 1024, 1024, 128, 128, 1024, 1024])
    q, k, v, glr, gr, sq, sk, sv, ga, gb = [w[:, o[i]:o[i + 1]] for i in range(10)]
    pad = jnp.zeros((w.shape[0], LANES - GLA_RANK), w.dtype)
    return jnp.concatenate([q, k, v, gr, sq, ga, gb, sk, sv, glr, pad], axis=1).astype(BF16)


def _gla_kernel(q_ref, k_ref, v_ref, r_ref, glr_ref, w2_ref, b2_ref, ng_ref, o_ref, st_ref, *, nchunk):
    @pl.when(pl.program_id(2) == 0)
    def _():
        st_ref[...] = jnp.zeros_like(st_ref)

    C = GLA_CHUNK
    row = lax.broadcasted_iota(I32, (C, C), 0)
    col = lax.broadcasted_iota(I32, (C, C), 1)
    tri = col <= row
    tri_f = tri.astype(F32)
    w2 = w2_ref[...]
    b2 = b2_ref[...]
    ng = ng_ref[...]
    for ci in range(nchunk):
        sl = pl.ds(ci * C, C)
        q = q_ref[sl, :].astype(F32) * (GLA_DK ** -0.5)
        k = k_ref[sl, :].astype(F32)
        v = v_ref[sl, :]
        z = jnp.dot(glr_ref[sl, :], w2, preferred_element_type=F32) + b2
        log_a = (jnp.minimum(z, 0.0) - jnp.log(1.0 + jnp.exp(-jnp.abs(z)))) * (1.0 / GLA_TAU)
        b = jnp.dot(tri_f, log_a, preferred_element_type=F32, precision=HIGHEST)
        b_last = b[C - 1:C, :]
        q_dec = (q * jnp.exp(b)).astype(BF16)
        k_inv = (k * jnp.exp(-b)).astype(BF16)
        attn = lax.dot_general(q_dec, k_inv, NT_DIMS, preferred_element_type=F32)
        attn = jnp.where(tri, attn, 0.0).astype(BF16)
        st = st_ref[...]
        o = (jnp.dot(attn, v, preferred_element_type=F32)
             + lax.dot_general(q_dec, st.astype(BF16), NT_DIMS, preferred_element_type=F32))
        k_tail = (k * jnp.exp(b_last - b)).astype(BF16)
        st_ref[...] = st * jnp.exp(b_last) + lax.dot_general(v, k_tail, TN_DIMS, preferred_element_type=F32)
        on = o * lax.rsqrt(jnp.mean(o * o, axis=-1, keepdims=True) + EPS) * ng
        r = r_ref[sl, :].astype(F32)
        o_ref[sl, :] = (on * (r * jax.nn.sigmoid(r))).astype(BF16)


def _gla(proj, w2p, b2, ng, tc=256):
    B, T, _ = proj.shape
    H = GLA_HEADS
    return pl.pallas_call(
        functools.partial(_gla_kernel, nchunk=tc // GLA_CHUNK),
        grid=(B, H, T // tc),
        in_specs=[
            pl.BlockSpec((None, tc, GLA_DK), lambda b, h, c: (b, c, COL_Q // GLA_DK + h)),
            pl.BlockSpec((None, tc, GLA_DK), lambda b, h, c: (b, c, COL_K // GLA_DK + h)),
            pl.BlockSpec((None, tc, GLA_DV), lambda b, h, c: (b, c, COL_V // GLA_DV + h)),
            pl.BlockSpec((None, tc, GLA_DV), lambda b, h, c: (b, c, COL_GR // GLA_DV + h)),
            pl.BlockSpec((None, tc, LANES), lambda b, h, c: (b, c, COL_GLR // LANES)),
            pl.BlockSpec((LANES, GLA_DK), lambda b, h, c: (0, h)),
            pl.BlockSpec((1, GLA_DK), lambda b, h, c: (0, h)),
            pl.BlockSpec((1, GLA_DV), lambda b, h, c: (0, 0)),
        ],
        out_specs=pl.BlockSpec((None, tc, GLA_DV), lambda b, h, c: (b, c, h)),
        out_shape=jax.ShapeDtypeStruct((B, T, H * GLA_DV), BF16),
        scratch_shapes=[pltpu.VMEM((GLA_DV, GLA_DK), F32)],
        compiler_params=_params(("arbitrary", "arbitrary", "arbitrary")),
        name="gla",
    )(proj, proj, proj, proj, proj, w2p, b2, ng)


def _t5_bucket(dist):
    max_exact = N_BUCKETS // 2
    d = np.maximum(dist, 1).astype(np.float32)
    large = max_exact + (np.log(d / max_exact) / np.log(MAX_DISTANCE / max_exact)
                         * (N_BUCKETS - max_exact)).astype(np.int32)
    large = np.minimum(large, N_BUCKETS - 1)
    return np.where(dist < max_exact, dist, large).astype(np.int32)


NEG_BIG = -1e30


def _swa_bias(rel_bias):
    blk = SWA_BLOCK
    qi = np.arange(blk)[:, None]
    sj = np.arange(2 * blk)[None, :]
    dist = blk + qi - sj
    band = (dist >= 0) & (dist < blk)
    bucket = _t5_bucket(np.clip(dist, 0, None))
    bias = rel_bias[bucket].astype(F32).transpose(2, 0, 1)
    return jnp.where(jnp.asarray(band)[None], bias, NEG_BIG)


def _swa_kernel(sink_ref, q_ref, kp_ref, kc_ref, vp_ref, vc_ref, bias_ref, qg_ref, kg_ref, o_ref):
    blk = SWA_BLOCK
    hd = SWA_HD
    group = SWA_HEADS // SWA_KV_HEADS
    first = pl.program_id(1) == 0
    k2 = jnp.concatenate([kp_ref[...], kc_ref[...]], axis=0).astype(F32)
    v2 = jnp.concatenate([vp_ref[...], vc_ref[...]], axis=0)
    col = lax.broadcasted_iota(I32, (blk, 2 * blk), 1)
    kill = col < jnp.where(first, blk, 0)
    qg = qg_ref[...]
    kg = kg_ref[...]
    outs = []
    for kh in range(SWA_KV_HEADS):
        kk = k2[:, kh * hd:(kh + 1) * hd]
        kk = (kk * lax.rsqrt(jnp.mean(kk * kk, axis=-1, keepdims=True) + EPS) * kg).astype(BF16)
        vv = v2[:, kh * hd:(kh + 1) * hd]
        for g in range(group):
            h = kh * group + g
            qh = q_ref[:, h * hd:(h + 1) * hd].astype(F32)
            qh = qh * lax.rsqrt(jnp.mean(qh * qh, axis=-1, keepdims=True) + EPS) * qg * (hd ** -0.5)
            logits = lax.dot_general(qh.astype(BF16), kk, NT_DIMS, preferred_element_type=F32) + bias_ref[h]
            logits = jnp.where(kill, NEG_BIG, logits)
            sink = sink_ref[h]
            m = jnp.maximum(jnp.max(logits, axis=-1, keepdims=True), sink)
            p = jnp.exp(logits - m)
            denom = jnp.sum(p, axis=-1, keepdims=True) + jnp.exp(sink - m)
            outs.append(jnp.dot((p / denom).astype(BF16), vv, preferred_element_type=F32))
    o_ref[...] = jnp.concatenate(outs, axis=-1).astype(BF16)


def _swa(proj, bias, qg, kg, sinks):
    B, T, _ = proj.shape
    blk = SWA_BLOCK
    qw = SWA_HEADS * SWA_HD
    kvw = SWA_KV_HEADS * SWA_HD
    prev = lambda b, i: jnp.maximum(i - 1, 0)
    return pl.pallas_call(
        _swa_kernel,
        grid=(B, T // blk),
        in_specs=[
            pl.BlockSpec(memory_space=pltpu.SMEM),
            pl.BlockSpec((None, blk, qw), lambda b, i: (b, i, COL_SQ // qw)),
            pl.BlockSpec((None, blk, kvw), lambda b, i: (b, prev(b, i), COL_SK // kvw)),
            pl.BlockSpec((None, blk, kvw), lambda b, i: (b, i, COL_SK // kvw)),
            pl.BlockSpec((None, blk, kvw), lambda b, i: (b, prev(b, i), COL_SV // kvw)),
            pl.BlockSpec((None, blk, kvw), lambda b, i: (b, i, COL_SV // kvw)),
            pl.BlockSpec((SWA_HEADS, blk, 2 * blk), lambda b, i: (0, 0, 0)),
            pl.BlockSpec((1, SWA_HD), lambda b, i: (0, 0)),
            pl.BlockSpec((1, SWA_HD), lambda b, i: (0, 0)),
        ],
        out_specs=pl.BlockSpec((None, blk, qw), lambda b, i: (b, i, 0)),
        out_shape=jax.ShapeDtypeStruct((B, T, qw), BF16),
        compiler_params=_params(("arbitrary", "arbitrary")),
        name="swa",
    )(sinks, proj, proj, proj, proj, proj, bias, qg, kg)


def _merge_kernel(x_ref, ga_ref, gb_ref, go_ref, so_ref, wa_ref, wb_ref, wo_ref, g1_ref,
                  n2_ref, sc2_ref, sh2_ref, we_ref, x1_ref, h2_ref, st_ref):
    ya = jnp.dot(go_ref[...], wa_ref[...], preferred_element_type=F32)
    yb = jnp.dot(so_ref[...], wb_ref[...], preferred_element_type=F32)
    m = jax.nn.sigmoid(ga_ref[...].astype(F32)) * ya + jax.nn.sigmoid(gb_ref[...].astype(F32)) * yb
    mixed = jnp.dot(m.astype(BF16), wo_ref[...], preferred_element_type=F32)
    x1 = x_ref[...] + g1_ref[0] * mixed
    x1_ref[...] = x1
    y = x1 * lax.rsqrt(jnp.mean(x1 * x1, axis=-1, keepdims=True) + EPS) * n2_ref[...]
    h2 = y * (1.0 + sc2_ref[0]) + sh2_ref[0]
    h2_ref[...] = h2
    st_ref[...] = lax.dot_general(we_ref[...], h2.astype(BF16), NT_DIMS, preferred_element_type=F32)


def _merge(xf, proj2, gla_o, swa_o, wa, wb, wo, g1, n2, sc2, sh2, weff_t, T, tm=256):
    N, D = xf.shape
    B = g1.shape[0]
    SW = weff_t.shape[0]
    bat = lambda i: ((i * tm) // T, 0, 0)
    full = lambda i: (0, 0)
    return pl.pallas_call(
        _merge_kernel,
        grid=(N // tm,),
        in_specs=[
            pl.BlockSpec((tm, D), lambda i: (i, 0)),
            pl.BlockSpec((tm, D), lambda i: (i, COL_GA // D)),
            pl.BlockSpec((tm, D), lambda i: (i, COL_GB // D)),
            pl.BlockSpec((tm, D), lambda i: (i, 0)),
            pl.BlockSpec((tm, D), lambda i: (i, 0)),
            pl.BlockSpec((D, D), full),
            pl.BlockSpec((D, D), full),
            pl.BlockSpec((D, D), full),
            pl.BlockSpec((1, 1, D), bat),
            pl.BlockSpec((1, D), full),
            pl.BlockSpec((1, 1, D), bat),
            pl.BlockSpec((1, 1, D), bat),
            pl.BlockSpec((SW, D), full),
        ],
        out_specs=[
            pl.BlockSpec((tm, D), lambda i: (i, 0)),
            pl.BlockSpec((tm, D), lambda i: (i, 0)),
            pl.BlockSpec((SW, tm), lambda i: (0, i)),
        ],
        out_shape=[
            jax.ShapeDtypeStruct((N, D), F32),
            jax.ShapeDtypeStruct((N, D), F32),
            jax.ShapeDtypeStruct((SW, N), F32),
        ],
        compiler_params=_params(("arbitrary",), VMEM_LIMIT_BYTES),
        name="merge",
    )(xf, proj2, proj2, gla_o, swa_o, wa, wb, wo, g1.reshape(B, 1, D), n2.reshape(1, D),
      sc2.reshape(B, 1, D), sh2.reshape(B, 1, D), weff_t)


def _next_best(s, rio, m_prev, pos_prev, nrows):
    after = jnp.logical_or(s < m_prev, jnp.logical_and(s == m_prev, rio > pos_prev))
    cand = jnp.where(after, s, -jnp.inf)
    m = jnp.max(cand, axis=0, keepdims=True)
    pos = jnp.min(jnp.where(cand == m, rio, nrows), axis=0, keepdims=True)
    return m, pos


def _topk_kernel(s_ref, e_ref, g_ref, cand_ref, cidx_ref):
    K = PEER_TOPK
    nk = PEER_NKEYS
    tm = s_ref.shape[1]
    ngroups = 2 * PEER_HEADS
    rio = lax.broadcasted_iota(I32, (nk, tm), 0)
    kio = lax.broadcasted_iota(I32, (K, tm), 0)
    inf_row = jnp.full((1, tm), jnp.inf, F32)
    neg_row = jnp.full((1, tm), -1, I32)

    def stage1(k, carry):
        new = []
        for g in range(ngroups):
            m_prev, pos_prev, vals, idxs = carry[g]
            s = s_ref[g * nk:(g + 1) * nk, :]
            m, pos = _next_best(s, rio, m_prev, pos_prev, nk)
            vals = jnp.where(kio == k, m, vals)
            idxs = jnp.where(kio == k, pos, idxs)
            new.append((m, pos, vals, idxs))
        return tuple(new)

    init = tuple((inf_row, neg_row, jnp.zeros((K, tm), F32), jnp.zeros((K, tm), I32))
                 for _ in range(ngroups))
    res = lax.fori_loop(0, K, stage1, init)

    for h in range(PEER_HEADS):
        _, _, v0, i0 = res[2 * h]
        _, _, v1, i1 = res[2 * h + 1]
        for a in range(K):
            cand_ref[h, a * K:(a + 1) * K, :] = v0[a:a + 1, :] + v1
            cidx_ref[h, a * K:(a + 1) * K, :] = i0[a:a + 1, :] * nk + i1

    rio2 = lax.broadcasted_iota(I32, (K * K, tm), 0)

    def stage2(k, carry):
        new = []
        for h in range(PEER_HEADS):
            m_prev, pos_prev, vals, idxs = carry[h]
            m, pos = _next_best(cand_ref[h], rio2, m_prev, pos_prev, K * K)
            e = jnp.max(jnp.where(rio2 == pos, cidx_ref[h], -1), axis=0, keepdims=True)
            vals = jnp.where(kio == k, m, vals)
            idxs = jnp.where(kio == k, e, idxs)
            new.append((m, pos, vals, idxs))
        return tuple(new)

    init2 = tuple((inf_row, neg_row, jnp.zeros((K, tm), F32), jnp.zeros((K, tm), I32))
                  for _ in range(PEER_HEADS))
    res2 = lax.fori_loop(0, K, stage2, init2)
    for h in range(PEER_HEADS):
        _, _, best, experts = res2[h]
        ex = jnp.exp(best - best[0:1, :])
        g_ref[h * K:(h + 1) * K, :] = ex / jnp.sum(ex, axis=0, keepdims=True)
        e_ref[h * K:(h + 1) * K, :] = experts


def _topk(s_t, tm=128):
    SW, N = s_t.shape
    K = PEER_TOPK
    return pl.pallas_call(
        _topk_kernel,
        grid=(N // tm,),
        in_specs=[pl.BlockSpec((SW, tm), lambda i: (0, i))],
        out_specs=[
            pl.BlockSpec((PEER_SLOTS, tm), lambda i: (0, i)),
            pl.BlockSpec((PEER_SLOTS, tm), lambda i: (0, i)),
        ],
        out_shape=[
            jax.ShapeDtypeStruct((PEER_SLOTS, N), I32),
            jax.ShapeDtypeStruct((PEER_SLOTS, N), F32),
        ],
        scratch_shapes=[pltpu.VMEM((PEER_HEADS, K * K, tm), F32),
                        pltpu.VMEM((PEER_HEADS, K * K, tm), I32)],
        compiler_params=_params(("arbitrary",)),
        name="peer_topk",
    )(s_t)


ROWS_PER_EXPERT = SUBLANES


def _table(t):
    n, d = t.shape
    return t.astype(BF16).reshape(n * (d // LANES), LANES)


def _expert_tile(tab_ref, e):
    row = pl.multiple_of(e * ROWS_PER_EXPERT, ROWS_PER_EXPERT)
    return tab_ref[pl.ds(row, ROWS_PER_EXPERT), :].astype(F32)


BITREV = (0, 4, 2, 6, 1, 5, 3, 7)


def _fold8(ps, sub):
    m4 = sub < 4
    m2 = (sub & 3) < 2
    m1 = (sub & 1) < 1
    s1 = []
    for a, b in zip(ps[0::2], ps[1::2]):
        s1.append(jnp.where(m4, a, b) + pltpu.roll(jnp.where(m4, b, a), 4, 0))
    s2 = []
    for a, b in zip(s1[0::2], s1[1::2]):
        s2.append(jnp.where(m2, a, pltpu.roll(b, 2, 0)) + jnp.where(m2, pltpu.roll(a, 6, 0), b))
    a, b = s2
    return jnp.where(m1, a, pltpu.roll(b, 1, 0)) + jnp.where(m1, pltpu.roll(a, 7, 0), b)


def _gelu_tanh(x):
    return 0.5 * x * (1.0 + jnp.tanh(np.sqrt(2.0 / np.pi) * (x + 0.044715 * (x * x * x))))


def _peer_u_kernel(e_ref, h_ref, g8_ref, tab_ref, o_ref, slab_ref, *, tb):
    sub = lax.broadcasted_iota(I32, (SUBLANES, LANES), 0)
    lane = lax.broadcasted_iota(I32, (SUBLANES, LANES), 1)
    ones = jnp.ones((LANES, LANES), BF16)
    groups_per_token = PEER_SLOTS // SUBLANES

    def group(gi, _):
        def token(tt, _):
            t = gi * SUBLANES + tt
            h = h_ref[t]
            base = t * PEER_SLOTS
            for j in range(groups_per_token):
                ps = []
                for qn in range(SUBLANES):
                    e = e_ref[base + j * SUBLANES + BITREV[qn]]
                    ps.append(_expert_tile(tab_ref, e) * h)
                r = _fold8(ps, sub)
                dst = pl.multiple_of(tt * PEER_SLOTS + j * SUBLANES, SUBLANES)
                slab_ref[pl.ds(dst, SUBLANES), :] = r
            return 0

        lax.fori_loop(0, SUBLANES, token, 0)
        s = slab_ref[...]
        hi = s.astype(BF16)
        lo = (s - hi.astype(F32)).astype(BF16)
        rs = (jnp.dot(hi, ones, preferred_element_type=F32)
              + jnp.dot(lo, ones, preferred_element_type=F32))
        out = jnp.zeros((SUBLANES, LANES), F32)
        for tt in range(SUBLANES):
            for j in range(groups_per_token):
                r0 = tt * PEER_SLOTS + j * SUBLANES
                out = jnp.where(lane == tt * groups_per_token + j, rs[r0:r0 + SUBLANES, :], out)
        o_ref[gi] = g8_ref[gi] * _gelu_tanh(out)
        return 0

    lax.fori_loop(0, tb // SUBLANES, group, 0)


def _peer_u(experts_flat, h3, g8, tab, tb=64):
    N = h3.shape[0]
    return pl.pallas_call(
        functools.partial(_peer_u_kernel, tb=tb),
        grid=(N // tb,),
        in_specs=[
            pl.BlockSpec((tb * PEER_SLOTS,), lambda i: (i,), memory_space=pltpu.SMEM),
            pl.BlockSpec((tb, SUBLANES, LANES), lambda i: (i, 0, 0)),
            pl.BlockSpec((tb // SUBLANES, SUBLANES, LANES), lambda i: (i, 0, 0)),
            pl.BlockSpec(tab.shape, lambda i: (0, 0), pipeline_mode=pl.Buffered(1)),
        ],
        out_specs=pl.BlockSpec((tb // SUBLANES, SUBLANES, LANES), lambda i: (i, 0, 0)),
        out_shape=jax.ShapeDtypeStruct((N // SUBLANES, SUBLANES, LANES), F32),
        scratch_shapes=[pltpu.VMEM((SUBLANES * PEER_SLOTS, LANES), F32)],
        compiler_params=_params(("arbitrary",), VMEM_LIMIT_BYTES),
        name="peer_u",
    )(experts_flat, h3, g8, tab)


def _peer_v_kernel(e_ref, c_ref, x1_ref, g2_ref, tab_ref, o_ref, *, tb):
    nacc = 4

    def token(t, _):
        base = t * PEER_SLOTS
        accs = [jnp.zeros((SUBLANES, LANES), F32) for _ in range(nacc)]
        for k in range(PEER_SLOTS):
            e = e_ref[base + k]
            c = c_ref[base + k]
            accs[k % nacc] = accs[k % nacc] + c * _expert_tile(tab_ref, e)
        y = (accs[0] + accs[1]) + (accs[2] + accs[3])
        o_ref[t] = x1_ref[t] + g2_ref[0] * y
        return 0

    lax.fori_loop(0, tb, token, 0)


def _peer_v(experts_flat, coef_flat, x1_3, g2_3, tab, T, tb=64):
    N = x1_3.shape[0]
    return pl.pallas_call(
        functools.partial(_peer_v_kernel, tb=tb),
        grid=(N // tb,),
        in_specs=[
            pl.BlockSpec((tb * PEER_SLOTS,), lambda i: (i,), memory_space=pltpu.SMEM),
            pl.BlockSpec((tb * PEER_SLOTS,), lambda i: (i,), memory_space=pltpu.SMEM),
            pl.BlockSpec((tb, SUBLANES, LANES), lambda i: (i, 0, 0)),
            pl.BlockSpec((1, SUBLANES, LANES), lambda i: ((i * tb) // T, 0, 0)),
            pl.BlockSpec(tab.shape, lambda i: (0, 0), pipeline_mode=pl.Buffered(1)),
        ],
        out_specs=pl.BlockSpec((tb, SUBLANES, LANES), lambda i: (i, 0, 0)),
        out_shape=jax.ShapeDtypeStruct((N, SUBLANES, LANES), F32),
        compiler_params=_params(("arbitrary",), VMEM_LIMIT_BYTES),
        name="peer_v",
    )(experts_flat, coef_flat, x1_3, g2_3, tab)


def _to_reg_layout(a):
    n = a.shape[0]
    return (a.reshape(n // SUBLANES, SUBLANES, PEER_SLOTS // SUBLANES, SUBLANES)
            .transpose(0, 3, 1, 2).reshape(n // SUBLANES, SUBLANES, LANES))


def _from_reg_layout(a):
    g = a.shape[0]
    return (a.reshape(g, SUBLANES, SUBLANES, PEER_SLOTS // SUBLANES)
            .transpose(0, 2, 3, 1).reshape(g * SUBLANES, PEER_SLOTS))


def kernel(x, c, w_ada, b_ada, norm1_g, w_in, gla_gate_w2, gla_gate_b, gla_norm_g, swa_qnorm_g,
           swa_knorm_g, swa_sinks, rel_bias, w_up_a, w_up_b, w_out, norm2_g, peer_wq, peer_subkeys,
           peer_u, peer_v):
    B, T, D = x.shape
    N = B * T
    L = w_ada.shape[0]
    mod = _adaln(c, w_ada, b_ada)
    weff_t = _fold_peer_keys(peer_wq, peer_subkeys)
    bias = _swa_bias(rel_bias)
    xf = x.reshape(N, D)
    for l in range(L):
        sh1, sc1, g1, sh2, sc2, g2 = [mod[l, :, i * D:(i + 1) * D] for i in range(6)]
        proj = _in_proj(xf, norm1_g[l], sc1, sh1, _pack_w_in(w_in[l]), T)
        proj3 = proj.reshape(B, T, PROJ_W)
        w2p = jnp.zeros((LANES, GLA_HEADS * GLA_DK), BF16).at[:GLA_RANK].set(gla_gate_w2[l].astype(BF16))
        gla_o = _gla(proj3, w2p, gla_gate_b[l].reshape(1, -1), gla_norm_g[l].reshape(1, -1))
        swa_o = _swa(proj3, bias, swa_qnorm_g[l].reshape(1, -1), swa_knorm_g[l].reshape(1, -1),
                     swa_sinks[l])
        x1, h2, s_t = _merge(xf, proj, gla_o.reshape(N, -1), swa_o.reshape(N, -1),
                             w_up_a[l].astype(BF16), w_up_b[l].astype(BF16), w_out[l].astype(BF16),
                             g1, norm2_g[l], sc2, sh2, weff_t[l], T)
        e_t, g_t = _topk(s_t)
        experts_flat = e_t.T.reshape(N * PEER_SLOTS)
        g8 = _to_reg_layout(g_t.T)
        coef8 = _peer_u(experts_flat, h2.reshape(N, SUBLANES, LANES), g8, _table(peer_u[l]))
        coef_flat = _from_reg_layout(coef8).reshape(N * PEER_SLOTS)
        g2_3 = g2.reshape(B, SUBLANES, LANES)
        xf = _peer_v(experts_flat, coef_flat, x1.reshape(N, SUBLANES, LANES), g2_3,
                     _table(peer_v[l]), T).reshape(N, D)
    return xf.reshape(B, T, D)
```

```python
import functools

import numpy as np
import jax
import jax.numpy as jnp
from jax import lax
from jax.experimental import pallas as pl
from jax.experimental.pallas import tpu as pltpu

F32 = jnp.float32
BF16 = jnp.bfloat16
I32 = jnp.int32
HIGHEST = lax.Precision.HIGHEST
EPS = 1e-6

GLA_HEADS = 4
GLA_DK = 128
GLA_DV = 256
GLA_RANK = 16
GLA_TAU = 16.0
GLA_CHUNK = 64
SWA_HEADS = 16
SWA_KV_HEADS = 2
SWA_HD = 64
SWA_BLOCK = 128
N_BUCKETS = 32
MAX_DISTANCE = 128
PEER_HEADS = 8
PEER_NKEYS = 128
PEER_TOPK = 16
PEER_SLOTS = PEER_HEADS * PEER_TOPK

SUBLANES = 8
LANES = 128
VMEM_LIMIT_BYTES = 56 * 1024 * 1024

NT_DIMS = (((1,), (1,)), ((), ()))
TN_DIMS = (((0,), (0,)), ((), ()))

COL_Q, COL_K, COL_V, COL_GR, COL_SQ, COL_GA, COL_GB = 0, 512, 1024, 2048, 3072, 4096, 5120
COL_SK, COL_SV, COL_GLR = 6144, 6272, 6400
PROJ_W = 6528


def _params(sem, vmem=None):
    return pltpu.CompilerParams(dimension_semantics=sem, vmem_limit_bytes=vmem)


def _adaln_kernel(c_ref, w_ref, b_ref, o_ref):
    c = c_ref[...]
    a = c * jax.nn.sigmoid(c)
    o_ref[0] = jnp.dot(a, w_ref[0], preferred_element_type=F32, precision=HIGHEST) + b_ref[0]


def _adaln(c, w_ada, b_ada):
    L, D, W = w_ada.shape
    B = c.shape[0]
    rows = -(-B // SUBLANES) * SUBLANES
    cp = jnp.zeros((rows, D), F32).at[:B].set(c)
    tn = W // 4
    out = pl.pallas_call(
        _adaln_kernel,
        grid=(L, W // tn),
        in_specs=[
            pl.BlockSpec((rows, D), lambda l, j: (0, 0)),
            pl.BlockSpec((1, D, tn), lambda l, j: (l, 0, j)),
            pl.BlockSpec((1, 1, tn), lambda l, j: (l, 0, j)),
        ],
        out_specs=pl.BlockSpec((1, rows, tn), lambda l, j: (l, 0, j)),
        out_shape=jax.ShapeDtypeStruct((L, rows, W), F32),
        compiler_params=_params(("arbitrary", "arbitrary")),
        name="adaln",
    )(cp, w_ada, b_ada.reshape(L, 1, W))
    return out[:, :B]


def _fold_kernel(sk_ref, wq_ref, o_ref):
    o_ref[0] = lax.dot_general(sk_ref[0, 0], wq_ref[0], NT_DIMS, precision=HIGHEST,
                               preferred_element_type=F32).astype(BF16)


def _fold_peer_keys(peer_wq, peer_subkeys):
    L, D, QW = peer_wq.shape
    half = peer_subkeys.shape[-1]
    ngroups = QW // half
    return pl.pallas_call(
        _fold_kernel,
        grid=(L, ngroups),
        in_specs=[
            pl.BlockSpec((1, 1, PEER_NKEYS, half), lambda l, g: (l, g % 2, 0, 0)),
            pl.BlockSpec((1, D, half), lambda l, g: (l, 0, g)),
        ],
        out_specs=pl.BlockSpec((1, PEER_NKEYS, D), lambda l, g: (l, g, 0)),
        out_shape=jax.ShapeDtypeStruct((L, ngroups * PEER_NKEYS, D), BF16),
        compiler_params=_params(("arbitrary", "arbitrary")),
        name="peer_fold",
    )(peer_subkeys, peer_wq)


def _inproj_kernel(x_ref, g_ref, sc_ref, sh_ref, w_ref, o_ref):
    x = x_ref[...]
    ms = jnp.mean(x * x, axis=-1, keepdims=True)
    y = x * lax.rsqrt(ms + EPS) * g_ref[...]
    h = y * (1.0 + sc_ref[0]) + sh_ref[0]
    o_ref[...] = jnp.dot(h.astype(BF16), w_ref[...], preferred_element_type=F32).astype(BF16)


def _in_proj(xf, g, sc, sh, wp, T, tm=512):
    N, D = xf.shape
    B = sc.shape[0]
    ncol = 3
    tn = PROJ_W // ncol
    return pl.pallas_call(
        _inproj_kernel,
        grid=(ncol, N // tm),
        in_specs=[
            pl.BlockSpec((tm, D), lambda j, i: (i, 0)),
            pl.BlockSpec((1, D), lambda j, i: (0, 0)),
            pl.BlockSpec((1, 1, D), lambda j, i: ((i * tm) // T, 0, 0)),
            pl.BlockSpec((1, 1, D), lambda j, i: ((i * tm) // T, 0, 0)),
            pl.BlockSpec((D, tn), lambda j, i: (0, j)),
        ],
        out_specs=pl.BlockSpec((tm, tn), lambda j, i: (i, j)),
        out_shape=jax.ShapeDtypeStruct((N, PROJ_W), BF16),
        compiler_params=_params(("arbitrary", "arbitrary")),
        name="in_proj",
    )(xf, g.reshape(1, D), sc.reshape(B, 1, D), sh.reshape(B, 1, D), wp)


def _pack_w_in(w):
    o = np.cumsum([0, 512, 512, 1024, GLA_RANK, 1024, 1024, 128, 128, 1024, 1024])
    q, k, v, glr, gr, sq, sk, sv, ga, gb = [w[:, o[i]:o[i + 1]] for i in range(10)]
    pad = jnp.zeros((w.shape[0], LANES - GLA_RANK), w.dtype)
    return jnp.concatenate([q, k, v, gr, sq, ga, gb, sk, sv, glr, pad], axis=1).astype(BF16)


def _gla_kernel(q_ref, k_ref, v_ref, r_ref, glr_ref, w2_ref, b2_ref, ng_ref, o_ref, st_ref, *, nchunk):
    @pl.when(pl.program_id(2) == 0)
    def _():
        st_ref[...] = jnp.zeros_like(st_ref)

    C = GLA_CHUNK
    row = lax.broadcasted_iota(I32, (C, C), 0)
    col = lax.broadcasted_iota(I32, (C, C), 1)
    tri = col <= row
    tri_f = tri.astype(F32)
    w2 = w2_ref[...]
    b2 = b2_ref[...]
    ng = ng_ref[...]
    for ci in range(nchunk):
        sl = pl.ds(ci * C, C)
        q = q_ref[sl, :].astype(F32) * (GLA_DK ** -0.5)
        k = k_ref[sl, :].astype(F32)
        v = v_ref[sl, :]
        z = jnp.dot(glr_ref[sl, :], w2, preferred_element_type=F32) + b2
        log_a = (jnp.minimum(z, 0.0) - jnp.log(1.0 + jnp.exp(-jnp.abs(z)))) * (1.0 / GLA_TAU)
        b = jnp.dot(tri_f, log_a, preferred_element_type=F32, precision=HIGHEST)
        b_last = b[C - 1:C, :]
        q_dec = (q * jnp.exp(b)).astype(BF16)
        k_inv = (k * jnp.exp(-b)).astype(BF16)
        attn = lax.dot_general(q_dec, k_inv, NT_DIMS, preferred_element_type=F32)
        attn = jnp.where(tri, attn, 0.0).astype(BF16)
        st = st_ref[...]
        o = (jnp.dot(attn, v, preferred_element_type=F32)
             + lax.dot_general(q_dec, st.astype(BF16), NT_DIMS, preferred_element_type=F32))
        k_tail = (k * jnp.exp(b_last - b)).astype(BF16)
        st_ref[...] = st * jnp.exp(b_last) + lax.dot_general(v, k_tail, TN_DIMS, preferred_element_type=F32)
        on = o * lax.rsqrt(jnp.mean(o * o, axis=-1, keepdims=True) + EPS) * ng
        r = r_ref[sl, :].astype(F32)
        o_ref[sl, :] = (on * (r * jax.nn.sigmoid(r))).astype(BF16)


def _gla(proj, w2p, b2, ng, tc=256):
    B, T, _ = proj.shape
    H = GLA_HEADS
    return pl.pallas_call(
        functools.partial(_gla_kernel, nchunk=tc // GLA_CHUNK),
        grid=(B, H, T // tc),
        in_specs=[
            pl.BlockSpec((None, tc, GLA_DK), lambda b, h, c: (b, c, COL_Q // GLA_DK + h)),
            pl.BlockSpec((None, tc, GLA_DK), lambda b, h, c: (b, c, COL_K // GLA_DK + h)),
            pl.BlockSpec((None, tc, GLA_DV), lambda b, h, c: (b, c, COL_V // GLA_DV + h)),
            pl.BlockSpec((None, tc, GLA_DV), lambda b, h, c: (b, c, COL_GR // GLA_DV + h)),
            pl.BlockSpec((None, tc, LANES), lambda b, h, c: (b, c, COL_GLR // LANES)),
            pl.BlockSpec((LANES, GLA_DK), lambda b, h, c: (0, h)),
            pl.BlockSpec((1, GLA_DK), lambda b, h, c: (0, h)),
            pl.BlockSpec((1, GLA_DV), lambda b, h, c: (0, 0)),
        ],
        out_specs=pl.BlockSpec((None, tc, GLA_DV), lambda b, h, c: (b, c, h)),
        out_shape=jax.ShapeDtypeStruct((B, T, H * GLA_DV), BF16),
        scratch_shapes=[pltpu.VMEM((GLA_DV, GLA_DK), F32)],
        compiler_params=_params(("arbitrary", "arbitrary", "arbitrary")),
        name="gla",
    )(proj, proj, proj, proj, proj, w2p, b2, ng)


def _t5_bucket(dist):
    max_exact = N_BUCKETS // 2
    d = np.maximum(dist, 1).astype(np.float32)
    large = max_exact + (np.log(d / max_exact) / np.log(MAX_DISTANCE / max_exact)
                         * (N_BUCKETS - max_exact)).astype(np.int32)
    large = np.minimum(large, N_BUCKETS - 1)
    return np.where(dist < max_exact, dist, large).astype(np.int32)


NEG_BIG = -1e30


def _swa_bias(rel_bias):
    blk = SWA_BLOCK
    qi = np.arange(blk)[:, None]
    sj = np.arange(2 * blk)[None, :]
    dist = blk + qi - sj
    band = (dist >= 0) & (dist < blk)
    bucket = _t5_bucket(np.clip(dist, 0, None))
    bias = rel_bias[bucket].astype(F32).transpose(2, 0, 1)
    return jnp.where(jnp.asarray(band)[None], bias, NEG_BIG)


def _swa_kernel(sink_ref, q_ref, kp_ref, kc_ref, vp_ref, vc_ref, bias_ref, qg_ref, kg_ref, o_ref):
    blk = SWA_BLOCK
    hd = SWA_HD
    group = SWA_HEADS // SWA_KV_HEADS
    first = pl.program_id(1) == 0
    k2 = jnp.concatenate([kp_ref[...], kc_ref[...]], axis=0).astype(F32)
    v2 = jnp.concatenate([vp_ref[...], vc_ref[...]], axis=0)
    col = lax.broadcasted_iota(I32, (blk, 2 * blk), 1)
    kill = col < jnp.where(first, blk, 0)
    qg = qg_ref[...]
    kg = kg_ref[...]
    outs = []
    for kh in range(SWA_KV_HEADS):
        kk = k2[:, kh * hd:(kh + 1) * hd]
        kk = (kk * lax.rsqrt(jnp.mean(kk * kk, axis=-1, keepdims=True) + EPS) * kg).astype(BF16)
        vv = v2[:, kh * hd:(kh + 1) * hd]
        for g in range(group):
            h = kh * group + g
            qh = q_ref[:, h * hd:(h + 1) * hd].astype(F32)
            qh = qh * lax.rsqrt(jnp.mean(qh * qh, axis=-1, keepdims=True) + EPS) * qg * (hd ** -0.5)
            logits = lax.dot_general(qh.astype(BF16), kk, NT_DIMS, preferred_element_type=F32) + bias_ref[h]
            logits = jnp.where(kill, NEG_BIG, logits)
            sink = sink_ref[h]
            m = jnp.maximum(jnp.max(logits, axis=-1, keepdims=True), sink)
            p = jnp.exp(logits - m)
            denom = jnp.sum(p, axis=-1, keepdims=True) + jnp.exp(sink - m)
            outs.append(jnp.dot((p / denom).astype(BF16), vv, preferred_element_type=F32))
    o_ref[...] = jnp.concatenate(outs, axis=-1).astype(BF16)


def _swa(proj, bias, qg, kg, sinks):
    B, T, _ = proj.shape
    blk = SWA_BLOCK
    qw = SWA_HEADS * SWA_HD
    kvw = SWA_KV_HEADS * SWA_HD
    prev = lambda b, i: jnp.maximum(i - 1, 0)
    return pl.pallas_call(
        _swa_kernel,
        grid=(B, T // blk),
        in_specs=[
            pl.BlockSpec(memory_space=pltpu.SMEM),
            pl.BlockSpec((None, blk, qw), lambda b, i: (b, i, COL_SQ // qw)),
            pl.BlockSpec((None, blk, kvw), lambda b, i: (b, prev(b, i), COL_SK // kvw)),
            pl.BlockSpec((None, blk, kvw), lambda b, i: (b, i, COL_SK // kvw)),
            pl.BlockSpec((None, blk, kvw), lambda b, i: (b, prev(b, i), COL_SV // kvw)),
            pl.BlockSpec((None, blk, kvw), lambda b, i: (b, i, COL_SV // kvw)),
            pl.BlockSpec((SWA_HEADS, blk, 2 * blk), lambda b, i: (0, 0, 0)),
            pl.BlockSpec((1, SWA_HD), lambda b, i: (0, 0)),
            pl.BlockSpec((1, SWA_HD), lambda b, i: (0, 0)),
        ],
        out_specs=pl.BlockSpec((None, blk, qw), lambda b, i: (b, i, 0)),
        out_shape=jax.ShapeDtypeStruct((B, T, qw), BF16),
        compiler_params=_params(("arbitrary", "arbitrary")),
        name="swa",
    )(sinks, proj, proj, proj, proj, proj, bias, qg, kg)


def _merge_kernel(x_ref, ga_ref, gb_ref, go_ref, so_ref, wa_ref, wb_ref, wo_ref, g1_ref,
                  n2_ref, sc2_ref, sh2_ref, we_ref, x1_ref, h2_ref, st_ref):
    ya = jnp.dot(go_ref[...], wa_ref[...], preferred_element_type=F32)
    yb = jnp.dot(so_ref[...], wb_ref[...], preferred_element_type=F32)
    m = jax.nn.sigmoid(ga_ref[...].astype(F32)) * ya + jax.nn.sigmoid(gb_ref[...].astype(F32)) * yb
    mixed = jnp.dot(m.astype(BF16), wo_ref[...], preferred_element_type=F32)
    x1 = x_ref[...] + g1_ref[0] * mixed
    x1_ref[...] = x1
    y = x1 * lax.rsqrt(jnp.mean(x1 * x1, axis=-1, keepdims=True) + EPS) * n2_ref[...]
    h2 = y * (1.0 + sc2_ref[0]) + sh2_ref[0]
    h2_ref[...] = h2
    st_ref[...] = lax.dot_general(we_ref[...], h2.astype(BF16), NT_DIMS, preferred_element_type=F32)


def _merge(xf, proj2, gla_o, swa_o, wa, wb, wo, g1, n2, sc2, sh2, weff_t, T, tm=256):
    N, D = xf.shape
    B = g1.shape[0]
    SW = weff_t.shape[0]
    bat = lambda i: ((i * tm) // T, 0, 0)
    full = lambda i: (0, 0)
    return pl.pallas_call(
        _merge_kernel,
        grid=(N // tm,),
        in_specs=[
            pl.BlockSpec((tm, D), lambda i: (i, 0)),
            pl.BlockSpec((tm, D), lambda i: (i, COL_GA // D)),
            pl.BlockSpec((tm, D), lambda i: (i, COL_GB // D)),
            pl.BlockSpec((tm, D), lambda i: (i, 0)),
            pl.BlockSpec((tm, D), lambda i: (i, 0)),
            pl.BlockSpec((D, D), full),
            pl.BlockSpec((D, D), full),
            pl.BlockSpec((D, D), full),
            pl.BlockSpec((1, 1, D), bat),
            pl.BlockSpec((1, D), full),
            pl.BlockSpec((1, 1, D), bat),
            pl.BlockSpec((1, 1, D), bat),
            pl.BlockSpec((SW, D), full),
        ],
        out_specs=[
            pl.BlockSpec((tm, D), lambda i: (i, 0)),
            pl.BlockSpec((tm, D), lambda i: (i, 0)),
            pl.BlockSpec((SW, tm), lambda i: (0, i)),
        ],
        out_shape=[
            jax.ShapeDtypeStruct((N, D), F32),
            jax.ShapeDtypeStruct((N, D), F32),
            jax.ShapeDtypeStruct((SW, N), F32),
        ],
        compiler_params=_params(("arbitrary",), VMEM_LIMIT_BYTES),
        name="merge",
    )(xf, proj2, proj2, gla_o, swa_o, wa, wb, wo, g1.reshape(B, 1, D), n2.reshape(1, D),
      sc2.reshape(B, 1, D), sh2.reshape(B, 1, D), weff_t)


def _next_best(s, rio, m_prev, pos_prev, nrows):
    after = jnp.logical_or(s < m_prev, jnp.logical_and(s == m_prev, rio > pos_prev))
    cand = jnp.where(after, s, -jnp.inf)
    m = jnp.max(cand, axis=0, keepdims=True)
    pos = jnp.min(jnp.where(cand == m, rio, nrows), axis=0, keepdims=True)
    return m, pos


def _topk_kernel(s_ref, e_ref, g_ref, cand_ref, cidx_ref):
    K = PEER_TOPK
    nk = PEER_NKEYS
    tm = s_ref.shape[1]
    ngroups = 2 * PEER_HEADS
    rio = lax.broadcasted_iota(I32, (nk, tm), 0)
    kio = lax.broadcasted_iota(I32, (K, tm), 0)
    inf_row = jnp.full((1, tm), jnp.inf, F32)
    neg_row = jnp.full((1, tm), -1, I32)

    def stage1(k, carry):
        new = []
        for g in range(ngroups):
            m_prev, pos_prev, vals, idxs = carry[g]
            s = s_ref[g * nk:(g + 1) * nk, :]
            m, pos = _next_best(s, rio, m_prev, pos_prev, nk)
            vals = jnp.where(kio == k, m, vals)
            idxs = jnp.where(kio == k, pos, idxs)
            new.append((m, pos, vals, idxs))
        return tuple(new)

    init = tuple((inf_row, neg_row, jnp.zeros((K, tm), F32), jnp.zeros((K, tm), I32))
                 for _ in range(ngroups))
    res = lax.fori_loop(0, K, stage1, init)

    for h in range(PEER_HEADS):
        _, _, v0, i0 = res[2 * h]
        _, _, v1, i1 = res[2 * h + 1]
        for a in range(K):
            cand_ref[h, a * K:(a + 1) * K, :] = v0[a:a + 1, :] + v1
            cidx_ref[h, a * K:(a + 1) * K, :] = i0[a:a + 1, :] * nk + i1

    rio2 = lax.broadcasted_iota(I32, (K * K, tm), 0)

    def stage2(k, carry):
        new = []
        for h in range(PEER_HEADS):
            m_prev, pos_prev, vals, idxs = carry[h]
            m, pos = _next_best(cand_ref[h], rio2, m_prev, pos_prev, K * K)
            e = jnp.max(jnp.where(rio2 == pos, cidx_ref[h], -1), axis=0, keepdims=True)
            vals = jnp.where(kio == k, m, vals)
            idxs = jnp.where(kio == k, e, idxs)
            new.append((m, pos, vals, idxs))
        return tuple(new)

    init2 = tuple((inf_row, neg_row, jnp.zeros((K, tm), F32), jnp.zeros((K, tm), I32))
                  for _ in range(PEER_HEADS))
    res2 = lax.fori_loop(0, K, stage2, init2)
    for h in range(PEER_HEADS):
        _, _, best, experts = res2[h]
        ex = jnp.exp(best - best[0:1, :])
        g_ref[h * K:(h + 1) * K, :] = ex / jnp.sum(ex, axis=0, keepdims=True)
        e_ref[h * K:(h + 1) * K, :] = experts * WORDS_PER_EXPERT


def _topk(s_t, tm=128):
    SW, N = s_t.shape
    K = PEER_TOPK
    return pl.pallas_call(
        _topk_kernel,
        grid=(N // tm,),
        in_specs=[pl.BlockSpec((SW, tm), lambda i: (0, i))],
        out_specs=[
            pl.BlockSpec((PEER_SLOTS, tm), lambda i: (0, i)),
            pl.BlockSpec((PEER_SLOTS, tm), lambda i: (0, i)),
        ],
        out_shape=[
            jax.ShapeDtypeStruct((PEER_SLOTS, N), I32),
            jax.ShapeDtypeStruct((PEER_SLOTS, N), F32),
        ],
        scratch_shapes=[pltpu.VMEM((PEER_HEADS, K * K, tm), F32),
                        pltpu.VMEM((PEER_HEADS, K * K, tm), I32)],
        compiler_params=_params(("arbitrary",)),
        name="peer_topk",
    )(s_t)


WORDS_PER_EXPERT = SUBLANES // 2


def _table(t):
    n, d = t.shape
    rows = d // LANES
    tb = t.astype(BF16).reshape(n, rows // 2, 2, LANES).transpose(0, 1, 3, 2)
    return lax.bitcast_convert_type(tb, I32).reshape(n * (rows // 2), LANES)


def _expert_tile(tab_ref, e4):
    words = tab_ref[pl.ds(pl.multiple_of(e4, WORDS_PER_EXPERT), WORDS_PER_EXPERT), :]
    return pltpu.bitcast(words, BF16).astype(F32)


BITREV = (0, 4, 2, 6, 1, 5, 3, 7)


def _fold8(ps, sub):
    m4 = sub < 4
    m2 = (sub & 3) < 2
    m1 = (sub & 1) < 1
    s1 = []
    for a, b in zip(ps[0::2], ps[1::2]):
        s1.append(jnp.where(m4, a, b) + pltpu.roll(jnp.where(m4, b, a), 4, 0))
    s2 = []
    for a, b in zip(s1[0::2], s1[1::2]):
        s2.append(jnp.where(m2, a, pltpu.roll(b, 2, 0)) + jnp.where(m2, pltpu.roll(a, 6, 0), b))
    a, b = s2
    return jnp.where(m1, a, pltpu.roll(b, 1, 0)) + jnp.where(m1, pltpu.roll(a, 7, 0), b)


def _gelu_tanh(x):
    return 0.5 * x * (1.0 + jnp.tanh(np.sqrt(2.0 / np.pi) * (x + 0.044715 * (x * x * x))))


def _row_sums(x):
    ones = jnp.ones((LANES, LANES), BF16)
    hi = x.astype(BF16)
    lo = (x - hi.astype(F32)).astype(BF16)
    return (jnp.dot(hi, ones, preferred_element_type=F32)
            + jnp.dot(lo, ones, preferred_element_type=F32))


def _peer_u_kernel(e_ref, h_ref, g8_ref, tab_ref, o_ref, slab_ref, *, tb):
    sub = lax.broadcasted_iota(I32, (SUBLANES, LANES), 0)
    lane = lax.broadcasted_iota(I32, (SUBLANES, LANES), 1)
    groups_per_token = PEER_SLOTS // SUBLANES

    def token(t, _):
        h = h_ref[t]
        for j in range(groups_per_token):
            ps = [_expert_tile(tab_ref, e_ref[t, j * SUBLANES + BITREV[qn]]) * h
                  for qn in range(SUBLANES)]
            dst = pl.multiple_of(t * PEER_SLOTS + j * SUBLANES, SUBLANES)
            slab_ref[pl.ds(dst, SUBLANES), :] = _fold8(ps, sub)
        return 0

    lax.fori_loop(0, tb, token, 0)
    rows = SUBLANES * PEER_SLOTS
    for gi in range(tb // SUBLANES):
        rs = _row_sums(slab_ref[gi * rows:(gi + 1) * rows, :])
        out = jnp.zeros((SUBLANES, LANES), F32)
        for tt in range(SUBLANES):
            for j in range(groups_per_token):
                r0 = tt * PEER_SLOTS + j * SUBLANES
                out = jnp.where(lane == tt * groups_per_token + j, rs[r0:r0 + SUBLANES, :], out)
        o_ref[gi] = g8_ref[gi] * _gelu_tanh(out)


def _peer_u(e4, h3, g8, tab, tb=64):
    N = h3.shape[0]
    return pl.pallas_call(
        functools.partial(_peer_u_kernel, tb=tb),
        grid=(N // tb,),
        in_specs=[
            pl.BlockSpec((tb, PEER_SLOTS), lambda i: (i, 0), memory_space=pltpu.SMEM),
            pl.BlockSpec((tb, SUBLANES, LANES), lambda i: (i, 0, 0)),
            pl.BlockSpec((tb // SUBLANES, SUBLANES, LANES), lambda i: (i, 0, 0)),
            pl.BlockSpec(tab.shape, lambda i: (0, 0), pipeline_mode=pl.Buffered(1)),
        ],
        out_specs=pl.BlockSpec((tb // SUBLANES, SUBLANES, LANES), lambda i: (i, 0, 0)),
        out_shape=jax.ShapeDtypeStruct((N // SUBLANES, SUBLANES, LANES), F32),
        scratch_shapes=[pltpu.VMEM((tb * PEER_SLOTS, LANES), F32)],
        compiler_params=_params(("arbitrary",), VMEM_LIMIT_BYTES),
        name="peer_u",
    )(e4, h3, g8, tab)


def _peer_v_kernel(e_ref, c_ref, x1_ref, g2_ref, tab_ref, o_ref, splat_ref, *, tb):
    nacc = 4
    rio = lax.broadcasted_iota(I32, (LANES, LANES), 0)
    lio = lax.broadcasted_iota(I32, (LANES, LANES), 1)
    diag = (rio == lio)[None]
    for t8 in range(tb // SUBLANES):
        cc = c_ref[t8 * SUBLANES:(t8 + 1) * SUBLANES, :]
        d = jnp.where(diag, cc[:, None, :], 0.0).reshape(SUBLANES * LANES, LANES)
        splat_ref[t8 * SUBLANES:(t8 + 1) * SUBLANES] = _row_sums(d).reshape(SUBLANES, LANES, LANES)

    def token(t, _):
        accs = [jnp.zeros((SUBLANES, LANES), F32) for _ in range(nacc)]
        for k in range(PEER_SLOTS):
            c = jnp.broadcast_to(splat_ref[t, k:k + 1, :], (SUBLANES, LANES))
            accs[k % nacc] = accs[k % nacc] + c * _expert_tile(tab_ref, e_ref[t, k])
        y = (accs[0] + accs[1]) + (accs[2] + accs[3])
        o_ref[t] = x1_ref[t] + g2_ref[0] * y
        return 0

    lax.fori_loop(0, tb, token, 0)


def _peer_v(e4, coef, x1_3, g2_3, tab, T, tb=64):
    N = x1_3.shape[0]
    return pl.pallas_call(
        functools.partial(_peer_v_kernel, tb=tb),
        grid=(N // tb,),
        in_specs=[
            pl.BlockSpec((tb, PEER_SLOTS), lambda i: (i, 0), memory_space=pltpu.SMEM),
            pl.BlockSpec((tb, PEER_SLOTS), lambda i: (i, 0)),
            pl.BlockSpec((tb, SUBLANES, LANES), lambda i: (i, 0, 0)),
            pl.BlockSpec((1, SUBLANES, LANES), lambda i: ((i * tb) // T, 0, 0)),
            pl.BlockSpec(tab.shape, lambda i: (0, 0), pipeline_mode=pl.Buffered(1)),
        ],
        out_specs=pl.BlockSpec((tb, SUBLANES, LANES), lambda i: (i, 0, 0)),
        out_shape=jax.ShapeDtypeStruct((N, SUBLANES, LANES), F32),
        scratch_shapes=[pltpu.VMEM((tb, LANES, LANES), F32)],
        compiler_params=_params(("arbitrary",), VMEM_LIMIT_BYTES),
        name="peer_v",
    )(e4, coef, x1_3, g2_3, tab)


def _to_reg_layout(a):
    n = a.shape[0]
    return (a.reshape(n // SUBLANES, SUBLANES, PEER_SLOTS // SUBLANES, SUBLANES)
            .transpose(0, 3, 1, 2).reshape(n // SUBLANES, SUBLANES, LANES))


def _from_reg_layout(a):
    g = a.shape[0]
    return (a.reshape(g, SUBLANES, SUBLANES, PEER_SLOTS // SUBLANES)
            .transpose(0, 2, 3, 1).reshape(g * SUBLANES, PEER_SLOTS))


def kernel(x, c, w_ada, b_ada, norm1_g, w_in, gla_gate_w2, gla_gate_b, gla_norm_g, swa_qnorm_g,
           swa_knorm_g, swa_sinks, rel_bias, w_up_a, w_up_b, w_out, norm2_g, peer_wq, peer_subkeys,
           peer_u, peer_v):
    B, T, D = x.shape
    N = B * T
    L = w_ada.shape[0]
    mod = _adaln(c, w_ada, b_ada)
    weff_t = _fold_peer_keys(peer_wq, peer_subkeys)
    bias = _swa_bias(rel_bias)
    xf = x.reshape(N, D)
    for l in range(L):
        sh1, sc1, g1, sh2, sc2, g2 = [mod[l, :, i * D:(i + 1) * D] for i in range(6)]
        proj = _in_proj(xf, norm1_g[l], sc1, sh1, _pack_w_in(w_in[l]), T)
        proj3 = proj.reshape(B, T, PROJ_W)
        w2p = jnp.zeros((LANES, GLA_HEADS * GLA_DK), BF16).at[:GLA_RANK].set(gla_gate_w2[l].astype(BF16))
        gla_o = _gla(proj3, w2p, gla_gate_b[l].reshape(1, -1), gla_norm_g[l].reshape(1, -1))
        swa_o = _swa(proj3, bias, swa_qnorm_g[l].reshape(1, -1), swa_knorm_g[l].reshape(1, -1),
                     swa_sinks[l])
        x1, h2, s_t = _merge(xf, proj, gla_o.reshape(N, -1), swa_o.reshape(N, -1),
                             w_up_a[l].astype(BF16), w_up_b[l].astype(BF16), w_out[l].astype(BF16),
                             g1, norm2_g[l], sc2, sh2, weff_t[l], T)
        e_t, g_t = _topk(s_t)
        e4 = e_t.T
        g8 = _to_reg_layout(g_t.T)
        coef8 = _peer_u(e4, h2.reshape(N, SUBLANES, LANES), g8, _table(peer_u[l]))
        g2_3 = g2.reshape(B, SUBLANES, LANES)
        xf = _peer_v(e4, _from_reg_layout(coef8), x1.reshape(N, SUBLANES, LANES), g2_3,
                     _table(peer_v[l]), T).reshape(N, D)
    return xf.reshape(B, T, D)
```

```python
import functools

import numpy as np
import jax
import jax.numpy as jnp
from jax import lax
from jax.experimental import pallas as pl
from jax.experimental.pallas import tpu as pltpu

F32 = jnp.float32
BF16 = jnp.bfloat16
I32 = jnp.int32
HIGHEST = lax.Precision.HIGHEST
EPS = 1e-6

GLA_HEADS = 4
GLA_DK = 128
GLA_DV = 256
GLA_RANK = 16
GLA_TAU = 16.0
GLA_CHUNK = 64
SWA_HEADS = 16
SWA_KV_HEADS = 2
SWA_HD = 64
SWA_BLOCK = 128
N_BUCKETS = 32
MAX_DISTANCE = 128
PEER_HEADS = 8
PEER_NKEYS = 128
PEER_TOPK = 16
PEER_SLOTS = PEER_HEADS * PEER_TOPK

SUBLANES = 8
LANES = 128
VMEM_LIMIT_BYTES = 56 * 1024 * 1024

NT_DIMS = (((1,), (1,)), ((), ()))
TN_DIMS = (((0,), (0,)), ((), ()))

COL_Q, COL_K, COL_V, COL_GR, COL_SQ, COL_GA, COL_GB = 0, 512, 1024, 2048, 3072, 4096, 5120
COL_SK, COL_SV, COL_GLR = 6144, 6272, 6400
PROJ_W = 6528


def _params(sem, vmem=None):
    return pltpu.CompilerParams(dimension_semantics=sem, vmem_limit_bytes=vmem)


def _adaln_kernel(c_ref, w_ref, b_ref, o_ref):
    c = c_ref[...]
    a = c * jax.nn.sigmoid(c)
    o_ref[0] = jnp.dot(a, w_ref[0], preferred_element_type=F32, precision=HIGHEST) + b_ref[0]


def _adaln(c, w_ada, b_ada):
    L, D, W = w_ada.shape
    B = c.shape[0]
    rows = -(-B // SUBLANES) * SUBLANES
    cp = jnp.zeros((rows, D), F32).at[:B].set(c)
    tn = W // 4
    out = pl.pallas_call(
        _adaln_kernel,
        grid=(L, W // tn),
        in_specs=[
            pl.BlockSpec((rows, D), lambda l, j: (0, 0)),
            pl.BlockSpec((1, D, tn), lambda l, j: (l, 0, j)),
            pl.BlockSpec((1, 1, tn), lambda l, j: (l, 0, j)),
        ],
        out_specs=pl.BlockSpec((1, rows, tn), lambda l, j: (l, 0, j)),
        out_shape=jax.ShapeDtypeStruct((L, rows, W), F32),
        compiler_params=_params(("arbitrary", "arbitrary")),
        name="adaln",
    )(cp, w_ada, b_ada.reshape(L, 1, W))
    return out[:, :B]


def _fold_kernel(sk_ref, wq_ref, o_ref):
    o_ref[0] = lax.dot_general(sk_ref[0, 0], wq_ref[0], NT_DIMS, precision=HIGHEST,
                               preferred_element_type=F32).astype(BF16)


def _fold_peer_keys(peer_wq, peer_subkeys):
    L, D, QW = peer_wq.shape
    half = peer_subkeys.shape[-1]
    ngroups = QW // half
    return pl.pallas_call(
        _fold_kernel,
        grid=(L, ngroups),
        in_specs=[
            pl.BlockSpec((1, 1, PEER_NKEYS, half), lambda l, g: (l, g % 2, 0, 0)),
            pl.BlockSpec((1, D, half), lambda l, g: (l, 0, g)),
        ],
        out_specs=pl.BlockSpec((1, PEER_NKEYS, D), lambda l, g: (l, g, 0)),
        out_shape=jax.ShapeDtypeStruct((L, ngroups * PEER_NKEYS, D), BF16),
        compiler_params=_params(("arbitrary", "arbitrary")),
        name="peer_fold",
    )(peer_subkeys, peer_wq)


def _inproj_kernel(x_ref, g_ref, sc_ref, sh_ref, w_ref, o_ref):
    x = x_ref[...]
    ms = jnp.mean(x * x, axis=-1, keepdims=True)
    y = x * lax.rsqrt(ms + EPS) * g_ref[...]
    h = y * (1.0 + sc_ref[0]) + sh_ref[0]
    o_ref[...] = jnp.dot(h.astype(BF16), w_ref[...], preferred_element_type=F32).astype(BF16)


def _in_proj(xf, g, sc, sh, wp, T, tm=512):
    N, D = xf.shape
    B = sc.shape[0]
    ncol = 3
    tn = PROJ_W // ncol
    return pl.pallas_call(
        _inproj_kernel,
        grid=(ncol, N // tm),
        in_specs=[
            pl.BlockSpec((tm, D), lambda j, i: (i, 0)),
            pl.BlockSpec((1, D), lambda j, i: (0, 0)),
            pl.BlockSpec((1, 1, D), lambda j, i: ((i * tm) // T, 0, 0)),
            pl.BlockSpec((1, 1, D), lambda j, i: ((i * tm) // T, 0, 0)),
            pl.BlockSpec((D, tn), lambda j, i: (0, j)),
        ],
        out_specs=pl.BlockSpec((tm, tn), lambda j, i: (i, j)),
        out_shape=jax.ShapeDtypeStruct((N, PROJ_W), BF16),
        compiler_params=_params(("arbitrary", "arbitrary")),
        name="in_proj",
    )(xf, g.reshape(1, D), sc.reshape(B, 1, D), sh.reshape(B, 1, D), wp)


def _pack_w_in(w):
    o = np.cumsum([0, 512, 512, 1024, GLA_RANK, 1024, 1024, 128, 128, 1024, 1024])
    q, k, v, glr, gr, sq, sk, sv, ga, gb = [w[:, o[i]:o[i + 1]] for i in range(10)]
    pad = jnp.zeros((w.shape[0], LANES - GLA_RANK), w.dtype)
    return jnp.concatenate([q, k, v, gr, sq, ga, gb, sk, sv, glr, pad], axis=1).astype(BF16)


def _gla_kernel(q_ref, k_ref, v_ref, r_ref, glr_ref, w2_ref, b2_ref, ng_ref, o_ref, st_ref, *, nchunk):
    @pl.when(pl.program_id(2) == 0)
    def _():
        st_ref[...] = jnp.zeros_like(st_ref)

    C = GLA_CHUNK
    row = lax.broadcasted_iota(I32, (C, C), 0)
    col = lax.broadcasted_iota(I32, (C, C), 1)
    tri = col <= row
    tri_f = tri.astype(F32)
    w2 = w2_ref[...]
    b2 = b2_ref[...]
    ng = ng_ref[...]
    for ci in range(nchunk):
        sl = pl.ds(ci * C, C)
        q = q_ref[sl, :].astype(F32) * (GLA_DK ** -0.5)
        k = k_ref[sl, :].astype(F32)
        v = v_ref[sl, :]
        z = jnp.dot(glr_ref[sl, :], w2, preferred_element_type=F32) + b2
        log_a = (jnp.minimum(z, 0.0) - jnp.log(1.0 + jnp.exp(-jnp.abs(z)))) * (1.0 / GLA_TAU)
        b = jnp.dot(tri_f, log_a, preferred_element_type=F32, precision=HIGHEST)
        b_last = b[C - 1:C, :]
        q_dec = (q * jnp.exp(b)).astype(BF16)
        k_inv = (k * jnp.exp(-b)).astype(BF16)
        attn = lax.dot_general(q_dec, k_inv, NT_DIMS, preferred_element_type=F32)
        attn = jnp.where(tri, attn, 0.0).astype(BF16)
        st = st_ref[...]
        o = (jnp.dot(attn, v, preferred_element_type=F32)
             + lax.dot_general(q_dec, st.astype(BF16), NT_DIMS, preferred_element_type=F32))
        k_tail = (k * jnp.exp(b_last - b)).astype(BF16)
        st_ref[...] = st * jnp.exp(b_last) + lax.dot_general(v, k_tail, TN_DIMS, preferred_element_type=F32)
        on = o * lax.rsqrt(jnp.mean(o * o, axis=-1, keepdims=True) + EPS) * ng
        r = r_ref[sl, :].astype(F32)
        o_ref[sl, :] = (on * (r * jax.nn.sigmoid(r))).astype(BF16)


def _gla(proj, w2p, b2, ng, tc=256):
    B, T, _ = proj.shape
    H = GLA_HEADS
    return pl.pallas_call(
        functools.partial(_gla_kernel, nchunk=tc // GLA_CHUNK),
        grid=(B, H, T // tc),
        in_specs=[
            pl.BlockSpec((None, tc, GLA_DK), lambda b, h, c: (b, c, COL_Q // GLA_DK + h)),
            pl.BlockSpec((None, tc, GLA_DK), lambda b, h, c: (b, c, COL_K // GLA_DK + h)),
            pl.BlockSpec((None, tc, GLA_DV), lambda b, h, c: (b, c, COL_V // GLA_DV + h)),
            pl.BlockSpec((None, tc, GLA_DV), lambda b, h, c: (b, c, COL_GR // GLA_DV + h)),
            pl.BlockSpec((None, tc, LANES), lambda b, h, c: (b, c, COL_GLR // LANES)),
            pl.BlockSpec((LANES, GLA_DK), lambda b, h, c: (0, h)),
            pl.BlockSpec((1, GLA_DK), lambda b, h, c: (0, h)),
            pl.BlockSpec((1, GLA_DV), lambda b, h, c: (0, 0)),
        ],
        out_specs=pl.BlockSpec((None, tc, GLA_DV), lambda b, h, c: (b, c, h)),
        out_shape=jax.ShapeDtypeStruct((B, T, H * GLA_DV), BF16),
        scratch_shapes=[pltpu.VMEM((GLA_DV, GLA_DK), F32)],
        compiler_params=_params(("arbitrary", "arbitrary", "arbitrary")),
        name="gla",
    )(proj, proj, proj, proj, proj, w2p, b2, ng)


def _t5_bucket(dist):
    max_exact = N_BUCKETS // 2
    d = np.maximum(dist, 1).astype(np.float32)
    large = max_exact + (np.log(d / max_exact) / np.log(MAX_DISTANCE / max_exact)
                         * (N_BUCKETS - max_exact)).astype(np.int32)
    large = np.minimum(large, N_BUCKETS - 1)
    return np.where(dist < max_exact, dist, large).astype(np.int32)


NEG_BIG = -1e30


def _swa_bias(rel_bias):
    blk = SWA_BLOCK
    qi = np.arange(blk)[:, None]
    sj = np.arange(2 * blk)[None, :]
    dist = blk + qi - sj
    band = (dist >= 0) & (dist < blk)
    bucket = _t5_bucket(np.clip(dist, 0, None))
    bias = rel_bias[bucket].astype(F32).transpose(2, 0, 1)
    return jnp.where(jnp.asarray(band)[None], bias, NEG_BIG)


def _swa_kernel(sink_ref, q_ref, kp_ref, kc_ref, vp_ref, vc_ref, bias_ref, qg_ref, kg_ref, o_ref):
    blk = SWA_BLOCK
    hd = SWA_HD
    group = SWA_HEADS // SWA_KV_HEADS
    first = pl.program_id(1) == 0
    k2 = jnp.concatenate([kp_ref[...], kc_ref[...]], axis=0).astype(F32)
    v2 = jnp.concatenate([vp_ref[...], vc_ref[...]], axis=0)
    col = lax.broadcasted_iota(I32, (blk, 2 * blk), 1)
    kill = col < jnp.where(first, blk, 0)
    qg = qg_ref[...]
    kg = kg_ref[...]
    outs = []
    for kh in range(SWA_KV_HEADS):
        kk = k2[:, kh * hd:(kh + 1) * hd]
        kk = (kk * lax.rsqrt(jnp.mean(kk * kk, axis=-1, keepdims=True) + EPS) * kg).astype(BF16)
        vv = v2[:, kh * hd:(kh + 1) * hd]
        for g in range(group):
            h = kh * group + g
            qh = q_ref[:, h * hd:(h + 1) * hd].astype(F32)
            qh = qh * lax.rsqrt(jnp.mean(qh * qh, axis=-1, keepdims=True) + EPS) * qg * (hd ** -0.5)
            logits = lax.dot_general(qh.astype(BF16), kk, NT_DIMS, preferred_element_type=F32) + bias_ref[h]
            logits = jnp.where(kill, NEG_BIG, logits)
            sink = sink_ref[h]
            m = jnp.maximum(jnp.max(logits, axis=-1, keepdims=True), sink)
            p = jnp.exp(logits - m)
            denom = jnp.sum(p, axis=-1, keepdims=True) + jnp.exp(sink - m)
            outs.append(jnp.dot((p / denom).astype(BF16), vv, preferred_element_type=F32))
    o_ref[...] = jnp.concatenate(outs, axis=-1).astype(BF16)


def _swa(proj, bias, qg, kg, sinks):
    B, T, _ = proj.shape
    blk = SWA_BLOCK
    qw = SWA_HEADS * SWA_HD
    kvw = SWA_KV_HEADS * SWA_HD
    prev = lambda b, i: jnp.maximum(i - 1, 0)
    return pl.pallas_call(
        _swa_kernel,
        grid=(B, T // blk),
        in_specs=[
            pl.BlockSpec(memory_space=pltpu.SMEM),
            pl.BlockSpec((None, blk, qw), lambda b, i: (b, i, COL_SQ // qw)),
            pl.BlockSpec((None, blk, kvw), lambda b, i: (b, prev(b, i), COL_SK // kvw)),
            pl.BlockSpec((None, blk, kvw), lambda b, i: (b, i, COL_SK // kvw)),
            pl.BlockSpec((None, blk, kvw), lambda b, i: (b, prev(b, i), COL_SV // kvw)),
            pl.BlockSpec((None, blk, kvw), lambda b, i: (b, i, COL_SV // kvw)),
            pl.BlockSpec((SWA_HEADS, blk, 2 * blk), lambda b, i: (0, 0, 0)),
            pl.BlockSpec((1, SWA_HD), lambda b, i: (0, 0)),
            pl.BlockSpec((1, SWA_HD), lambda b, i: (0, 0)),
        ],
        out_specs=pl.BlockSpec((None, blk, qw), lambda b, i: (b, i, 0)),
        out_shape=jax.ShapeDtypeStruct((B, T, qw), BF16),
        compiler_params=_params(("arbitrary", "arbitrary")),
        name="swa",
    )(sinks, proj, proj, proj, proj, proj, bias, qg, kg)


def _merge_kernel(x_ref, ga_ref, gb_ref, go_ref, so_ref, wa_ref, wb_ref, wo_ref, g1_ref,
                  n2_ref, sc2_ref, sh2_ref, we_ref, x1_ref, h2_ref, st_ref):
    ya = jnp.dot(go_ref[...], wa_ref[...], preferred_element_type=F32)
    yb = jnp.dot(so_ref[...], wb_ref[...], preferred_element_type=F32)
    m = jax.nn.sigmoid(ga_ref[...].astype(F32)) * ya + jax.nn.sigmoid(gb_ref[...].astype(F32)) * yb
    mixed = jnp.dot(m.astype(BF16), wo_ref[...], preferred_element_type=F32)
    x1 = x_ref[...] + g1_ref[0] * mixed
    x1_ref[...] = x1
    y = x1 * lax.rsqrt(jnp.mean(x1 * x1, axis=-1, keepdims=True) + EPS) * n2_ref[...]
    h2 = y * (1.0 + sc2_ref[0]) + sh2_ref[0]
    h2_ref[...] = h2
    st_ref[...] = lax.dot_general(we_ref[...], h2.astype(BF16), NT_DIMS, preferred_element_type=F32)


def _merge(xf, proj2, gla_o, swa_o, wa, wb, wo, g1, n2, sc2, sh2, weff_t, T, tm=256):
    N, D = xf.shape
    B = g1.shape[0]
    SW = weff_t.shape[0]
    bat = lambda i: ((i * tm) // T, 0, 0)
    full = lambda i: (0, 0)
    return pl.pallas_call(
        _merge_kernel,
        grid=(N // tm,),
        in_specs=[
            pl.BlockSpec((tm, D), lambda i: (i, 0)),
            pl.BlockSpec((tm, D), lambda i: (i, COL_GA // D)),
            pl.BlockSpec((tm, D), lambda i: (i, COL_GB // D)),
            pl.BlockSpec((tm, D), lambda i: (i, 0)),
            pl.BlockSpec((tm, D), lambda i: (i, 0)),
            pl.BlockSpec((D, D), full),
            pl.BlockSpec((D, D), full),
            pl.BlockSpec((D, D), full),
            pl.BlockSpec((1, 1, D), bat),
            pl.BlockSpec((1, D), full),
            pl.BlockSpec((1, 1, D), bat),
            pl.BlockSpec((1, 1, D), bat),
            pl.BlockSpec((SW, D), full),
        ],
        out_specs=[
            pl.BlockSpec((tm, D), lambda i: (i, 0)),
            pl.BlockSpec((tm, D), lambda i: (i, 0)),
            pl.BlockSpec((SW, tm), lambda i: (0, i)),
        ],
        out_shape=[
            jax.ShapeDtypeStruct((N, D), F32),
            jax.ShapeDtypeStruct((N, D), F32),
            jax.ShapeDtypeStruct((SW, N), F32),
        ],
        compiler_params=_params(("arbitrary",), VMEM_LIMIT_BYTES),
        name="merge",
    )(xf, proj2, proj2, gla_o, swa_o, wa, wb, wo, g1.reshape(B, 1, D), n2.reshape(1, D),
      sc2.reshape(B, 1, D), sh2.reshape(B, 1, D), weff_t)


def _tree(op, xs):
    xs = list(xs)
    while len(xs) > 1:
        xs = [op(xs[i], xs[i + 1]) for i in range(0, len(xs) - 1, 2)] + ([xs[-1]] if len(xs) % 2 else [])
    return xs[0]


def _extract_best(ref, n):
    m = _tree(jnp.maximum, [ref[i] for i in range(n)])
    pos = _tree(jnp.minimum, [jnp.where(ref[i] == m, i, n) for i in range(n)])
    for i in range(n):
        ref[i] = jnp.where(pos == i, -jnp.inf, ref[i])
    return m, pos


TOPK_PAIRS = tuple((a, b) for a in range(PEER_TOPK) for b in range(PEER_TOPK)
                   if (a + 1) * (b + 1) <= PEER_TOPK)


def _topk_kernel(s_ref, e_ref, g_ref, work_ref, v_ref, i_ref, cand_ref, cidx_ref, best_ref):
    K = PEER_TOPK
    nk = PEER_NKEYS
    H = PEER_HEADS
    work_ref[...] = s_ref[...].reshape(work_ref.shape)

    def stage1(k, _):
        m, pos = _extract_best(work_ref, nk)
        v_ref[k] = m
        i_ref[k] = pos
        return 0

    lax.fori_loop(0, K, stage1, 0)

    for ci, (a, b) in enumerate(TOPK_PAIRS):
        cand_ref[ci] = v_ref[a, 0:H, :] + v_ref[b, H:2 * H, :]
        cidx_ref[ci] = (i_ref[a, 0:H, :] * nk + i_ref[b, H:2 * H, :]) * WORDS_PER_EXPERT
    ncand = len(TOPK_PAIRS)

    def stage2(k, _):
        m, pos = _extract_best(cand_ref, ncand)
        row = _tree(jnp.maximum, [jnp.where(pos == ci, cidx_ref[ci], -1) for ci in range(ncand)])
        best_ref[k] = m
        e_ref[pl.ds(pl.multiple_of(k * H, H), H), :] = row
        return 0

    lax.fori_loop(0, K, stage2, 0)
    best = best_ref[...]
    ex = jnp.exp(best - best[0:1])
    g_ref[...] = (ex / jnp.sum(ex, axis=0, keepdims=True)).reshape(g_ref.shape)


def _topk(s_t, tm=128):
    SW, N = s_t.shape
    K = PEER_TOPK
    G = 2 * PEER_HEADS
    return pl.pallas_call(
        _topk_kernel,
        grid=(N // tm,),
        in_specs=[pl.BlockSpec((SW, tm), lambda i: (0, i))],
        out_specs=[
            pl.BlockSpec((PEER_SLOTS, tm), lambda i: (0, i)),
            pl.BlockSpec((PEER_SLOTS, tm), lambda i: (0, i)),
        ],
        out_shape=[
            jax.ShapeDtypeStruct((PEER_SLOTS, N), I32),
            jax.ShapeDtypeStruct((PEER_SLOTS, N), F32),
        ],
        scratch_shapes=[pltpu.VMEM((PEER_NKEYS, G, tm), F32),
                        pltpu.VMEM((K, G, tm), F32),
                        pltpu.VMEM((K, G, tm), I32),
                        pltpu.VMEM((len(TOPK_PAIRS), PEER_HEADS, tm), F32),
                        pltpu.VMEM((len(TOPK_PAIRS), PEER_HEADS, tm), I32),
                        pltpu.VMEM((K, PEER_HEADS, tm), F32)],
        compiler_params=_params(("arbitrary",)),
        name="peer_topk",
    )(s_t)


WORDS_PER_EXPERT = SUBLANES // 2


def _table(t):
    n, d = t.shape
    rows = d // LANES
    tb = t.astype(BF16).reshape(n, rows // 2, 2, LANES).transpose(0, 1, 3, 2)
    return lax.bitcast_convert_type(tb, I32).reshape(n * (rows // 2), LANES)


def _expert_tile(tab_ref, e4):
    words = tab_ref[pl.ds(pl.multiple_of(e4, WORDS_PER_EXPERT), WORDS_PER_EXPERT), :]
    return pltpu.bitcast(words, BF16).astype(F32)


BITREV = (0, 4, 2, 6, 1, 5, 3, 7)


def _fold8(ps, sub):
    m4 = sub < 4
    m2 = (sub & 3) < 2
    m1 = (sub & 1) < 1
    s1 = []
    for a, b in zip(ps[0::2], ps[1::2]):
        s1.append(jnp.where(m4, a, b) + pltpu.roll(jnp.where(m4, b, a), 4, 0))
    s2 = []
    for a, b in zip(s1[0::2], s1[1::2]):
        s2.append(jnp.where(m2, a, pltpu.roll(b, 2, 0)) + jnp.where(m2, pltpu.roll(a, 6, 0), b))
    a, b = s2
    return jnp.where(m1, a, pltpu.roll(b, 1, 0)) + jnp.where(m1, pltpu.roll(a, 7, 0), b)


def _gelu_tanh(x):
    return 0.5 * x * (1.0 + jnp.tanh(np.sqrt(2.0 / np.pi) * (x + 0.044715 * (x * x * x))))


def _row_sums(x):
    ones = jnp.ones((LANES, LANES), BF16)
    hi = x.astype(BF16)
    lo = (x - hi.astype(F32)).astype(BF16)
    return (jnp.dot(hi, ones, preferred_element_type=F32)
            + jnp.dot(lo, ones, preferred_element_type=F32))


def _peer_u_kernel(e_ref, h_ref, g8_ref, tab_ref, o_ref, slab_ref, *, tb):
    sub = lax.broadcasted_iota(I32, (SUBLANES, LANES), 0)
    lane = lax.broadcasted_iota(I32, (SUBLANES, LANES), 1)
    groups_per_token = PEER_SLOTS // SUBLANES

    def token(t, _):
        h = h_ref[t]
        for j in range(groups_per_token):
            ps = [_expert_tile(tab_ref, e_ref[t, j * SUBLANES + BITREV[qn]]) * h
                  for qn in range(SUBLANES)]
            dst = pl.multiple_of(t * PEER_SLOTS + j * SUBLANES, SUBLANES)
            slab_ref[pl.ds(dst, SUBLANES), :] = _fold8(ps, sub)
        return 0

    lax.fori_loop(0, tb, token, 0)
    rows = SUBLANES * PEER_SLOTS
    for gi in range(tb // SUBLANES):
        rs = _row_sums(slab_ref[gi * rows:(gi + 1) * rows, :])
        out = jnp.zeros((SUBLANES, LANES), F32)
        for tt in range(SUBLANES):
            for j in range(groups_per_token):
                r0 = tt * PEER_SLOTS + j * SUBLANES
                out = jnp.where(lane == tt * groups_per_token + j, rs[r0:r0 + SUBLANES, :], out)
        o_ref[gi] = g8_ref[gi] * _gelu_tanh(out)


def _peer_u(e4, h3, g8, tab, tb=64):
    N = h3.shape[0]
    return pl.pallas_call(
        functools.partial(_peer_u_kernel, tb=tb),
        grid=(N // tb,),
        in_specs=[
            pl.BlockSpec((tb, PEER_SLOTS), lambda i: (i, 0), memory_space=pltpu.SMEM),
            pl.BlockSpec((tb, SUBLANES, LANES), lambda i: (i, 0, 0)),
            pl.BlockSpec((tb // SUBLANES, SUBLANES, LANES), lambda i: (i, 0, 0)),
            pl.BlockSpec(tab.shape, lambda i: (0, 0), pipeline_mode=pl.Buffered(1)),
        ],
        out_specs=pl.BlockSpec((tb // SUBLANES, SUBLANES, LANES), lambda i: (i, 0, 0)),
        out_shape=jax.ShapeDtypeStruct((N // SUBLANES, SUBLANES, LANES), F32),
        scratch_shapes=[pltpu.VMEM((tb * PEER_SLOTS, LANES), F32)],
        compiler_params=_params(("arbitrary",), VMEM_LIMIT_BYTES),
        name="peer_u",
    )(e4, h3, g8, tab)


def _peer_v_kernel(e_ref, c_ref, x1_ref, g2_ref, tab_ref, o_ref, splat_ref, *, tb):
    nacc = 4
    rio = lax.broadcasted_iota(I32, (LANES, LANES), 0)
    lio = lax.broadcasted_iota(I32, (LANES, LANES), 1)
    diag = (rio == lio)[None]
    for t8 in range(tb // SUBLANES):
        cc = c_ref[t8 * SUBLANES:(t8 + 1) * SUBLANES, :]
        d = jnp.where(diag, cc[:, None, :], 0.0).reshape(SUBLANES * LANES, LANES)
        splat_ref[t8 * SUBLANES:(t8 + 1) * SUBLANES] = _row_sums(d).reshape(SUBLANES, LANES, LANES)

    def token(t, _):
        accs = [jnp.zeros((SUBLANES, LANES), F32) for _ in range(nacc)]
        for k in range(PEER_SLOTS):
            c = jnp.broadcast_to(splat_ref[t, k:k + 1, :], (SUBLANES, LANES))
            accs[k % nacc] = accs[k % nacc] + c * _expert_tile(tab_ref, e_ref[t, k])
        y = (accs[0] + accs[1]) + (accs[2] + accs[3])
        o_ref[t] = x1_ref[t] + g2_ref[0] * y
        return 0

    lax.fori_loop(0, tb, token, 0)


def _peer_v(e4, coef, x1_3, g2_3, tab, T, tb=64):
    N = x1_3.shape[0]
    return pl.pallas_call(
        functools.partial(_peer_v_kernel, tb=tb),
        grid=(N // tb,),
        in_specs=[
            pl.BlockSpec((tb, PEER_SLOTS), lambda i: (i, 0), memory_space=pltpu.SMEM),
            pl.BlockSpec((tb, PEER_SLOTS), lambda i: (i, 0)),
            pl.BlockSpec((tb, SUBLANES, LANES), lambda i: (i, 0, 0)),
            pl.BlockSpec((1, SUBLANES, LANES), lambda i: ((i * tb) // T, 0, 0)),
            pl.BlockSpec(tab.shape, lambda i: (0, 0), pipeline_mode=pl.Buffered(1)),
        ],
        out_specs=pl.BlockSpec((tb, SUBLANES, LANES), lambda i: (i, 0, 0)),
        out_shape=jax.ShapeDtypeStruct((N, SUBLANES, LANES), F32),
        scratch_shapes=[pltpu.VMEM((tb, LANES, LANES), F32)],
        compiler_params=_params(("arbitrary",), VMEM_LIMIT_BYTES),
        name="peer_v",
    )(e4, coef, x1_3, g2_3, tab)


def _to_reg_layout(a):
    n = a.shape[0]
    return (a.reshape(n // SUBLANES, SUBLANES, PEER_SLOTS // SUBLANES, SUBLANES)
            .transpose(0, 3, 1, 2).reshape(n // SUBLANES, SUBLANES, LANES))


def _from_reg_layout(a):
    g = a.shape[0]
    return (a.reshape(g, SUBLANES, SUBLANES, PEER_SLOTS // SUBLANES)
            .transpose(0, 2, 3, 1).reshape(g * SUBLANES, PEER_SLOTS))


def kernel(x, c, w_ada, b_ada, norm1_g, w_in, gla_gate_w2, gla_gate_b, gla_norm_g, swa_qnorm_g,
           swa_knorm_g, swa_sinks, rel_bias, w_up_a, w_up_b, w_out, norm2_g, peer_wq, peer_subkeys,
           peer_u, peer_v):
    B, T, D = x.shape
    N = B * T
    L = w_ada.shape[0]
    mod = _adaln(c, w_ada, b_ada)
    weff_t = _fold_peer_keys(peer_wq, peer_subkeys)
    weff_t = (weff_t.reshape(L, PEER_HEADS, 2, PEER_NKEYS, D).transpose(0, 3, 2, 1, 4)
              .reshape(L, 2 * PEER_HEADS * PEER_NKEYS, D))
    bias = _swa_bias(rel_bias)
    xf = x.reshape(N, D)
    for l in range(L):
        sh1, sc1, g1, sh2, sc2, g2 = [mod[l, :, i * D:(i + 1) * D] for i in range(6)]
        proj = _in_proj(xf, norm1_g[l], sc1, sh1, _pack_w_in(w_in[l]), T)
        proj3 = proj.reshape(B, T, PROJ_W)
        w2p = jnp.zeros((LANES, GLA_HEADS * GLA_DK), BF16).at[:GLA_RANK].set(gla_gate_w2[l].astype(BF16))
        gla_o = _gla(proj3, w2p, gla_gate_b[l].reshape(1, -1), gla_norm_g[l].reshape(1, -1))
        swa_o = _swa(proj3, bias, swa_qnorm_g[l].reshape(1, -1), swa_knorm_g[l].reshape(1, -1),
                     swa_sinks[l])
        x1, h2, s_t = _merge(xf, proj, gla_o.reshape(N, -1), swa_o.reshape(N, -1),
                             w_up_a[l].astype(BF16), w_up_b[l].astype(BF16), w_out[l].astype(BF16),
                             g1, norm2_g[l], sc2, sh2, weff_t[l], T)
        e_t, g_t = _topk(s_t)
        e4 = e_t.T
        g8 = _to_reg_layout(g_t.T)
        coef8 = _peer_u(e4, h2.reshape(N, SUBLANES, LANES), g8, _table(peer_u[l]))
        g2_3 = g2.reshape(B, SUBLANES, LANES)
        xf = _peer_v(e4, _from_reg_layout(coef8), x1.reshape(N, SUBLANES, LANES), g2_3,
                     _table(peer_v[l]), T).reshape(N, D)
    return xf.reshape(B, T, D)
```

```python
import functools

import numpy as np
import jax
import jax.numpy as jnp
from jax import lax
from jax.experimental import pallas as pl
from jax.experimental.pallas import tpu as pltpu

F32 = jnp.float32
BF16 = jnp.bfloat16
I32 = jnp.int32
HIGHEST = lax.Precision.HIGHEST
EPS = 1e-6

GLA_HEADS = 4
GLA_DK = 128
GLA_DV = 256
GLA_RANK = 16
GLA_TAU = 16.0
GLA_CHUNK = 64
SWA_HEADS = 16
SWA_KV_HEADS = 2
SWA_HD = 64
SWA_BLOCK = 128
N_BUCKETS = 32
MAX_DISTANCE = 128
PEER_HEADS = 8
PEER_NKEYS = 128
PEER_TOPK = 16
PEER_SLOTS = PEER_HEADS * PEER_TOPK

SUBLANES = 8
LANES = 128
VMEM_LIMIT_BYTES = 56 * 1024 * 1024

NT_DIMS = (((1,), (1,)), ((), ()))
TN_DIMS = (((0,), (0,)), ((), ()))

COL_Q, COL_K, COL_V, COL_GR, COL_SQ, COL_GA, COL_GB = 0, 512, 1024, 2048, 3072, 4096, 5120
COL_SK, COL_SV, COL_GLR = 6144, 6272, 6400
PROJ_W = 6528


def _params(sem, vmem=None):
    return pltpu.CompilerParams(dimension_semantics=sem, vmem_limit_bytes=vmem)


def _adaln_kernel(c_ref, w_ref, b_ref, o_ref):
    c = c_ref[...]
    a = c * jax.nn.sigmoid(c)
    o_ref[0] = jnp.dot(a, w_ref[0], preferred_element_type=F32, precision=HIGHEST) + b_ref[0]


def _adaln(c, w_ada, b_ada):
    L, D, W = w_ada.shape
    B = c.shape[0]
    rows = -(-B // SUBLANES) * SUBLANES
    cp = jnp.zeros((rows, D), F32).at[:B].set(c)
    tn = W // 4
    out = pl.pallas_call(
        _adaln_kernel,
        grid=(L, W // tn),
        in_specs=[
            pl.BlockSpec((rows, D), lambda l, j: (0, 0)),
            pl.BlockSpec((1, D, tn), lambda l, j: (l, 0, j)),
            pl.BlockSpec((1, 1, tn), lambda l, j: (l, 0, j)),
        ],
        out_specs=pl.BlockSpec((1, rows, tn), lambda l, j: (l, 0, j)),
        out_shape=jax.ShapeDtypeStruct((L, rows, W), F32),
        compiler_params=_params(("arbitrary", "arbitrary")),
        name="adaln",
    )(cp, w_ada, b_ada.reshape(L, 1, W))
    return out[:, :B]


def _fold_kernel(sk_ref, wq_ref, o_ref):
    o_ref[0] = lax.dot_general(sk_ref[0, 0], wq_ref[0], NT_DIMS, precision=HIGHEST,
                               preferred_element_type=F32).astype(BF16)


def _fold_peer_keys(peer_wq, peer_subkeys):
    L, D, QW = peer_wq.shape
    half = peer_subkeys.shape[-1]
    ngroups = QW // half
    return pl.pallas_call(
        _fold_kernel,
        grid=(L, ngroups),
        in_specs=[
            pl.BlockSpec((1, 1, PEER_NKEYS, half), lambda l, g: (l, g % 2, 0, 0)),
            pl.BlockSpec((1, D, half), lambda l, g: (l, 0, g)),
        ],
        out_specs=pl.BlockSpec((1, PEER_NKEYS, D), lambda l, g: (l, g, 0)),
        out_shape=jax.ShapeDtypeStruct((L, ngroups * PEER_NKEYS, D), BF16),
        compiler_params=_params(("arbitrary", "arbitrary")),
        name="peer_fold",
    )(peer_subkeys, peer_wq)


def _inproj_kernel(x_ref, g_ref, sc_ref, sh_ref, w_ref, o_ref):
    x = x_ref[...]
    ms = jnp.mean(x * x, axis=-1, keepdims=True)
    y = x * lax.rsqrt(ms + EPS) * g_ref[...]
    h = y * (1.0 + sc_ref[0]) + sh_ref[0]
    o_ref[...] = jnp.dot(h.astype(BF16), w_ref[...], preferred_element_type=F32).astype(BF16)


def _in_proj(xf, g, sc, sh, wp, T, tm=512):
    N, D = xf.shape
    B = sc.shape[0]
    ncol = 3
    tn = PROJ_W // ncol
    return pl.pallas_call(
        _inproj_kernel,
        grid=(ncol, N // tm),
        in_specs=[
            pl.BlockSpec((tm, D), lambda j, i: (i, 0)),
            pl.BlockSpec((1, D), lambda j, i: (0, 0)),
            pl.BlockSpec((1, 1, D), lambda j, i: ((i * tm) // T, 0, 0)),
            pl.BlockSpec((1, 1, D), lambda j, i: ((i * tm) // T, 0, 0)),
            pl.BlockSpec((D, tn), lambda j, i: (0, j)),
        ],
        out_specs=pl.BlockSpec((tm, tn), lambda j, i: (i, j)),
        out_shape=jax.ShapeDtypeStruct((N, PROJ_W), BF16),
        compiler_params=_params(("arbitrary", "arbitrary")),
        name="in_proj",
    )(xf, g.reshape(1, D), sc.reshape(B, 1, D), sh.reshape(B, 1, D), wp)


def _pack_w_in(w):
    o = np.cumsum([0, 512, 512, 1024, GLA_RANK, 1024, 1024, 128, 128, 1024, 1024])
    q, k, v, glr, gr, sq, sk, sv, ga, gb = [w[:, o[i]:o[i + 1]] for i in range(10)]
    pad = jnp.zeros((w.shape[0], LANES - GLA_RANK), w.dtype)
    return jnp.concatenate([q, k, v, gr, sq, ga, gb, sk, sv, glr, pad], axis=1).astype(BF16)


def _gla_kernel(q_ref, k_ref, v_ref, r_ref, glr_ref, w2_ref, b2_ref, ng_ref, o_ref, st_ref, *, nchunk):
    @pl.when(pl.program_id(2) == 0)
    def _():
        st_ref[...] = jnp.zeros_like(st_ref)

    C = GLA_CHUNK
    row = lax.broadcasted_iota(I32, (C, C), 0)
    col = lax.broadcasted_iota(I32, (C, C), 1)
    tri = col <= row
    tri_f = tri.astype(F32)
    w2 = w2_ref[...]
    b2 = b2_ref[...]
    ng = ng_ref[...]
    for ci in range(nchunk):
        sl = pl.ds(ci * C, C)
        q = q_ref[sl, :].astype(F32) * (GLA_DK ** -0.5)
        k = k_ref[sl, :].astype(F32)
        v = v_ref[sl, :]
        z = jnp.dot(glr_ref[sl, :], w2, preferred_element_type=F32) + b2
        log_a = (jnp.minimum(z, 0.0) - jnp.log(1.0 + jnp.exp(-jnp.abs(z)))) * (1.0 / GLA_TAU)
        b = jnp.dot(tri_f, log_a, preferred_element_type=F32, precision=HIGHEST)
        b_last = b[C - 1:C, :]
        q_dec = (q * jnp.exp(b)).astype(BF16)
        k_inv = (k * jnp.exp(-b)).astype(BF16)
        attn = lax.dot_general(q_dec, k_inv, NT_DIMS, preferred_element_type=F32)
        attn = jnp.where(tri, attn, 0.0).astype(BF16)
        st = st_ref[...]
        o = (jnp.dot(attn, v, preferred_element_type=F32)
             + lax.dot_general(q_dec, st.astype(BF16), NT_DIMS, preferred_element_type=F32))
        k_tail = (k * jnp.exp(b_last - b)).astype(BF16)
        st_ref[...] = st * jnp.exp(b_last) + lax.dot_general(v, k_tail, TN_DIMS, preferred_element_type=F32)
        on = o * lax.rsqrt(jnp.mean(o * o, axis=-1, keepdims=True) + EPS) * ng
        r = r_ref[sl, :].astype(F32)
        o_ref[sl, :] = (on * (r * jax.nn.sigmoid(r))).astype(BF16)


def _gla(proj, w2p, b2, ng, tc=256):
    B, T, _ = proj.shape
    H = GLA_HEADS
    return pl.pallas_call(
        functools.partial(_gla_kernel, nchunk=tc // GLA_CHUNK),
        grid=(B, H, T // tc),
        in_specs=[
            pl.BlockSpec((None, tc, GLA_DK), lambda b, h, c: (b, c, COL_Q // GLA_DK + h)),
            pl.BlockSpec((None, tc, GLA_DK), lambda b, h, c: (b, c, COL_K // GLA_DK + h)),
            pl.BlockSpec((None, tc, GLA_DV), lambda b, h, c: (b, c, COL_V // GLA_DV + h)),
            pl.BlockSpec((None, tc, GLA_DV), lambda b, h, c: (b, c, COL_GR // GLA_DV + h)),
            pl.BlockSpec((None, tc, LANES), lambda b, h, c: (b, c, COL_GLR // LANES)),
            pl.BlockSpec((LANES, GLA_DK), lambda b, h, c: (0, h)),
            pl.BlockSpec((1, GLA_DK), lambda b, h, c: (0, h)),
            pl.BlockSpec((1, GLA_DV), lambda b, h, c: (0, 0)),
        ],
        out_specs=pl.BlockSpec((None, tc, GLA_DV), lambda b, h, c: (b, c, h)),
        out_shape=jax.ShapeDtypeStruct((B, T, H * GLA_DV), BF16),
        scratch_shapes=[pltpu.VMEM((GLA_DV, GLA_DK), F32)],
        compiler_params=_params(("arbitrary", "arbitrary", "arbitrary")),
        name="gla",
    )(proj, proj, proj, proj, proj, w2p, b2, ng)


def _t5_bucket(dist):
    max_exact = N_BUCKETS // 2
    d = np.maximum(dist, 1).astype(np.float32)
    large = max_exact + (np.log(d / max_exact) / np.log(MAX_DISTANCE / max_exact)
                         * (N_BUCKETS - max_exact)).astype(np.int32)
    large = np.minimum(large, N_BUCKETS - 1)
    return np.where(dist < max_exact, dist, large).astype(np.int32)


NEG_BIG = -1e30


def _swa_bias(rel_bias):
    blk = SWA_BLOCK
    qi = np.arange(blk)[:, None]
    sj = np.arange(2 * blk)[None, :]
    dist = blk + qi - sj
    band = (dist >= 0) & (dist < blk)
    bucket = _t5_bucket(np.clip(dist, 0, None))
    bias = rel_bias[bucket].astype(F32).transpose(2, 0, 1)
    return jnp.where(jnp.asarray(band)[None], bias, NEG_BIG)


def _swa_kernel(sink_ref, q_ref, kp_ref, kc_ref, vp_ref, vc_ref, bias_ref, qg_ref, kg_ref, o_ref):
    blk = SWA_BLOCK
    hd = SWA_HD
    group = SWA_HEADS // SWA_KV_HEADS
    first = pl.program_id(1) == 0
    k2 = jnp.concatenate([kp_ref[...], kc_ref[...]], axis=0).astype(F32)
    v2 = jnp.concatenate([vp_ref[...], vc_ref[...]], axis=0)
    col = lax.broadcasted_iota(I32, (blk, 2 * blk), 1)
    kill = col < jnp.where(first, blk, 0)
    qg = qg_ref[...]
    kg = kg_ref[...]
    outs = []
    for kh in range(SWA_KV_HEADS):
        kk = k2[:, kh * hd:(kh + 1) * hd]
        kk = (kk * lax.rsqrt(jnp.mean(kk * kk, axis=-1, keepdims=True) + EPS) * kg).astype(BF16)
        vv = v2[:, kh * hd:(kh + 1) * hd]
        for g in range(group):
            h = kh * group + g
            qh = q_ref[:, h * hd:(h + 1) * hd].astype(F32)
            qh = qh * lax.rsqrt(jnp.mean(qh * qh, axis=-1, keepdims=True) + EPS) * qg * (hd ** -0.5)
            logits = lax.dot_general(qh.astype(BF16), kk, NT_DIMS, preferred_element_type=F32) + bias_ref[h]
            logits = jnp.where(kill, NEG_BIG, logits)
            sink = sink_ref[h]
            m = jnp.maximum(jnp.max(logits, axis=-1, keepdims=True), sink)
            p = jnp.exp(logits - m)
            denom = jnp.sum(p, axis=-1, keepdims=True) + jnp.exp(sink - m)
            outs.append(jnp.dot((p / denom).astype(BF16), vv, preferred_element_type=F32))
    o_ref[...] = jnp.concatenate(outs, axis=-1).astype(BF16)


def _swa(proj, bias, qg, kg, sinks):
    B, T, _ = proj.shape
    blk = SWA_BLOCK
    qw = SWA_HEADS * SWA_HD
    kvw = SWA_KV_HEADS * SWA_HD
    prev = lambda b, i: jnp.maximum(i - 1, 0)
    return pl.pallas_call(
        _swa_kernel,
        grid=(B, T // blk),
        in_specs=[
            pl.BlockSpec(memory_space=pltpu.SMEM),
            pl.BlockSpec((None, blk, qw), lambda b, i: (b, i, COL_SQ // qw)),
            pl.BlockSpec((None, blk, kvw), lambda b, i: (b, prev(b, i), COL_SK // kvw)),
            pl.BlockSpec((None, blk, kvw), lambda b, i: (b, i, COL_SK // kvw)),
            pl.BlockSpec((None, blk, kvw), lambda b, i: (b, prev(b, i), COL_SV // kvw)),
            pl.BlockSpec((None, blk, kvw), lambda b, i: (b, i, COL_SV // kvw)),
            pl.BlockSpec((SWA_HEADS, blk, 2 * blk), lambda b, i: (0, 0, 0)),
            pl.BlockSpec((1, SWA_HD), lambda b, i: (0, 0)),
            pl.BlockSpec((1, SWA_HD), lambda b, i: (0, 0)),
        ],
        out_specs=pl.BlockSpec((None, blk, qw), lambda b, i: (b, i, 0)),
        out_shape=jax.ShapeDtypeStruct((B, T, qw), BF16),
        compiler_params=_params(("arbitrary", "arbitrary")),
        name="swa",
    )(sinks, proj, proj, proj, proj, proj, bias, qg, kg)


def _merge_kernel(x_ref, ga_ref, gb_ref, go_ref, so_ref, wa_ref, wb_ref, wo_ref, g1_ref,
                  n2_ref, sc2_ref, sh2_ref, we_ref, x1_ref, h2_ref, st_ref):
    ya = jnp.dot(go_ref[...], wa_ref[...], preferred_element_type=F32)
    yb = jnp.dot(so_ref[...], wb_ref[...], preferred_element_type=F32)
    m = jax.nn.sigmoid(ga_ref[...].astype(F32)) * ya + jax.nn.sigmoid(gb_ref[...].astype(F32)) * yb
    mixed = jnp.dot(m.astype(BF16), wo_ref[...], preferred_element_type=F32)
    x1 = x_ref[...] + g1_ref[0] * mixed
    x1_ref[...] = x1
    y = x1 * lax.rsqrt(jnp.mean(x1 * x1, axis=-1, keepdims=True) + EPS) * n2_ref[...]
    h2 = y * (1.0 + sc2_ref[0]) + sh2_ref[0]
    h2_ref[...] = h2
    st_ref[...] = lax.dot_general(we_ref[...], h2.astype(BF16), NT_DIMS, preferred_element_type=F32)


def _merge(xf, proj2, gla_o, swa_o, wa, wb, wo, g1, n2, sc2, sh2, weff_t, T, tm=256):
    N, D = xf.shape
    B = g1.shape[0]
    SW = weff_t.shape[0]
    bat = lambda i: ((i * tm) // T, 0, 0)
    full = lambda i: (0, 0)
    return pl.pallas_call(
        _merge_kernel,
        grid=(N // tm,),
        in_specs=[
            pl.BlockSpec((tm, D), lambda i: (i, 0)),
            pl.BlockSpec((tm, D), lambda i: (i, COL_GA // D)),
            pl.BlockSpec((tm, D), lambda i: (i, COL_GB // D)),
            pl.BlockSpec((tm, D), lambda i: (i, 0)),
            pl.BlockSpec((tm, D), lambda i: (i, 0)),
            pl.BlockSpec((D, D), full),
            pl.BlockSpec((D, D), full),
            pl.BlockSpec((D, D), full),
            pl.BlockSpec((1, 1, D), bat),
            pl.BlockSpec((1, D), full),
            pl.BlockSpec((1, 1, D), bat),
            pl.BlockSpec((1, 1, D), bat),
            pl.BlockSpec((SW, D), full),
        ],
        out_specs=[
            pl.BlockSpec((tm, D), lambda i: (i, 0)),
            pl.BlockSpec((tm, D), lambda i: (i, 0)),
            pl.BlockSpec((SW, tm), lambda i: (0, i)),
        ],
        out_shape=[
            jax.ShapeDtypeStruct((N, D), F32),
            jax.ShapeDtypeStruct((N, D), F32),
            jax.ShapeDtypeStruct((SW, N), F32),
        ],
        compiler_params=_params(("arbitrary",), VMEM_LIMIT_BYTES),
        name="merge",
    )(xf, proj2, proj2, gla_o, swa_o, wa, wb, wo, g1.reshape(B, 1, D), n2.reshape(1, D),
      sc2.reshape(B, 1, D), sh2.reshape(B, 1, D), weff_t)


def _tree(op, xs):
    xs = list(xs)
    while len(xs) > 1:
        xs = [op(xs[i], xs[i + 1]) for i in range(0, len(xs) - 1, 2)] + ([xs[-1]] if len(xs) % 2 else [])
    return xs[0]


def _extract_best(ref, n):
    m = _tree(jnp.maximum, [ref[i] for i in range(n)])
    pos = _tree(jnp.minimum, [jnp.where(ref[i] == m, i, n) for i in range(n)])
    for i in range(n):
        ref[i] = jnp.where(pos == i, -jnp.inf, ref[i])
    return m, pos


TOPK_PAIRS = tuple((a, b) for a in range(PEER_TOPK) for b in range(PEER_TOPK)
                   if (a + 1) * (b + 1) <= PEER_TOPK)


def _topk_kernel(s_ref, e_ref, g_ref, work_ref, v_ref, i_ref, cand_ref, cidx_ref, best_ref, row_ref):
    K = PEER_TOPK
    nk = PEER_NKEYS
    H = PEER_HEADS
    work_ref[...] = s_ref[...].reshape(work_ref.shape)

    def stage1(k, _):
        m, pos = _extract_best(work_ref, nk)
        v_ref[k] = m
        i_ref[k] = pos
        return 0

    lax.fori_loop(0, K, stage1, 0)

    for ci, (a, b) in enumerate(TOPK_PAIRS):
        cand_ref[ci] = v_ref[a, 0:H, :] + v_ref[b, H:2 * H, :]
        cidx_ref[ci] = (i_ref[a, 0:H, :] * nk + i_ref[b, H:2 * H, :]) * WORDS_PER_EXPERT
    ncand = len(TOPK_PAIRS)

    def stage2(k, _):
        m, pos = _extract_best(cand_ref, ncand)
        row = _tree(jnp.maximum, [jnp.where(pos == ci, cidx_ref[ci], -1) for ci in range(ncand)])
        best_ref[k] = m
        row_ref[k] = row
        return 0

    lax.fori_loop(0, K, stage2, 0)
    best = best_ref[...]
    ex = jnp.exp(best - best[0:1])
    gates = ex / jnp.sum(ex, axis=0, keepdims=True)
    tm = s_ref.shape[1]
    g_ref[...] = gates.reshape(K * H, tm).T
    e_ref[...] = row_ref[...].reshape(K * H, tm).astype(F32).T.astype(I32)


def _topk(s_t, tm=128):
    SW, N = s_t.shape
    K = PEER_TOPK
    G = 2 * PEER_HEADS
    return pl.pallas_call(
        _topk_kernel,
        grid=(N // tm,),
        in_specs=[pl.BlockSpec((SW, tm), lambda i: (0, i))],
        out_specs=[
            pl.BlockSpec((tm, PEER_SLOTS), lambda i: (i, 0)),
            pl.BlockSpec((tm, PEER_SLOTS), lambda i: (i, 0)),
        ],
        out_shape=[
            jax.ShapeDtypeStruct((N, PEER_SLOTS), I32),
            jax.ShapeDtypeStruct((N, PEER_SLOTS), F32),
        ],
        scratch_shapes=[pltpu.VMEM((PEER_NKEYS, G, tm), F32),
                        pltpu.VMEM((K, G, tm), F32),
                        pltpu.VMEM((K, G, tm), I32),
                        pltpu.VMEM((len(TOPK_PAIRS), PEER_HEADS, tm), F32),
                        pltpu.VMEM((len(TOPK_PAIRS), PEER_HEADS, tm), I32),
                        pltpu.VMEM((K, PEER_HEADS, tm), F32),
                        pltpu.VMEM((K, PEER_HEADS, tm), I32)],
        compiler_params=_params(("arbitrary",)),
        name="peer_topk",
    )(s_t)


WORDS_PER_EXPERT = SUBLANES // 2


TILE_BLOCK = tuple((r % 2) * WORDS_PER_EXPERT + r // 2 for r in range(SUBLANES))
BLOCK_ROW = tuple(TILE_BLOCK.index(q) for q in range(SUBLANES))


def _bf16_bits(x):
    return lax.bitcast_convert_type(x.astype(BF16).astype(F32), jnp.uint32)


def _table_kernel(t_ref, o_ref):
    rows = t_ref.shape[0]
    for s in range(WORDS_PER_EXPERT):
        lo = _bf16_bits(t_ref[:, s * LANES:(s + 1) * LANES])
        hi = _bf16_bits(t_ref[:, (s + WORDS_PER_EXPERT) * LANES:(s + WORDS_PER_EXPERT + 1) * LANES])
        word = lax.shift_right_logical(lo, jnp.uint32(16)) | (hi & jnp.uint32(0xFFFF0000))
        o_ref[pl.ds(s, rows, stride=WORDS_PER_EXPERT), :] = lax.bitcast_convert_type(word, I32)


def _table(t, te=512):
    n, d = t.shape
    return pl.pallas_call(
        _table_kernel,
        grid=(n // te,),
        in_specs=[pl.BlockSpec((te, d), lambda i: (i, 0))],
        out_specs=pl.BlockSpec((te * WORDS_PER_EXPERT, LANES), lambda i: (i, 0)),
        out_shape=jax.ShapeDtypeStruct((n * WORDS_PER_EXPERT, LANES), I32),
        compiler_params=_params(("arbitrary",)),
        name="peer_table",
    )(t)


def _expert_tile(tab_ref, e4):
    words = tab_ref[pl.ds(pl.multiple_of(e4, WORDS_PER_EXPERT), WORDS_PER_EXPERT), :]
    return pltpu.bitcast(words, BF16).astype(F32)


def _to_tile_rows(x):
    return jnp.concatenate([x[q:q + 1, :] for q in TILE_BLOCK], axis=0)


def _from_tile_rows(x):
    return jnp.concatenate([x[r:r + 1, :] for r in BLOCK_ROW], axis=0)


BITREV = (0, 4, 2, 6, 1, 5, 3, 7)


def _fold8(ps, sub):
    m4 = sub < 4
    m2 = (sub & 3) < 2
    m1 = (sub & 1) < 1
    s1 = []
    for a, b in zip(ps[0::2], ps[1::2]):
        s1.append(jnp.where(m4, a, b) + pltpu.roll(jnp.where(m4, b, a), 4, 0))
    s2 = []
    for a, b in zip(s1[0::2], s1[1::2]):
        s2.append(jnp.where(m2, a, pltpu.roll(b, 2, 0)) + jnp.where(m2, pltpu.roll(a, 6, 0), b))
    a, b = s2
    return jnp.where(m1, a, pltpu.roll(b, 1, 0)) + jnp.where(m1, pltpu.roll(a, 7, 0), b)


def _gelu_tanh(x):
    return 0.5 * x * (1.0 + jnp.tanh(np.sqrt(2.0 / np.pi) * (x + 0.044715 * (x * x * x))))


def _row_sums(x):
    ones = jnp.ones((LANES, LANES), BF16)
    hi = x.astype(BF16)
    lo = (x - hi.astype(F32)).astype(BF16)
    return (jnp.dot(hi, ones, preferred_element_type=F32)
            + jnp.dot(lo, ones, preferred_element_type=F32))


def _peer_u_kernel(e_ref, h_ref, g_ref, tab_ref, o_ref, slab_ref, *, tb):
    sub = lax.broadcasted_iota(I32, (SUBLANES, LANES), 0)
    groups_per_token = PEER_SLOTS // SUBLANES

    def token(t, _):
        h = _to_tile_rows(h_ref[t])
        for j in range(groups_per_token):
            ps = [_expert_tile(tab_ref, e_ref[t, j * SUBLANES + BITREV[qn]]) * h
                  for qn in range(SUBLANES)]
            dst = pl.multiple_of(t * PEER_SLOTS + j * SUBLANES, SUBLANES)
            slab_ref[pl.ds(dst, SUBLANES), :] = _fold8(ps, sub)
        return 0

    lax.fori_loop(0, tb, token, 0)
    rio = lax.broadcasted_iota(I32, (LANES, LANES), 0)
    lio = lax.broadcasted_iota(I32, (LANES, LANES), 1)
    diag = (rio == lio)[None]
    rows = SUBLANES * PEER_SLOTS
    for gi in range(tb // SUBLANES):
        tok = slice(gi * SUBLANES, (gi + 1) * SUBLANES)
        rs = _row_sums(slab_ref[gi * rows:(gi + 1) * rows, :]).reshape(SUBLANES, PEER_SLOTS, LANES)
        a = jnp.sum(jnp.where(diag, rs, 0.0), axis=1)
        o_ref[tok, :] = g_ref[tok, :] * _gelu_tanh(a)


def _peer_u(e4, h3, gates, tab, tb=64):
    N = h3.shape[0]
    return pl.pallas_call(
        functools.partial(_peer_u_kernel, tb=tb),
        grid=(N // tb,),
        in_specs=[
            pl.BlockSpec((tb, PEER_SLOTS), lambda i: (i, 0), memory_space=pltpu.SMEM),
            pl.BlockSpec((tb, SUBLANES, LANES), lambda i: (i, 0, 0)),
            pl.BlockSpec((tb, PEER_SLOTS), lambda i: (i, 0)),
            pl.BlockSpec(tab.shape, lambda i: (0, 0), pipeline_mode=pl.Buffered(1)),
        ],
        out_specs=pl.BlockSpec((tb, PEER_SLOTS), lambda i: (i, 0)),
        out_shape=jax.ShapeDtypeStruct((N, PEER_SLOTS), F32),
        scratch_shapes=[pltpu.VMEM((tb * PEER_SLOTS, LANES), F32)],
        compiler_params=_params(("arbitrary",), VMEM_LIMIT_BYTES),
        name="peer_u",
    )(e4, h3, gates, tab)


def _peer_v_kernel(e_ref, c_ref, x1_ref, g2_ref, tab_ref, o_ref, splat_ref, *, tb):
    nacc = 4
    rio = lax.broadcasted_iota(I32, (LANES, LANES), 0)
    lio = lax.broadcasted_iota(I32, (LANES, LANES), 1)
    diag = (rio == lio)[None]
    for t8 in range(tb // SUBLANES):
        cc = c_ref[t8 * SUBLANES:(t8 + 1) * SUBLANES, :]
        d = jnp.where(diag, cc[:, None, :], 0.0).reshape(SUBLANES * LANES, LANES)
        splat_ref[t8 * SUBLANES:(t8 + 1) * SUBLANES] = _row_sums(d).reshape(SUBLANES, LANES, LANES)

    def token(t, _):
        accs = [jnp.zeros((SUBLANES, LANES), F32) for _ in range(nacc)]
        for k in range(PEER_SLOTS):
            c = jnp.broadcast_to(splat_ref[t, k:k + 1, :], (SUBLANES, LANES))
            accs[k % nacc] = accs[k % nacc] + c * _expert_tile(tab_ref, e_ref[t, k])
        y = (accs[0] + accs[1]) + (accs[2] + accs[3])
        o_ref[t] = x1_ref[t] + g2_ref[0] * _from_tile_rows(y)
        return 0

    lax.fori_loop(0, tb, token, 0)


def _peer_v(e4, coef, x1_3, g2_3, tab, T, tb=64):
    N = x1_3.shape[0]
    return pl.pallas_call(
        functools.partial(_peer_v_kernel, tb=tb),
        grid=(N // tb,),
        in_specs=[
            pl.BlockSpec((tb, PEER_SLOTS), lambda i: (i, 0), memory_space=pltpu.SMEM),
            pl.BlockSpec((tb, PEER_SLOTS), lambda i: (i, 0)),
            pl.BlockSpec((tb, SUBLANES, LANES), lambda i: (i, 0, 0)),
            pl.BlockSpec((1, SUBLANES, LANES), lambda i: ((i * tb) // T, 0, 0)),
            pl.BlockSpec(tab.shape, lambda i: (0, 0), pipeline_mode=pl.Buffered(1)),
        ],
        out_specs=pl.BlockSpec((tb, SUBLANES, LANES), lambda i: (i, 0, 0)),
        out_shape=jax.ShapeDtypeStruct((N, SUBLANES, LANES), F32),
        scratch_shapes=[pltpu.VMEM((tb, LANES, LANES), F32)],
        compiler_params=_params(("arbitrary",), VMEM_LIMIT_BYTES),
        name="peer_v",
    )(e4, coef, x1_3, g2_3, tab)


def kernel(x, c, w_ada, b_ada, norm1_g, w_in, gla_gate_w2, gla_gate_b, gla_norm_g, swa_qnorm_g,
           swa_knorm_g, swa_sinks, rel_bias, w_up_a, w_up_b, w_out, norm2_g, peer_wq, peer_subkeys,
           peer_u, peer_v):
    B, T, D = x.shape
    N = B * T
    L = w_ada.shape[0]
    mod = _adaln(c, w_ada, b_ada)
    weff_t = _fold_peer_keys(peer_wq, peer_subkeys)
    weff_t = (weff_t.reshape(L, PEER_HEADS, 2, PEER_NKEYS, D).transpose(0, 3, 2, 1, 4)
              .reshape(L, 2 * PEER_HEADS * PEER_NKEYS, D))
    bias = _swa_bias(rel_bias)
    xf = x.reshape(N, D)
    for l in range(L):
        sh1, sc1, g1, sh2, sc2, g2 = [mod[l, :, i * D:(i + 1) * D] for i in range(6)]
        proj = _in_proj(xf, norm1_g[l], sc1, sh1, _pack_w_in(w_in[l]), T)
        proj3 = proj.reshape(B, T, PROJ_W)
        w2p = jnp.zeros((LANES, GLA_HEADS * GLA_DK), BF16).at[:GLA_RANK].set(gla_gate_w2[l].astype(BF16))
        gla_o = _gla(proj3, w2p, gla_gate_b[l].reshape(1, -1), gla_norm_g[l].reshape(1, -1))
        swa_o = _swa(proj3, bias, swa_qnorm_g[l].reshape(1, -1), swa_knorm_g[l].reshape(1, -1),
                     swa_sinks[l])
        x1, h2, s_t = _merge(xf, proj, gla_o.reshape(N, -1), swa_o.reshape(N, -1),
                             w_up_a[l].astype(BF16), w_up_b[l].astype(BF16), w_out[l].astype(BF16),
                             g1, norm2_g[l], sc2, sh2, weff_t[l], T)
        e4, gates = _topk(s_t)
        coef = _peer_u(e4, h2.reshape(N, SUBLANES, LANES), gates, _table(peer_u[l]))
        g2_3 = g2.reshape(B, SUBLANES, LANES)
        xf = _peer_v(e4, coef, x1.reshape(N, SUBLANES, LANES), g2_3, _table(peer_v[l]), T).reshape(N, D)
    return xf.reshape(B, T, D)
```

```python
import functools

import numpy as np
import jax
import jax.numpy as jnp
from jax import lax
from jax.experimental import pallas as pl
from jax.experimental.pallas import tpu as pltpu

F32 = jnp.float32
BF16 = jnp.bfloat16
I32 = jnp.int32
HIGHEST = lax.Precision.HIGHEST
EPS = 1e-6

GLA_HEADS = 4
GLA_DK = 128
GLA_DV = 256
GLA_RANK = 16
GLA_TAU = 16.0
GLA_CHUNK = 64
SWA_HEADS = 16
SWA_KV_HEADS = 2
SWA_HD = 64
SWA_BLOCK = 128
N_BUCKETS = 32
MAX_DISTANCE = 128
PEER_HEADS = 8
PEER_NKEYS = 128
PEER_TOPK = 16
PEER_SLOTS = PEER_HEADS * PEER_TOPK

SUBLANES = 8
LANES = 128
VMEM_LIMIT_BYTES = 56 * 1024 * 1024

NT_DIMS = (((1,), (1,)), ((), ()))
TN_DIMS = (((0,), (0,)), ((), ()))

COL_Q, COL_K, COL_V, COL_GR, COL_SQ, COL_GA, COL_GB = 0, 512, 1024, 2048, 3072, 4096, 5120
COL_SK, COL_SV, COL_GLR = 6144, 6272, 6400
PROJ_W = 6528


def _params(sem, vmem=None):
    return pltpu.CompilerParams(dimension_semantics=sem, vmem_limit_bytes=vmem)


def _adaln_kernel(c_ref, w_ref, b_ref, o_ref):
    c = c_ref[...]
    a = c * jax.nn.sigmoid(c)
    o_ref[0] = jnp.dot(a, w_ref[0], preferred_element_type=F32, precision=HIGHEST) + b_ref[0]


def _adaln(c, w_ada, b_ada):
    L, D, W = w_ada.shape
    B = c.shape[0]
    rows = -(-B // SUBLANES) * SUBLANES
    cp = jnp.zeros((rows, D), F32).at[:B].set(c)
    tn = W // 4
    out = pl.pallas_call(
        _adaln_kernel,
        grid=(L, W // tn),
        in_specs=[
            pl.BlockSpec((rows, D), lambda l, j: (0, 0)),
            pl.BlockSpec((1, D, tn), lambda l, j: (l, 0, j)),
            pl.BlockSpec((1, 1, tn), lambda l, j: (l, 0, j)),
        ],
        out_specs=pl.BlockSpec((1, rows, tn), lambda l, j: (l, 0, j)),
        out_shape=jax.ShapeDtypeStruct((L, rows, W), F32),
        compiler_params=_params(("arbitrary", "arbitrary")),
        name="adaln",
    )(cp, w_ada, b_ada.reshape(L, 1, W))
    return out[:, :B]


def _fold_kernel(sk_ref, wq_ref, o_ref):
    o_ref[0] = lax.dot_general(sk_ref[0, 0], wq_ref[0], NT_DIMS, precision=HIGHEST,
                               preferred_element_type=F32).astype(BF16)


def _fold_peer_keys(peer_wq, peer_subkeys):
    L, D, QW = peer_wq.shape
    half = peer_subkeys.shape[-1]
    ngroups = QW // half
    return pl.pallas_call(
        _fold_kernel,
        grid=(L, ngroups),
        in_specs=[
            pl.BlockSpec((1, 1, PEER_NKEYS, half), lambda l, g: (l, g % 2, 0, 0)),
            pl.BlockSpec((1, D, half), lambda l, g: (l, 0, g)),
        ],
        out_specs=pl.BlockSpec((1, PEER_NKEYS, D), lambda l, g: (l, g, 0)),
        out_shape=jax.ShapeDtypeStruct((L, ngroups * PEER_NKEYS, D), BF16),
        compiler_params=_params(("arbitrary", "arbitrary")),
        name="peer_fold",
    )(peer_subkeys, peer_wq)


def _inproj_kernel(x_ref, g_ref, sc_ref, sh_ref, w_ref, o_ref):
    x = x_ref[...]
    ms = jnp.mean(x * x, axis=-1, keepdims=True)
    y = x * lax.rsqrt(ms + EPS) * g_ref[...]
    h = y * (1.0 + sc_ref[0]) + sh_ref[0]
    o_ref[...] = jnp.dot(h.astype(BF16), w_ref[...], preferred_element_type=F32).astype(BF16)


def _in_proj(xf, g, sc, sh, wp, T, tm=512):
    N, D = xf.shape
    B = sc.shape[0]
    ncol = 3
    tn = PROJ_W // ncol
    return pl.pallas_call(
        _inproj_kernel,
        grid=(ncol, N // tm),
        in_specs=[
            pl.BlockSpec((tm, D), lambda j, i: (i, 0)),
            pl.BlockSpec((1, D), lambda j, i: (0, 0)),
            pl.BlockSpec((1, 1, D), lambda j, i: ((i * tm) // T, 0, 0)),
            pl.BlockSpec((1, 1, D), lambda j, i: ((i * tm) // T, 0, 0)),
            pl.BlockSpec((D, tn), lambda j, i: (0, j)),
        ],
        out_specs=pl.BlockSpec((tm, tn), lambda j, i: (i, j)),
        out_shape=jax.ShapeDtypeStruct((N, PROJ_W), BF16),
        compiler_params=_params(("arbitrary", "arbitrary")),
        name="in_proj",
    )(xf, g.reshape(1, D), sc.reshape(B, 1, D), sh.reshape(B, 1, D), wp)


def _pack_w_in(w):
    o = np.cumsum([0, 512, 512, 1024, GLA_RANK, 1024, 1024, 128, 128, 1024, 1024])
    q, k, v, glr, gr, sq, sk, sv, ga, gb = [w[:, o[i]:o[i + 1]] for i in range(10)]
    pad = jnp.zeros((w.shape[0], LANES - GLA_RANK), w.dtype)
    return jnp.concatenate([q, k, v, gr, sq, ga, gb, sk, sv, glr, pad], axis=1).astype(BF16)


def _gla_kernel(q_ref, k_ref, v_ref, r_ref, glr_ref, w2_ref, b2_ref, ng_ref, o_ref, st_ref, *, nchunk):
    @pl.when(pl.program_id(1) == 0)
    def _():
        st_ref[...] = jnp.zeros_like(st_ref)

    C = GLA_CHUNK
    dk, dv = GLA_DK, GLA_DV
    row = lax.broadcasted_iota(I32, (C, C), 0)
    col = lax.broadcasted_iota(I32, (C, C), 1)
    tri = col <= row
    tri_f = tri.astype(F32)
    w2 = w2_ref[...]
    b2 = b2_ref[...]
    ng = ng_ref[...]
    for ci in range(nchunk):
        sl = pl.ds(ci * C, C)
        z = jnp.dot(glr_ref[sl, :], w2, preferred_element_type=F32) + b2
        log_a = (jnp.minimum(z, 0.0) - jnp.log(1.0 + jnp.exp(-jnp.abs(z)))) * (1.0 / GLA_TAU)
        b = jnp.dot(tri_f, log_a, preferred_element_type=F32, precision=HIGHEST)
        b_last = b[C - 1:C, :]
        q = q_ref[sl, :].astype(F32) * (dk ** -0.5)
        k = k_ref[sl, :].astype(F32)
        q_dec = (q * jnp.exp(b)).astype(BF16)
        k_inv = (k * jnp.exp(-b)).astype(BF16)
        k_tail = (k * jnp.exp(b_last - b)).astype(BF16)
        decay = jnp.exp(b_last)
        for h in range(GLA_HEADS):
            kc = slice(h * dk, (h + 1) * dk)
            vc = slice(h * dv, (h + 1) * dv)
            v = v_ref[sl, vc]
            attn = lax.dot_general(q_dec[:, kc], k_inv[:, kc], NT_DIMS, preferred_element_type=F32)
            attn = jnp.where(tri, attn, 0.0).astype(BF16)
            st = st_ref[h]
            o = (jnp.dot(attn, v, preferred_element_type=F32)
                 + lax.dot_general(q_dec[:, kc], st.astype(BF16), NT_DIMS, preferred_element_type=F32))
            st_ref[h] = st * decay[:, kc] + lax.dot_general(v, k_tail[:, kc], TN_DIMS,
                                                            preferred_element_type=F32)
            on = o * lax.rsqrt(jnp.mean(o * o, axis=-1, keepdims=True) + EPS) * ng
            r = r_ref[sl, vc].astype(F32)
            o_ref[sl, vc] = (on * (r * jax.nn.sigmoid(r))).astype(BF16)


def _gla(proj, w2p, b2, ng, tc=256):
    B, T, _ = proj.shape
    H = GLA_HEADS
    kw, vw = H * GLA_DK, H * GLA_DV
    return pl.pallas_call(
        functools.partial(_gla_kernel, nchunk=tc // GLA_CHUNK),
        grid=(B, T // tc),
        in_specs=[
            pl.BlockSpec((None, tc, kw), lambda b, c: (b, c, COL_Q // kw)),
            pl.BlockSpec((None, tc, kw), lambda b, c: (b, c, COL_K // kw)),
            pl.BlockSpec((None, tc, vw), lambda b, c: (b, c, COL_V // vw)),
            pl.BlockSpec((None, tc, vw), lambda b, c: (b, c, COL_GR // vw)),
            pl.BlockSpec((None, tc, LANES), lambda b, c: (b, c, COL_GLR // LANES)),
            pl.BlockSpec((LANES, kw), lambda b, c: (0, 0)),
            pl.BlockSpec((1, kw), lambda b, c: (0, 0)),
            pl.BlockSpec((1, GLA_DV), lambda b, c: (0, 0)),
        ],
        out_specs=pl.BlockSpec((None, tc, vw), lambda b, c: (b, c, 0)),
        out_shape=jax.ShapeDtypeStruct((B, T, vw), BF16),
        scratch_shapes=[pltpu.VMEM((H, GLA_DV, GLA_DK), F32)],
        compiler_params=_params(("arbitrary", "arbitrary")),
        name="gla",
    )(proj, proj, proj, proj, proj, w2p, b2, ng)


def _t5_bucket(dist):
    max_exact = N_BUCKETS // 2
    d = np.maximum(dist, 1).astype(np.float32)
    large = max_exact + (np.log(d / max_exact) / np.log(MAX_DISTANCE / max_exact)
                         * (N_BUCKETS - max_exact)).astype(np.int32)
    large = np.minimum(large, N_BUCKETS - 1)
    return np.where(dist < max_exact, dist, large).astype(np.int32)


NEG_BIG = -1e30


def _swa_bias(rel_bias):
    blk = SWA_BLOCK
    qi = np.arange(blk)[:, None]
    sj = np.arange(2 * blk)[None, :]
    dist = blk + qi - sj
    band = (dist >= 0) & (dist < blk)
    bucket = _t5_bucket(np.clip(dist, 0, None))
    bias = rel_bias[bucket].astype(F32).transpose(2, 0, 1)
    masks = np.stack([band, band & (sj >= blk)])[:, None]
    return jnp.where(jnp.asarray(masks), bias[None], NEG_BIG)


def _segment_sums(x, seg):
    hi = x.astype(BF16)
    lo = (x - hi.astype(F32)).astype(BF16)
    return (jnp.dot(hi, seg, preferred_element_type=F32) + jnp.dot(lo, seg, preferred_element_type=F32))


def _head_rms_scale(x, seg, seg_t):
    inv = lax.rsqrt(_segment_sums(x * x, seg) * (1.0 / SWA_HD) + EPS)
    return _segment_sums(inv, seg_t)


def _swa_kernel(sink_ref, q_ref, kp_ref, kc_ref, vp_ref, vc_ref, bias_ref, qg_ref, kg_ref, seg_ref,
                segt_ref, o_ref):
    blk = SWA_BLOCK
    hd = SWA_HD
    group = SWA_HEADS // SWA_KV_HEADS
    kvw = SWA_KV_HEADS * hd
    seg = seg_ref[...]
    seg_t = segt_ref[...]
    q = q_ref[...].astype(F32)
    qn = (q * _head_rms_scale(q, seg, seg_t) * qg_ref[...] * (hd ** -0.5)).astype(BF16)
    k2 = jnp.concatenate([kp_ref[...], kc_ref[...]], axis=0).astype(F32)
    kn = (k2 * _head_rms_scale(k2, seg[:kvw], seg_t[:, :kvw]) * kg_ref[...]).astype(BF16)
    v2 = jnp.concatenate([vp_ref[...], vc_ref[...]], axis=0)
    lane = lax.broadcasted_iota(I32, (2 * blk, kvw), 1)
    outs = []
    for kh in range(SWA_KV_HEADS):
        kk = kn[:, kh * hd:(kh + 1) * hd]
        vsh = v2 if kh == 0 else jnp.concatenate([v2[:, kh * hd:], v2[:, :kh * hd]], axis=1)
        vv = jnp.where(lane < hd, vsh, jnp.ones_like(vsh))
        qs = jnp.concatenate([qn[:, (kh * group + g) * hd:(kh * group + g + 1) * hd]
                              for g in range(group)], axis=0)
        logits = lax.dot_general(qs, kk, NT_DIMS, preferred_element_type=F32)
        logits = logits + bias_ref[kh * group:(kh + 1) * group].reshape(group * blk, 2 * blk)
        for g in range(group):
            lg = logits[g * blk:(g + 1) * blk]
            sink = sink_ref[kh * group + g]
            m = jnp.maximum(jnp.max(lg, axis=-1, keepdims=True), sink)
            p = jnp.exp(lg - m).astype(BF16)
            pv = jnp.dot(p, vv, preferred_element_type=F32)
            denom = pv[:, hd:hd + 1] + jnp.exp(sink - m)
            outs.append(pv[:, :hd] / denom)
    o_ref[...] = jnp.concatenate(outs, axis=-1).astype(BF16)


def _swa(proj, bias, qg, kg, sinks):
    B, T, _ = proj.shape
    blk = SWA_BLOCK
    qw = SWA_HEADS * SWA_HD
    kvw = SWA_KV_HEADS * SWA_HD
    prev = lambda b, i: jnp.maximum(i - 1, 0)
    seg = (np.arange(qw)[:, None] // SWA_HD == np.arange(LANES)[None, :])
    seg = jnp.asarray(seg, BF16)
    return pl.pallas_call(
        _swa_kernel,
        grid=(B, T // blk),
        in_specs=[
            pl.BlockSpec(memory_space=pltpu.SMEM),
            pl.BlockSpec((None, blk, qw), lambda b, i: (b, i, COL_SQ // qw)),
            pl.BlockSpec((None, blk, kvw), lambda b, i: (b, prev(b, i), COL_SK // kvw)),
            pl.BlockSpec((None, blk, kvw), lambda b, i: (b, i, COL_SK // kvw)),
            pl.BlockSpec((None, blk, kvw), lambda b, i: (b, prev(b, i), COL_SV // kvw)),
            pl.BlockSpec((None, blk, kvw), lambda b, i: (b, i, COL_SV // kvw)),
            pl.BlockSpec((None, SWA_HEADS, blk, 2 * blk), lambda b, i: (jnp.where(i == 0, 1, 0), 0, 0, 0)),
            pl.BlockSpec((1, qw), lambda b, i: (0, 0)),
            pl.BlockSpec((1, kvw), lambda b, i: (0, 0)),
            pl.BlockSpec((qw, LANES), lambda b, i: (0, 0)),
            pl.BlockSpec((LANES, qw), lambda b, i: (0, 0)),
        ],
        out_specs=pl.BlockSpec((None, blk, qw), lambda b, i: (b, i, 0)),
        out_shape=jax.ShapeDtypeStruct((B, T, qw), BF16),
        compiler_params=_params(("arbitrary", "arbitrary")),
        name="swa",
    )(sinks, proj, proj, proj, proj, proj, bias, jnp.tile(qg, (1, SWA_HEADS)),
      jnp.tile(kg, (1, SWA_KV_HEADS)), seg, seg.T)


def _merge_kernel(x_ref, ga_ref, gb_ref, go_ref, so_ref, wa_ref, wb_ref, wo_ref, g1_ref,
                  n2_ref, sc2_ref, sh2_ref, we_ref, x1_ref, h2_ref, st_ref):
    ya = jnp.dot(go_ref[...], wa_ref[...], preferred_element_type=F32)
    yb = jnp.dot(so_ref[...], wb_ref[...], preferred_element_type=F32)
    m = jax.nn.sigmoid(ga_ref[...].astype(F32)) * ya + jax.nn.sigmoid(gb_ref[...].astype(F32)) * yb
    mixed = jnp.dot(m.astype(BF16), wo_ref[...], preferred_element_type=F32)
    x1 = x_ref[...] + g1_ref[0] * mixed
    x1_ref[...] = x1
    y = x1 * lax.rsqrt(jnp.mean(x1 * x1, axis=-1, keepdims=True) + EPS) * n2_ref[...]
    h2 = y * (1.0 + sc2_ref[0]) + sh2_ref[0]
    h2_ref[...] = h2
    st_ref[...] = lax.dot_general(we_ref[...], h2.astype(BF16), NT_DIMS, preferred_element_type=F32)


def _merge(xf, proj2, gla_o, swa_o, wa, wb, wo, g1, n2, sc2, sh2, weff_t, T, tm=256):
    N, D = xf.shape
    B = g1.shape[0]
    SW = weff_t.shape[0]
    bat = lambda i: ((i * tm) // T, 0, 0)
    full = lambda i: (0, 0)
    return pl.pallas_call(
        _merge_kernel,
        grid=(N // tm,),
        in_specs=[
            pl.BlockSpec((tm, D), lambda i: (i, 0)),
            pl.BlockSpec((tm, D), lambda i: (i, COL_GA // D)),
            pl.BlockSpec((tm, D), lambda i: (i, COL_GB // D)),
            pl.BlockSpec((tm, D), lambda i: (i, 0)),
            pl.BlockSpec((tm, D), lambda i: (i, 0)),
            pl.BlockSpec((D, D), full),
            pl.BlockSpec((D, D), full),
            pl.BlockSpec((D, D), full),
            pl.BlockSpec((1, 1, D), bat),
            pl.BlockSpec((1, D), full),
            pl.BlockSpec((1, 1, D), bat),
            pl.BlockSpec((1, 1, D), bat),
            pl.BlockSpec((SW, D), full),
        ],
        out_specs=[
            pl.BlockSpec((tm, D), lambda i: (i, 0)),
            pl.BlockSpec((tm, D), lambda i: (i, 0)),
            pl.BlockSpec((SW, tm), lambda i: (0, i)),
        ],
        out_shape=[
            jax.ShapeDtypeStruct((N, D), F32),
            jax.ShapeDtypeStruct((N, D), F32),
            jax.ShapeDtypeStruct((SW, N), F32),
        ],
        compiler_params=_params(("arbitrary",), VMEM_LIMIT_BYTES),
        name="merge",
    )(xf, proj2, proj2, gla_o, swa_o, wa, wb, wo, g1.reshape(B, 1, D), n2.reshape(1, D),
      sc2.reshape(B, 1, D), sh2.reshape(B, 1, D), weff_t)


def _tree(op, xs):
    xs = list(xs)
    while len(xs) > 1:
        xs = [op(xs[i], xs[i + 1]) for i in range(0, len(xs) - 1, 2)] + ([xs[-1]] if len(xs) % 2 else [])
    return xs[0]


def _extract_best(ref, n):
    m = _tree(jnp.maximum, [ref[i] for i in range(n)])
    pos = _tree(jnp.minimum, [jnp.where(ref[i] == m, i, n) for i in range(n)])
    for i in range(n):
        ref[i] = jnp.where(pos == i, -jnp.inf, ref[i])
    return m, pos


TOPK_PAIRS = tuple((a, b) for a in range(PEER_TOPK) for b in range(PEER_TOPK)
                   if (a + 1) * (b + 1) <= PEER_TOPK)


def _topk_kernel(s_ref, e_ref, g_ref, work_ref, v_ref, i_ref, cand_ref, cidx_ref, best_ref, row_ref):
    K = PEER_TOPK
    nk = PEER_NKEYS
    H = PEER_HEADS
    work_ref[...] = s_ref[...].reshape(work_ref.shape)

    def stage1(k, _):
        m, pos = _extract_best(work_ref, nk)
        v_ref[k] = m
        i_ref[k] = pos
        return 0

    lax.fori_loop(0, K, stage1, 0)

    for ci, (a, b) in enumerate(TOPK_PAIRS):
        cand_ref[ci] = v_ref[a, 0:H, :] + v_ref[b, H:2 * H, :]
        cidx_ref[ci] = (i_ref[a, 0:H, :] * nk + i_ref[b, H:2 * H, :]) * WORDS_PER_EXPERT
    ncand = len(TOPK_PAIRS)

    def stage2(k, _):
        m, pos = _extract_best(cand_ref, ncand)
        row = _tree(jnp.maximum, [jnp.where(pos == ci, cidx_ref[ci], -1) for ci in range(ncand)])
        best_ref[k] = m
        row_ref[k] = row
        return 0

    lax.fori_loop(0, K, stage2, 0)
    best = best_ref[...]
    ex = jnp.exp(best - best[0:1])
    gates = ex / jnp.sum(ex, axis=0, keepdims=True)
    tm = s_ref.shape[1]
    g_ref[...] = gates.reshape(K * H, tm).T
    e_ref[...] = row_ref[...].reshape(K * H, tm).astype(F32).T.astype(I32)


def _topk(s_t, tm=128):
    SW, N = s_t.shape
    K = PEER_TOPK
    G = 2 * PEER_HEADS
    return pl.pallas_call(
        _topk_kernel,
        grid=(N // tm,),
        in_specs=[pl.BlockSpec((SW, tm), lambda i: (0, i))],
        out_specs=[
            pl.BlockSpec((tm, PEER_SLOTS), lambda i: (i, 0)),
            pl.BlockSpec((tm, PEER_SLOTS), lambda i: (i, 0)),
        ],
        out_shape=[
            jax.ShapeDtypeStruct((N, PEER_SLOTS), I32),
            jax.ShapeDtypeStruct((N, PEER_SLOTS), F32),
        ],
        scratch_shapes=[pltpu.VMEM((PEER_NKEYS, G, tm), F32),
                        pltpu.VMEM((K, G, tm), F32),
                        pltpu.VMEM((K, G, tm), I32),
                        pltpu.VMEM((len(TOPK_PAIRS), PEER_HEADS, tm), F32),
                        pltpu.VMEM((len(TOPK_PAIRS), PEER_HEADS, tm), I32),
                        pltpu.VMEM((K, PEER_HEADS, tm), F32),
                        pltpu.VMEM((K, PEER_HEADS, tm), I32)],
        compiler_params=_params(("arbitrary",)),
        name="peer_topk",
    )(s_t)


WORDS_PER_EXPERT = SUBLANES // 2


TILE_BLOCK = tuple((r % 2) * WORDS_PER_EXPERT + r // 2 for r in range(SUBLANES))
BLOCK_ROW = tuple(TILE_BLOCK.index(q) for q in range(SUBLANES))


def _bf16_bits(x):
    return lax.bitcast_convert_type(x.astype(BF16).astype(F32), jnp.uint32)


def _table_kernel(t_ref, o_ref):
    rows = t_ref.shape[0]
    for s in range(WORDS_PER_EXPERT):
        lo = _bf16_bits(t_ref[:, s * LANES:(s + 1) * LANES])
        hi = _bf16_bits(t_ref[:, (s + WORDS_PER_EXPERT) * LANES:(s + WORDS_PER_EXPERT + 1) * LANES])
        word = lax.shift_right_logical(lo, jnp.uint32(16)) | (hi & jnp.uint32(0xFFFF0000))
        o_ref[pl.ds(s, rows, stride=WORDS_PER_EXPERT), :] = lax.bitcast_convert_type(word, I32)


def _table(t, te=512):
    n, d = t.shape
    return pl.pallas_call(
        _table_kernel,
        grid=(n // te,),
        in_specs=[pl.BlockSpec((te, d), lambda i: (i, 0))],
        out_specs=pl.BlockSpec((te * WORDS_PER_EXPERT, LANES), lambda i: (i, 0)),
        out_shape=jax.ShapeDtypeStruct((n * WORDS_PER_EXPERT, LANES), I32),
        compiler_params=_params(("arbitrary",)),
        name="peer_table",
    )(t)


def _expert_tile(tab_ref, e4):
    words = tab_ref[pl.ds(pl.multiple_of(e4, WORDS_PER_EXPERT), WORDS_PER_EXPERT), :]
    return pltpu.bitcast(words, BF16).astype(F32)


def _to_tile_rows(x):
    return jnp.concatenate([x[q:q + 1, :] for q in TILE_BLOCK], axis=0)


def _from_tile_rows(x):
    return jnp.concatenate([x[r:r + 1, :] for r in BLOCK_ROW], axis=0)


BITREV = (0, 4, 2, 6, 1, 5, 3, 7)


def _fold8(ps, sub):
    m4 = sub < 4
    m2 = (sub & 3) < 2
    m1 = (sub & 1) < 1
    s1 = []
    for a, b in zip(ps[0::2], ps[1::2]):
        s1.append(jnp.where(m4, a, b) + pltpu.roll(jnp.where(m4, b, a), 4, 0))
    s2 = []
    for a, b in zip(s1[0::2], s1[1::2]):
        s2.append(jnp.where(m2, a, pltpu.roll(b, 2, 0)) + jnp.where(m2, pltpu.roll(a, 6, 0), b))
    a, b = s2
    return jnp.where(m1, a, pltpu.roll(b, 1, 0)) + jnp.where(m1, pltpu.roll(a, 7, 0), b)


def _gelu_tanh(x):
    return 0.5 * x * (1.0 + jnp.tanh(np.sqrt(2.0 / np.pi) * (x + 0.044715 * (x * x * x))))


def _row_sums(x):
    ones = jnp.ones((LANES, LANES), BF16)
    hi = x.astype(BF16)
    lo = (x - hi.astype(F32)).astype(BF16)
    return (jnp.dot(hi, ones, preferred_element_type=F32)
            + jnp.dot(lo, ones, preferred_element_type=F32))


def _peer_u_kernel(e_ref, h_ref, g_ref, tab_ref, o_ref, slab_ref, *, tb):
    sub = lax.broadcasted_iota(I32, (SUBLANES, LANES), 0)
    groups_per_token = PEER_SLOTS // SUBLANES

    def token(t, _):
        h = _to_tile_rows(h_ref[t])
        for j in range(groups_per_token):
            ps = [_expert_tile(tab_ref, e_ref[t, j * SUBLANES + BITREV[qn]]) * h
                  for qn in range(SUBLANES)]
            dst = pl.multiple_of(t * PEER_SLOTS + j * SUBLANES, SUBLANES)
            slab_ref[pl.ds(dst, SUBLANES), :] = _fold8(ps, sub)
        return 0

    lax.fori_loop(0, tb, token, 0)
    rio = lax.broadcasted_iota(I32, (LANES, LANES), 0)
    lio = lax.broadcasted_iota(I32, (LANES, LANES), 1)
    diag = (rio == lio)[None]
    rows = SUBLANES * PEER_SLOTS
    for gi in range(tb // SUBLANES):
        tok = slice(gi * SUBLANES, (gi + 1) * SUBLANES)
        rs = _row_sums(slab_ref[gi * rows:(gi + 1) * rows, :]).reshape(SUBLANES, PEER_SLOTS, LANES)
        a = jnp.sum(jnp.where(diag, rs, 0.0), axis=1)
        o_ref[tok, :] = g_ref[tok, :] * _gelu_tanh(a)


def _peer_u(e4, h3, gates, tab, tb=64):
    N = h3.shape[0]
    return pl.pallas_call(
        functools.partial(_peer_u_kernel, tb=tb),
        grid=(N // tb,),
        in_specs=[
            pl.BlockSpec((tb, PEER_SLOTS), lambda i: (i, 0), memory_space=pltpu.SMEM),
            pl.BlockSpec((tb, SUBLANES, LANES), lambda i: (i, 0, 0)),
            pl.BlockSpec((tb, PEER_SLOTS), lambda i: (i, 0)),
            pl.BlockSpec(tab.shape, lambda i: (0, 0), pipeline_mode=pl.Buffered(1)),
        ],
        out_specs=pl.BlockSpec((tb, PEER_SLOTS), lambda i: (i, 0)),
        out_shape=jax.ShapeDtypeStruct((N, PEER_SLOTS), F32),
        scratch_shapes=[pltpu.VMEM((tb * PEER_SLOTS, LANES), F32)],
        compiler_params=_params(("arbitrary",), VMEM_LIMIT_BYTES),
        name="peer_u",
    )(e4, h3, gates, tab)


def _peer_v_kernel(e_ref, c_ref, x1_ref, g2_ref, tab_ref, o_ref, splat_ref, *, tb):
    nacc = 4
    rio = lax.broadcasted_iota(I32, (LANES, LANES), 0)
    lio = lax.broadcasted_iota(I32, (LANES, LANES), 1)
    diag = (rio == lio)[None]
    for t8 in range(tb // SUBLANES):
        cc = c_ref[t8 * SUBLANES:(t8 + 1) * SUBLANES, :]
        d = jnp.where(diag, cc[:, None, :], 0.0).reshape(SUBLANES * LANES, LANES)
        splat_ref[t8 * SUBLANES:(t8 + 1) * SUBLANES] = _row_sums(d).reshape(SUBLANES, LANES, LANES)

    def token(t, _):
        accs = [jnp.zeros((SUBLANES, LANES), F32) for _ in range(nacc)]
        for k in range(PEER_SLOTS):
            c = jnp.broadcast_to(splat_ref[t, k:k + 1, :], (SUBLANES, LANES))
            accs[k % nacc] = accs[k % nacc] + c * _expert_tile(tab_ref, e_ref[t, k])
        y = (accs[0] + accs[1]) + (accs[2] + accs[3])
        o_ref[t] = x1_ref[t] + g2_ref[0] * _from_tile_rows(y)
        return 0

    lax.fori_loop(0, tb, token, 0)


def _peer_v(e4, coef, x1_3, g2_3, tab, T, tb=64):
    N = x1_3.shape[0]
    return pl.pallas_call(
        functools.partial(_peer_v_kernel, tb=tb),
        grid=(N // tb,),
        in_specs=[
            pl.BlockSpec((tb, PEER_SLOTS), lambda i: (i, 0), memory_space=pltpu.SMEM),
            pl.BlockSpec((tb, PEER_SLOTS), lambda i: (i, 0)),
            pl.BlockSpec((tb, SUBLANES, LANES), lambda i: (i, 0, 0)),
            pl.BlockSpec((1, SUBLANES, LANES), lambda i: ((i * tb) // T, 0, 0)),
            pl.BlockSpec(tab.shape, lambda i: (0, 0), pipeline_mode=pl.Buffered(1)),
        ],
        out_specs=pl.BlockSpec((tb, SUBLANES, LANES), lambda i: (i, 0, 0)),
        out_shape=jax.ShapeDtypeStruct((N, SUBLANES, LANES), F32),
        scratch_shapes=[pltpu.VMEM((tb, LANES, LANES), F32)],
        compiler_params=_params(("arbitrary",), VMEM_LIMIT_BYTES),
        name="peer_v",
    )(e4, coef, x1_3, g2_3, tab)


def kernel(x, c, w_ada, b_ada, norm1_g, w_in, gla_gate_w2, gla_gate_b, gla_norm_g, swa_qnorm_g,
           swa_knorm_g, swa_sinks, rel_bias, w_up_a, w_up_b, w_out, norm2_g, peer_wq, peer_subkeys,
           peer_u, peer_v):
    B, T, D = x.shape
    N = B * T
    L = w_ada.shape[0]
    mod = _adaln(c, w_ada, b_ada)
    weff_t = _fold_peer_keys(peer_wq, peer_subkeys)
    weff_t = (weff_t.reshape(L, PEER_HEADS, 2, PEER_NKEYS, D).transpose(0, 3, 2, 1, 4)
              .reshape(L, 2 * PEER_HEADS * PEER_NKEYS, D))
    bias = _swa_bias(rel_bias)
    xf = x.reshape(N, D)
    for l in range(L):
        sh1, sc1, g1, sh2, sc2, g2 = [mod[l, :, i * D:(i + 1) * D] for i in range(6)]
        proj = _in_proj(xf, norm1_g[l], sc1, sh1, _pack_w_in(w_in[l]), T)
        proj3 = proj.reshape(B, T, PROJ_W)
        w2p = jnp.zeros((LANES, GLA_HEADS * GLA_DK), BF16).at[:GLA_RANK].set(gla_gate_w2[l].astype(BF16))
        gla_o = _gla(proj3, w2p, gla_gate_b[l].reshape(1, -1), gla_norm_g[l].reshape(1, -1))
        swa_o = _swa(proj3, bias, swa_qnorm_g[l].reshape(1, -1), swa_knorm_g[l].reshape(1, -1),
                     swa_sinks[l])
        x1, h2, s_t = _merge(xf, proj, gla_o.reshape(N, -1), swa_o.reshape(N, -1),
                             w_up_a[l].astype(BF16), w_up_b[l].astype(BF16), w_out[l].astype(BF16),
                             g1, norm2_g[l], sc2, sh2, weff_t[l], T)
        e4, gates = _topk(s_t)
        coef = _peer_u(e4, h2.reshape(N, SUBLANES, LANES), gates, _table(peer_u[l]))
        g2_3 = g2.reshape(B, SUBLANES, LANES)
        xf = _peer_v(e4, coef, x1.reshape(N, SUBLANES, LANES), g2_3, _table(peer_v[l]), T).reshape(N, D)
    return xf.reshape(B, T, D)
```

```python
import functools

import numpy as np
import jax
import jax.numpy as jnp
from jax import lax
from jax.experimental import pallas as pl
from jax.experimental.pallas import tpu as pltpu

F32 = jnp.float32
BF16 = jnp.bfloat16
I32 = jnp.int32
HIGHEST = lax.Precision.HIGHEST
EPS = 1e-6

GLA_HEADS = 4
GLA_DK = 128
GLA_DV = 256
GLA_RANK = 16
GLA_TAU = 16.0
GLA_CHUNK = 64
SWA_HEADS = 16
SWA_KV_HEADS = 2
SWA_HD = 64
SWA_BLOCK = 128
N_BUCKETS = 32
MAX_DISTANCE = 128
PEER_HEADS = 8
PEER_NKEYS = 128
PEER_TOPK = 16
PEER_SLOTS = PEER_HEADS * PEER_TOPK

SUBLANES = 8
LANES = 128
VMEM_LIMIT_BYTES = 56 * 1024 * 1024

NT_DIMS = (((1,), (1,)), ((), ()))
TN_DIMS = (((0,), (0,)), ((), ()))

COL_Q, COL_K, COL_V, COL_GR, COL_SQ, COL_GA, COL_GB = 0, 512, 1024, 2048, 3072, 4096, 5120
COL_SK, COL_SV, COL_GLR = 6144, 6272, 6400
PROJ_W = 6528


def _params(sem, vmem=None):
    return pltpu.CompilerParams(dimension_semantics=sem, vmem_limit_bytes=vmem)


def _adaln_kernel(c_ref, w_ref, b_ref, o_ref):
    c = c_ref[...]
    a = c * jax.nn.sigmoid(c)
    o_ref[0] = jnp.dot(a, w_ref[0], preferred_element_type=F32, precision=HIGHEST) + b_ref[0]


def _adaln(c, w_ada, b_ada):
    L, D, W = w_ada.shape
    B = c.shape[0]
    rows = -(-B // SUBLANES) * SUBLANES
    cp = jnp.zeros((rows, D), F32).at[:B].set(c)
    tn = W // 4
    out = pl.pallas_call(
        _adaln_kernel,
        grid=(L, W // tn),
        in_specs=[
            pl.BlockSpec((rows, D), lambda l, j: (0, 0)),
            pl.BlockSpec((1, D, tn), lambda l, j: (l, 0, j)),
            pl.BlockSpec((1, 1, tn), lambda l, j: (l, 0, j)),
        ],
        out_specs=pl.BlockSpec((1, rows, tn), lambda l, j: (l, 0, j)),
        out_shape=jax.ShapeDtypeStruct((L, rows, W), F32),
        compiler_params=_params(("arbitrary", "arbitrary")),
        name="adaln",
    )(cp, w_ada, b_ada.reshape(L, 1, W))
    return out[:, :B]


def _fold_kernel(sk_ref, wq_ref, o_ref):
    o_ref[0] = lax.dot_general(sk_ref[0, 0], wq_ref[0], NT_DIMS, precision=HIGHEST,
                               preferred_element_type=F32).astype(BF16)


def _fold_peer_keys(peer_wq, peer_subkeys):
    L, D, QW = peer_wq.shape
    half = peer_subkeys.shape[-1]
    ngroups = QW // half
    return pl.pallas_call(
        _fold_kernel,
        grid=(L, ngroups),
        in_specs=[
            pl.BlockSpec((1, 1, PEER_NKEYS, half), lambda l, g: (l, g % 2, 0, 0)),
            pl.BlockSpec((1, D, half), lambda l, g: (l, 0, g)),
        ],
        out_specs=pl.BlockSpec((1, PEER_NKEYS, D), lambda l, g: (l, g, 0)),
        out_shape=jax.ShapeDtypeStruct((L, ngroups * PEER_NKEYS, D), BF16),
        compiler_params=_params(("arbitrary", "arbitrary")),
        name="peer_fold",
    )(peer_subkeys, peer_wq)


def _inproj_kernel(x_ref, g_ref, sc_ref, sh_ref, w_ref, o_ref):
    x = x_ref[...]
    ms = jnp.mean(x * x, axis=-1, keepdims=True)
    y = x * lax.rsqrt(ms + EPS) * g_ref[...]
    h = y * (1.0 + sc_ref[0]) + sh_ref[0]
    o_ref[...] = jnp.dot(h.astype(BF16), w_ref[...], preferred_element_type=F32).astype(BF16)


def _in_proj(xf, g, sc, sh, wp, T, tm=512):
    N, D = xf.shape
    B = sc.shape[0]
    ncol = 3
    tn = PROJ_W // ncol
    return pl.pallas_call(
        _inproj_kernel,
        grid=(ncol, N // tm),
        in_specs=[
            pl.BlockSpec((tm, D), lambda j, i: (i, 0)),
            pl.BlockSpec((1, D), lambda j, i: (0, 0)),
            pl.BlockSpec((1, 1, D), lambda j, i: ((i * tm) // T, 0, 0)),
            pl.BlockSpec((1, 1, D), lambda j, i: ((i * tm) // T, 0, 0)),
            pl.BlockSpec((D, tn), lambda j, i: (0, j)),
        ],
        out_specs=pl.BlockSpec((tm, tn), lambda j, i: (i, j)),
        out_shape=jax.ShapeDtypeStruct((N, PROJ_W), BF16),
        compiler_params=_params(("arbitrary", "arbitrary")),
        name="in_proj",
    )(xf, g.reshape(1, D), sc.reshape(B, 1, D), sh.reshape(B, 1, D), wp)


def _pack_w_in(w):
    o = np.cumsum([0, 512, 512, 1024, GLA_RANK, 1024, 1024, 128, 128, 1024, 1024])
    q, k, v, glr, gr, sq, sk, sv, ga, gb = [w[:, o[i]:o[i + 1]] for i in range(10)]
    pad = jnp.zeros((w.shape[0], LANES - GLA_RANK), w.dtype)
    return jnp.concatenate([q, k, v, gr, sq, ga, gb, sk, sv, glr, pad], axis=1).astype(BF16)


def _gla_kernel(q_ref, k_ref, v_ref, r_ref, glr_ref, w2_ref, b2_ref, ng_ref, o_ref, st_ref, *, nchunk):
    @pl.when(pl.program_id(1) == 0)
    def _():
        st_ref[...] = jnp.zeros_like(st_ref)

    C = GLA_CHUNK
    dk, dv = GLA_DK, GLA_DV
    row = lax.broadcasted_iota(I32, (C, C), 0)
    col = lax.broadcasted_iota(I32, (C, C), 1)
    tri = col <= row
    tri_f = tri.astype(F32)
    w2 = w2_ref[...]
    b2 = b2_ref[...]
    ng = ng_ref[...]
    for ci in range(nchunk):
        sl = pl.ds(ci * C, C)
        z = jnp.dot(glr_ref[sl, :], w2, preferred_element_type=F32) + b2
        log_a = (jnp.minimum(z, 0.0) - jnp.log(1.0 + jnp.exp(-jnp.abs(z)))) * (1.0 / GLA_TAU)
        b = jnp.dot(tri_f, log_a, preferred_element_type=F32, precision=HIGHEST)
        b_last = b[C - 1:C, :]
        q = q_ref[sl, :].astype(F32) * (dk ** -0.5)
        k = k_ref[sl, :].astype(F32)
        q_dec = (q * jnp.exp(b)).astype(BF16)
        k_inv = (k * jnp.exp(-b)).astype(BF16)
        k_tail = (k * jnp.exp(b_last - b)).astype(BF16)
        decay = jnp.exp(b_last)
        for h in range(GLA_HEADS):
            kc = slice(h * dk, (h + 1) * dk)
            vc = slice(h * dv, (h + 1) * dv)
            v = v_ref[sl, vc]
            attn = lax.dot_general(q_dec[:, kc], k_inv[:, kc], NT_DIMS, preferred_element_type=F32)
            attn = jnp.where(tri, attn, 0.0).astype(BF16)
            st = st_ref[h]
            o = (jnp.dot(attn, v, preferred_element_type=F32)
                 + lax.dot_general(q_dec[:, kc], st.astype(BF16), NT_DIMS, preferred_element_type=F32))
            st_ref[h] = st * decay[:, kc] + lax.dot_general(v, k_tail[:, kc], TN_DIMS,
                                                            preferred_element_type=F32)
            on = o * lax.rsqrt(jnp.mean(o * o, axis=-1, keepdims=True) + EPS) * ng
            r = r_ref[sl, vc].astype(F32)
            o_ref[sl, vc] = (on * (r * jax.nn.sigmoid(r))).astype(BF16)


def _gla(proj, w2p, b2, ng, tc=256):
    B, T, _ = proj.shape
    H = GLA_HEADS
    kw, vw = H * GLA_DK, H * GLA_DV
    return pl.pallas_call(
        functools.partial(_gla_kernel, nchunk=tc // GLA_CHUNK),
        grid=(B, T // tc),
        in_specs=[
            pl.BlockSpec((None, tc, kw), lambda b, c: (b, c, COL_Q // kw)),
            pl.BlockSpec((None, tc, kw), lambda b, c: (b, c, COL_K // kw)),
            pl.BlockSpec((None, tc, vw), lambda b, c: (b, c, COL_V // vw)),
            pl.BlockSpec((None, tc, vw), lambda b, c: (b, c, COL_GR // vw)),
            pl.BlockSpec((None, tc, LANES), lambda b, c: (b, c, COL_GLR // LANES)),
            pl.BlockSpec((LANES, kw), lambda b, c: (0, 0)),
            pl.BlockSpec((1, kw), lambda b, c: (0, 0)),
            pl.BlockSpec((1, GLA_DV), lambda b, c: (0, 0)),
        ],
        out_specs=pl.BlockSpec((None, tc, vw), lambda b, c: (b, c, 0)),
        out_shape=jax.ShapeDtypeStruct((B, T, vw), BF16),
        scratch_shapes=[pltpu.VMEM((H, GLA_DV, GLA_DK), F32)],
        compiler_params=_params(("arbitrary", "arbitrary")),
        name="gla",
    )(proj, proj, proj, proj, proj, w2p, b2, ng)


def _t5_bucket(dist):
    max_exact = N_BUCKETS // 2
    d = np.maximum(dist, 1).astype(np.float32)
    large = max_exact + (np.log(d / max_exact) / np.log(MAX_DISTANCE / max_exact)
                         * (N_BUCKETS - max_exact)).astype(np.int32)
    large = np.minimum(large, N_BUCKETS - 1)
    return np.where(dist < max_exact, dist, large).astype(np.int32)


NEG_BIG = -1e30


def _swa_bias(rel_bias):
    blk = SWA_BLOCK
    qi = np.arange(blk)[:, None]
    sj = np.arange(2 * blk)[None, :]
    dist = blk + qi - sj
    band = (dist >= 0) & (dist < blk)
    bucket = _t5_bucket(np.clip(dist, 0, None))
    bias = rel_bias[bucket].astype(F32).transpose(2, 0, 1)
    masks = np.stack([band, band & (sj >= blk)])[:, None]
    return jnp.where(jnp.asarray(masks), bias[None], NEG_BIG)


def _segment_sums(x, seg):
    hi = x.astype(BF16)
    lo = (x - hi.astype(F32)).astype(BF16)
    return (jnp.dot(hi, seg, preferred_element_type=F32) + jnp.dot(lo, seg, preferred_element_type=F32))


def _head_rms_scale(x, seg, seg_t):
    inv = lax.rsqrt(_segment_sums(x * x, seg) * (1.0 / SWA_HD) + EPS)
    return _segment_sums(inv, seg_t)


def _swa_kernel(sink_ref, q_ref, kp_ref, kc_ref, vp_ref, vc_ref, bias_ref, qg_ref, kg_ref, seg_ref,
                segt_ref, o_ref):
    blk = SWA_BLOCK
    hd = SWA_HD
    group = SWA_HEADS // SWA_KV_HEADS
    kvw = SWA_KV_HEADS * hd
    seg = seg_ref[...]
    seg_t = segt_ref[...]
    q = q_ref[...].astype(F32)
    qn = (q * _head_rms_scale(q, seg, seg_t) * qg_ref[...] * (hd ** -0.5)).astype(BF16)
    k2 = jnp.concatenate([kp_ref[...], kc_ref[...]], axis=0).astype(F32)
    kn = (k2 * _head_rms_scale(k2, seg[:kvw], seg_t[:, :kvw]) * kg_ref[...]).astype(BF16)
    v2 = jnp.concatenate([vp_ref[...], vc_ref[...]], axis=0)
    lane = lax.broadcasted_iota(I32, (2 * blk, kvw), 1)
    outs = []
    for kh in range(SWA_KV_HEADS):
        kk = kn[:, kh * hd:(kh + 1) * hd]
        vsh = v2 if kh == 0 else jnp.concatenate([v2[:, kh * hd:], v2[:, :kh * hd]], axis=1)
        vv = jnp.where(lane < hd, vsh, jnp.ones_like(vsh))
        qs = jnp.concatenate([qn[:, (kh * group + g) * hd:(kh * group + g + 1) * hd]
                              for g in range(group)], axis=0)
        logits = lax.dot_general(qs, kk, NT_DIMS, preferred_element_type=F32)
        logits = logits + bias_ref[kh * group:(kh + 1) * group].reshape(group * blk, 2 * blk)
        for g in range(group):
            lg = logits[g * blk:(g + 1) * blk]
            sink = sink_ref[kh * group + g]
            m = jnp.maximum(jnp.max(lg, axis=-1, keepdims=True), sink)
            p = jnp.exp(lg - m).astype(BF16)
            pv = jnp.dot(p, vv, preferred_element_type=F32)
            denom = pv[:, hd:hd + 1] + jnp.exp(sink - m)
            outs.append(pv[:, :hd] / denom)
    o_ref[...] = jnp.concatenate(outs, axis=-1).astype(BF16)


def _swa(proj, bias, qg, kg, sinks):
    B, T, _ = proj.shape
    blk = SWA_BLOCK
    qw = SWA_HEADS * SWA_HD
    kvw = SWA_KV_HEADS * SWA_HD
    prev = lambda b, i: jnp.maximum(i - 1, 0)
    seg = (np.arange(qw)[:, None] // SWA_HD == np.arange(LANES)[None, :])
    seg = jnp.asarray(seg, BF16)
    return pl.pallas_call(
        _swa_kernel,
        grid=(B, T // blk),
        in_specs=[
            pl.BlockSpec(memory_space=pltpu.SMEM),
            pl.BlockSpec((None, blk, qw), lambda b, i: (b, i, COL_SQ // qw)),
            pl.BlockSpec((None, blk, kvw), lambda b, i: (b, prev(b, i), COL_SK // kvw)),
            pl.BlockSpec((None, blk, kvw), lambda b, i: (b, i, COL_SK // kvw)),
            pl.BlockSpec((None, blk, kvw), lambda b, i: (b, prev(b, i), COL_SV // kvw)),
            pl.BlockSpec((None, blk, kvw), lambda b, i: (b, i, COL_SV // kvw)),
            pl.BlockSpec((None, SWA_HEADS, blk, 2 * blk), lambda b, i: (jnp.where(i == 0, 1, 0), 0, 0, 0)),
            pl.BlockSpec((1, qw), lambda b, i: (0, 0)),
            pl.BlockSpec((1, kvw), lambda b, i: (0, 0)),
            pl.BlockSpec((qw, LANES), lambda b, i: (0, 0)),
            pl.BlockSpec((LANES, qw), lambda b, i: (0, 0)),
        ],
        out_specs=pl.BlockSpec((None, blk, qw), lambda b, i: (b, i, 0)),
        out_shape=jax.ShapeDtypeStruct((B, T, qw), BF16),
        compiler_params=_params(("arbitrary", "arbitrary")),
        name="swa",
    )(sinks, proj, proj, proj, proj, proj, bias, jnp.tile(qg, (1, SWA_HEADS)),
      jnp.tile(kg, (1, SWA_KV_HEADS)), seg, seg.T)


def _merge_kernel(x_ref, ga_ref, gb_ref, go_ref, so_ref, wa_ref, wb_ref, wo_ref, g1_ref,
                  n2_ref, sc2_ref, sh2_ref, we_ref, x1_ref, h2_ref, st_ref):
    ya = jnp.dot(go_ref[...], wa_ref[...], preferred_element_type=F32)
    yb = jnp.dot(so_ref[...], wb_ref[...], preferred_element_type=F32)
    m = jax.nn.sigmoid(ga_ref[...].astype(F32)) * ya + jax.nn.sigmoid(gb_ref[...].astype(F32)) * yb
    mixed = jnp.dot(m.astype(BF16), wo_ref[...], preferred_element_type=F32)
    x1 = x_ref[...] + g1_ref[0] * mixed
    x1_ref[...] = x1
    y = x1 * lax.rsqrt(jnp.mean(x1 * x1, axis=-1, keepdims=True) + EPS) * n2_ref[...]
    h2 = y * (1.0 + sc2_ref[0]) + sh2_ref[0]
    h2_ref[...] = h2
    st_ref[...] = lax.dot_general(we_ref[...], h2.astype(BF16), NT_DIMS, preferred_element_type=F32)


def _merge(xf, proj2, gla_o, swa_o, wa, wb, wo, g1, n2, sc2, sh2, weff_t, T, tm=256):
    N, D = xf.shape
    B = g1.shape[0]
    SW = weff_t.shape[0]
    bat = lambda i: ((i * tm) // T, 0, 0)
    full = lambda i: (0, 0)
    return pl.pallas_call(
        _merge_kernel,
        grid=(N // tm,),
        in_specs=[
            pl.BlockSpec((tm, D), lambda i: (i, 0)),
            pl.BlockSpec((tm, D), lambda i: (i, COL_GA // D)),
            pl.BlockSpec((tm, D), lambda i: (i, COL_GB // D)),
            pl.BlockSpec((tm, D), lambda i: (i, 0)),
            pl.BlockSpec((tm, D), lambda i: (i, 0)),
            pl.BlockSpec((D, D), full),
            pl.BlockSpec((D, D), full),
            pl.BlockSpec((D, D), full),
            pl.BlockSpec((1, 1, D), bat),
            pl.BlockSpec((1, D), full),
            pl.BlockSpec((1, 1, D), bat),
            pl.BlockSpec((1, 1, D), bat),
            pl.BlockSpec((SW, D), full),
        ],
        out_specs=[
            pl.BlockSpec((tm, D), lambda i: (i, 0)),
            pl.BlockSpec((tm, D), lambda i: (i, 0)),
            pl.BlockSpec((SW, tm), lambda i: (0, i)),
        ],
        out_shape=[
            jax.ShapeDtypeStruct((N, D), F32),
            jax.ShapeDtypeStruct((N, D), F32),
            jax.ShapeDtypeStruct((SW, N), F32),
        ],
        compiler_params=_params(("arbitrary",), VMEM_LIMIT_BYTES),
        name="merge",
    )(xf, proj2, proj2, gla_o, swa_o, wa, wb, wo, g1.reshape(B, 1, D), n2.reshape(1, D),
      sc2.reshape(B, 1, D), sh2.reshape(B, 1, D), weff_t)


def _tree(op, xs):
    xs = list(xs)
    while len(xs) > 1:
        xs = [op(xs[i], xs[i + 1]) for i in range(0, len(xs) - 1, 2)] + ([xs[-1]] if len(xs) % 2 else [])
    return xs[0]


def _extract_best(ref, n):
    m = _tree(jnp.maximum, [ref[i] for i in range(n)])
    pos = _tree(jnp.minimum, [jnp.where(ref[i] == m, i, n) for i in range(n)])
    for i in range(n):
        ref[i] = jnp.where(pos == i, -jnp.inf, ref[i])
    return m, pos


TOPK_PAIRS = tuple((a, b) for a in range(PEER_TOPK) for b in range(PEER_TOPK)
                   if (a + 1) * (b + 1) <= PEER_TOPK)


def _topk_kernel(s_ref, e_ref, g_ref, work_ref, v_ref, i_ref, cand_ref, cidx_ref, best_ref, row_ref):
    K = PEER_TOPK
    nk = PEER_NKEYS
    H = PEER_HEADS
    work_ref[...] = s_ref[...].reshape(work_ref.shape)

    def stage1(k, _):
        m, pos = _extract_best(work_ref, nk)
        v_ref[k] = m
        i_ref[k] = pos
        return 0

    lax.fori_loop(0, K, stage1, 0)

    for ci, (a, b) in enumerate(TOPK_PAIRS):
        cand_ref[ci] = v_ref[a, 0:H, :] + v_ref[b, H:2 * H, :]
        cidx_ref[ci] = (i_ref[a, 0:H, :] * nk + i_ref[b, H:2 * H, :]) * WORDS_PER_EXPERT
    ncand = len(TOPK_PAIRS)

    def stage2(k, _):
        m, pos = _extract_best(cand_ref, ncand)
        row = _tree(jnp.maximum, [jnp.where(pos == ci, cidx_ref[ci], -1) for ci in range(ncand)])
        best_ref[k] = m
        row_ref[k] = row
        return 0

    lax.fori_loop(0, K, stage2, 0)
    best = best_ref[...]
    ex = jnp.exp(best - best[0:1])
    gates = ex / jnp.sum(ex, axis=0, keepdims=True)
    tm = s_ref.shape[1]
    g_ref[...] = gates.reshape(K * H, tm).T
    e_ref[...] = row_ref[...].reshape(K * H, tm).astype(F32).T.astype(I32)


def _topk(s_t, tm=128):
    SW, N = s_t.shape
    K = PEER_TOPK
    G = 2 * PEER_HEADS
    return pl.pallas_call(
        _topk_kernel,
        grid=(N // tm,),
        in_specs=[pl.BlockSpec((SW, tm), lambda i: (0, i))],
        out_specs=[
            pl.BlockSpec((tm, PEER_SLOTS), lambda i: (i, 0)),
            pl.BlockSpec((tm, PEER_SLOTS), lambda i: (i, 0)),
        ],
        out_shape=[
            jax.ShapeDtypeStruct((N, PEER_SLOTS), I32),
            jax.ShapeDtypeStruct((N, PEER_SLOTS), F32),
        ],
        scratch_shapes=[pltpu.VMEM((PEER_NKEYS, G, tm), F32),
                        pltpu.VMEM((K, G, tm), F32),
                        pltpu.VMEM((K, G, tm), I32),
                        pltpu.VMEM((len(TOPK_PAIRS), PEER_HEADS, tm), F32),
                        pltpu.VMEM((len(TOPK_PAIRS), PEER_HEADS, tm), I32),
                        pltpu.VMEM((K, PEER_HEADS, tm), F32),
                        pltpu.VMEM((K, PEER_HEADS, tm), I32)],
        compiler_params=_params(("arbitrary",)),
        name="peer_topk",
    )(s_t)


WORDS_PER_EXPERT = SUBLANES // 2


TILE_BLOCK = tuple((r % 2) * WORDS_PER_EXPERT + r // 2 for r in range(SUBLANES))
BLOCK_ROW = tuple(TILE_BLOCK.index(q) for q in range(SUBLANES))


def _bf16_bits(x):
    return lax.bitcast_convert_type(x.astype(BF16).astype(F32), jnp.uint32)


def _table_kernel(t_ref, o_ref):
    rows = t_ref.shape[0]
    for s in range(WORDS_PER_EXPERT):
        lo = _bf16_bits(t_ref[:, s * LANES:(s + 1) * LANES])
        hi = _bf16_bits(t_ref[:, (s + WORDS_PER_EXPERT) * LANES:(s + WORDS_PER_EXPERT + 1) * LANES])
        word = lax.shift_right_logical(lo, jnp.uint32(16)) | (hi & jnp.uint32(0xFFFF0000))
        o_ref[pl.ds(s, rows, stride=WORDS_PER_EXPERT), :] = lax.bitcast_convert_type(word, I32)


def _table(t, te=512):
    n, d = t.shape
    return pl.pallas_call(
        _table_kernel,
        grid=(n // te,),
        in_specs=[pl.BlockSpec((te, d), lambda i: (i, 0))],
        out_specs=pl.BlockSpec((te * WORDS_PER_EXPERT, LANES), lambda i: (i, 0)),
        out_shape=jax.ShapeDtypeStruct((n * WORDS_PER_EXPERT, LANES), I32),
        compiler_params=_params(("arbitrary",)),
        name="peer_table",
    )(t)


def _expert_tile(tab_ref, e4):
    words = tab_ref[pl.ds(pl.multiple_of(e4, WORDS_PER_EXPERT), WORDS_PER_EXPERT), :]
    return pltpu.bitcast(words, BF16).astype(F32)


def _to_tile_rows(x):
    return jnp.concatenate([x[q:q + 1, :] for q in TILE_BLOCK], axis=0)


def _from_tile_rows(x):
    return jnp.concatenate([x[r:r + 1, :] for r in BLOCK_ROW], axis=0)


BITREV = (0, 4, 2, 6, 1, 5, 3, 7)


def _fold8(ps, sub):
    m4 = sub < 4
    m2 = (sub & 3) < 2
    m1 = (sub & 1) < 1
    s1 = []
    for a, b in zip(ps[0::2], ps[1::2]):
        s1.append(jnp.where(m4, a, b) + pltpu.roll(jnp.where(m4, b, a), 4, 0))
    s2 = []
    for a, b in zip(s1[0::2], s1[1::2]):
        s2.append(jnp.where(m2, a, pltpu.roll(b, 2, 0)) + jnp.where(m2, pltpu.roll(a, 6, 0), b))
    a, b = s2
    return jnp.where(m1, a, pltpu.roll(b, 1, 0)) + jnp.where(m1, pltpu.roll(a, 7, 0), b)


def _gelu_tanh(x):
    return 0.5 * x * (1.0 + jnp.tanh(np.sqrt(2.0 / np.pi) * (x + 0.044715 * (x * x * x))))


def _row_sums(x):
    ones = jnp.ones((LANES, LANES), BF16)
    hi = x.astype(BF16)
    lo = (x - hi.astype(F32)).astype(BF16)
    return (jnp.dot(hi, ones, preferred_element_type=F32)
            + jnp.dot(lo, ones, preferred_element_type=F32))


def _peer_u_kernel(e_ref, h_ref, g_ref, tab_ref, o_ref, slab_ref, *, tb):
    sub = lax.broadcasted_iota(I32, (SUBLANES, LANES), 0)
    groups_per_token = PEER_SLOTS // SUBLANES

    def token(t, _):
        h = _to_tile_rows(h_ref[t])
        for j in range(groups_per_token):
            ps = [_expert_tile(tab_ref, e_ref[t, j * SUBLANES + BITREV[qn]]) * h
                  for qn in range(SUBLANES)]
            dst = pl.multiple_of(t * PEER_SLOTS + j * SUBLANES, SUBLANES)
            slab_ref[pl.ds(dst, SUBLANES), :] = _fold8(ps, sub)
        return 0

    lax.fori_loop(0, tb, token, 0)
    rio = lax.broadcasted_iota(I32, (LANES, LANES), 0)
    lio = lax.broadcasted_iota(I32, (LANES, LANES), 1)
    diag = (rio == lio)[None]
    rows = SUBLANES * PEER_SLOTS
    for gi in range(tb // SUBLANES):
        tok = slice(gi * SUBLANES, (gi + 1) * SUBLANES)
        rs = jnp.sum(slab_ref[gi * rows:(gi + 1) * rows, :], axis=-1, keepdims=True)
        rs = rs.reshape(SUBLANES, PEER_SLOTS, 1)
        a = jnp.sum(jnp.where(diag, rs, 0.0), axis=1)
        o_ref[tok, :] = g_ref[tok, :] * _gelu_tanh(a)


def _peer_u(e4, h3, gates, tab, tb=64):
    N = h3.shape[0]
    return pl.pallas_call(
        functools.partial(_peer_u_kernel, tb=tb),
        grid=(N // tb,),
        in_specs=[
            pl.BlockSpec((tb, PEER_SLOTS), lambda i: (i, 0), memory_space=pltpu.SMEM),
            pl.BlockSpec((tb, SUBLANES, LANES), lambda i: (i, 0, 0)),
            pl.BlockSpec((tb, PEER_SLOTS), lambda i: (i, 0)),
            pl.BlockSpec(tab.shape, lambda i: (0, 0), pipeline_mode=pl.Buffered(1)),
        ],
        out_specs=pl.BlockSpec((tb, PEER_SLOTS), lambda i: (i, 0)),
        out_shape=jax.ShapeDtypeStruct((N, PEER_SLOTS), F32),
        scratch_shapes=[pltpu.VMEM((tb * PEER_SLOTS, LANES), F32)],
        compiler_params=_params(("arbitrary",), VMEM_LIMIT_BYTES),
        name="peer_u",
    )(e4, h3, gates, tab)


def _peer_v_kernel(e_ref, c_ref, x1_ref, g2_ref, tab_ref, o_ref, splat_ref, *, tb):
    nacc = 3
    rio = lax.broadcasted_iota(I32, (LANES, LANES), 0)
    lio = lax.broadcasted_iota(I32, (LANES, LANES), 1)
    diag = (rio == lio)[None]
    for t8 in range(tb // SUBLANES):
        cc = c_ref[t8 * SUBLANES:(t8 + 1) * SUBLANES, :]
        s = jnp.sum(jnp.where(diag, cc[:, None, :], 0.0), axis=-1, keepdims=True)
        splat_ref[t8 * SUBLANES:(t8 + 1) * SUBLANES] = jnp.broadcast_to(s, (SUBLANES, LANES, LANES))

    def token(t, _):
        accs = [jnp.zeros((SUBLANES, LANES), F32) for _ in range(nacc)]
        for k in range(PEER_SLOTS):
            c = jnp.broadcast_to(splat_ref[t, k:k + 1, :], (SUBLANES, LANES))
            accs[k % nacc] = accs[k % nacc] + c * _expert_tile(tab_ref, e_ref[t, k])
        y = _tree(jnp.add, accs)
        o_ref[t] = x1_ref[t] + g2_ref[0] * _from_tile_rows(y)
        return 0

    lax.fori_loop(0, tb, token, 0)


def _peer_v(e4, coef, x1_3, g2_3, tab, T, tb=64):
    N = x1_3.shape[0]
    return pl.pallas_call(
        functools.partial(_peer_v_kernel, tb=tb),
        grid=(N // tb,),
        in_specs=[
            pl.BlockSpec((tb, PEER_SLOTS), lambda i: (i, 0), memory_space=pltpu.SMEM),
            pl.BlockSpec((tb, PEER_SLOTS), lambda i: (i, 0)),
            pl.BlockSpec((tb, SUBLANES, LANES), lambda i: (i, 0, 0)),
            pl.BlockSpec((1, SUBLANES, LANES), lambda i: ((i * tb) // T, 0, 0)),
            pl.BlockSpec(tab.shape, lambda i: (0, 0), pipeline_mode=pl.Buffered(1)),
        ],
        out_specs=pl.BlockSpec((tb, SUBLANES, LANES), lambda i: (i, 0, 0)),
        out_shape=jax.ShapeDtypeStruct((N, SUBLANES, LANES), F32),
        scratch_shapes=[pltpu.VMEM((tb, LANES, LANES), F32)],
        compiler_params=_params(("arbitrary",), VMEM_LIMIT_BYTES),
        name="peer_v",
    )(e4, coef, x1_3, g2_3, tab)


def kernel(x, c, w_ada, b_ada, norm1_g, w_in, gla_gate_w2, gla_gate_b, gla_norm_g, swa_qnorm_g,
           swa_knorm_g, swa_sinks, rel_bias, w_up_a, w_up_b, w_out, norm2_g, peer_wq, peer_subkeys,
           peer_u, peer_v):
    B, T, D = x.shape
    N = B * T
    L = w_ada.shape[0]
    mod = _adaln(c, w_ada, b_ada)
    weff_t = _fold_peer_keys(peer_wq, peer_subkeys)
    weff_t = (weff_t.reshape(L, PEER_HEADS, 2, PEER_NKEYS, D).transpose(0, 3, 2, 1, 4)
              .reshape(L, 2 * PEER_HEADS * PEER_NKEYS, D))
    bias = _swa_bias(rel_bias)
    xf = x.reshape(N, D)
    for l in range(L):
        sh1, sc1, g1, sh2, sc2, g2 = [mod[l, :, i * D:(i + 1) * D] for i in range(6)]
        proj = _in_proj(xf, norm1_g[l], sc1, sh1, _pack_w_in(w_in[l]), T)
        proj3 = proj.reshape(B, T, PROJ_W)
        w2p = jnp.zeros((LANES, GLA_HEADS * GLA_DK), BF16).at[:GLA_RANK].set(gla_gate_w2[l].astype(BF16))
        gla_o = _gla(proj3, w2p, gla_gate_b[l].reshape(1, -1), gla_norm_g[l].reshape(1, -1))
        swa_o = _swa(proj3, bias, swa_qnorm_g[l].reshape(1, -1), swa_knorm_g[l].reshape(1, -1),
                     swa_sinks[l])
        x1, h2, s_t = _merge(xf, proj, gla_o.reshape(N, -1), swa_o.reshape(N, -1),
                             w_up_a[l].astype(BF16), w_up_b[l].astype(BF16), w_out[l].astype(BF16),
                             g1, norm2_g[l], sc2, sh2, weff_t[l], T)
        e4, gates = _topk(s_t)
        coef = _peer_u(e4, h2.reshape(N, SUBLANES, LANES), gates, _table(peer_u[l]))
        g2_3 = g2.reshape(B, SUBLANES, LANES)
        xf = _peer_v(e4, coef, x1.reshape(N, SUBLANES, LANES), g2_3, _table(peer_v[l]), T).reshape(N, D)
    return xf.reshape(B, T, D)
```

```python
import functools

import numpy as np
import jax
import jax.numpy as jnp
from jax import lax
from jax.experimental import pallas as pl
from jax.experimental.pallas import tpu as pltpu

F32 = jnp.float32
BF16 = jnp.bfloat16
I32 = jnp.int32
HIGHEST = lax.Precision.HIGHEST
EPS = 1e-6

GLA_HEADS = 4
GLA_DK = 128
GLA_DV = 256
GLA_RANK = 16
GLA_TAU = 16.0
GLA_CHUNK = 64
SWA_HEADS = 16
SWA_KV_HEADS = 2
SWA_HD = 64
SWA_BLOCK = 128
N_BUCKETS = 32
MAX_DISTANCE = 128
PEER_HEADS = 8
PEER_NKEYS = 128
PEER_TOPK = 16
PEER_SLOTS = PEER_HEADS * PEER_TOPK

SUBLANES = 8
LANES = 128
VMEM_LIMIT_BYTES = 56 * 1024 * 1024

NT_DIMS = (((1,), (1,)), ((), ()))
TN_DIMS = (((0,), (0,)), ((), ()))

COL_Q, COL_K, COL_V, COL_GR, COL_SQ, COL_GA, COL_GB = 0, 512, 1024, 2048, 3072, 4096, 5120
COL_SK, COL_SV, COL_GLR = 6144, 6272, 6400
PROJ_W = 6528


def _params(sem, vmem=None):
    return pltpu.CompilerParams(dimension_semantics=sem, vmem_limit_bytes=vmem)


def _adaln_kernel(c_ref, w_ref, b_ref, o_ref):
    c = c_ref[...]
    a = c * jax.nn.sigmoid(c)
    o_ref[0] = jnp.dot(a, w_ref[0], preferred_element_type=F32, precision=HIGHEST) + b_ref[0]


def _adaln(c, w_ada, b_ada):
    L, D, W = w_ada.shape
    B = c.shape[0]
    rows = -(-B // SUBLANES) * SUBLANES
    cp = jnp.zeros((rows, D), F32).at[:B].set(c)
    tn = W // 4
    out = pl.pallas_call(
        _adaln_kernel,
        grid=(L, W // tn),
        in_specs=[
            pl.BlockSpec((rows, D), lambda l, j: (0, 0)),
            pl.BlockSpec((1, D, tn), lambda l, j: (l, 0, j)),
            pl.BlockSpec((1, 1, tn), lambda l, j: (l, 0, j)),
        ],
        out_specs=pl.BlockSpec((1, rows, tn), lambda l, j: (l, 0, j)),
        out_shape=jax.ShapeDtypeStruct((L, rows, W), F32),
        compiler_params=_params(("arbitrary", "arbitrary")),
        name="adaln",
    )(cp, w_ada, b_ada.reshape(L, 1, W))
    return out[:, :B]


def _fold_kernel(sk_ref, wq_ref, o_ref):
    o_ref[0] = lax.dot_general(sk_ref[0, 0], wq_ref[0], NT_DIMS, precision=HIGHEST,
                               preferred_element_type=F32).astype(BF16)


def _fold_peer_keys(peer_wq, peer_subkeys):
    L, D, QW = peer_wq.shape
    half = peer_subkeys.shape[-1]
    ngroups = QW // half
    return pl.pallas_call(
        _fold_kernel,
        grid=(L, ngroups),
        in_specs=[
            pl.BlockSpec((1, 1, PEER_NKEYS, half), lambda l, g: (l, g % 2, 0, 0)),
            pl.BlockSpec((1, D, half), lambda l, g: (l, 0, g)),
        ],
        out_specs=pl.BlockSpec((1, PEER_NKEYS, D), lambda l, g: (l, g, 0)),
        out_shape=jax.ShapeDtypeStruct((L, ngroups * PEER_NKEYS, D), BF16),
        compiler_params=_params(("arbitrary", "arbitrary")),
        name="peer_fold",
    )(peer_subkeys, peer_wq)


def _inproj_kernel(x_ref, g_ref, sc_ref, sh_ref, w_ref, o_ref):
    x = x_ref[...]
    ms = jnp.mean(x * x, axis=-1, keepdims=True)
    y = x * lax.rsqrt(ms + EPS) * g_ref[...]
    h = y * (1.0 + sc_ref[0]) + sh_ref[0]
    o_ref[...] = jnp.dot(h.astype(BF16), w_ref[...], preferred_element_type=F32).astype(BF16)


def _in_proj(xf, g, sc, sh, wp, T, tm=512):
    N, D = xf.shape
    B = sc.shape[0]
    ncol = 3
    tn = PROJ_W // ncol
    return pl.pallas_call(
        _inproj_kernel,
        grid=(ncol, N // tm),
        in_specs=[
            pl.BlockSpec((tm, D), lambda j, i: (i, 0)),
            pl.BlockSpec((1, D), lambda j, i: (0, 0)),
            pl.BlockSpec((1, 1, D), lambda j, i: ((i * tm) // T, 0, 0)),
            pl.BlockSpec((1, 1, D), lambda j, i: ((i * tm) // T, 0, 0)),
            pl.BlockSpec((D, tn), lambda j, i: (0, j)),
        ],
        out_specs=pl.BlockSpec((tm, tn), lambda j, i: (i, j)),
        out_shape=jax.ShapeDtypeStruct((N, PROJ_W), BF16),
        compiler_params=_params(("arbitrary", "arbitrary")),
        name="in_proj",
    )(xf, g.reshape(1, D), sc.reshape(B, 1, D), sh.reshape(B, 1, D), wp)


def _pack_w_in(w):
    o = np.cumsum([0, 512, 512, 1024, GLA_RANK, 1024, 1024, 128, 128, 1024, 1024])
    q, k, v, glr, gr, sq, sk, sv, ga, gb = [w[:, o[i]:o[i + 1]] for i in range(10)]
    pad = jnp.zeros((w.shape[0], LANES - GLA_RANK), w.dtype)
    return jnp.concatenate([q, k, v, gr, sq, ga, gb, sk, sv, glr, pad], axis=1).astype(BF16)


def _gla_kernel(q_ref, k_ref, v_ref, r_ref, glr_ref, w2_ref, b2_ref, ng_ref, o_ref, st_ref, *, nchunk):
    @pl.when(pl.program_id(1) == 0)
    def _():
        st_ref[...] = jnp.zeros_like(st_ref)

    C = GLA_CHUNK
    dk, dv = GLA_DK, GLA_DV
    row = lax.broadcasted_iota(I32, (C, C), 0)
    col = lax.broadcasted_iota(I32, (C, C), 1)
    tri = col <= row
    tri_f = tri.astype(F32)
    w2 = w2_ref[...]
    b2 = b2_ref[...]
    ng = ng_ref[...]
    for ci in range(nchunk):
        sl = pl.ds(ci * C, C)
        z = jnp.dot(glr_ref[sl, :], w2, preferred_element_type=F32) + b2
        log_a = (jnp.minimum(z, 0.0) - jnp.log(1.0 + jnp.exp(-jnp.abs(z)))) * (1.0 / GLA_TAU)
        b = jnp.dot(tri_f, log_a, preferred_element_type=F32, precision=HIGHEST)
        b_last = b[C - 1:C, :]
        q = q_ref[sl, :].astype(F32) * (dk ** -0.5)
        k = k_ref[sl, :].astype(F32)
        q_dec = (q * jnp.exp(b)).astype(BF16)
        k_inv = (k * jnp.exp(-b)).astype(BF16)
        k_tail = (k * jnp.exp(b_last - b)).astype(BF16)
        decay = jnp.exp(b_last)
        for h in range(GLA_HEADS):
            kc = slice(h * dk, (h + 1) * dk)
            vc = slice(h * dv, (h + 1) * dv)
            v = v_ref[sl, vc]
            attn = lax.dot_general(q_dec[:, kc], k_inv[:, kc], NT_DIMS, preferred_element_type=F32)
            attn = jnp.where(tri, attn, 0.0).astype(BF16)
            st = st_ref[h]
            o = (jnp.dot(attn, v, preferred_element_type=F32)
                 + lax.dot_general(q_dec[:, kc], st.astype(BF16), NT_DIMS, preferred_element_type=F32))
            st_ref[h] = st * decay[:, kc] + lax.dot_general(v, k_tail[:, kc], TN_DIMS,
                                                            preferred_element_type=F32)
            on = o * lax.rsqrt(jnp.mean(o * o, axis=-1, keepdims=True) + EPS) * ng
            r = r_ref[sl, vc].astype(F32)
            o_ref[sl, vc] = (on * (r * jax.nn.sigmoid(r))).astype(BF16)


def _gla(proj, w2p, b2, ng, tc=256):
    B, T, _ = proj.shape
    H = GLA_HEADS
    kw, vw = H * GLA_DK, H * GLA_DV
    return pl.pallas_call(
        functools.partial(_gla_kernel, nchunk=tc // GLA_CHUNK),
        grid=(B, T // tc),
        in_specs=[
            pl.BlockSpec((None, tc, kw), lambda b, c: (b, c, COL_Q // kw)),
            pl.BlockSpec((None, tc, kw), lambda b, c: (b, c, COL_K // kw)),
            pl.BlockSpec((None, tc, vw), lambda b, c: (b, c, COL_V // vw)),
            pl.BlockSpec((None, tc, vw), lambda b, c: (b, c, COL_GR // vw)),
            pl.BlockSpec((None, tc, LANES), lambda b, c: (b, c, COL_GLR // LANES)),
            pl.BlockSpec((LANES, kw), lambda b, c: (0, 0)),
            pl.BlockSpec((1, kw), lambda b, c: (0, 0)),
            pl.BlockSpec((1, GLA_DV), lambda b, c: (0, 0)),
        ],
        out_specs=pl.BlockSpec((None, tc, vw), lambda b, c: (b, c, 0)),
        out_shape=jax.ShapeDtypeStruct((B, T, vw), BF16),
        scratch_shapes=[pltpu.VMEM((H, GLA_DV, GLA_DK), F32)],
        compiler_params=_params(("arbitrary", "arbitrary")),
        name="gla",
    )(proj, proj, proj, proj, proj, w2p, b2, ng)


def _t5_bucket(dist):
    max_exact = N_BUCKETS // 2
    d = np.maximum(dist, 1).astype(np.float32)
    large = max_exact + (np.log(d / max_exact) / np.log(MAX_DISTANCE / max_exact)
                         * (N_BUCKETS - max_exact)).astype(np.int32)
    large = np.minimum(large, N_BUCKETS - 1)
    return np.where(dist < max_exact, dist, large).astype(np.int32)


NEG_BIG = -1e30


def _swa_bias(rel_bias):
    blk = SWA_BLOCK
    qi = np.arange(blk)[:, None]
    sj = np.arange(2 * blk)[None, :]
    dist = blk + qi - sj
    band = (dist >= 0) & (dist < blk)
    bucket = _t5_bucket(np.clip(dist, 0, None))
    bias = rel_bias[bucket].astype(F32).transpose(2, 0, 1)
    masks = np.stack([band, band & (sj >= blk)])[:, None]
    return jnp.where(jnp.asarray(masks), bias[None], NEG_BIG)


def _segment_sums(x, seg):
    hi = x.astype(BF16)
    lo = (x - hi.astype(F32)).astype(BF16)
    return (jnp.dot(hi, seg, preferred_element_type=F32) + jnp.dot(lo, seg, preferred_element_type=F32))


def _head_rms_scale(x, seg, seg_t):
    inv = lax.rsqrt(_segment_sums(x * x, seg) * (1.0 / SWA_HD) + EPS)
    return _segment_sums(inv, seg_t)


def _swa_kernel(sink_ref, q_ref, kp_ref, kc_ref, vp_ref, vc_ref, bias_ref, qg_ref, kg_ref, seg_ref,
                segt_ref, o_ref):
    blk = SWA_BLOCK
    hd = SWA_HD
    group = SWA_HEADS // SWA_KV_HEADS
    kvw = SWA_KV_HEADS * hd
    seg = seg_ref[...]
    seg_t = segt_ref[...]
    q = q_ref[...].astype(F32)
    qn = (q * _head_rms_scale(q, seg, seg_t) * qg_ref[...] * (hd ** -0.5)).astype(BF16)
    k2 = jnp.concatenate([kp_ref[...], kc_ref[...]], axis=0).astype(F32)
    kn = (k2 * _head_rms_scale(k2, seg[:kvw], seg_t[:, :kvw]) * kg_ref[...]).astype(BF16)
    v2 = jnp.concatenate([vp_ref[...], vc_ref[...]], axis=0)
    lane = lax.broadcasted_iota(I32, (2 * blk, kvw), 1)
    outs = []
    for kh in range(SWA_KV_HEADS):
        kk = kn[:, kh * hd:(kh + 1) * hd]
        vsh = v2 if kh == 0 else jnp.concatenate([v2[:, kh * hd:], v2[:, :kh * hd]], axis=1)
        vv = jnp.where(lane < hd, vsh, jnp.ones_like(vsh))
        qs = jnp.concatenate([qn[:, (kh * group + g) * hd:(kh * group + g + 1) * hd]
                              for g in range(group)], axis=0)
        logits = lax.dot_general(qs, kk, NT_DIMS, preferred_element_type=F32)
        logits = logits + bias_ref[kh * group:(kh + 1) * group].reshape(group * blk, 2 * blk)
        for g in range(group):
            lg = logits[g * blk:(g + 1) * blk]
            sink = sink_ref[kh * group + g]
            m = jnp.maximum(jnp.max(lg, axis=-1, keepdims=True), sink)
            p = jnp.exp(lg - m).astype(BF16)
            pv = jnp.dot(p, vv, preferred_element_type=F32)
            denom = pv[:, hd:hd + 1] + jnp.exp(sink - m)
            outs.append(pv[:, :hd] / denom)
    o_ref[...] = jnp.concatenate(outs, axis=-1).astype(BF16)


def _swa(proj, bias, qg, kg, sinks):
    B, T, _ = proj.shape
    blk = SWA_BLOCK
    qw = SWA_HEADS * SWA_HD
    kvw = SWA_KV_HEADS * SWA_HD
    prev = lambda b, i: jnp.maximum(i - 1, 0)
    seg = (np.arange(qw)[:, None] // SWA_HD == np.arange(LANES)[None, :])
    seg = jnp.asarray(seg, BF16)
    return pl.pallas_call(
        _swa_kernel,
        grid=(B, T // blk),
        in_specs=[
            pl.BlockSpec(memory_space=pltpu.SMEM),
            pl.BlockSpec((None, blk, qw), lambda b, i: (b, i, COL_SQ // qw)),
            pl.BlockSpec((None, blk, kvw), lambda b, i: (b, prev(b, i), COL_SK // kvw)),
            pl.BlockSpec((None, blk, kvw), lambda b, i: (b, i, COL_SK // kvw)),
            pl.BlockSpec((None, blk, kvw), lambda b, i: (b, prev(b, i), COL_SV // kvw)),
            pl.BlockSpec((None, blk, kvw), lambda b, i: (b, i, COL_SV // kvw)),
            pl.BlockSpec((None, SWA_HEADS, blk, 2 * blk), lambda b, i: (jnp.where(i == 0, 1, 0), 0, 0, 0)),
            pl.BlockSpec((1, qw), lambda b, i: (0, 0)),
            pl.BlockSpec((1, kvw), lambda b, i: (0, 0)),
            pl.BlockSpec((qw, LANES), lambda b, i: (0, 0)),
            pl.BlockSpec((LANES, qw), lambda b, i: (0, 0)),
        ],
        out_specs=pl.BlockSpec((None, blk, qw), lambda b, i: (b, i, 0)),
        out_shape=jax.ShapeDtypeStruct((B, T, qw), BF16),
        compiler_params=_params(("arbitrary", "arbitrary")),
        name="swa",
    )(sinks, proj, proj, proj, proj, proj, bias, jnp.tile(qg, (1, SWA_HEADS)),
      jnp.tile(kg, (1, SWA_KV_HEADS)), seg, seg.T)


def _merge_kernel(x_ref, ga_ref, gb_ref, go_ref, so_ref, wa_ref, wb_ref, wo_ref, g1_ref,
                  n2_ref, sc2_ref, sh2_ref, we_ref, x1_ref, h2_ref, st_ref):
    ya = jnp.dot(go_ref[...], wa_ref[...], preferred_element_type=F32)
    yb = jnp.dot(so_ref[...], wb_ref[...], preferred_element_type=F32)
    m = jax.nn.sigmoid(ga_ref[...].astype(F32)) * ya + jax.nn.sigmoid(gb_ref[...].astype(F32)) * yb
    mixed = jnp.dot(m.astype(BF16), wo_ref[...], preferred_element_type=F32)
    x1 = x_ref[...] + g1_ref[0] * mixed
    x1_ref[...] = x1
    y = x1 * lax.rsqrt(jnp.mean(x1 * x1, axis=-1, keepdims=True) + EPS) * n2_ref[...]
    h2 = y * (1.0 + sc2_ref[0]) + sh2_ref[0]
    h2_ref[...] = h2
    st_ref[...] = lax.dot_general(we_ref[...], h2.astype(BF16), NT_DIMS, preferred_element_type=F32)


def _merge(xf, proj2, gla_o, swa_o, wa, wb, wo, g1, n2, sc2, sh2, weff_t, T, tm=256):
    N, D = xf.shape
    B = g1.shape[0]
    SW = weff_t.shape[0]
    bat = lambda i: ((i * tm) // T, 0, 0)
    full = lambda i: (0, 0)
    return pl.pallas_call(
        _merge_kernel,
        grid=(N // tm,),
        in_specs=[
            pl.BlockSpec((tm, D), lambda i: (i, 0)),
            pl.BlockSpec((tm, D), lambda i: (i, COL_GA // D)),
            pl.BlockSpec((tm, D), lambda i: (i, COL_GB // D)),
            pl.BlockSpec((tm, D), lambda i: (i, 0)),
            pl.BlockSpec((tm, D), lambda i: (i, 0)),
            pl.BlockSpec((D, D), full),
            pl.BlockSpec((D, D), full),
            pl.BlockSpec((D, D), full),
            pl.BlockSpec((1, 1, D), bat),
            pl.BlockSpec((1, D), full),
            pl.BlockSpec((1, 1, D), bat),
            pl.BlockSpec((1, 1, D), bat),
            pl.BlockSpec((SW, D), full),
        ],
        out_specs=[
            pl.BlockSpec((tm, D), lambda i: (i, 0)),
            pl.BlockSpec((tm, D), lambda i: (i, 0)),
            pl.BlockSpec((SW, tm), lambda i: (0, i)),
        ],
        out_shape=[
            jax.ShapeDtypeStruct((N, D), F32),
            jax.ShapeDtypeStruct((N, D), F32),
            jax.ShapeDtypeStruct((SW, N), F32),
        ],
        compiler_params=_params(("arbitrary",), VMEM_LIMIT_BYTES),
        name="merge",
    )(xf, proj2, proj2, gla_o, swa_o, wa, wb, wo, g1.reshape(B, 1, D), n2.reshape(1, D),
      sc2.reshape(B, 1, D), sh2.reshape(B, 1, D), weff_t)


def _tree(op, xs):
    xs = list(xs)
    while len(xs) > 1:
        xs = [op(xs[i], xs[i + 1]) for i in range(0, len(xs) - 1, 2)] + ([xs[-1]] if len(xs) % 2 else [])
    return xs[0]


def _extract_best(ref, n):
    m = _tree(jnp.maximum, [ref[i] for i in range(n)])
    pos = _tree(jnp.minimum, [jnp.where(ref[i] == m, i, n) for i in range(n)])
    for i in range(n):
        ref[i] = jnp.where(pos == i, -jnp.inf, ref[i])
    return m, pos


TOPK_PAIRS = tuple((a, b) for a in range(PEER_TOPK) for b in range(PEER_TOPK)
                   if (a + 1) * (b + 1) <= PEER_TOPK)


def _topk_kernel(s_ref, e_ref, g_ref, work_ref, v_ref, i_ref, cand_ref, cidx_ref, best_ref, row_ref):
    K = PEER_TOPK
    nk = PEER_NKEYS
    H = PEER_HEADS
    work_ref[...] = s_ref[...].reshape(work_ref.shape)

    def stage1(k, _):
        m, pos = _extract_best(work_ref, nk)
        v_ref[k] = m
        i_ref[k] = pos
        return 0

    lax.fori_loop(0, K, stage1, 0)

    for ci, (a, b) in enumerate(TOPK_PAIRS):
        cand_ref[ci] = v_ref[a, 0:H, :] + v_ref[b, H:2 * H, :]
        cidx_ref[ci] = (i_ref[a, 0:H, :] * nk + i_ref[b, H:2 * H, :]) * WORDS_PER_EXPERT
    ncand = len(TOPK_PAIRS)

    def stage2(k, _):
        m, pos = _extract_best(cand_ref, ncand)
        row = _tree(jnp.maximum, [jnp.where(pos == ci, cidx_ref[ci], -1) for ci in range(ncand)])
        best_ref[k] = m
        row_ref[k] = row
        return 0

    lax.fori_loop(0, K, stage2, 0)
    best = best_ref[...]
    ex = jnp.exp(best - best[0:1])
    gates = ex / jnp.sum(ex, axis=0, keepdims=True)
    tm = s_ref.shape[1]
    g_ref[...] = gates.reshape(K * H, tm).T
    e_ref[...] = row_ref[...].reshape(K * H, tm).astype(F32).T.astype(I32)


def _topk(s_t, tm=128):
    SW, N = s_t.shape
    K = PEER_TOPK
    G = 2 * PEER_HEADS
    return pl.pallas_call(
        _topk_kernel,
        grid=(N // tm,),
        in_specs=[pl.BlockSpec((SW, tm), lambda i: (0, i))],
        out_specs=[
            pl.BlockSpec((tm, PEER_SLOTS), lambda i: (i, 0)),
            pl.BlockSpec((tm, PEER_SLOTS), lambda i: (i, 0)),
        ],
        out_shape=[
            jax.ShapeDtypeStruct((N, PEER_SLOTS), I32),
            jax.ShapeDtypeStruct((N, PEER_SLOTS), F32),
        ],
        scratch_shapes=[pltpu.VMEM((PEER_NKEYS, G, tm), F32),
                        pltpu.VMEM((K, G, tm), F32),
                        pltpu.VMEM((K, G, tm), I32),
                        pltpu.VMEM((len(TOPK_PAIRS), PEER_HEADS, tm), F32),
                        pltpu.VMEM((len(TOPK_PAIRS), PEER_HEADS, tm), I32),
                        pltpu.VMEM((K, PEER_HEADS, tm), F32),
                        pltpu.VMEM((K, PEER_HEADS, tm), I32)],
        compiler_params=_params(("arbitrary",)),
        name="peer_topk",
    )(s_t)


WORDS_PER_EXPERT = SUBLANES // 2


TILE_BLOCK = tuple((r % 2) * WORDS_PER_EXPERT + r // 2 for r in range(SUBLANES))
BLOCK_ROW = tuple(TILE_BLOCK.index(q) for q in range(SUBLANES))


def _bf16_bits(x):
    return lax.bitcast_convert_type(x.astype(BF16).astype(F32), jnp.uint32)


def _table_kernel(t_ref, o_ref):
    rows = t_ref.shape[0]
    for s in range(WORDS_PER_EXPERT):
        lo = _bf16_bits(t_ref[:, s * LANES:(s + 1) * LANES])
        hi = _bf16_bits(t_ref[:, (s + WORDS_PER_EXPERT) * LANES:(s + WORDS_PER_EXPERT + 1) * LANES])
        word = lax.shift_right_logical(lo, jnp.uint32(16)) | (hi & jnp.uint32(0xFFFF0000))
        o_ref[pl.ds(s, rows, stride=WORDS_PER_EXPERT), :] = lax.bitcast_convert_type(word, I32)


def _table(t, te=512):
    n, d = t.shape
    return pl.pallas_call(
        _table_kernel,
        grid=(n // te,),
        in_specs=[pl.BlockSpec((te, d), lambda i: (i, 0))],
        out_specs=pl.BlockSpec((te * WORDS_PER_EXPERT, LANES), lambda i: (i, 0)),
        out_shape=jax.ShapeDtypeStruct((n * WORDS_PER_EXPERT, LANES), I32),
        compiler_params=_params(("arbitrary",)),
        name="peer_table",
    )(t)


def _expert_tile(tab_ref, e4):
    words = tab_ref[pl.ds(pl.multiple_of(e4, WORDS_PER_EXPERT), WORDS_PER_EXPERT), :]
    return pltpu.bitcast(words, BF16).astype(F32)


def _to_tile_rows(x):
    return jnp.concatenate([x[q:q + 1, :] for q in TILE_BLOCK], axis=0)


def _from_tile_rows(x):
    return jnp.concatenate([x[r:r + 1, :] for r in BLOCK_ROW], axis=0)


BITREV = (0, 4, 2, 6, 1, 5, 3, 7)


def _fold8(ps, sub):
    m4 = sub < 4
    m2 = (sub & 3) < 2
    m1 = (sub & 1) < 1
    s1 = []
    for a, b in zip(ps[0::2], ps[1::2]):
        s1.append(jnp.where(m4, a, b) + pltpu.roll(jnp.where(m4, b, a), 4, 0))
    s2 = []
    for a, b in zip(s1[0::2], s1[1::2]):
        s2.append(jnp.where(m2, a, pltpu.roll(b, 2, 0)) + jnp.where(m2, pltpu.roll(a, 6, 0), b))
    a, b = s2
    return jnp.where(m1, a, pltpu.roll(b, 1, 0)) + jnp.where(m1, pltpu.roll(a, 7, 0), b)


def _gelu_tanh(x):
    return 0.5 * x * (1.0 + jnp.tanh(np.sqrt(2.0 / np.pi) * (x + 0.044715 * (x * x * x))))


def _row_sums(x):
    ones = jnp.ones((LANES, LANES), BF16)
    hi = x.astype(BF16)
    lo = (x - hi.astype(F32)).astype(BF16)
    return (jnp.dot(hi, ones, preferred_element_type=F32)
            + jnp.dot(lo, ones, preferred_element_type=F32))


RING_SLOTS = 4
SLOT_TOKENS = 4


def _for_each_token(e_ref, idx_ref, sem, tb, token_body):
    nquads = tb // SLOT_TOKENS

    def fetch(q, slot):
        return pltpu.make_async_copy(e_ref.at[q], idx_ref.at[slot], sem.at[slot])

    for slot in range(RING_SLOTS):
        fetch(slot, slot).start()

    def ring(j, _):
        for slot in range(RING_SLOTS):
            q = j * RING_SLOTS + slot
            fetch(q, slot).wait()
            for tt in range(SLOT_TOKENS):
                token_body(q * SLOT_TOKENS + tt, lambda k, slot=slot, tt=tt: idx_ref[slot, tt, k])
            fetch(jnp.minimum(q + RING_SLOTS, nquads - 1), slot).start()
        return 0

    lax.fori_loop(0, nquads // RING_SLOTS, ring, 0)
    for slot in range(RING_SLOTS):
        fetch(nquads - 1, slot).wait()


def _peer_u_kernel(e_ref, h_ref, g_ref, tab_ref, o_ref, slab_ref, *, tb):
    sub = lax.broadcasted_iota(I32, (SUBLANES, LANES), 0)
    groups_per_token = PEER_SLOTS // SUBLANES

    def token(t, _):
        h = _to_tile_rows(h_ref[t])
        for j in range(groups_per_token):
            ps = [_expert_tile(tab_ref, e_ref[t, j * SUBLANES + BITREV[qn]]) * h
                  for qn in range(SUBLANES)]
            dst = pl.multiple_of(t * PEER_SLOTS + j * SUBLANES, SUBLANES)
            slab_ref[pl.ds(dst, SUBLANES), :] = _fold8(ps, sub)
        return 0

    lax.fori_loop(0, tb, token, 0)
    rio = lax.broadcasted_iota(I32, (LANES, LANES), 0)
    lio = lax.broadcasted_iota(I32, (LANES, LANES), 1)
    diag = (rio == lio)[None]
    rows = SUBLANES * PEER_SLOTS
    for gi in range(tb // SUBLANES):
        tok = slice(gi * SUBLANES, (gi + 1) * SUBLANES)
        rs = jnp.sum(slab_ref[gi * rows:(gi + 1) * rows, :], axis=-1, keepdims=True)
        rs = rs.reshape(SUBLANES, PEER_SLOTS, 1)
        a = jnp.sum(jnp.where(diag, rs, 0.0), axis=1)
        o_ref[tok, :] = g_ref[tok, :] * _gelu_tanh(a)


def _peer_u(e4, h3, gates, tab, tb=64):
    N = h3.shape[0]
    assert tb % SUBLANES == 0
    return pl.pallas_call(
        functools.partial(_peer_u_kernel, tb=tb),
        grid=(N // tb,),
        in_specs=[
            pl.BlockSpec((tb, PEER_SLOTS), lambda i: (i, 0), memory_space=pltpu.SMEM),
            pl.BlockSpec((tb, SUBLANES, LANES), lambda i: (i, 0, 0)),
            pl.BlockSpec((tb, PEER_SLOTS), lambda i: (i, 0)),
            pl.BlockSpec(tab.shape, lambda i: (0, 0), pipeline_mode=pl.Buffered(1)),
        ],
        out_specs=pl.BlockSpec((tb, PEER_SLOTS), lambda i: (i, 0)),
        out_shape=jax.ShapeDtypeStruct((N, PEER_SLOTS), F32),
        scratch_shapes=[pltpu.VMEM((tb * PEER_SLOTS, LANES), F32)],
        compiler_params=_params(("arbitrary",), VMEM_LIMIT_BYTES),
        name="peer_u",
    )(e4, h3, gates, tab)


def _peer_v_kernel(e_ref, c_ref, x1_ref, g2_ref, tab_ref, o_ref, splat_ref, idx_ref, sem, *, tb):
    nacc = 3
    rio = lax.broadcasted_iota(I32, (LANES, LANES), 0)
    lio = lax.broadcasted_iota(I32, (LANES, LANES), 1)
    diag = (rio == lio)[None]
    for t8 in range(tb // SUBLANES):
        cc = c_ref[t8 * SUBLANES:(t8 + 1) * SUBLANES, :]
        s = jnp.sum(jnp.where(diag, cc[:, None, :], 0.0), axis=-1, keepdims=True)
        splat_ref[t8 * SUBLANES:(t8 + 1) * SUBLANES] = jnp.broadcast_to(s, (SUBLANES, LANES, LANES))

    def token(t, row):
        accs = [jnp.zeros((SUBLANES, LANES), F32) for _ in range(nacc)]
        for k in range(PEER_SLOTS):
            c = jnp.broadcast_to(splat_ref[t, k:k + 1, :], (SUBLANES, LANES))
            accs[k % nacc] = accs[k % nacc] + c * _expert_tile(tab_ref, row(k))
        y = _tree(jnp.add, accs)
        o_ref[t] = x1_ref[t] + g2_ref[0] * _from_tile_rows(y)

    _for_each_token(e_ref, idx_ref, sem, tb, token)


def _peer_v(e4, coef, x1_3, g2_3, tab, T, tb=64):
    N = x1_3.shape[0]
    assert tb % (RING_SLOTS * SLOT_TOKENS) == 0 and tb % SUBLANES == 0
    return pl.pallas_call(
        functools.partial(_peer_v_kernel, tb=tb),
        grid=(N // tb,),
        in_specs=[
            pl.BlockSpec((tb // SLOT_TOKENS, SLOT_TOKENS, PEER_SLOTS), lambda i: (i, 0, 0)),
            pl.BlockSpec((tb, PEER_SLOTS), lambda i: (i, 0)),
            pl.BlockSpec((tb, SUBLANES, LANES), lambda i: (i, 0, 0)),
            pl.BlockSpec((1, SUBLANES, LANES), lambda i: ((i * tb) // T, 0, 0)),
            pl.BlockSpec(tab.shape, lambda i: (0, 0), pipeline_mode=pl.Buffered(1)),
        ],
        out_specs=pl.BlockSpec((tb, SUBLANES, LANES), lambda i: (i, 0, 0)),
        out_shape=jax.ShapeDtypeStruct((N, SUBLANES, LANES), F32),
        scratch_shapes=[pltpu.VMEM((tb, LANES, LANES), F32),
                        pltpu.SMEM((RING_SLOTS, SLOT_TOKENS, PEER_SLOTS), I32),
                        pltpu.SemaphoreType.DMA((RING_SLOTS,))],
        compiler_params=_params(("arbitrary",), VMEM_LIMIT_BYTES),
        name="peer_v",
    )(e4.reshape(N // SLOT_TOKENS, SLOT_TOKENS, PEER_SLOTS), coef, x1_3, g2_3, tab)


def kernel(x, c, w_ada, b_ada, norm1_g, w_in, gla_gate_w2, gla_gate_b, gla_norm_g, swa_qnorm_g,
           swa_knorm_g, swa_sinks, rel_bias, w_up_a, w_up_b, w_out, norm2_g, peer_wq, peer_subkeys,
           peer_u, peer_v):
    B, T, D = x.shape
    N = B * T
    L = w_ada.shape[0]
    mod = _adaln(c, w_ada, b_ada)
    weff_t = _fold_peer_keys(peer_wq, peer_subkeys)
    weff_t = (weff_t.reshape(L, PEER_HEADS, 2, PEER_NKEYS, D).transpose(0, 3, 2, 1, 4)
              .reshape(L, 2 * PEER_HEADS * PEER_NKEYS, D))
    bias = _swa_bias(rel_bias)
    xf = x.reshape(N, D)
    for l in range(L):
        sh1, sc1, g1, sh2, sc2, g2 = [mod[l, :, i * D:(i + 1) * D] for i in range(6)]
        proj = _in_proj(xf, norm1_g[l], sc1, sh1, _pack_w_in(w_in[l]), T)
        proj3 = proj.reshape(B, T, PROJ_W)
        w2p = jnp.zeros((LANES, GLA_HEADS * GLA_DK), BF16).at[:GLA_RANK].set(gla_gate_w2[l].astype(BF16))
        gla_o = _gla(proj3, w2p, gla_gate_b[l].reshape(1, -1), gla_norm_g[l].reshape(1, -1))
        swa_o = _swa(proj3, bias, swa_qnorm_g[l].reshape(1, -1), swa_knorm_g[l].reshape(1, -1),
                     swa_sinks[l])
        x1, h2, s_t = _merge(xf, proj, gla_o.reshape(N, -1), swa_o.reshape(N, -1),
                             w_up_a[l].astype(BF16), w_up_b[l].astype(BF16), w_out[l].astype(BF16),
                             g1, norm2_g[l], sc2, sh2, weff_t[l], T)
        e4, gates = _topk(s_t)
        coef = _peer_u(e4, h2.reshape(N, SUBLANES, LANES), gates, _table(peer_u[l]))
        g2_3 = g2.reshape(B, SUBLANES, LANES)
        xf = _peer_v(e4, coef, x1.reshape(N, SUBLANES, LANES), g2_3, _table(peer_v[l]), T).reshape(N, D)
    return xf.reshape(B, T, D)
```

```python
import functools

import numpy as np
import jax
import jax.numpy as jnp
from jax import lax
from jax.experimental import pallas as pl
from jax.experimental.pallas import tpu as pltpu

F32 = jnp.float32
BF16 = jnp.bfloat16
I32 = jnp.int32
HIGHEST = lax.Precision.HIGHEST
EPS = 1e-6

GLA_HEADS = 4
GLA_DK = 128
GLA_DV = 256
GLA_RANK = 16
GLA_TAU = 16.0
GLA_CHUNK = 64
SWA_HEADS = 16
SWA_KV_HEADS = 2
SWA_HD = 64
SWA_BLOCK = 128
N_BUCKETS = 32
MAX_DISTANCE = 128
PEER_HEADS = 8
PEER_NKEYS = 128
PEER_TOPK = 16
PEER_SLOTS = PEER_HEADS * PEER_TOPK

SUBLANES = 8
LANES = 128
VMEM_LIMIT_BYTES = 56 * 1024 * 1024

NT_DIMS = (((1,), (1,)), ((), ()))
TN_DIMS = (((0,), (0,)), ((), ()))

COL_Q, COL_K, COL_V, COL_GR, COL_SQ, COL_GA, COL_GB = 0, 512, 1024, 2048, 3072, 4096, 5120
COL_SK, COL_SV, COL_GLR = 6144, 6272, 6400
PROJ_W = 6528


def _params(sem, vmem=None):
    return pltpu.CompilerParams(dimension_semantics=sem, vmem_limit_bytes=vmem)


def _adaln_kernel(c_ref, w_ref, b_ref, o_ref):
    c = c_ref[...]
    a = c * jax.nn.sigmoid(c)
    o_ref[0] = jnp.dot(a, w_ref[0], preferred_element_type=F32, precision=HIGHEST) + b_ref[0]


def _adaln(c, w_ada, b_ada):
    L, D, W = w_ada.shape
    B = c.shape[0]
    rows = -(-B // SUBLANES) * SUBLANES
    cp = jnp.zeros((rows, D), F32).at[:B].set(c)
    tn = W // 4
    out = pl.pallas_call(
        _adaln_kernel,
        grid=(L, W // tn),
        in_specs=[
            pl.BlockSpec((rows, D), lambda l, j: (0, 0)),
            pl.BlockSpec((1, D, tn), lambda l, j: (l, 0, j)),
            pl.BlockSpec((1, 1, tn), lambda l, j: (l, 0, j)),
        ],
        out_specs=pl.BlockSpec((1, rows, tn), lambda l, j: (l, 0, j)),
        out_shape=jax.ShapeDtypeStruct((L, rows, W), F32),
        compiler_params=_params(("arbitrary", "arbitrary")),
        name="adaln",
    )(cp, w_ada, b_ada.reshape(L, 1, W))
    return out[:, :B]


def _fold_kernel(sk_ref, wq_ref, o_ref, rows_ref):
    half = sk_ref.shape[-1]
    groups = 2 * PEER_HEADS
    for h in range(PEER_HEADS):
        for p in range(2):
            g = 2 * h + p
            res = lax.dot_general(sk_ref[0, p], wq_ref[0, :, g * half:(g + 1) * half], NT_DIMS,
                                  precision=HIGHEST, preferred_element_type=F32)
            for cb in range(rows_ref.shape[0]):
                rows_ref[cb, pl.ds(p * PEER_HEADS + h, PEER_NKEYS, stride=groups), :] = (
                    res[:, cb * LANES:(cb + 1) * LANES])
    for cb in range(rows_ref.shape[0]):
        o_ref[0, :, cb * LANES:(cb + 1) * LANES] = rows_ref[cb].astype(BF16)


def _fold_peer_keys(peer_wq, peer_subkeys):
    L, D, QW = peer_wq.shape
    half = peer_subkeys.shape[-1]
    rows = (QW // half) * PEER_NKEYS
    return pl.pallas_call(
        _fold_kernel,
        grid=(L,),
        in_specs=[
            pl.BlockSpec((1, 2, PEER_NKEYS, half), lambda l: (l, 0, 0, 0)),
            pl.BlockSpec((1, D, QW), lambda l: (l, 0, 0)),
        ],
        out_specs=pl.BlockSpec((1, rows, D), lambda l: (l, 0, 0)),
        out_shape=jax.ShapeDtypeStruct((L, rows, D), BF16),
        scratch_shapes=[pltpu.VMEM((D // LANES, rows, LANES), F32)],
        compiler_params=_params(("arbitrary",), VMEM_LIMIT_BYTES),
        name="peer_fold",
    )(peer_subkeys, peer_wq)


def _inproj_kernel(x_ref, g_ref, sc_ref, sh_ref, w_ref, o_ref):
    x = x_ref[...]
    ms = jnp.mean(x * x, axis=-1, keepdims=True)
    y = x * lax.rsqrt(ms + EPS) * g_ref[...]
    h = y * (1.0 + sc_ref[0]) + sh_ref[0]
    o_ref[...] = jnp.dot(h.astype(BF16), w_ref[...], preferred_element_type=F32).astype(BF16)


def _in_proj(xf, g, sc, sh, wp, T, tm=512):
    N, D = xf.shape
    B = sc.shape[0]
    ncol = 3
    tn = PROJ_W // ncol
    return pl.pallas_call(
        _inproj_kernel,
        grid=(ncol, N // tm),
        in_specs=[
            pl.BlockSpec((tm, D), lambda j, i: (i, 0)),
            pl.BlockSpec((1, D), lambda j, i: (0, 0)),
            pl.BlockSpec((1, 1, D), lambda j, i: ((i * tm) // T, 0, 0)),
            pl.BlockSpec((1, 1, D), lambda j, i: ((i * tm) // T, 0, 0)),
            pl.BlockSpec((D, tn), lambda j, i: (0, j)),
        ],
        out_specs=pl.BlockSpec((tm, tn), lambda j, i: (i, j)),
        out_shape=jax.ShapeDtypeStruct((N, PROJ_W), BF16),
        compiler_params=_params(("arbitrary", "arbitrary")),
        name="in_proj",
    )(xf, g.reshape(1, D), sc.reshape(B, 1, D), sh.reshape(B, 1, D), wp)


def _pack_w_in(w):
    o = np.cumsum([0, 512, 512, 1024, GLA_RANK, 1024, 1024, 128, 128, 1024, 1024])
    q, k, v, glr, gr, sq, sk, sv, ga, gb = [w[:, o[i]:o[i + 1]] for i in range(10)]
    pad = jnp.zeros((w.shape[0], LANES - GLA_RANK), w.dtype)
    return jnp.concatenate([q, k, v, gr, sq, ga, gb, sk, sv, glr, pad], axis=1).astype(BF16)


def _gla_kernel(q_ref, k_ref, v_ref, r_ref, glr_ref, w2_ref, b2_ref, ng_ref, o_ref, st_ref, *, nchunk):
    @pl.when(pl.program_id(1) == 0)
    def _():
        st_ref[...] = jnp.zeros_like(st_ref)

    C = GLA_CHUNK
    dk, dv = GLA_DK, GLA_DV
    row = lax.broadcasted_iota(I32, (C, C), 0)
    col = lax.broadcasted_iota(I32, (C, C), 1)
    tri = col <= row
    tri_f = tri.astype(F32)
    w2 = w2_ref[...]
    b2 = b2_ref[...]
    ng = ng_ref[...]
    for ci in range(nchunk):
        sl = pl.ds(ci * C, C)
        z = jnp.dot(glr_ref[sl, :], w2, preferred_element_type=F32) + b2
        log_a = (jnp.minimum(z, 0.0) - jnp.log(1.0 + jnp.exp(-jnp.abs(z)))) * (1.0 / GLA_TAU)
        b = jnp.dot(tri_f, log_a, preferred_element_type=F32, precision=HIGHEST)
        b_last = b[C - 1:C, :]
        q = q_ref[sl, :].astype(F32) * (dk ** -0.5)
        k = k_ref[sl, :].astype(F32)
        q_dec = (q * jnp.exp(b)).astype(BF16)
        k_inv = (k * jnp.exp(-b)).astype(BF16)
        k_tail = (k * jnp.exp(b_last - b)).astype(BF16)
        decay = jnp.exp(b_last)
        for h in range(GLA_HEADS):
            kc = slice(h * dk, (h + 1) * dk)
            vc = slice(h * dv, (h + 1) * dv)
            v = v_ref[sl, vc]
            attn = lax.dot_general(q_dec[:, kc], k_inv[:, kc], NT_DIMS, preferred_element_type=F32)
            attn = jnp.where(tri, attn, 0.0).astype(BF16)
            st = st_ref[h]
            o = (jnp.dot(attn, v, preferred_element_type=F32)
                 + lax.dot_general(q_dec[:, kc], st.astype(BF16), NT_DIMS, preferred_element_type=F32))
            st_ref[h] = st * decay[:, kc] + lax.dot_general(v, k_tail[:, kc], TN_DIMS,
                                                            preferred_element_type=F32)
            on = o * lax.rsqrt(jnp.mean(o * o, axis=-1, keepdims=True) + EPS) * ng
            r = r_ref[sl, vc].astype(F32)
            o_ref[sl, vc] = (on * (r * jax.nn.sigmoid(r))).astype(BF16)


def _gla(proj, w2p, b2, ng, tc=256):
    B, T, _ = proj.shape
    H = GLA_HEADS
    kw, vw = H * GLA_DK, H * GLA_DV
    return pl.pallas_call(
        functools.partial(_gla_kernel, nchunk=tc // GLA_CHUNK),
        grid=(B, T // tc),
        in_specs=[
            pl.BlockSpec((None, tc, kw), lambda b, c: (b, c, COL_Q // kw)),
            pl.BlockSpec((None, tc, kw), lambda b, c: (b, c, COL_K // kw)),
            pl.BlockSpec((None, tc, vw), lambda b, c: (b, c, COL_V // vw)),
            pl.BlockSpec((None, tc, vw), lambda b, c: (b, c, COL_GR // vw)),
            pl.BlockSpec((None, tc, LANES), lambda b, c: (b, c, COL_GLR // LANES)),
            pl.BlockSpec((LANES, kw), lambda b, c: (0, 0)),
            pl.BlockSpec((1, kw), lambda b, c: (0, 0)),
            pl.BlockSpec((1, GLA_DV), lambda b, c: (0, 0)),
        ],
        out_specs=pl.BlockSpec((None, tc, vw), lambda b, c: (b, c, 0)),
        out_shape=jax.ShapeDtypeStruct((B, T, vw), BF16),
        scratch_shapes=[pltpu.VMEM((H, GLA_DV, GLA_DK), F32)],
        compiler_params=_params(("arbitrary", "arbitrary")),
        name="gla",
    )(proj, proj, proj, proj, proj, w2p, b2, ng)


def _t5_bucket(dist):
    max_exact = N_BUCKETS // 2
    d = np.maximum(dist, 1).astype(np.float32)
    large = max_exact + (np.log(d / max_exact) / np.log(MAX_DISTANCE / max_exact)
                         * (N_BUCKETS - max_exact)).astype(np.int32)
    large = np.minimum(large, N_BUCKETS - 1)
    return np.where(dist < max_exact, dist, large).astype(np.int32)


NEG_BIG = -1e30


def _swa_bias(rel_bias):
    blk = SWA_BLOCK
    qi = np.arange(blk)[:, None]
    sj = np.arange(2 * blk)[None, :]
    dist = blk + qi - sj
    band = (dist >= 0) & (dist < blk)
    bucket = _t5_bucket(np.clip(dist, 0, None))
    bias = rel_bias[bucket].astype(F32).transpose(2, 0, 1)
    masks = np.stack([band, band & (sj >= blk)])[:, None]
    return jnp.where(jnp.asarray(masks), bias[None], NEG_BIG)


def _segment_sums(x, seg):
    hi = x.astype(BF16)
    lo = (x - hi.astype(F32)).astype(BF16)
    return (jnp.dot(hi, seg, preferred_element_type=F32) + jnp.dot(lo, seg, preferred_element_type=F32))


def _head_rms_scale(x, seg, seg_t):
    inv = lax.rsqrt(_segment_sums(x * x, seg) * (1.0 / SWA_HD) + EPS)
    return _segment_sums(inv, seg_t)


def _swa_kernel(sink_ref, q_ref, kp_ref, kc_ref, vp_ref, vc_ref, bias_ref, qg_ref, kg_ref, seg_ref,
                segt_ref, o_ref):
    blk = SWA_BLOCK
    hd = SWA_HD
    group = SWA_HEADS // SWA_KV_HEADS
    kvw = SWA_KV_HEADS * hd
    seg = seg_ref[...]
    seg_t = segt_ref[...]
    q = q_ref[...].astype(F32)
    qn = (q * _head_rms_scale(q, seg, seg_t) * qg_ref[...] * (hd ** -0.5)).astype(BF16)
    k2 = jnp.concatenate([kp_ref[...], kc_ref[...]], axis=0).astype(F32)
    kn = (k2 * _head_rms_scale(k2, seg[:kvw], seg_t[:, :kvw]) * kg_ref[...]).astype(BF16)
    v2 = jnp.concatenate([vp_ref[...], vc_ref[...]], axis=0)
    lane = lax.broadcasted_iota(I32, (2 * blk, kvw), 1)
    outs = []
    for kh in range(SWA_KV_HEADS):
        kk = kn[:, kh * hd:(kh + 1) * hd]
        vsh = v2 if kh == 0 else jnp.concatenate([v2[:, kh * hd:], v2[:, :kh * hd]], axis=1)
        vv = jnp.where(lane < hd, vsh, jnp.ones_like(vsh))
        qs = jnp.concatenate([qn[:, (kh * group + g) * hd:(kh * group + g + 1) * hd]
                              for g in range(group)], axis=0)
        logits = lax.dot_general(qs, kk, NT_DIMS, preferred_element_type=F32)
        logits = logits + bias_ref[kh * group:(kh + 1) * group].reshape(group * blk, 2 * blk)
        for g in range(group):
            lg = logits[g * blk:(g + 1) * blk]
            sink = sink_ref[kh * group + g]
            m = jnp.maximum(jnp.max(lg, axis=-1, keepdims=True), sink)
            p = jnp.exp(lg - m).astype(BF16)
            pv = jnp.dot(p, vv, preferred_element_type=F32)
            denom = pv[:, hd:hd + 1] + jnp.exp(sink - m)
            outs.append(pv[:, :hd] / denom)
    o_ref[...] = jnp.concatenate(outs, axis=-1).astype(BF16)


def _swa(proj, bias, qg, kg, sinks):
    B, T, _ = proj.shape
    blk = SWA_BLOCK
    qw = SWA_HEADS * SWA_HD
    kvw = SWA_KV_HEADS * SWA_HD
    prev = lambda b, i: jnp.maximum(i - 1, 0)
    seg = (np.arange(qw)[:, None] // SWA_HD == np.arange(LANES)[None, :])
    seg = jnp.asarray(seg, BF16)
    return pl.pallas_call(
        _swa_kernel,
        grid=(B, T // blk),
        in_specs=[
            pl.BlockSpec(memory_space=pltpu.SMEM),
            pl.BlockSpec((None, blk, qw), lambda b, i: (b, i, COL_SQ // qw)),
            pl.BlockSpec((None, blk, kvw), lambda b, i: (b, prev(b, i), COL_SK // kvw)),
            pl.BlockSpec((None, blk, kvw), lambda b, i: (b, i, COL_SK // kvw)),
            pl.BlockSpec((None, blk, kvw), lambda b, i: (b, prev(b, i), COL_SV // kvw)),
            pl.BlockSpec((None, blk, kvw), lambda b, i: (b, i, COL_SV // kvw)),
            pl.BlockSpec((None, SWA_HEADS, blk, 2 * blk), lambda b, i: (jnp.where(i == 0, 1, 0), 0, 0, 0)),
            pl.BlockSpec((1, qw), lambda b, i: (0, 0)),
            pl.BlockSpec((1, kvw), lambda b, i: (0, 0)),
            pl.BlockSpec((qw, LANES), lambda b, i: (0, 0)),
            pl.BlockSpec((LANES, qw), lambda b, i: (0, 0)),
        ],
        out_specs=pl.BlockSpec((None, blk, qw), lambda b, i: (b, i, 0)),
        out_shape=jax.ShapeDtypeStruct((B, T, qw), BF16),
        compiler_params=_params(("arbitrary", "arbitrary")),
        name="swa",
    )(sinks, proj, proj, proj, proj, proj, bias, jnp.tile(qg, (1, SWA_HEADS)),
      jnp.tile(kg, (1, SWA_KV_HEADS)), seg, seg.T)


def _merge_kernel(x_ref, ga_ref, gb_ref, go_ref, so_ref, wa_ref, wb_ref, wo_ref, g1_ref,
                  n2_ref, sc2_ref, sh2_ref, we_ref, x1_ref, h2_ref, st_ref):
    ya = jnp.dot(go_ref[...], wa_ref[...], preferred_element_type=F32)
    yb = jnp.dot(so_ref[...], wb_ref[...], preferred_element_type=F32)
    m = jax.nn.sigmoid(ga_ref[...].astype(F32)) * ya + jax.nn.sigmoid(gb_ref[...].astype(F32)) * yb
    mixed = jnp.dot(m.astype(BF16), wo_ref[...], preferred_element_type=F32)
    x1 = x_ref[...] + g1_ref[0] * mixed
    x1_ref[...] = x1
    y = x1 * lax.rsqrt(jnp.mean(x1 * x1, axis=-1, keepdims=True) + EPS) * n2_ref[...]
    h2 = y * (1.0 + sc2_ref[0]) + sh2_ref[0]
    h2_ref[...] = h2
    st_ref[...] = lax.dot_general(we_ref[...], h2.astype(BF16), NT_DIMS, preferred_element_type=F32)


def _merge(xf, proj2, gla_o, swa_o, wa, wb, wo, g1, n2, sc2, sh2, weff_t, T, tm=256):
    N, D = xf.shape
    B = g1.shape[0]
    SW = weff_t.shape[0]
    bat = lambda i: ((i * tm) // T, 0, 0)
    full = lambda i: (0, 0)
    return pl.pallas_call(
        _merge_kernel,
        grid=(N // tm,),
        in_specs=[
            pl.BlockSpec((tm, D), lambda i: (i, 0)),
            pl.BlockSpec((tm, D), lambda i: (i, COL_GA // D)),
            pl.BlockSpec((tm, D), lambda i: (i, COL_GB // D)),
            pl.BlockSpec((tm, D), lambda i: (i, 0)),
            pl.BlockSpec((tm, D), lambda i: (i, 0)),
            pl.BlockSpec((D, D), full),
            pl.BlockSpec((D, D), full),
            pl.BlockSpec((D, D), full),
            pl.BlockSpec((1, 1, D), bat),
            pl.BlockSpec((1, D), full),
            pl.BlockSpec((1, 1, D), bat),
            pl.BlockSpec((1, 1, D), bat),
            pl.BlockSpec((SW, D), full),
        ],
        out_specs=[
            pl.BlockSpec((tm, D), lambda i: (i, 0)),
            pl.BlockSpec((tm, D), lambda i: (i, 0)),
            pl.BlockSpec((SW, tm), lambda i: (0, i)),
        ],
        out_shape=[
            jax.ShapeDtypeStruct((N, D), F32),
            jax.ShapeDtypeStruct((N, D), F32),
            jax.ShapeDtypeStruct((SW, N), F32),
        ],
        compiler_params=_params(("arbitrary",), VMEM_LIMIT_BYTES),
        name="merge",
    )(xf, proj2, proj2, gla_o, swa_o, wa, wb, wo, g1.reshape(B, 1, D), n2.reshape(1, D),
      sc2.reshape(B, 1, D), sh2.reshape(B, 1, D), weff_t)


def _tree(op, xs):
    xs = list(xs)
    while len(xs) > 1:
        xs = [op(xs[i], xs[i + 1]) for i in range(0, len(xs) - 1, 2)] + ([xs[-1]] if len(xs) % 2 else [])
    return xs[0]


def _extract_best(ref, n):
    m = _tree(jnp.maximum, [ref[i] for i in range(n)])
    pos = _tree(jnp.minimum, [jnp.where(ref[i] == m, i, n) for i in range(n)])
    for i in range(n):
        ref[i] = jnp.where(pos == i, -jnp.inf, ref[i])
    return m, pos


TOPK_PAIRS = tuple((a, b) for a in range(PEER_TOPK) for b in range(PEER_TOPK)
                   if (a + 1) * (b + 1) <= PEER_TOPK)


def _topk_kernel(s_ref, e_ref, g_ref, work_ref, v_ref, i_ref, cand_ref, cidx_ref, best_ref, row_ref):
    K = PEER_TOPK
    nk = PEER_NKEYS
    H = PEER_HEADS
    work_ref[...] = s_ref[...].reshape(work_ref.shape)

    def stage1(k, _):
        m, pos = _extract_best(work_ref, nk)
        v_ref[k] = m
        i_ref[k] = pos
        return 0

    lax.fori_loop(0, K, stage1, 0)

    for ci, (a, b) in enumerate(TOPK_PAIRS):
        cand_ref[ci] = v_ref[a, 0:H, :] + v_ref[b, H:2 * H, :]
        cidx_ref[ci] = (i_ref[a, 0:H, :] * nk + i_ref[b, H:2 * H, :]) * WORDS_PER_EXPERT
    ncand = len(TOPK_PAIRS)

    def stage2(k, _):
        m, pos = _extract_best(cand_ref, ncand)
        row = _tree(jnp.maximum, [jnp.where(pos == ci, cidx_ref[ci], -1) for ci in range(ncand)])
        best_ref[k] = m
        row_ref[k] = row
        return 0

    lax.fori_loop(0, K, stage2, 0)
    best = best_ref[...]
    ex = jnp.exp(best - best[0:1])
    gates = ex / jnp.sum(ex, axis=0, keepdims=True)
    tm = s_ref.shape[1]
    g_ref[...] = gates.reshape(K * H, tm).T
    e_ref[...] = row_ref[...].reshape(K * H, tm).astype(F32).T.astype(I32)


def _topk(s_t, tm=128):
    SW, N = s_t.shape
    K = PEER_TOPK
    G = 2 * PEER_HEADS
    return pl.pallas_call(
        _topk_kernel,
        grid=(N // tm,),
        in_specs=[pl.BlockSpec((SW, tm), lambda i: (0, i))],
        out_specs=[
            pl.BlockSpec((tm, PEER_SLOTS), lambda i: (i, 0)),
            pl.BlockSpec((tm, PEER_SLOTS), lambda i: (i, 0)),
        ],
        out_shape=[
            jax.ShapeDtypeStruct((N, PEER_SLOTS), I32),
            jax.ShapeDtypeStruct((N, PEER_SLOTS), F32),
        ],
        scratch_shapes=[pltpu.VMEM((PEER_NKEYS, G, tm), F32),
                        pltpu.VMEM((K, G, tm), F32),
                        pltpu.VMEM((K, G, tm), I32),
                        pltpu.VMEM((len(TOPK_PAIRS), PEER_HEADS, tm), F32),
                        pltpu.VMEM((len(TOPK_PAIRS), PEER_HEADS, tm), I32),
                        pltpu.VMEM((K, PEER_HEADS, tm), F32),
                        pltpu.VMEM((K, PEER_HEADS, tm), I32)],
        compiler_params=_params(("arbitrary",)),
        name="peer_topk",
    )(s_t)


WORDS_PER_EXPERT = SUBLANES // 2


TILE_BLOCK = tuple((r % 2) * WORDS_PER_EXPERT + r // 2 for r in range(SUBLANES))
BLOCK_ROW = tuple(TILE_BLOCK.index(q) for q in range(SUBLANES))


def _bf16_bits(x):
    return lax.bitcast_convert_type(x.astype(BF16).astype(F32), jnp.uint32)


def _table_kernel(t_ref, o_ref):
    rows = t_ref.shape[0]
    for s in range(WORDS_PER_EXPERT):
        lo = _bf16_bits(t_ref[:, s * LANES:(s + 1) * LANES])
        hi = _bf16_bits(t_ref[:, (s + WORDS_PER_EXPERT) * LANES:(s + WORDS_PER_EXPERT + 1) * LANES])
        word = lax.shift_right_logical(lo, jnp.uint32(16)) | (hi & jnp.uint32(0xFFFF0000))
        o_ref[pl.ds(s, rows, stride=WORDS_PER_EXPERT), :] = lax.bitcast_convert_type(word, I32)


def _table(t, te=512):
    n, d = t.shape
    return pl.pallas_call(
        _table_kernel,
        grid=(n // te,),
        in_specs=[pl.BlockSpec((te, d), lambda i: (i, 0))],
        out_specs=pl.BlockSpec((te * WORDS_PER_EXPERT, LANES), lambda i: (i, 0)),
        out_shape=jax.ShapeDtypeStruct((n * WORDS_PER_EXPERT, LANES), I32),
        compiler_params=_params(("arbitrary",)),
        name="peer_table",
    )(t)


def _expert_tile(tab_ref, e4):
    words = tab_ref[pl.ds(pl.multiple_of(e4, WORDS_PER_EXPERT), WORDS_PER_EXPERT), :]
    return pltpu.bitcast(words, BF16).astype(F32)


def _to_tile_rows(x):
    return jnp.concatenate([x[q:q + 1, :] for q in TILE_BLOCK], axis=0)


def _from_tile_rows(x):
    return jnp.concatenate([x[r:r + 1, :] for r in BLOCK_ROW], axis=0)


BITREV = (0, 4, 2, 6, 1, 5, 3, 7)


def _fold8(ps, sub):
    m4 = sub < 4
    m2 = (sub & 3) < 2
    m1 = (sub & 1) < 1
    s1 = []
    for a, b in zip(ps[0::2], ps[1::2]):
        s1.append(jnp.where(m4, a, b) + pltpu.roll(jnp.where(m4, b, a), 4, 0))
    s2 = []
    for a, b in zip(s1[0::2], s1[1::2]):
        s2.append(jnp.where(m2, a, pltpu.roll(b, 2, 0)) + jnp.where(m2, pltpu.roll(a, 6, 0), b))
    a, b = s2
    return jnp.where(m1, a, pltpu.roll(b, 1, 0)) + jnp.where(m1, pltpu.roll(a, 7, 0), b)


def _gelu_tanh(x):
    return 0.5 * x * (1.0 + jnp.tanh(np.sqrt(2.0 / np.pi) * (x + 0.044715 * (x * x * x))))


def _row_sums(x):
    ones = jnp.ones((LANES, LANES), BF16)
    hi = x.astype(BF16)
    lo = (x - hi.astype(F32)).astype(BF16)
    return (jnp.dot(hi, ones, preferred_element_type=F32)
            + jnp.dot(lo, ones, preferred_element_type=F32))


RING_SLOTS = 4
SLOT_TOKENS = 4


def _for_each_token(e_ref, idx_ref, sem, tb, token_body):
    nquads = tb // SLOT_TOKENS

    def fetch(q, slot):
        return pltpu.make_async_copy(e_ref.at[q], idx_ref.at[slot], sem.at[slot])

    for slot in range(RING_SLOTS):
        fetch(slot, slot).start()

    def ring(j, _):
        for slot in range(RING_SLOTS):
            q = j * RING_SLOTS + slot
            fetch(q, slot).wait()
            for tt in range(SLOT_TOKENS):
                token_body(q * SLOT_TOKENS + tt, lambda k, slot=slot, tt=tt: idx_ref[slot, tt, k],
                           slot * SLOT_TOKENS + tt)
            fetch(jnp.minimum(q + RING_SLOTS, nquads - 1), slot).start()
        return 0

    lax.fori_loop(0, nquads // RING_SLOTS, ring, 0)
    for slot in range(RING_SLOTS):
        fetch(nquads - 1, slot).wait()


def _peer_u_kernel(e_ref, h_ref, g_ref, tab_ref, o_ref, slab_ref, *, tb):
    sub = lax.broadcasted_iota(I32, (SUBLANES, LANES), 0)
    groups_per_token = PEER_SLOTS // SUBLANES

    def token(t, _):
        h = _to_tile_rows(h_ref[t])
        for j in range(groups_per_token):
            ps = [_expert_tile(tab_ref, e_ref[t, j * SUBLANES + BITREV[qn]]) * h
                  for qn in range(SUBLANES)]
            dst = pl.multiple_of(t * PEER_SLOTS + j * SUBLANES, SUBLANES)
            slab_ref[pl.ds(dst, SUBLANES), :] = _fold8(ps, sub)
        return 0

    lax.fori_loop(0, tb, token, 0)
    rio = lax.broadcasted_iota(I32, (LANES, LANES), 0)
    lio = lax.broadcasted_iota(I32, (LANES, LANES), 1)
    diag = (rio == lio)[None]
    rows = SUBLANES * PEER_SLOTS
    for gi in range(tb // SUBLANES):
        tok = slice(gi * SUBLANES, (gi + 1) * SUBLANES)
        rs = jnp.sum(slab_ref[gi * rows:(gi + 1) * rows, :], axis=-1, keepdims=True)
        rs = rs.reshape(SUBLANES, PEER_SLOTS, 1)
        a = jnp.sum(jnp.where(diag, rs, 0.0), axis=1)
        o_ref[tok, :] = g_ref[tok, :] * _gelu_tanh(a)


def _peer_u(e4, h3, gates, tab, tb=128):
    N = h3.shape[0]
    assert tb % SUBLANES == 0
    return pl.pallas_call(
        functools.partial(_peer_u_kernel, tb=tb),
        grid=(N // tb,),
        in_specs=[
            pl.BlockSpec((tb, PEER_SLOTS), lambda i: (i, 0), memory_space=pltpu.SMEM),
            pl.BlockSpec((tb, SUBLANES, LANES), lambda i: (i, 0, 0)),
            pl.BlockSpec((tb, PEER_SLOTS), lambda i: (i, 0)),
            pl.BlockSpec(tab.shape, lambda i: (0, 0), pipeline_mode=pl.Buffered(1)),
        ],
        out_specs=pl.BlockSpec((tb, PEER_SLOTS), lambda i: (i, 0)),
        out_shape=jax.ShapeDtypeStruct((N, PEER_SLOTS), F32),
        scratch_shapes=[pltpu.VMEM((tb * PEER_SLOTS, LANES), F32)],
        compiler_params=_params(("arbitrary",), VMEM_LIMIT_BYTES),
        name="peer_u",
    )(e4, h3, gates, tab)


def _peer_v_kernel(e_ref, c_ref, x1_ref, g2_ref, tab_ref, o_ref, splat_ref, idx_ref, sem, *, tb):
    nacc = 3
    rio = lax.broadcasted_iota(I32, (LANES, LANES), 0)
    lio = lax.broadcasted_iota(I32, (LANES, LANES), 1)
    diag = (rio == lio)[None]
    for t8 in range(tb // SUBLANES):
        cc = c_ref[t8 * SUBLANES:(t8 + 1) * SUBLANES, :]
        s = jnp.sum(jnp.where(diag, cc[:, None, :], 0.0), axis=-1, keepdims=True)
        splat_ref[t8 * SUBLANES:(t8 + 1) * SUBLANES] = jnp.broadcast_to(s, (SUBLANES, LANES, LANES))

    def token(t, row, pos):
        del pos
        accs = [jnp.zeros((SUBLANES, LANES), F32) for _ in range(nacc)]
        for k in range(PEER_SLOTS):
            c = jnp.broadcast_to(splat_ref[t, k:k + 1, :], (SUBLANES, LANES))
            accs[k % nacc] = accs[k % nacc] + c * _expert_tile(tab_ref, row(k))
        y = _tree(jnp.add, accs)
        o_ref[t] = x1_ref[t] + g2_ref[0] * _from_tile_rows(y)

    _for_each_token(e_ref, idx_ref, sem, tb, token)


def _peer_v(e4, coef, x1_3, g2_3, tab, T, tb=128):
    N = x1_3.shape[0]
    assert tb % (RING_SLOTS * SLOT_TOKENS) == 0 and tb % SUBLANES == 0
    return pl.pallas_call(
        functools.partial(_peer_v_kernel, tb=tb),
        grid=(N // tb,),
        in_specs=[
            pl.BlockSpec((tb // SLOT_TOKENS, SLOT_TOKENS, PEER_SLOTS), lambda i: (i, 0, 0)),
            pl.BlockSpec((tb, PEER_SLOTS), lambda i: (i, 0)),
            pl.BlockSpec((tb, SUBLANES, LANES), lambda i: (i, 0, 0)),
            pl.BlockSpec((1, SUBLANES, LANES), lambda i: ((i * tb) // T, 0, 0)),
            pl.BlockSpec(tab.shape, lambda i: (0, 0), pipeline_mode=pl.Buffered(1)),
        ],
        out_specs=pl.BlockSpec((tb, SUBLANES, LANES), lambda i: (i, 0, 0)),
        out_shape=jax.ShapeDtypeStruct((N, SUBLANES, LANES), F32),
        scratch_shapes=[pltpu.VMEM((tb, LANES, LANES), F32),
                        pltpu.SMEM((RING_SLOTS, SLOT_TOKENS, PEER_SLOTS), I32),
                        pltpu.SemaphoreType.DMA((RING_SLOTS,))],
        compiler_params=_params(("arbitrary",), VMEM_LIMIT_BYTES),
        name="peer_v",
    )(e4.reshape(N // SLOT_TOKENS, SLOT_TOKENS, PEER_SLOTS), coef, x1_3, g2_3, tab)


def kernel(x, c, w_ada, b_ada, norm1_g, w_in, gla_gate_w2, gla_gate_b, gla_norm_g, swa_qnorm_g,
           swa_knorm_g, swa_sinks, rel_bias, w_up_a, w_up_b, w_out, norm2_g, peer_wq, peer_subkeys,
           peer_u, peer_v):
    B, T, D = x.shape
    N = B * T
    L = w_ada.shape[0]
    mod = _adaln(c, w_ada, b_ada)
    weff_t = _fold_peer_keys(peer_wq, peer_subkeys)
    bias = _swa_bias(rel_bias)
    xf = x.reshape(N, D)
    for l in range(L):
        sh1, sc1, g1, sh2, sc2, g2 = [mod[l, :, i * D:(i + 1) * D] for i in range(6)]
        proj = _in_proj(xf, norm1_g[l], sc1, sh1, _pack_w_in(w_in[l]), T)
        proj3 = proj.reshape(B, T, PROJ_W)
        w2p = jnp.zeros((LANES, GLA_HEADS * GLA_DK), BF16).at[:GLA_RANK].set(gla_gate_w2[l].astype(BF16))
        gla_o = _gla(proj3, w2p, gla_gate_b[l].reshape(1, -1), gla_norm_g[l].reshape(1, -1))
        swa_o = _swa(proj3, bias, swa_qnorm_g[l].reshape(1, -1), swa_knorm_g[l].reshape(1, -1),
                     swa_sinks[l])
        x1, h2, s_t = _merge(xf, proj, gla_o.reshape(N, -1), swa_o.reshape(N, -1),
                             w_up_a[l].astype(BF16), w_up_b[l].astype(BF16), w_out[l].astype(BF16),
                             g1, norm2_g[l], sc2, sh2, weff_t[l], T)
        e4, gates = _topk(s_t)
        coef = _peer_u(e4, h2.reshape(N, SUBLANES, LANES), gates, _table(peer_u[l]))
        g2_3 = g2.reshape(B, SUBLANES, LANES)
        xf = _peer_v(e4, coef, x1.reshape(N, SUBLANES, LANES), g2_3, _table(peer_v[l]), T).reshape(N, D)
    return xf.reshape(B, T, D)
```

```python
import functools

import numpy as np
import jax
import jax.numpy as jnp
from jax import lax
from jax.experimental import pallas as pl
from jax.experimental.pallas import tpu as pltpu

F32 = jnp.float32
BF16 = jnp.bfloat16
I32 = jnp.int32
HIGHEST = lax.Precision.HIGHEST
EPS = 1e-6

GLA_HEADS = 4
GLA_DK = 128
GLA_DV = 256
GLA_RANK = 16
GLA_TAU = 16.0
GLA_CHUNK = 64
SWA_HEADS = 16
SWA_KV_HEADS = 2
SWA_HD = 64
SWA_BLOCK = 128
N_BUCKETS = 32
MAX_DISTANCE = 128
PEER_HEADS = 8
PEER_NKEYS = 128
PEER_TOPK = 16
PEER_SLOTS = PEER_HEADS * PEER_TOPK

SUBLANES = 8
LANES = 128
VMEM_LIMIT_BYTES = 56 * 1024 * 1024

NT_DIMS = (((1,), (1,)), ((), ()))
TN_DIMS = (((0,), (0,)), ((), ()))

COL_Q, COL_K, COL_V, COL_GR, COL_SQ, COL_GA, COL_GB = 0, 512, 1024, 2048, 3072, 4096, 5120
COL_SK, COL_SV, COL_GLR = 6144, 6272, 6400
PROJ_W = 6528


def _params(sem, vmem=None):
    return pltpu.CompilerParams(dimension_semantics=sem, vmem_limit_bytes=vmem)


def _adaln_kernel(c_ref, w_ref, b_ref, o_ref):
    c = c_ref[...]
    a = c * jax.nn.sigmoid(c)
    o_ref[0] = jnp.dot(a, w_ref[0], preferred_element_type=F32, precision=HIGHEST) + b_ref[0]


def _adaln(c, w_ada, b_ada):
    L, D, W = w_ada.shape
    B = c.shape[0]
    rows = -(-B // SUBLANES) * SUBLANES
    cp = jnp.zeros((rows, D), F32).at[:B].set(c)
    tn = W // 4
    out = pl.pallas_call(
        _adaln_kernel,
        grid=(L, W // tn),
        in_specs=[
            pl.BlockSpec((rows, D), lambda l, j: (0, 0)),
            pl.BlockSpec((1, D, tn), lambda l, j: (l, 0, j)),
            pl.BlockSpec((1, 1, tn), lambda l, j: (l, 0, j)),
        ],
        out_specs=pl.BlockSpec((1, rows, tn), lambda l, j: (l, 0, j)),
        out_shape=jax.ShapeDtypeStruct((L, rows, W), F32),
        compiler_params=_params(("arbitrary", "arbitrary")),
        name="adaln",
    )(cp, w_ada, b_ada.reshape(L, 1, W))
    return out[:, :B]


def _fold_kernel(sk_ref, wq_ref, o_ref, rows_ref):
    half = sk_ref.shape[-1]
    groups = 2 * PEER_HEADS
    for h in range(PEER_HEADS):
        for p in range(2):
            g = 2 * h + p
            res = lax.dot_general(sk_ref[0, p], wq_ref[0, :, g * half:(g + 1) * half], NT_DIMS,
                                  precision=HIGHEST, preferred_element_type=F32)
            for cb in range(rows_ref.shape[0]):
                rows_ref[cb, pl.ds(p * PEER_HEADS + h, PEER_NKEYS, stride=groups), :] = (
                    res[:, cb * LANES:(cb + 1) * LANES])
    for cb in range(rows_ref.shape[0]):
        o_ref[0, :, cb * LANES:(cb + 1) * LANES] = rows_ref[cb].astype(BF16)


def _fold_peer_keys(peer_wq, peer_subkeys):
    L, D, QW = peer_wq.shape
    half = peer_subkeys.shape[-1]
    rows = (QW // half) * PEER_NKEYS
    return pl.pallas_call(
        _fold_kernel,
        grid=(L,),
        in_specs=[
            pl.BlockSpec((1, 2, PEER_NKEYS, half), lambda l: (l, 0, 0, 0)),
            pl.BlockSpec((1, D, QW), lambda l: (l, 0, 0)),
        ],
        out_specs=pl.BlockSpec((1, rows, D), lambda l: (l, 0, 0)),
        out_shape=jax.ShapeDtypeStruct((L, rows, D), BF16),
        scratch_shapes=[pltpu.VMEM((D // LANES, rows, LANES), F32)],
        compiler_params=_params(("arbitrary",), VMEM_LIMIT_BYTES),
        name="peer_fold",
    )(peer_subkeys, peer_wq)


def _inproj_kernel(x_ref, g_ref, sc_ref, sh_ref, w_ref, o_ref):
    x = x_ref[...]
    ms = jnp.mean(x * x, axis=-1, keepdims=True)
    y = x * lax.rsqrt(ms + EPS) * g_ref[...]
    h = y * (1.0 + sc_ref[0]) + sh_ref[0]
    o_ref[...] = jnp.dot(h.astype(BF16), w_ref[...], preferred_element_type=F32).astype(BF16)


def _in_proj(xf, g, sc, sh, wp, T, tm=512):
    N, D = xf.shape
    B = sc.shape[0]
    ncol = 3
    tn = PROJ_W // ncol
    return pl.pallas_call(
        _inproj_kernel,
        grid=(ncol, N // tm),
        in_specs=[
            pl.BlockSpec((tm, D), lambda j, i: (i, 0)),
            pl.BlockSpec((1, D), lambda j, i: (0, 0)),
            pl.BlockSpec((1, 1, D), lambda j, i: ((i * tm) // T, 0, 0)),
            pl.BlockSpec((1, 1, D), lambda j, i: ((i * tm) // T, 0, 0)),
            pl.BlockSpec((D, tn), lambda j, i: (0, j)),
        ],
        out_specs=pl.BlockSpec((tm, tn), lambda j, i: (i, j)),
        out_shape=jax.ShapeDtypeStruct((N, PROJ_W), BF16),
        compiler_params=_params(("arbitrary", "arbitrary")),
        name="in_proj",
    )(xf, g.reshape(1, D), sc.reshape(B, 1, D), sh.reshape(B, 1, D), wp)


_SRC = dict(zip(("q", "k", "v", "glr", "gr", "sq", "sk", "sv", "ga", "gb"),
                np.cumsum([0, 512, 512, 1024, GLA_RANK, 1024, 1024, 128, 128, 1024])))
W_IN_SLABS = ((COL_Q, _SRC["q"], 512), (COL_K, _SRC["k"], 512), (COL_V, _SRC["v"], 1024),
              (COL_GR, _SRC["gr"], 1024), (COL_SQ, _SRC["sq"], 1024), (COL_GA, _SRC["ga"], 1024),
              (COL_GB, _SRC["gb"], 1024), (COL_SK, _SRC["sk"], 128), (COL_SV, _SRC["sv"], 128),
              (COL_GLR, _SRC["glr"], GLA_RANK))


def _pack_w_in_kernel(w_ref, o_ref):
    o_ref[:, COL_GLR:] = jnp.zeros((o_ref.shape[0], PROJ_W - COL_GLR), BF16)
    for dst, src, width in W_IN_SLABS:
        o_ref[:, dst:dst + width] = w_ref[:, int(src):int(src) + width].astype(BF16)


def _pack_w_in(w, tr=128):
    rows, cols = w.shape
    return pl.pallas_call(
        _pack_w_in_kernel,
        grid=(rows // tr,),
        in_specs=[pl.BlockSpec((tr, cols), lambda i: (i, 0))],
        out_specs=pl.BlockSpec((tr, PROJ_W), lambda i: (i, 0)),
        out_shape=jax.ShapeDtypeStruct((rows, PROJ_W), BF16),
        compiler_params=_params(("arbitrary",)),
        name="pack_w_in",
    )(w)


def _gla_kernel(q_ref, k_ref, v_ref, r_ref, glr_ref, w2_ref, b2_ref, ng_ref, o_ref, st_ref, *, nchunk):
    @pl.when(pl.program_id(1) == 0)
    def _():
        st_ref[...] = jnp.zeros_like(st_ref)

    C = GLA_CHUNK
    dk, dv = GLA_DK, GLA_DV
    row = lax.broadcasted_iota(I32, (C, C), 0)
    col = lax.broadcasted_iota(I32, (C, C), 1)
    tri = col <= row
    tri_f = tri.astype(F32)
    w2 = w2_ref[...]
    b2 = b2_ref[...]
    ng = ng_ref[...]
    for ci in range(nchunk):
        sl = pl.ds(ci * C, C)
        z = jnp.dot(glr_ref[sl, :], w2, preferred_element_type=F32) + b2
        log_a = (jnp.minimum(z, 0.0) - jnp.log(1.0 + jnp.exp(-jnp.abs(z)))) * (1.0 / GLA_TAU)
        b = jnp.dot(tri_f, log_a, preferred_element_type=F32, precision=HIGHEST)
        b_last = b[C - 1:C, :]
        q = q_ref[sl, :].astype(F32) * (dk ** -0.5)
        k = k_ref[sl, :].astype(F32)
        q_dec = (q * jnp.exp(b)).astype(BF16)
        k_inv = (k * jnp.exp(-b)).astype(BF16)
        k_tail = (k * jnp.exp(b_last - b)).astype(BF16)
        decay = jnp.exp(b_last)
        for h in range(GLA_HEADS):
            kc = slice(h * dk, (h + 1) * dk)
            vc = slice(h * dv, (h + 1) * dv)
            v = v_ref[sl, vc]
            attn = lax.dot_general(q_dec[:, kc], k_inv[:, kc], NT_DIMS, preferred_element_type=F32)
            attn = jnp.where(tri, attn, 0.0).astype(BF16)
            st = st_ref[h]
            o = (jnp.dot(attn, v, preferred_element_type=F32)
                 + lax.dot_general(q_dec[:, kc], st.astype(BF16), NT_DIMS, preferred_element_type=F32))
            st_ref[h] = st * decay[:, kc] + lax.dot_general(v, k_tail[:, kc], TN_DIMS,
                                                            preferred_element_type=F32)
            on = o * lax.rsqrt(jnp.mean(o * o, axis=-1, keepdims=True) + EPS) * ng
            r = r_ref[sl, vc].astype(F32)
            o_ref[sl, vc] = (on * (r * jax.nn.sigmoid(r))).astype(BF16)


def _gla(proj, w2p, b2, ng, tc=256):
    B, T, _ = proj.shape
    H = GLA_HEADS
    kw, vw = H * GLA_DK, H * GLA_DV
    return pl.pallas_call(
        functools.partial(_gla_kernel, nchunk=tc // GLA_CHUNK),
        grid=(B, T // tc),
        in_specs=[
            pl.BlockSpec((None, tc, kw), lambda b, c: (b, c, COL_Q // kw)),
            pl.BlockSpec((None, tc, kw), lambda b, c: (b, c, COL_K // kw)),
            pl.BlockSpec((None, tc, vw), lambda b, c: (b, c, COL_V // vw)),
            pl.BlockSpec((None, tc, vw), lambda b, c: (b, c, COL_GR // vw)),
            pl.BlockSpec((None, tc, LANES), lambda b, c: (b, c, COL_GLR // LANES)),
            pl.BlockSpec((LANES, kw), lambda b, c: (0, 0)),
            pl.BlockSpec((1, kw), lambda b, c: (0, 0)),
            pl.BlockSpec((1, GLA_DV), lambda b, c: (0, 0)),
        ],
        out_specs=pl.BlockSpec((None, tc, vw), lambda b, c: (b, c, 0)),
        out_shape=jax.ShapeDtypeStruct((B, T, vw), BF16),
        scratch_shapes=[pltpu.VMEM((H, GLA_DV, GLA_DK), F32)],
        compiler_params=_params(("arbitrary", "arbitrary")),
        name="gla",
    )(proj, proj, proj, proj, proj, w2p, b2, ng)


def _t5_bucket(dist):
    max_exact = N_BUCKETS // 2
    d = np.maximum(dist, 1).astype(np.float32)
    large = max_exact + (np.log(d / max_exact) / np.log(MAX_DISTANCE / max_exact)
                         * (N_BUCKETS - max_exact)).astype(np.int32)
    large = np.minimum(large, N_BUCKETS - 1)
    return np.where(dist < max_exact, dist, large).astype(np.int32)


NEG_BIG = -1e30


def _swa_bias(rel_bias):
    blk = SWA_BLOCK
    qi = np.arange(blk)[:, None]
    sj = np.arange(2 * blk)[None, :]
    dist = blk + qi - sj
    band = (dist >= 0) & (dist < blk)
    bucket = _t5_bucket(np.clip(dist, 0, None))
    bias = rel_bias[bucket].astype(F32).transpose(2, 0, 1)
    masks = np.stack([band, band & (sj >= blk)])[:, None]
    return jnp.where(jnp.asarray(masks), bias[None], NEG_BIG)


def _segment_sums(x, seg):
    hi = x.astype(BF16)
    lo = (x - hi.astype(F32)).astype(BF16)
    return (jnp.dot(hi, seg, preferred_element_type=F32) + jnp.dot(lo, seg, preferred_element_type=F32))


def _head_rms_scale(x, seg, seg_t):
    inv = lax.rsqrt(_segment_sums(x * x, seg) * (1.0 / SWA_HD) + EPS)
    return _segment_sums(inv, seg_t)


def _swa_kernel(sink_ref, q_ref, kp_ref, kc_ref, vp_ref, vc_ref, bias_ref, qg_ref, kg_ref, seg_ref,
                segt_ref, o_ref):
    blk = SWA_BLOCK
    hd = SWA_HD
    group = SWA_HEADS // SWA_KV_HEADS
    kvw = SWA_KV_HEADS * hd
    seg = seg_ref[...]
    seg_t = segt_ref[...]
    q = q_ref[...].astype(F32)
    qn = (q * _head_rms_scale(q, seg, seg_t) * qg_ref[...] * (hd ** -0.5)).astype(BF16)
    k2 = jnp.concatenate([kp_ref[...], kc_ref[...]], axis=0).astype(F32)
    kn = (k2 * _head_rms_scale(k2, seg[:kvw], seg_t[:, :kvw]) * kg_ref[...]).astype(BF16)
    v2 = jnp.concatenate([vp_ref[...], vc_ref[...]], axis=0)
    lane = lax.broadcasted_iota(I32, (2 * blk, kvw), 1)
    outs = []
    for kh in range(SWA_KV_HEADS):
        kk = kn[:, kh * hd:(kh + 1) * hd]
        vsh = v2 if kh == 0 else jnp.concatenate([v2[:, kh * hd:], v2[:, :kh * hd]], axis=1)
        vv = jnp.where(lane < hd, vsh, jnp.ones_like(vsh))
        qs = jnp.concatenate([qn[:, (kh * group + g) * hd:(kh * group + g + 1) * hd]
                              for g in range(group)], axis=0)
        logits = lax.dot_general(qs, kk, NT_DIMS, preferred_element_type=F32)
        logits = logits + bias_ref[kh * group:(kh + 1) * group].reshape(group * blk, 2 * blk)
        for g in range(group):
            lg = logits[g * blk:(g + 1) * blk]
            sink = sink_ref[kh * group + g]
            m = jnp.maximum(jnp.max(lg, axis=-1, keepdims=True), sink)
            p = jnp.exp(lg - m).astype(BF16)
            pv = jnp.dot(p, vv, preferred_element_type=F32)
            denom = pv[:, hd:hd + 1] + jnp.exp(sink - m)
            outs.append(pv[:, :hd] / denom)
    o_ref[...] = jnp.concatenate(outs, axis=-1).astype(BF16)


def _swa(proj, bias, qg, kg, sinks):
    B, T, _ = proj.shape
    blk = SWA_BLOCK
    qw = SWA_HEADS * SWA_HD
    kvw = SWA_KV_HEADS * SWA_HD
    prev = lambda b, i: jnp.maximum(i - 1, 0)
    seg = (np.arange(qw)[:, None] // SWA_HD == np.arange(LANES)[None, :])
    seg = jnp.asarray(seg, BF16)
    return pl.pallas_call(
        _swa_kernel,
        grid=(B, T // blk),
        in_specs=[
            pl.BlockSpec(memory_space=pltpu.SMEM),
            pl.BlockSpec((None, blk, qw), lambda b, i: (b, i, COL_SQ // qw)),
            pl.BlockSpec((None, blk, kvw), lambda b, i: (b, prev(b, i), COL_SK // kvw)),
            pl.BlockSpec((None, blk, kvw), lambda b, i: (b, i, COL_SK // kvw)),
            pl.BlockSpec((None, blk, kvw), lambda b, i: (b, prev(b, i), COL_SV // kvw)),
            pl.BlockSpec((None, blk, kvw), lambda b, i: (b, i, COL_SV // kvw)),
            pl.BlockSpec((None, SWA_HEADS, blk, 2 * blk), lambda b, i: (jnp.where(i == 0, 1, 0), 0, 0, 0)),
            pl.BlockSpec((1, qw), lambda b, i: (0, 0)),
            pl.BlockSpec((1, kvw), lambda b, i: (0, 0)),
            pl.BlockSpec((qw, LANES), lambda b, i: (0, 0)),
            pl.BlockSpec((LANES, qw), lambda b, i: (0, 0)),
        ],
        out_specs=pl.BlockSpec((None, blk, qw), lambda b, i: (b, i, 0)),
        out_shape=jax.ShapeDtypeStruct((B, T, qw), BF16),
        compiler_params=_params(("arbitrary", "arbitrary")),
        name="swa",
    )(sinks, proj, proj, proj, proj, proj, bias, jnp.tile(qg, (1, SWA_HEADS)),
      jnp.tile(kg, (1, SWA_KV_HEADS)), seg, seg.T)


def _merge_kernel(x_ref, ga_ref, gb_ref, go_ref, so_ref, wa_ref, wb_ref, wo_ref, g1_ref,
                  n2_ref, sc2_ref, sh2_ref, we_ref, x1_ref, h2_ref, st_ref):
    ya = jnp.dot(go_ref[...], wa_ref[...], preferred_element_type=F32)
    yb = jnp.dot(so_ref[...], wb_ref[...], preferred_element_type=F32)
    m = jax.nn.sigmoid(ga_ref[...].astype(F32)) * ya + jax.nn.sigmoid(gb_ref[...].astype(F32)) * yb
    mixed = jnp.dot(m.astype(BF16), wo_ref[...], preferred_element_type=F32)
    x1 = x_ref[...] + g1_ref[0] * mixed
    x1_ref[...] = x1
    y = x1 * lax.rsqrt(jnp.mean(x1 * x1, axis=-1, keepdims=True) + EPS) * n2_ref[...]
    h2 = y * (1.0 + sc2_ref[0]) + sh2_ref[0]
    h2_ref[...] = h2
    st_ref[...] = lax.dot_general(we_ref[...], h2.astype(BF16), NT_DIMS, preferred_element_type=F32)


def _merge(xf, proj2, gla_o, swa_o, wa, wb, wo, g1, n2, sc2, sh2, weff_t, T, tm=256):
    N, D = xf.shape
    B = g1.shape[0]
    SW = weff_t.shape[0]
    bat = lambda i: ((i * tm) // T, 0, 0)
    full = lambda i: (0, 0)
    return pl.pallas_call(
        _merge_kernel,
        grid=(N // tm,),
        in_specs=[
            pl.BlockSpec((tm, D), lambda i: (i, 0)),
            pl.BlockSpec((tm, D), lambda i: (i, COL_GA // D)),
            pl.BlockSpec((tm, D), lambda i: (i, COL_GB // D)),
            pl.BlockSpec((tm, D), lambda i: (i, 0)),
            pl.BlockSpec((tm, D), lambda i: (i, 0)),
            pl.BlockSpec((D, D), full),
            pl.BlockSpec((D, D), full),
            pl.BlockSpec((D, D), full),
            pl.BlockSpec((1, 1, D), bat),
            pl.BlockSpec((1, D), full),
            pl.BlockSpec((1, 1, D), bat),
            pl.BlockSpec((1, 1, D), bat),
            pl.BlockSpec((SW, D), full),
        ],
        out_specs=[
            pl.BlockSpec((tm, D), lambda i: (i, 0)),
            pl.BlockSpec((tm, D), lambda i: (i, 0)),
            pl.BlockSpec((SW, tm), lambda i: (0, i)),
        ],
        out_shape=[
            jax.ShapeDtypeStruct((N, D), F32),
            jax.ShapeDtypeStruct((N, D), F32),
            jax.ShapeDtypeStruct((SW, N), F32),
        ],
        compiler_params=_params(("arbitrary",), VMEM_LIMIT_BYTES),
        name="merge",
    )(xf, proj2, proj2, gla_o, swa_o, wa, wb, wo, g1.reshape(B, 1, D), n2.reshape(1, D),
      sc2.reshape(B, 1, D), sh2.reshape(B, 1, D), weff_t)


def _tree(op, xs):
    xs = list(xs)
    while len(xs) > 1:
        xs = [op(xs[i], xs[i + 1]) for i in range(0, len(xs) - 1, 2)] + ([xs[-1]] if len(xs) % 2 else [])
    return xs[0]


EXTRACT_CHAINS = 4


def _extract_best(ref, n):
    lanes = min(EXTRACT_CHAINS, n)
    best = [ref[i] for i in range(lanes)]
    for i in range(lanes, n):
        best[i % lanes] = jnp.maximum(best[i % lanes], ref[i])
    m = _tree(jnp.maximum, best)
    first = [jnp.where(ref[i] == m, i, n) for i in range(lanes)]
    for i in range(lanes, n):
        first[i % lanes] = jnp.minimum(first[i % lanes], jnp.where(ref[i] == m, i, n))
    pos = _tree(jnp.minimum, first)
    for i in range(n):
        ref[i] = jnp.where(pos == i, -jnp.inf, ref[i])
    return m, pos


TOPK_PAIRS = tuple((a, b) for a in range(PEER_TOPK) for b in range(PEER_TOPK)
                   if (a + 1) * (b + 1) <= PEER_TOPK)


def _topk_kernel(s_ref, e_ref, g_ref, work_ref, v_ref, i_ref, cand_ref, cidx_ref, best_ref, row_ref):
    K = PEER_TOPK
    nk = PEER_NKEYS
    H = PEER_HEADS
    work_ref[...] = s_ref[...].reshape(work_ref.shape)

    def stage1(k, _):
        m, pos = _extract_best(work_ref, nk)
        v_ref[k] = m
        i_ref[k] = pos
        return 0

    lax.fori_loop(0, K, stage1, 0)

    for ci, (a, b) in enumerate(TOPK_PAIRS):
        cand_ref[ci] = v_ref[a, 0:H, :] + v_ref[b, H:2 * H, :]
        cidx_ref[ci] = (i_ref[a, 0:H, :] * nk + i_ref[b, H:2 * H, :]) * WORDS_PER_EXPERT
    ncand = len(TOPK_PAIRS)

    def stage2(k, _):
        m, pos = _extract_best(cand_ref, ncand)
        row = _tree(jnp.maximum, [jnp.where(pos == ci, cidx_ref[ci], -1) for ci in range(ncand)])
        best_ref[k] = m
        row_ref[k] = row
        return 0

    lax.fori_loop(0, K, stage2, 0)
    best = best_ref[...]
    ex = jnp.exp(best - best[0:1])
    gates = ex / jnp.sum(ex, axis=0, keepdims=True)
    tm = s_ref.shape[1]
    g_ref[...] = gates.reshape(K * H, tm).T
    e_ref[...] = row_ref[...].reshape(K * H, tm).astype(F32).T.astype(I32)


def _topk(s_t, tm=128):
    SW, N = s_t.shape
    K = PEER_TOPK
    G = 2 * PEER_HEADS
    return pl.pallas_call(
        _topk_kernel,
        grid=(N // tm,),
        in_specs=[pl.BlockSpec((SW, tm), lambda i: (0, i))],
        out_specs=[
            pl.BlockSpec((tm, PEER_SLOTS), lambda i: (i, 0)),
            pl.BlockSpec((tm, PEER_SLOTS), lambda i: (i, 0)),
        ],
        out_shape=[
            jax.ShapeDtypeStruct((N, PEER_SLOTS), I32),
            jax.ShapeDtypeStruct((N, PEER_SLOTS), F32),
        ],
        scratch_shapes=[pltpu.VMEM((PEER_NKEYS, G, tm), F32),
                        pltpu.VMEM((K, G, tm), F32),
                        pltpu.VMEM((K, G, tm), I32),
                        pltpu.VMEM((len(TOPK_PAIRS), PEER_HEADS, tm), F32),
                        pltpu.VMEM((len(TOPK_PAIRS), PEER_HEADS, tm), I32),
                        pltpu.VMEM((K, PEER_HEADS, tm), F32),
                        pltpu.VMEM((K, PEER_HEADS, tm), I32)],
        compiler_params=_params(("arbitrary",)),
        name="peer_topk",
    )(s_t)


WORDS_PER_EXPERT = SUBLANES // 2


TILE_BLOCK = tuple((r % 2) * WORDS_PER_EXPERT + r // 2 for r in range(SUBLANES))
BLOCK_ROW = tuple(TILE_BLOCK.index(q) for q in range(SUBLANES))


def _bf16_bits(x):
    return lax.bitcast_convert_type(x.astype(BF16).astype(F32), jnp.uint32)


def _table_kernel(t_ref, o_ref):
    rows = t_ref.shape[0]
    for s in range(WORDS_PER_EXPERT):
        lo = _bf16_bits(t_ref[:, s * LANES:(s + 1) * LANES])
        hi = _bf16_bits(t_ref[:, (s + WORDS_PER_EXPERT) * LANES:(s + WORDS_PER_EXPERT + 1) * LANES])
        word = lax.shift_right_logical(lo, jnp.uint32(16)) | (hi & jnp.uint32(0xFFFF0000))
        o_ref[pl.ds(s, rows, stride=WORDS_PER_EXPERT), :] = lax.bitcast_convert_type(word, I32)


def _table(t, te=2048):
    n, d = t.shape
    return pl.pallas_call(
        _table_kernel,
        grid=(n // te,),
        in_specs=[pl.BlockSpec((te, d), lambda i: (i, 0))],
        out_specs=pl.BlockSpec((te * WORDS_PER_EXPERT, LANES), lambda i: (i, 0)),
        out_shape=jax.ShapeDtypeStruct((n * WORDS_PER_EXPERT, LANES), I32),
        compiler_params=_params(("arbitrary",)),
        name="peer_table",
    )(t)


def _expert_tile(tab_ref, e4):
    words = tab_ref[pl.ds(pl.multiple_of(e4, WORDS_PER_EXPERT), WORDS_PER_EXPERT), :]
    return pltpu.bitcast(words, BF16).astype(F32)


def _to_tile_rows(x):
    return jnp.concatenate([x[q:q + 1, :] for q in TILE_BLOCK], axis=0)


def _from_tile_rows(x):
    return jnp.concatenate([x[r:r + 1, :] for r in BLOCK_ROW], axis=0)


BITREV = (0, 4, 2, 6, 1, 5, 3, 7)


def _fold8(ps, sub):
    m4 = sub < 4
    m2 = (sub & 3) < 2
    m1 = (sub & 1) < 1
    s1 = []
    for a, b in zip(ps[0::2], ps[1::2]):
        s1.append(jnp.where(m4, a, b) + pltpu.roll(jnp.where(m4, b, a), 4, 0))
    s2 = []
    for a, b in zip(s1[0::2], s1[1::2]):
        s2.append(jnp.where(m2, a, pltpu.roll(b, 2, 0)) + jnp.where(m2, pltpu.roll(a, 6, 0), b))
    a, b = s2
    return jnp.where(m1, a, pltpu.roll(b, 1, 0)) + jnp.where(m1, pltpu.roll(a, 7, 0), b)


def _gelu_tanh(x):
    return 0.5 * x * (1.0 + jnp.tanh(np.sqrt(2.0 / np.pi) * (x + 0.044715 * (x * x * x))))


def _row_sums(x):
    ones = jnp.ones((LANES, LANES), BF16)
    hi = x.astype(BF16)
    lo = (x - hi.astype(F32)).astype(BF16)
    return (jnp.dot(hi, ones, preferred_element_type=F32)
            + jnp.dot(lo, ones, preferred_element_type=F32))


RING_SLOTS = 4
SLOT_TOKENS = 4


def _for_each_token(e_ref, idx_ref, sem, tb, token_body):
    nquads = tb // SLOT_TOKENS

    def fetch(q, slot):
        return pltpu.make_async_copy(e_ref.at[q], idx_ref.at[slot], sem.at[slot])

    for slot in range(RING_SLOTS):
        fetch(slot, slot).start()

    def ring(j, _):
        for slot in range(RING_SLOTS):
            q = j * RING_SLOTS + slot
            fetch(q, slot).wait()
            for tt in range(SLOT_TOKENS):
                token_body(q * SLOT_TOKENS + tt, lambda k, slot=slot, tt=tt: idx_ref[slot, tt, k],
                           slot * SLOT_TOKENS + tt)
            fetch(jnp.minimum(q + RING_SLOTS, nquads - 1), slot).start()
        return 0

    lax.fori_loop(0, nquads // RING_SLOTS, ring, 0)
    for slot in range(RING_SLOTS):
        fetch(nquads - 1, slot).wait()


def _peer_u_kernel(e_ref, h_ref, g_ref, tab_ref, o_ref, slab_ref, *, tb):
    sub = lax.broadcasted_iota(I32, (SUBLANES, LANES), 0)
    groups_per_token = PEER_SLOTS // SUBLANES

    def token(t, _):
        h = _to_tile_rows(h_ref[t])
        for j in range(groups_per_token):
            ps = [_expert_tile(tab_ref, e_ref[t, j * SUBLANES + BITREV[qn]]) * h
                  for qn in range(SUBLANES)]
            dst = pl.multiple_of(t * PEER_SLOTS + j * SUBLANES, SUBLANES)
            slab_ref[pl.ds(dst, SUBLANES), :] = _fold8(ps, sub)
        return 0

    lax.fori_loop(0, tb, token, 0)
    rio = lax.broadcasted_iota(I32, (LANES, LANES), 0)
    lio = lax.broadcasted_iota(I32, (LANES, LANES), 1)
    diag = (rio == lio)[None]
    rows = SUBLANES * PEER_SLOTS
    for gi in range(tb // SUBLANES):
        tok = slice(gi * SUBLANES, (gi + 1) * SUBLANES)
        rs = jnp.sum(slab_ref[gi * rows:(gi + 1) * rows, :], axis=-1, keepdims=True)
        rs = rs.reshape(SUBLANES, PEER_SLOTS, 1)
        a = jnp.sum(jnp.where(diag, rs, 0.0), axis=1)
        o_ref[tok, :] = g_ref[tok, :] * _gelu_tanh(a)


def _peer_u(e4, h3, gates, tab, tb=128):
    N = h3.shape[0]
    assert tb % SUBLANES == 0
    return pl.pallas_call(
        functools.partial(_peer_u_kernel, tb=tb),
        grid=(N // tb,),
        in_specs=[
            pl.BlockSpec((tb, PEER_SLOTS), lambda i: (i, 0), memory_space=pltpu.SMEM),
            pl.BlockSpec((tb, SUBLANES, LANES), lambda i: (i, 0, 0)),
            pl.BlockSpec((tb, PEER_SLOTS), lambda i: (i, 0)),
            pl.BlockSpec(tab.shape, lambda i: (0, 0), pipeline_mode=pl.Buffered(1)),
        ],
        out_specs=pl.BlockSpec((tb, PEER_SLOTS), lambda i: (i, 0)),
        out_shape=jax.ShapeDtypeStruct((N, PEER_SLOTS), F32),
        scratch_shapes=[pltpu.VMEM((tb * PEER_SLOTS, LANES), F32)],
        compiler_params=_params(("arbitrary",), VMEM_LIMIT_BYTES),
        name="peer_u",
    )(e4, h3, gates, tab)


def _peer_v_kernel(e_ref, c_ref, x1_ref, g2_ref, tab_ref, o_ref, splat_ref, idx_ref, sem, *, tb):
    nacc = 3
    rio = lax.broadcasted_iota(I32, (LANES, LANES), 0)
    lio = lax.broadcasted_iota(I32, (LANES, LANES), 1)
    diag = (rio == lio)[None]
    for t8 in range(tb // SUBLANES):
        cc = c_ref[t8 * SUBLANES:(t8 + 1) * SUBLANES, :]
        s = jnp.sum(jnp.where(diag, cc[:, None, :], 0.0), axis=-1, keepdims=True)
        splat_ref[t8 * SUBLANES:(t8 + 1) * SUBLANES] = jnp.broadcast_to(s, (SUBLANES, LANES, LANES))

    def token(t, row, pos):
        del pos
        accs = [jnp.zeros((SUBLANES, LANES), F32) for _ in range(nacc)]
        for k in range(PEER_SLOTS):
            c = jnp.broadcast_to(splat_ref[t, k:k + 1, :], (SUBLANES, LANES))
            accs[k % nacc] = accs[k % nacc] + c * _expert_tile(tab_ref, row(k))
        y = _tree(jnp.add, accs)
        o_ref[t] = x1_ref[t] + g2_ref[0] * _from_tile_rows(y)

    _for_each_token(e_ref, idx_ref, sem, tb, token)


def _peer_v(e4, coef, x1_3, g2_3, tab, T, tb=128):
    N = x1_3.shape[0]
    assert tb % (RING_SLOTS * SLOT_TOKENS) == 0 and tb % SUBLANES == 0
    return pl.pallas_call(
        functools.partial(_peer_v_kernel, tb=tb),
        grid=(N // tb,),
        in_specs=[
            pl.BlockSpec((tb // SLOT_TOKENS, SLOT_TOKENS, PEER_SLOTS), lambda i: (i, 0, 0)),
            pl.BlockSpec((tb, PEER_SLOTS), lambda i: (i, 0)),
            pl.BlockSpec((tb, SUBLANES, LANES), lambda i: (i, 0, 0)),
            pl.BlockSpec((1, SUBLANES, LANES), lambda i: ((i * tb) // T, 0, 0)),
            pl.BlockSpec(tab.shape, lambda i: (0, 0), pipeline_mode=pl.Buffered(1)),
        ],
        out_specs=pl.BlockSpec((tb, SUBLANES, LANES), lambda i: (i, 0, 0)),
        out_shape=jax.ShapeDtypeStruct((N, SUBLANES, LANES), F32),
        scratch_shapes=[pltpu.VMEM((tb, LANES, LANES), F32),
                        pltpu.SMEM((RING_SLOTS, SLOT_TOKENS, PEER_SLOTS), I32),
                        pltpu.SemaphoreType.DMA((RING_SLOTS,))],
        compiler_params=_params(("arbitrary",), VMEM_LIMIT_BYTES),
        name="peer_v",
    )(e4.reshape(N // SLOT_TOKENS, SLOT_TOKENS, PEER_SLOTS), coef, x1_3, g2_3, tab)


def kernel(x, c, w_ada, b_ada, norm1_g, w_in, gla_gate_w2, gla_gate_b, gla_norm_g, swa_qnorm_g,
           swa_knorm_g, swa_sinks, rel_bias, w_up_a, w_up_b, w_out, norm2_g, peer_wq, peer_subkeys,
           peer_u, peer_v):
    B, T, D = x.shape
    N = B * T
    L = w_ada.shape[0]
    mod = _adaln(c, w_ada, b_ada)
    weff_t = _fold_peer_keys(peer_wq, peer_subkeys)
    bias = _swa_bias(rel_bias)
    xf = x.reshape(N, D)
    for l in range(L):
        sh1, sc1, g1, sh2, sc2, g2 = [mod[l, :, i * D:(i + 1) * D] for i in range(6)]
        proj = _in_proj(xf, norm1_g[l], sc1, sh1, _pack_w_in(w_in[l]), T)
        proj3 = proj.reshape(B, T, PROJ_W)
        w2p = jnp.zeros((LANES, GLA_HEADS * GLA_DK), BF16).at[:GLA_RANK].set(gla_gate_w2[l].astype(BF16))
        gla_o = _gla(proj3, w2p, gla_gate_b[l].reshape(1, -1), gla_norm_g[l].reshape(1, -1))
        swa_o = _swa(proj3, bias, swa_qnorm_g[l].reshape(1, -1), swa_knorm_g[l].reshape(1, -1),
                     swa_sinks[l])
        x1, h2, s_t = _merge(xf, proj, gla_o.reshape(N, -1), swa_o.reshape(N, -1),
                             w_up_a[l].astype(BF16), w_up_b[l].astype(BF16), w_out[l].astype(BF16),
                             g1, norm2_g[l], sc2, sh2, weff_t[l], T)
        e4, gates = _topk(s_t)
        coef = _peer_u(e4, h2.reshape(N, SUBLANES, LANES), gates, _table(peer_u[l]))
        g2_3 = g2.reshape(B, SUBLANES, LANES)
        xf = _peer_v(e4, coef, x1.reshape(N, SUBLANES, LANES), g2_3, _table(peer_v[l]), T).reshape(N, D)
    return xf.reshape(B, T, D)
```

```python
import functools

import numpy as np
import jax
import jax.numpy as jnp
from jax import lax
from jax.experimental import pallas as pl
from jax.experimental.pallas import tpu as pltpu

F32 = jnp.float32
BF16 = jnp.bfloat16
I32 = jnp.int32
HIGHEST = lax.Precision.HIGHEST
EPS = 1e-6

GLA_HEADS = 4
GLA_DK = 128
GLA_DV = 256
GLA_RANK = 16
GLA_TAU = 16.0
GLA_CHUNK = 64
SWA_HEADS = 16
SWA_KV_HEADS = 2
SWA_HD = 64
SWA_BLOCK = 128
N_BUCKETS = 32
MAX_DISTANCE = 128
PEER_HEADS = 8
PEER_NKEYS = 128
PEER_TOPK = 16
PEER_SLOTS = PEER_HEADS * PEER_TOPK

SUBLANES = 8
LANES = 128
VMEM_LIMIT_BYTES = 56 * 1024 * 1024

NT_DIMS = (((1,), (1,)), ((), ()))
TN_DIMS = (((0,), (0,)), ((), ()))

COL_Q, COL_K, COL_V, COL_GR, COL_SQ, COL_GA, COL_GB = 0, 512, 1024, 2048, 3072, 4096, 5120
COL_SK, COL_SV, COL_GLR = 6144, 6272, 6400
PROJ_W = 6528


def _params(sem, vmem=None):
    return pltpu.CompilerParams(dimension_semantics=sem, vmem_limit_bytes=vmem)


def _adaln_kernel(c_ref, w_ref, b_ref, o_ref):
    c = c_ref[...]
    a = c * jax.nn.sigmoid(c)
    o_ref[0] = jnp.dot(a, w_ref[0], preferred_element_type=F32, precision=HIGHEST) + b_ref[0]


def _adaln(c, w_ada, b_ada):
    L, D, W = w_ada.shape
    B = c.shape[0]
    rows = -(-B // SUBLANES) * SUBLANES
    cp = jnp.zeros((rows, D), F32).at[:B].set(c)
    tn = W // 4
    out = pl.pallas_call(
        _adaln_kernel,
        grid=(L, W // tn),
        in_specs=[
            pl.BlockSpec((rows, D), lambda l, j: (0, 0)),
            pl.BlockSpec((1, D, tn), lambda l, j: (l, 0, j)),
            pl.BlockSpec((1, 1, tn), lambda l, j: (l, 0, j)),
        ],
        out_specs=pl.BlockSpec((1, rows, tn), lambda l, j: (l, 0, j)),
        out_shape=jax.ShapeDtypeStruct((L, rows, W), F32),
        compiler_params=_params(("arbitrary", "arbitrary")),
        name="adaln",
    )(cp, w_ada, b_ada.reshape(L, 1, W))
    return out[:, :B]


def _fold_kernel(sk_ref, wq_ref, o_ref, rows_ref):
    half = sk_ref.shape[-1]
    groups = 2 * PEER_HEADS
    for h in range(PEER_HEADS):
        for p in range(2):
            g = 2 * h + p
            res = lax.dot_general(sk_ref[0, p], wq_ref[0, :, g * half:(g + 1) * half], NT_DIMS,
                                  precision=HIGHEST, preferred_element_type=F32)
            for cb in range(rows_ref.shape[0]):
                rows_ref[cb, pl.ds(p * PEER_HEADS + h, PEER_NKEYS, stride=groups), :] = (
                    res[:, cb * LANES:(cb + 1) * LANES])
    for cb in range(rows_ref.shape[0]):
        o_ref[0, :, cb * LANES:(cb + 1) * LANES] = rows_ref[cb].astype(BF16)


def _fold_peer_keys(peer_wq, peer_subkeys):
    L, D, QW = peer_wq.shape
    half = peer_subkeys.shape[-1]
    rows = (QW // half) * PEER_NKEYS
    return pl.pallas_call(
        _fold_kernel,
        grid=(L,),
        in_specs=[
            pl.BlockSpec((1, 2, PEER_NKEYS, half), lambda l: (l, 0, 0, 0)),
            pl.BlockSpec((1, D, QW), lambda l: (l, 0, 0)),
        ],
        out_specs=pl.BlockSpec((1, rows, D), lambda l: (l, 0, 0)),
        out_shape=jax.ShapeDtypeStruct((L, rows, D), BF16),
        scratch_shapes=[pltpu.VMEM((D // LANES, rows, LANES), F32)],
        compiler_params=_params(("arbitrary",), VMEM_LIMIT_BYTES),
        name="peer_fold",
    )(peer_subkeys, peer_wq)


def _inproj_kernel(x_ref, g_ref, sc_ref, sh_ref, w_ref, o_ref):
    x = x_ref[...]
    ms = jnp.mean(x * x, axis=-1, keepdims=True)
    y = x * lax.rsqrt(ms + EPS) * g_ref[...]
    h = y * (1.0 + sc_ref[0]) + sh_ref[0]
    o_ref[...] = jnp.dot(h.astype(BF16), w_ref[...], preferred_element_type=F32).astype(BF16)


def _in_proj(xf, g, sc, sh, wp, T, tm=512):
    N, D = xf.shape
    B = sc.shape[0]
    ncol = 3
    tn = PROJ_W // ncol
    return pl.pallas_call(
        _inproj_kernel,
        grid=(ncol, N // tm),
        in_specs=[
            pl.BlockSpec((tm, D), lambda j, i: (i, 0)),
            pl.BlockSpec((1, D), lambda j, i: (0, 0)),
            pl.BlockSpec((1, 1, D), lambda j, i: ((i * tm) // T, 0, 0)),
            pl.BlockSpec((1, 1, D), lambda j, i: ((i * tm) // T, 0, 0)),
            pl.BlockSpec((D, tn), lambda j, i: (0, j)),
        ],
        out_specs=pl.BlockSpec((tm, tn), lambda j, i: (i, j)),
        out_shape=jax.ShapeDtypeStruct((N, PROJ_W), BF16),
        compiler_params=_params(("arbitrary", "arbitrary")),
        name="in_proj",
    )(xf, g.reshape(1, D), sc.reshape(B, 1, D), sh.reshape(B, 1, D), wp)


_SRC = dict(zip(("q", "k", "v", "glr", "gr", "sq", "sk", "sv", "ga", "gb"),
                np.cumsum([0, 512, 512, 1024, GLA_RANK, 1024, 1024, 128, 128, 1024])))
W_IN_SLABS = ((COL_Q, _SRC["q"], 512), (COL_K, _SRC["k"], 512), (COL_V, _SRC["v"], 1024),
              (COL_GR, _SRC["gr"], 1024), (COL_SQ, _SRC["sq"], 1024), (COL_GA, _SRC["ga"], 1024),
              (COL_GB, _SRC["gb"], 1024), (COL_SK, _SRC["sk"], 128), (COL_SV, _SRC["sv"], 128),
              (COL_GLR, _SRC["glr"], GLA_RANK))


def _pack_w_in_kernel(w_ref, o_ref):
    o_ref[:, COL_GLR:] = jnp.zeros((o_ref.shape[0], PROJ_W - COL_GLR), BF16)
    for dst, src, width in W_IN_SLABS:
        o_ref[:, dst:dst + width] = w_ref[:, int(src):int(src) + width].astype(BF16)


def _pack_w_in(w_in, layer, tr=128):
    _, rows, cols = w_in.shape
    return pl.pallas_call(
        _pack_w_in_kernel,
        grid=(rows // tr,),
        in_specs=[pl.BlockSpec((None, tr, cols), lambda i: (layer, i, 0))],
        out_specs=pl.BlockSpec((tr, PROJ_W), lambda i: (i, 0)),
        out_shape=jax.ShapeDtypeStruct((rows, PROJ_W), BF16),
        compiler_params=_params(("arbitrary",)),
        name="pack_w_in",
    )(w_in)


def _gla_kernel(q_ref, k_ref, v_ref, r_ref, glr_ref, w2_ref, b2_ref, ng_ref, o_ref, st_ref, *, nchunk):
    @pl.when(pl.program_id(1) == 0)
    def _():
        st_ref[...] = jnp.zeros_like(st_ref)

    C = GLA_CHUNK
    dk, dv = GLA_DK, GLA_DV
    row = lax.broadcasted_iota(I32, (C, C), 0)
    col = lax.broadcasted_iota(I32, (C, C), 1)
    tri = col <= row
    tri_f = tri.astype(F32)
    w2 = w2_ref[...]
    b2 = b2_ref[...]
    ng = ng_ref[...]
    for ci in range(nchunk):
        sl = pl.ds(ci * C, C)
        z = jnp.dot(glr_ref[sl, :], w2, preferred_element_type=F32) + b2
        log_a = (jnp.minimum(z, 0.0) - jnp.log(1.0 + jnp.exp(-jnp.abs(z)))) * (1.0 / GLA_TAU)
        b = jnp.dot(tri_f, log_a, preferred_element_type=F32, precision=HIGHEST)
        b_last = b[C - 1:C, :]
        q = q_ref[sl, :].astype(F32) * (dk ** -0.5)
        k = k_ref[sl, :].astype(F32)
        q_dec = (q * jnp.exp(b)).astype(BF16)
        k_inv = (k * jnp.exp(-b)).astype(BF16)
        k_tail = (k * jnp.exp(b_last - b)).astype(BF16)
        decay = jnp.exp(b_last)
        for h in range(GLA_HEADS):
            kc = slice(h * dk, (h + 1) * dk)
            vc = slice(h * dv, (h + 1) * dv)
            v = v_ref[sl, vc]
            attn = lax.dot_general(q_dec[:, kc], k_inv[:, kc], NT_DIMS, preferred_element_type=F32)
            attn = jnp.where(tri, attn, 0.0).astype(BF16)
            st = st_ref[h]
            o = (jnp.dot(attn, v, preferred_element_type=F32)
                 + lax.dot_general(q_dec[:, kc], st.astype(BF16), NT_DIMS, preferred_element_type=F32))
            st_ref[h] = st * decay[:, kc] + lax.dot_general(v, k_tail[:, kc], TN_DIMS,
                                                            preferred_element_type=F32)
            on = o * lax.rsqrt(jnp.mean(o * o, axis=-1, keepdims=True) + EPS) * ng
            r = r_ref[sl, vc].astype(F32)
            o_ref[sl, vc] = (on * (r * jax.nn.sigmoid(r))).astype(BF16)


def _gla(proj, w2p, b2, ng, tc=256):
    B, T, _ = proj.shape
    H = GLA_HEADS
    kw, vw = H * GLA_DK, H * GLA_DV
    return pl.pallas_call(
        functools.partial(_gla_kernel, nchunk=tc // GLA_CHUNK),
        grid=(B, T // tc),
        in_specs=[
            pl.BlockSpec((None, tc, kw), lambda b, c: (b, c, COL_Q // kw)),
            pl.BlockSpec((None, tc, kw), lambda b, c: (b, c, COL_K // kw)),
            pl.BlockSpec((None, tc, vw), lambda b, c: (b, c, COL_V // vw)),
            pl.BlockSpec((None, tc, vw), lambda b, c: (b, c, COL_GR // vw)),
            pl.BlockSpec((None, tc, LANES), lambda b, c: (b, c, COL_GLR // LANES)),
            pl.BlockSpec((LANES, kw), lambda b, c: (0, 0)),
            pl.BlockSpec((1, kw), lambda b, c: (0, 0)),
            pl.BlockSpec((1, GLA_DV), lambda b, c: (0, 0)),
        ],
        out_specs=pl.BlockSpec((None, tc, vw), lambda b, c: (b, c, 0)),
        out_shape=jax.ShapeDtypeStruct((B, T, vw), BF16),
        scratch_shapes=[pltpu.VMEM((H, GLA_DV, GLA_DK), F32)],
        compiler_params=_params(("arbitrary", "arbitrary")),
        name="gla",
    )(proj, proj, proj, proj, proj, w2p, b2, ng)


def _t5_bucket(dist):
    max_exact = N_BUCKETS // 2
    d = np.maximum(dist, 1).astype(np.float32)
    large = max_exact + (np.log(d / max_exact) / np.log(MAX_DISTANCE / max_exact)
                         * (N_BUCKETS - max_exact)).astype(np.int32)
    large = np.minimum(large, N_BUCKETS - 1)
    return np.where(dist < max_exact, dist, large).astype(np.int32)


NEG_BIG = -1e30


def _swa_bias(rel_bias):
    blk = SWA_BLOCK
    qi = np.arange(blk)[:, None]
    sj = np.arange(2 * blk)[None, :]
    dist = blk + qi - sj
    band = (dist >= 0) & (dist < blk)
    bucket = _t5_bucket(np.clip(dist, 0, None))
    onehot = jnp.asarray(bucket.reshape(-1, 1) == np.arange(N_BUCKETS)[None, :], F32)
    bias = jnp.dot(onehot, rel_bias.astype(F32), precision=HIGHEST).T.reshape(-1, blk, 2 * blk)
    masks = np.stack([band, band & (sj >= blk)])[:, None]
    return jnp.where(jnp.asarray(masks), bias[None], NEG_BIG)


def _segment_sums(x, seg):
    hi = x.astype(BF16)
    lo = (x - hi.astype(F32)).astype(BF16)
    return (jnp.dot(hi, seg, preferred_element_type=F32) + jnp.dot(lo, seg, preferred_element_type=F32))


def _head_rms_scale(x, seg, seg_t):
    inv = lax.rsqrt(_segment_sums(x * x, seg) * (1.0 / SWA_HD) + EPS)
    return _segment_sums(inv, seg_t)


def _swa_kernel(sink_ref, q_ref, kp_ref, kc_ref, vp_ref, vc_ref, bias_ref, qg_ref, kg_ref, seg_ref,
                segt_ref, o_ref):
    blk = SWA_BLOCK
    hd = SWA_HD
    group = SWA_HEADS // SWA_KV_HEADS
    kvw = SWA_KV_HEADS * hd
    seg = seg_ref[...]
    seg_t = segt_ref[...]
    q = q_ref[...].astype(F32)
    qn = (q * _head_rms_scale(q, seg, seg_t) * qg_ref[...] * (hd ** -0.5)).astype(BF16)
    k2 = jnp.concatenate([kp_ref[...], kc_ref[...]], axis=0).astype(F32)
    kn = (k2 * _head_rms_scale(k2, seg[:kvw], seg_t[:, :kvw]) * kg_ref[...]).astype(BF16)
    v2 = jnp.concatenate([vp_ref[...], vc_ref[...]], axis=0)
    lane = lax.broadcasted_iota(I32, (2 * blk, kvw), 1)
    outs = []
    for kh in range(SWA_KV_HEADS):
        kk = kn[:, kh * hd:(kh + 1) * hd]
        vsh = v2 if kh == 0 else jnp.concatenate([v2[:, kh * hd:], v2[:, :kh * hd]], axis=1)
        vv = jnp.where(lane < hd, vsh, jnp.ones_like(vsh))
        qs = jnp.concatenate([qn[:, (kh * group + g) * hd:(kh * group + g + 1) * hd]
                              for g in range(group)], axis=0)
        logits = lax.dot_general(qs, kk, NT_DIMS, preferred_element_type=F32)
        logits = logits + bias_ref[kh * group:(kh + 1) * group].reshape(group * blk, 2 * blk)
        for g in range(group):
            lg = logits[g * blk:(g + 1) * blk]
            sink = sink_ref[kh * group + g]
            m = jnp.maximum(jnp.max(lg, axis=-1, keepdims=True), sink)
            p = jnp.exp(lg - m).astype(BF16)
            pv = jnp.dot(p, vv, preferred_element_type=F32)
            denom = pv[:, hd:hd + 1] + jnp.exp(sink - m)
            outs.append(pv[:, :hd] / denom)
    o_ref[...] = jnp.concatenate(outs, axis=-1).astype(BF16)


def _swa(proj, bias, qg, kg, sinks):
    B, T, _ = proj.shape
    blk = SWA_BLOCK
    qw = SWA_HEADS * SWA_HD
    kvw = SWA_KV_HEADS * SWA_HD
    prev = lambda b, i: jnp.maximum(i - 1, 0)
    seg = (np.arange(qw)[:, None] // SWA_HD == np.arange(LANES)[None, :])
    seg = jnp.asarray(seg, BF16)
    return pl.pallas_call(
        _swa_kernel,
        grid=(B, T // blk),
        in_specs=[
            pl.BlockSpec(memory_space=pltpu.SMEM),
            pl.BlockSpec((None, blk, qw), lambda b, i: (b, i, COL_SQ // qw)),
            pl.BlockSpec((None, blk, kvw), lambda b, i: (b, prev(b, i), COL_SK // kvw)),
            pl.BlockSpec((None, blk, kvw), lambda b, i: (b, i, COL_SK // kvw)),
            pl.BlockSpec((None, blk, kvw), lambda b, i: (b, prev(b, i), COL_SV // kvw)),
            pl.BlockSpec((None, blk, kvw), lambda b, i: (b, i, COL_SV // kvw)),
            pl.BlockSpec((None, SWA_HEADS, blk, 2 * blk), lambda b, i: (jnp.where(i == 0, 1, 0), 0, 0, 0)),
            pl.BlockSpec((1, qw), lambda b, i: (0, 0)),
            pl.BlockSpec((1, kvw), lambda b, i: (0, 0)),
            pl.BlockSpec((qw, LANES), lambda b, i: (0, 0)),
            pl.BlockSpec((LANES, qw), lambda b, i: (0, 0)),
        ],
        out_specs=pl.BlockSpec((None, blk, qw), lambda b, i: (b, i, 0)),
        out_shape=jax.ShapeDtypeStruct((B, T, qw), BF16),
        compiler_params=_params(("arbitrary", "arbitrary")),
        name="swa",
    )(sinks, proj, proj, proj, proj, proj, bias, jnp.tile(qg, (1, SWA_HEADS)),
      jnp.tile(kg, (1, SWA_KV_HEADS)), seg, seg.T)


def _merge_kernel(x_ref, ga_ref, gb_ref, go_ref, so_ref, wa_ref, wb_ref, wo_ref, g1_ref,
                  n2_ref, sc2_ref, sh2_ref, we_ref, x1_ref, h2_ref, st_ref):
    ya = jnp.dot(go_ref[...], wa_ref[...], preferred_element_type=F32)
    yb = jnp.dot(so_ref[...], wb_ref[...], preferred_element_type=F32)
    m = jax.nn.sigmoid(ga_ref[...].astype(F32)) * ya + jax.nn.sigmoid(gb_ref[...].astype(F32)) * yb
    mixed = jnp.dot(m.astype(BF16), wo_ref[...], preferred_element_type=F32)
    x1 = x_ref[...] + g1_ref[0] * mixed
    x1_ref[...] = x1
    y = x1 * lax.rsqrt(jnp.mean(x1 * x1, axis=-1, keepdims=True) + EPS) * n2_ref[...]
    h2 = y * (1.0 + sc2_ref[0]) + sh2_ref[0]
    h2_ref[...] = h2
    st_ref[...] = lax.dot_general(we_ref[...], h2.astype(BF16), NT_DIMS, preferred_element_type=F32)


def _merge(xf, proj2, gla_o, swa_o, wa, wb, wo, g1, n2, sc2, sh2, weff_t, T, tm=256):
    N, D = xf.shape
    B = g1.shape[0]
    SW = weff_t.shape[0]
    bat = lambda i: ((i * tm) // T, 0, 0)
    full = lambda i: (0, 0)
    return pl.pallas_call(
        _merge_kernel,
        grid=(N // tm,),
        in_specs=[
            pl.BlockSpec((tm, D), lambda i: (i, 0)),
            pl.BlockSpec((tm, D), lambda i: (i, COL_GA // D)),
            pl.BlockSpec((tm, D), lambda i: (i, COL_GB // D)),
            pl.BlockSpec((tm, D), lambda i: (i, 0)),
            pl.BlockSpec((tm, D), lambda i: (i, 0)),
            pl.BlockSpec((D, D), full),
            pl.BlockSpec((D, D), full),
            pl.BlockSpec((D, D), full),
            pl.BlockSpec((1, 1, D), bat),
            pl.BlockSpec((1, D), full),
            pl.BlockSpec((1, 1, D), bat),
            pl.BlockSpec((1, 1, D), bat),
            pl.BlockSpec((SW, D), full),
        ],
        out_specs=[
            pl.BlockSpec((tm, D), lambda i: (i, 0)),
            pl.BlockSpec((tm, D), lambda i: (i, 0)),
            pl.BlockSpec((SW, tm), lambda i: (0, i)),
        ],
        out_shape=[
            jax.ShapeDtypeStruct((N, D), F32),
            jax.ShapeDtypeStruct((N, D), F32),
            jax.ShapeDtypeStruct((SW, N), F32),
        ],
        compiler_params=_params(("arbitrary",), VMEM_LIMIT_BYTES),
        name="merge",
    )(xf, proj2, proj2, gla_o, swa_o, wa, wb, wo, g1.reshape(B, 1, D), n2.reshape(1, D),
      sc2.reshape(B, 1, D), sh2.reshape(B, 1, D), weff_t)


def _tree(op, xs):
    xs = list(xs)
    while len(xs) > 1:
        xs = [op(xs[i], xs[i + 1]) for i in range(0, len(xs) - 1, 2)] + ([xs[-1]] if len(xs) % 2 else [])
    return xs[0]


EXTRACT_CHAINS = 4


def _extract_best(ref, n):
    lanes = min(EXTRACT_CHAINS, n)
    best = [ref[i] for i in range(lanes)]
    for i in range(lanes, n):
        best[i % lanes] = jnp.maximum(best[i % lanes], ref[i])
    m = _tree(jnp.maximum, best)
    first = [jnp.where(ref[i] == m, i, n) for i in range(lanes)]
    for i in range(lanes, n):
        first[i % lanes] = jnp.minimum(first[i % lanes], jnp.where(ref[i] == m, i, n))
    pos = _tree(jnp.minimum, first)
    for i in range(n):
        ref[i] = jnp.where(pos == i, -jnp.inf, ref[i])
    return m, pos


TOPK_PAIRS = tuple((a, b) for a in range(PEER_TOPK) for b in range(PEER_TOPK)
                   if (a + 1) * (b + 1) <= PEER_TOPK)


def _topk_kernel(s_ref, e_ref, g_ref, work_ref, v_ref, i_ref, cand_ref, cidx_ref, best_ref, row_ref):
    K = PEER_TOPK
    nk = PEER_NKEYS
    H = PEER_HEADS
    work_ref[...] = s_ref[...].reshape(work_ref.shape)

    def stage1(k, _):
        m, pos = _extract_best(work_ref, nk)
        v_ref[k] = m
        i_ref[k] = pos
        return 0

    lax.fori_loop(0, K, stage1, 0)

    for ci, (a, b) in enumerate(TOPK_PAIRS):
        cand_ref[ci] = v_ref[a, 0:H, :] + v_ref[b, H:2 * H, :]
        cidx_ref[ci] = (i_ref[a, 0:H, :] * nk + i_ref[b, H:2 * H, :]) * WORDS_PER_EXPERT
    ncand = len(TOPK_PAIRS)

    def stage2(k, _):
        m, pos = _extract_best(cand_ref, ncand)
        row = _tree(jnp.maximum, [jnp.where(pos == ci, cidx_ref[ci], -1) for ci in range(ncand)])
        best_ref[k] = m
        row_ref[k] = row
        return 0

    lax.fori_loop(0, K, stage2, 0)
    best = best_ref[...]
    ex = jnp.exp(best - best[0:1])
    gates = ex / jnp.sum(ex, axis=0, keepdims=True)
    tm = s_ref.shape[1]
    g_ref[...] = gates.reshape(K * H, tm).T
    e_ref[...] = row_ref[...].reshape(K * H, tm).astype(F32).T.astype(I32)


def _topk(s_t, tm=128):
    SW, N = s_t.shape
    K = PEER_TOPK
    G = 2 * PEER_HEADS
    return pl.pallas_call(
        _topk_kernel,
        grid=(N // tm,),
        in_specs=[pl.BlockSpec((SW, tm), lambda i: (0, i))],
        out_specs=[
            pl.BlockSpec((tm, PEER_SLOTS), lambda i: (i, 0)),
            pl.BlockSpec((tm, PEER_SLOTS), lambda i: (i, 0)),
        ],
        out_shape=[
            jax.ShapeDtypeStruct((N, PEER_SLOTS), I32),
            jax.ShapeDtypeStruct((N, PEER_SLOTS), F32),
        ],
        scratch_shapes=[pltpu.VMEM((PEER_NKEYS, G, tm), F32),
                        pltpu.VMEM((K, G, tm), F32),
                        pltpu.VMEM((K, G, tm), I32),
                        pltpu.VMEM((len(TOPK_PAIRS), PEER_HEADS, tm), F32),
                        pltpu.VMEM((len(TOPK_PAIRS), PEER_HEADS, tm), I32),
                        pltpu.VMEM((K, PEER_HEADS, tm), F32),
                        pltpu.VMEM((K, PEER_HEADS, tm), I32)],
        compiler_params=_params(("arbitrary",)),
        name="peer_topk",
    )(s_t)


WORDS_PER_EXPERT = SUBLANES // 2


TILE_BLOCK = tuple((r % 2) * WORDS_PER_EXPERT + r // 2 for r in range(SUBLANES))
BLOCK_ROW = tuple(TILE_BLOCK.index(q) for q in range(SUBLANES))


def _bf16_bits(x):
    return lax.bitcast_convert_type(x.astype(BF16).astype(F32), jnp.uint32)


def _table_kernel(t_ref, o_ref):
    rows = t_ref.shape[0]
    for s in range(WORDS_PER_EXPERT):
        lo = _bf16_bits(t_ref[:, s * LANES:(s + 1) * LANES])
        hi = _bf16_bits(t_ref[:, (s + WORDS_PER_EXPERT) * LANES:(s + WORDS_PER_EXPERT + 1) * LANES])
        word = lax.shift_right_logical(lo, jnp.uint32(16)) | (hi & jnp.uint32(0xFFFF0000))
        o_ref[pl.ds(s, rows, stride=WORDS_PER_EXPERT), :] = lax.bitcast_convert_type(word, I32)


def _table(tables, layer, te=2048):
    _, n, d = tables.shape
    return pl.pallas_call(
        _table_kernel,
        grid=(n // te,),
        in_specs=[pl.BlockSpec((None, te, d), lambda i: (layer, i, 0))],
        out_specs=pl.BlockSpec((te * WORDS_PER_EXPERT, LANES), lambda i: (i, 0)),
        out_shape=jax.ShapeDtypeStruct((n * WORDS_PER_EXPERT, LANES), I32),
        compiler_params=_params(("arbitrary",)),
        name="peer_table",
    )(tables)


def _expert_tile(tab_ref, e4):
    words = tab_ref[pl.ds(pl.multiple_of(e4, WORDS_PER_EXPERT), WORDS_PER_EXPERT), :]
    return pltpu.bitcast(words, BF16).astype(F32)


def _to_tile_rows(x):
    return jnp.concatenate([x[q:q + 1, :] for q in TILE_BLOCK], axis=0)


def _from_tile_rows(x):
    return jnp.concatenate([x[r:r + 1, :] for r in BLOCK_ROW], axis=0)


BITREV = (0, 4, 2, 6, 1, 5, 3, 7)


def _fold8(ps, sub):
    m4 = sub < 4
    m2 = (sub & 3) < 2
    m1 = (sub & 1) < 1
    s1 = []
    for a, b in zip(ps[0::2], ps[1::2]):
        s1.append(jnp.where(m4, a, b) + pltpu.roll(jnp.where(m4, b, a), 4, 0))
    s2 = []
    for a, b in zip(s1[0::2], s1[1::2]):
        s2.append(jnp.where(m2, a, pltpu.roll(b, 2, 0)) + jnp.where(m2, pltpu.roll(a, 6, 0), b))
    a, b = s2
    return jnp.where(m1, a, pltpu.roll(b, 1, 0)) + jnp.where(m1, pltpu.roll(a, 7, 0), b)


def _gelu_tanh(x):
    return 0.5 * x * (1.0 + jnp.tanh(np.sqrt(2.0 / np.pi) * (x + 0.044715 * (x * x * x))))


def _row_sums(x):
    ones = jnp.ones((LANES, LANES), BF16)
    hi = x.astype(BF16)
    lo = (x - hi.astype(F32)).astype(BF16)
    return (jnp.dot(hi, ones, preferred_element_type=F32)
            + jnp.dot(lo, ones, preferred_element_type=F32))


RING_SLOTS = 2
SLOT_TOKENS = 8


def _for_each_token(e_ref, idx_ref, sem, tb, token_body):
    nquads = tb // SLOT_TOKENS

    def fetch(q, slot):
        return pltpu.make_async_copy(e_ref.at[q], idx_ref.at[slot], sem.at[slot])

    for slot in range(RING_SLOTS):
        fetch(slot, slot).start()

    def ring(j, _):
        for slot in range(RING_SLOTS):
            q = j * RING_SLOTS + slot
            fetch(q, slot).wait()
            for tt in range(SLOT_TOKENS):
                token_body(q * SLOT_TOKENS + tt, lambda k, slot=slot, tt=tt: idx_ref[slot, tt, k],
                           slot * SLOT_TOKENS + tt)
            fetch(jnp.minimum(q + RING_SLOTS, nquads - 1), slot).start()
        return 0

    lax.fori_loop(0, nquads // RING_SLOTS, ring, 0)
    for slot in range(RING_SLOTS):
        fetch(nquads - 1, slot).wait()


def _peer_u_kernel(e_ref, h_ref, g_ref, tab_ref, o_ref, slab_ref, *, tb):
    sub = lax.broadcasted_iota(I32, (SUBLANES, LANES), 0)
    groups_per_token = PEER_SLOTS // SUBLANES

    def token(t, _):
        h = _to_tile_rows(h_ref[t])
        for j in range(groups_per_token):
            ps = [_expert_tile(tab_ref, e_ref[t, j * SUBLANES + BITREV[qn]]) * h
                  for qn in range(SUBLANES)]
            dst = pl.multiple_of(t * PEER_SLOTS + j * SUBLANES, SUBLANES)
            slab_ref[pl.ds(dst, SUBLANES), :] = _fold8(ps, sub)
        return 0

    lax.fori_loop(0, tb, token, 0)
    rio = lax.broadcasted_iota(I32, (LANES, LANES), 0)
    lio = lax.broadcasted_iota(I32, (LANES, LANES), 1)
    diag = (rio == lio)[None]
    rows = SUBLANES * PEER_SLOTS
    for gi in range(tb // SUBLANES):
        tok = slice(gi * SUBLANES, (gi + 1) * SUBLANES)
        rs = jnp.sum(slab_ref[gi * rows:(gi + 1) * rows, :], axis=-1, keepdims=True)
        rs = rs.reshape(SUBLANES, PEER_SLOTS, 1)
        a = jnp.sum(jnp.where(diag, rs, 0.0), axis=1)
        o_ref[tok, :] = g_ref[tok, :] * _gelu_tanh(a)


def _peer_u(e4, h3, gates, tab, tb=128):
    N = h3.shape[0]
    assert tb % SUBLANES == 0
    return pl.pallas_call(
        functools.partial(_peer_u_kernel, tb=tb),
        grid=(N // tb,),
        in_specs=[
            pl.BlockSpec((tb, PEER_SLOTS), lambda i: (i, 0), memory_space=pltpu.SMEM),
            pl.BlockSpec((tb, SUBLANES, LANES), lambda i: (i, 0, 0)),
            pl.BlockSpec((tb, PEER_SLOTS), lambda i: (i, 0)),
            pl.BlockSpec(tab.shape, lambda i: (0, 0), pipeline_mode=pl.Buffered(1)),
        ],
        out_specs=pl.BlockSpec((tb, PEER_SLOTS), lambda i: (i, 0)),
        out_shape=jax.ShapeDtypeStruct((N, PEER_SLOTS), F32),
        scratch_shapes=[pltpu.VMEM((tb * PEER_SLOTS, LANES), F32)],
        compiler_params=_params(("arbitrary",), VMEM_LIMIT_BYTES),
        name="peer_u",
    )(e4, h3, gates, tab)


def _peer_v_kernel(e_ref, c_ref, x1_ref, g2_ref, tab_ref, o_ref, splat_ref, idx_ref, sem, *, tb):
    nacc = 3
    rio = lax.broadcasted_iota(I32, (LANES, LANES), 0)
    lio = lax.broadcasted_iota(I32, (LANES, LANES), 1)
    diag = (rio == lio)[None]
    for t8 in range(tb // SUBLANES):
        cc = c_ref[t8 * SUBLANES:(t8 + 1) * SUBLANES, :]
        s = jnp.sum(jnp.where(diag, cc[:, None, :], 0.0), axis=-1, keepdims=True)
        splat_ref[t8 * SUBLANES:(t8 + 1) * SUBLANES] = jnp.broadcast_to(s, (SUBLANES, LANES, LANES))

    def token(t, row, pos):
        del pos
        accs = [jnp.zeros((SUBLANES, LANES), F32) for _ in range(nacc)]
        for k in range(PEER_SLOTS):
            c = jnp.broadcast_to(splat_ref[t, k:k + 1, :], (SUBLANES, LANES))
            accs[k % nacc] = accs[k % nacc] + c * _expert_tile(tab_ref, row(k))
        y = _tree(jnp.add, accs)
        o_ref[t] = x1_ref[t] + g2_ref[0] * _from_tile_rows(y)

    _for_each_token(e_ref, idx_ref, sem, tb, token)


def _peer_v(e4, coef, x1_3, g2_3, tab, T, tb=128):
    N = x1_3.shape[0]
    assert tb % (RING_SLOTS * SLOT_TOKENS) == 0 and tb % SUBLANES == 0
    return pl.pallas_call(
        functools.partial(_peer_v_kernel, tb=tb),
        grid=(N // tb,),
        in_specs=[
            pl.BlockSpec((tb // SLOT_TOKENS, SLOT_TOKENS, PEER_SLOTS), lambda i: (i, 0, 0)),
            pl.BlockSpec((tb, PEER_SLOTS), lambda i: (i, 0)),
            pl.BlockSpec((tb, SUBLANES, LANES), lambda i: (i, 0, 0)),
            pl.BlockSpec((1, SUBLANES, LANES), lambda i: ((i * tb) // T, 0, 0)),
            pl.BlockSpec(tab.shape, lambda i: (0, 0), pipeline_mode=pl.Buffered(1)),
        ],
        out_specs=pl.BlockSpec((tb, SUBLANES, LANES), lambda i: (i, 0, 0)),
        out_shape=jax.ShapeDtypeStruct((N, SUBLANES, LANES), F32),
        scratch_shapes=[pltpu.VMEM((tb, LANES, LANES), F32),
                        pltpu.SMEM((RING_SLOTS, SLOT_TOKENS, PEER_SLOTS), I32),
                        pltpu.SemaphoreType.DMA((RING_SLOTS,))],
        compiler_params=_params(("arbitrary",), VMEM_LIMIT_BYTES),
        name="peer_v",
    )(e4.reshape(N // SLOT_TOKENS, SLOT_TOKENS, PEER_SLOTS), coef, x1_3, g2_3, tab)


def kernel(x, c, w_ada, b_ada, norm1_g, w_in, gla_gate_w2, gla_gate_b, gla_norm_g, swa_qnorm_g,
           swa_knorm_g, swa_sinks, rel_bias, w_up_a, w_up_b, w_out, norm2_g, peer_wq, peer_subkeys,
           peer_u, peer_v):
    B, T, D = x.shape
    N = B * T
    L = w_ada.shape[0]
    mod = _adaln(c, w_ada, b_ada)
    weff_t = _fold_peer_keys(peer_wq, peer_subkeys)
    bias = _swa_bias(rel_bias)
    xf = x.reshape(N, D)
    for l in range(L):
        sh1, sc1, g1, sh2, sc2, g2 = [mod[l, :, i * D:(i + 1) * D] for i in range(6)]
        proj = _in_proj(xf, norm1_g[l], sc1, sh1, _pack_w_in(w_in, l), T)
        proj3 = proj.reshape(B, T, PROJ_W)
        w2p = jnp.zeros((LANES, GLA_HEADS * GLA_DK), BF16).at[:GLA_RANK].set(gla_gate_w2[l].astype(BF16))
        gla_o = _gla(proj3, w2p, gla_gate_b[l].reshape(1, -1), gla_norm_g[l].reshape(1, -1))
        swa_o = _swa(proj3, bias, swa_qnorm_g[l].reshape(1, -1), swa_knorm_g[l].reshape(1, -1),
                     swa_sinks[l])
        x1, h2, s_t = _merge(xf, proj, gla_o.reshape(N, -1), swa_o.reshape(N, -1),
                             w_up_a[l].astype(BF16), w_up_b[l].astype(BF16), w_out[l].astype(BF16),
                             g1, norm2_g[l], sc2, sh2, weff_t[l], T)
        e4, gates = _topk(s_t)
        coef = _peer_u(e4, h2.reshape(N, SUBLANES, LANES), gates, _table(peer_u, l))
        g2_3 = g2.reshape(B, SUBLANES, LANES)
        xf = _peer_v(e4, coef, x1.reshape(N, SUBLANES, LANES), g2_3, _table(peer_v, l), T).reshape(N, D)
    return xf.reshape(B, T, D)
```

```python
import functools

import numpy as np
import jax
import jax.numpy as jnp
from jax import lax
from jax.experimental import pallas as pl
from jax.experimental.pallas import tpu as pltpu

F32 = jnp.float32
BF16 = jnp.bfloat16
I32 = jnp.int32
HIGHEST = lax.Precision.HIGHEST
EPS = 1e-6

GLA_HEADS = 4
GLA_DK = 128
GLA_DV = 256
GLA_RANK = 16
GLA_TAU = 16.0
GLA_CHUNK = 64
SWA_HEADS = 16
SWA_KV_HEADS = 2
SWA_HD = 64
SWA_BLOCK = 128
N_BUCKETS = 32
MAX_DISTANCE = 128
PEER_HEADS = 8
PEER_NKEYS = 128
PEER_TOPK = 16
PEER_SLOTS = PEER_HEADS * PEER_TOPK

SUBLANES = 8
LANES = 128
VMEM_LIMIT_BYTES = 56 * 1024 * 1024

NT_DIMS = (((1,), (1,)), ((), ()))
TN_DIMS = (((0,), (0,)), ((), ()))

COL_Q, COL_K, COL_V, COL_GR, COL_SQ, COL_GA, COL_GB = 0, 512, 1024, 2048, 3072, 4096, 5120
COL_SK, COL_SV, COL_GLR = 6144, 6272, 6400
PROJ_W = 6528


def _params(sem, vmem=None):
    return pltpu.CompilerParams(dimension_semantics=sem, vmem_limit_bytes=vmem)


def _adaln_kernel(c_ref, w_ref, b_ref, o_ref):
    c = c_ref[...]
    a = c * jax.nn.sigmoid(c)
    o_ref[0] = jnp.dot(a, w_ref[0], preferred_element_type=F32, precision=HIGHEST) + b_ref[0]


def _adaln(c, w_ada, b_ada):
    L, D, W = w_ada.shape
    B = c.shape[0]
    rows = -(-B // SUBLANES) * SUBLANES
    cp = jnp.zeros((rows, D), F32).at[:B].set(c)
    tn = W // 4
    out = pl.pallas_call(
        _adaln_kernel,
        grid=(L, W // tn),
        in_specs=[
            pl.BlockSpec((rows, D), lambda l, j: (0, 0)),
            pl.BlockSpec((1, D, tn), lambda l, j: (l, 0, j)),
            pl.BlockSpec((1, 1, tn), lambda l, j: (l, 0, j)),
        ],
        out_specs=pl.BlockSpec((1, rows, tn), lambda l, j: (l, 0, j)),
        out_shape=jax.ShapeDtypeStruct((L, rows, W), F32),
        compiler_params=_params(("arbitrary", "arbitrary")),
        name="adaln",
    )(cp, w_ada, b_ada.reshape(L, 1, W))
    return out[:, :B]


def _fold_kernel(sk_ref, wq_ref, o_ref, rows_ref):
    half = sk_ref.shape[-1]
    groups = 2 * PEER_HEADS
    for h in range(PEER_HEADS):
        for p in range(2):
            g = 2 * h + p
            res = lax.dot_general(sk_ref[0, p], wq_ref[0, :, g * half:(g + 1) * half], NT_DIMS,
                                  precision=HIGHEST, preferred_element_type=F32)
            for cb in range(rows_ref.shape[0]):
                rows_ref[cb, pl.ds(p * PEER_HEADS + h, PEER_NKEYS, stride=groups), :] = (
                    res[:, cb * LANES:(cb + 1) * LANES])
    for cb in range(rows_ref.shape[0]):
        o_ref[0, :, cb * LANES:(cb + 1) * LANES] = rows_ref[cb].astype(BF16)


def _fold_peer_keys(peer_wq, peer_subkeys):
    L, D, QW = peer_wq.shape
    half = peer_subkeys.shape[-1]
    rows = (QW // half) * PEER_NKEYS
    return pl.pallas_call(
        _fold_kernel,
        grid=(L,),
        in_specs=[
            pl.BlockSpec((1, 2, PEER_NKEYS, half), lambda l: (l, 0, 0, 0)),
            pl.BlockSpec((1, D, QW), lambda l: (l, 0, 0)),
        ],
        out_specs=pl.BlockSpec((1, rows, D), lambda l: (l, 0, 0)),
        out_shape=jax.ShapeDtypeStruct((L, rows, D), BF16),
        scratch_shapes=[pltpu.VMEM((D // LANES, rows, LANES), F32)],
        compiler_params=_params(("arbitrary",), VMEM_LIMIT_BYTES),
        name="peer_fold",
    )(peer_subkeys, peer_wq)


def _inproj_kernel(x_ref, g_ref, sc_ref, sh_ref, w_ref, o_ref):
    x = x_ref[...]
    ms = jnp.mean(x * x, axis=-1, keepdims=True)
    y = x * lax.rsqrt(ms + EPS) * g_ref[...]
    h = y * (1.0 + sc_ref[0]) + sh_ref[0]
    o_ref[...] = jnp.dot(h.astype(BF16), w_ref[...], preferred_element_type=F32).astype(BF16)


def _in_proj(xf, g, sc, sh, wp, T, tm=512):
    N, D = xf.shape
    B = sc.shape[0]
    ncol = 3
    tn = PROJ_W // ncol
    return pl.pallas_call(
        _inproj_kernel,
        grid=(ncol, N // tm),
        in_specs=[
            pl.BlockSpec((tm, D), lambda j, i: (i, 0)),
            pl.BlockSpec((1, D), lambda j, i: (0, 0)),
            pl.BlockSpec((1, 1, D), lambda j, i: ((i * tm) // T, 0, 0)),
            pl.BlockSpec((1, 1, D), lambda j, i: ((i * tm) // T, 0, 0)),
            pl.BlockSpec((D, tn), lambda j, i: (0, j)),
        ],
        out_specs=pl.BlockSpec((tm, tn), lambda j, i: (i, j)),
        out_shape=jax.ShapeDtypeStruct((N, PROJ_W), BF16),
        compiler_params=_params(("arbitrary", "arbitrary")),
        name="in_proj",
    )(xf, g.reshape(1, D), sc.reshape(B, 1, D), sh.reshape(B, 1, D), wp)


_SRC = dict(zip(("q", "k", "v", "glr", "gr", "sq", "sk", "sv", "ga", "gb"),
                np.cumsum([0, 512, 512, 1024, GLA_RANK, 1024, 1024, 128, 128, 1024])))
W_IN_SLABS = ((COL_Q, _SRC["q"], 512), (COL_K, _SRC["k"], 512), (COL_V, _SRC["v"], 1024),
              (COL_GR, _SRC["gr"], 1024), (COL_SQ, _SRC["sq"], 1024), (COL_GA, _SRC["ga"], 1024),
              (COL_GB, _SRC["gb"], 1024), (COL_SK, _SRC["sk"], 128), (COL_SV, _SRC["sv"], 128),
              (COL_GLR, _SRC["glr"], GLA_RANK))


def _pack_w_in_kernel(w_ref, o_ref):
    o_ref[:, COL_GLR:] = jnp.zeros((o_ref.shape[0], PROJ_W - COL_GLR), BF16)
    for dst, src, width in W_IN_SLABS:
        o_ref[:, dst:dst + width] = w_ref[:, int(src):int(src) + width].astype(BF16)


def _pack_w_in(w_in, layer, tr=128):
    _, rows, cols = w_in.shape
    return pl.pallas_call(
        _pack_w_in_kernel,
        grid=(rows // tr,),
        in_specs=[pl.BlockSpec((None, tr, cols), lambda i: (layer, i, 0))],
        out_specs=pl.BlockSpec((tr, PROJ_W), lambda i: (i, 0)),
        out_shape=jax.ShapeDtypeStruct((rows, PROJ_W), BF16),
        compiler_params=_params(("arbitrary",)),
        name="pack_w_in",
    )(w_in)


def _gla_kernel(q_ref, k_ref, v_ref, r_ref, glr_ref, w2_ref, b2_ref, ng_ref, o_ref, st_ref, *, nchunk):
    @pl.when(pl.program_id(1) == 0)
    def _():
        st_ref[...] = jnp.zeros_like(st_ref)

    C = GLA_CHUNK
    dk, dv = GLA_DK, GLA_DV
    row = lax.broadcasted_iota(I32, (C, C), 0)
    col = lax.broadcasted_iota(I32, (C, C), 1)
    tri = col <= row
    tri_f = tri.astype(F32)
    w2 = w2_ref[...]
    b2 = b2_ref[...]
    ng = ng_ref[...]
    for ci in range(nchunk):
        sl = pl.ds(ci * C, C)
        z = jnp.dot(glr_ref[sl, :], w2, preferred_element_type=F32) + b2
        log_a = (jnp.minimum(z, 0.0) - jnp.log(1.0 + jnp.exp(-jnp.abs(z)))) * (1.0 / GLA_TAU)
        b = jnp.dot(tri_f, log_a, preferred_element_type=F32, precision=HIGHEST)
        b_last = b[C - 1:C, :]
        q = q_ref[sl, :].astype(F32) * (dk ** -0.5)
        k = k_ref[sl, :].astype(F32)
        q_dec = (q * jnp.exp(b)).astype(BF16)
        k_inv = (k * jnp.exp(-b)).astype(BF16)
        k_tail = (k * jnp.exp(b_last - b)).astype(BF16)
        decay = jnp.exp(b_last)
        for h in range(GLA_HEADS):
            kc = slice(h * dk, (h + 1) * dk)
            vc = slice(h * dv, (h + 1) * dv)
            v = v_ref[sl, vc]
            attn = lax.dot_general(q_dec[:, kc], k_inv[:, kc], NT_DIMS, preferred_element_type=F32)
            attn = jnp.where(tri, attn, 0.0).astype(BF16)
            st = st_ref[h]
            o = (jnp.dot(attn, v, preferred_element_type=F32)
                 + lax.dot_general(q_dec[:, kc], st.astype(BF16), NT_DIMS, preferred_element_type=F32))
            st_ref[h] = st * decay[:, kc] + lax.dot_general(v, k_tail[:, kc], TN_DIMS,
                                                            preferred_element_type=F32)
            on = o * lax.rsqrt(jnp.mean(o * o, axis=-1, keepdims=True) + EPS) * ng
            r = r_ref[sl, vc].astype(F32)
            o_ref[sl, vc] = (on * (r * jax.nn.sigmoid(r))).astype(BF16)


def _gla(proj, w2p, b2, ng, tc=256):
    B, T, _ = proj.shape
    H = GLA_HEADS
    kw, vw = H * GLA_DK, H * GLA_DV
    return pl.pallas_call(
        functools.partial(_gla_kernel, nchunk=tc // GLA_CHUNK),
        grid=(B, T // tc),
        in_specs=[
            pl.BlockSpec((None, tc, kw), lambda b, c: (b, c, COL_Q // kw)),
            pl.BlockSpec((None, tc, kw), lambda b, c: (b, c, COL_K // kw)),
            pl.BlockSpec((None, tc, vw), lambda b, c: (b, c, COL_V // vw)),
            pl.BlockSpec((None, tc, vw), lambda b, c: (b, c, COL_GR // vw)),
            pl.BlockSpec((None, tc, LANES), lambda b, c: (b, c, COL_GLR // LANES)),
            pl.BlockSpec((LANES, kw), lambda b, c: (0, 0)),
            pl.BlockSpec((1, kw), lambda b, c: (0, 0)),
            pl.BlockSpec((1, GLA_DV), lambda b, c: (0, 0)),
        ],
        out_specs=pl.BlockSpec((None, tc, vw), lambda b, c: (b, c, 0)),
        out_shape=jax.ShapeDtypeStruct((B, T, vw), BF16),
        scratch_shapes=[pltpu.VMEM((H, GLA_DV, GLA_DK), F32)],
        compiler_params=_params(("arbitrary", "arbitrary")),
        name="gla",
    )(proj, proj, proj, proj, proj, w2p, b2, ng)


def _t5_bucket(dist):
    max_exact = N_BUCKETS // 2
    d = np.maximum(dist, 1).astype(np.float32)
    large = max_exact + (np.log(d / max_exact) / np.log(MAX_DISTANCE / max_exact)
                         * (N_BUCKETS - max_exact)).astype(np.int32)
    large = np.minimum(large, N_BUCKETS - 1)
    return np.where(dist < max_exact, dist, large).astype(np.int32)


NEG_BIG = -1e30


def _swa_bias(rel_bias):
    blk = SWA_BLOCK
    qi = np.arange(blk)[:, None]
    sj = np.arange(2 * blk)[None, :]
    dist = blk + qi - sj
    band = (dist >= 0) & (dist < blk)
    bucket = _t5_bucket(np.clip(dist, 0, None))
    onehot = jnp.asarray(bucket.reshape(-1, 1) == np.arange(N_BUCKETS)[None, :], F32)
    bias = jnp.dot(onehot, rel_bias.astype(F32), precision=HIGHEST).T.reshape(-1, blk, 2 * blk)
    masks = np.stack([band, band & (sj >= blk)])[:, None]
    return jnp.where(jnp.asarray(masks), bias[None], NEG_BIG)


def _segment_sums(x, seg):
    hi = x.astype(BF16)
    lo = (x - hi.astype(F32)).astype(BF16)
    return (jnp.dot(hi, seg, preferred_element_type=F32) + jnp.dot(lo, seg, preferred_element_type=F32))


def _head_rms_scale(x, seg, seg_t):
    inv = lax.rsqrt(_segment_sums(x * x, seg) * (1.0 / SWA_HD) + EPS)
    return _segment_sums(inv, seg_t)


def _swa_kernel(sink_ref, q_ref, kp_ref, kc_ref, vp_ref, vc_ref, bias_ref, qg_ref, kg_ref, seg_ref,
                segt_ref, o_ref):
    blk = SWA_BLOCK
    hd = SWA_HD
    group = SWA_HEADS // SWA_KV_HEADS
    kvw = SWA_KV_HEADS * hd
    seg = seg_ref[...]
    seg_t = segt_ref[...]
    q = q_ref[...].astype(F32)
    qn = (q * _head_rms_scale(q, seg, seg_t) * qg_ref[...] * (hd ** -0.5)).astype(BF16)
    k2 = jnp.concatenate([kp_ref[...], kc_ref[...]], axis=0).astype(F32)
    kn = (k2 * _head_rms_scale(k2, seg[:kvw], seg_t[:, :kvw]) * kg_ref[...]).astype(BF16)
    v2 = jnp.concatenate([vp_ref[...], vc_ref[...]], axis=0)
    lane = lax.broadcasted_iota(I32, (2 * blk, kvw), 1)
    outs = []
    for kh in range(SWA_KV_HEADS):
        kk = kn[:, kh * hd:(kh + 1) * hd]
        vsh = v2 if kh == 0 else jnp.concatenate([v2[:, kh * hd:], v2[:, :kh * hd]], axis=1)
        vv = jnp.where(lane < hd, vsh, jnp.ones_like(vsh))
        qs = jnp.concatenate([qn[:, (kh * group + g) * hd:(kh * group + g + 1) * hd]
                              for g in range(group)], axis=0)
        logits = lax.dot_general(qs, kk, NT_DIMS, preferred_element_type=F32)
        logits = logits + bias_ref[kh * group:(kh + 1) * group].reshape(group * blk, 2 * blk)
        for g in range(group):
            lg = logits[g * blk:(g + 1) * blk]
            sink = sink_ref[kh * group + g]
            m = jnp.maximum(jnp.max(lg, axis=-1, keepdims=True), sink)
            p = jnp.exp(lg - m).astype(BF16)
            pv = jnp.dot(p, vv, preferred_element_type=F32)
            denom = pv[:, hd:hd + 1] + jnp.exp(sink - m)
            outs.append(pv[:, :hd] / denom)
    o_ref[...] = jnp.concatenate(outs, axis=-1).astype(BF16)


def _swa(proj, bias, qg, kg, sinks):
    B, T, _ = proj.shape
    blk = SWA_BLOCK
    qw = SWA_HEADS * SWA_HD
    kvw = SWA_KV_HEADS * SWA_HD
    prev = lambda b, i: jnp.maximum(i - 1, 0)
    seg = (np.arange(qw)[:, None] // SWA_HD == np.arange(LANES)[None, :])
    seg = jnp.asarray(seg, BF16)
    return pl.pallas_call(
        _swa_kernel,
        grid=(B, T // blk),
        in_specs=[
            pl.BlockSpec(memory_space=pltpu.SMEM),
            pl.BlockSpec((None, blk, qw), lambda b, i: (b, i, COL_SQ // qw)),
            pl.BlockSpec((None, blk, kvw), lambda b, i: (b, prev(b, i), COL_SK // kvw)),
            pl.BlockSpec((None, blk, kvw), lambda b, i: (b, i, COL_SK // kvw)),
            pl.BlockSpec((None, blk, kvw), lambda b, i: (b, prev(b, i), COL_SV // kvw)),
            pl.BlockSpec((None, blk, kvw), lambda b, i: (b, i, COL_SV // kvw)),
            pl.BlockSpec((None, SWA_HEADS, blk, 2 * blk), lambda b, i: (jnp.where(i == 0, 1, 0), 0, 0, 0)),
            pl.BlockSpec((1, qw), lambda b, i: (0, 0)),
            pl.BlockSpec((1, kvw), lambda b, i: (0, 0)),
            pl.BlockSpec((qw, LANES), lambda b, i: (0, 0)),
            pl.BlockSpec((LANES, qw), lambda b, i: (0, 0)),
        ],
        out_specs=pl.BlockSpec((None, blk, qw), lambda b, i: (b, i, 0)),
        out_shape=jax.ShapeDtypeStruct((B, T, qw), BF16),
        compiler_params=_params(("arbitrary", "arbitrary")),
        name="swa",
    )(sinks, proj, proj, proj, proj, proj, bias, jnp.tile(qg, (1, SWA_HEADS)),
      jnp.tile(kg, (1, SWA_KV_HEADS)), seg, seg.T)


def _merge_kernel(x_ref, ga_ref, gb_ref, go_ref, so_ref, wa_ref, wb_ref, wo_ref, g1_ref,
                  n2_ref, sc2_ref, sh2_ref, we_ref, x1_ref, h2_ref, st_ref):
    ya = jnp.dot(go_ref[...], wa_ref[...], preferred_element_type=F32)
    yb = jnp.dot(so_ref[...], wb_ref[...], preferred_element_type=F32)
    m = jax.nn.sigmoid(ga_ref[...].astype(F32)) * ya + jax.nn.sigmoid(gb_ref[...].astype(F32)) * yb
    mixed = jnp.dot(m.astype(BF16), wo_ref[...], preferred_element_type=F32)
    x1 = x_ref[...] + g1_ref[0] * mixed
    x1_ref[...] = x1
    y = x1 * lax.rsqrt(jnp.mean(x1 * x1, axis=-1, keepdims=True) + EPS) * n2_ref[...]
    h2 = y * (1.0 + sc2_ref[0]) + sh2_ref[0]
    h2_ref[...] = h2
    st_ref[...] = lax.dot_general(we_ref[...], h2.astype(BF16), NT_DIMS, preferred_element_type=F32)


def _merge(xf, proj2, gla_o, swa_o, wa, wb, wo, g1, n2, sc2, sh2, weff_t, T, tm=256):
    N, D = xf.shape
    B = g1.shape[0]
    SW = weff_t.shape[0]
    bat = lambda i: ((i * tm) // T, 0, 0)
    full = lambda i: (0, 0)
    return pl.pallas_call(
        _merge_kernel,
        grid=(N // tm,),
        in_specs=[
            pl.BlockSpec((tm, D), lambda i: (i, 0)),
            pl.BlockSpec((tm, D), lambda i: (i, COL_GA // D)),
            pl.BlockSpec((tm, D), lambda i: (i, COL_GB // D)),
            pl.BlockSpec((tm, D), lambda i: (i, 0)),
            pl.BlockSpec((tm, D), lambda i: (i, 0)),
            pl.BlockSpec((D, D), full),
            pl.BlockSpec((D, D), full),
            pl.BlockSpec((D, D), full),
            pl.BlockSpec((1, 1, D), bat),
            pl.BlockSpec((1, D), full),
            pl.BlockSpec((1, 1, D), bat),
            pl.BlockSpec((1, 1, D), bat),
            pl.BlockSpec((SW, D), full),
        ],
        out_specs=[
            pl.BlockSpec((tm, D), lambda i: (i, 0)),
            pl.BlockSpec((tm, D), lambda i: (i, 0)),
            pl.BlockSpec((SW, tm), lambda i: (0, i)),
        ],
        out_shape=[
            jax.ShapeDtypeStruct((N, D), F32),
            jax.ShapeDtypeStruct((N, D), F32),
            jax.ShapeDtypeStruct((SW, N), F32),
        ],
        compiler_params=_params(("arbitrary",), VMEM_LIMIT_BYTES),
        name="merge",
    )(xf, proj2, proj2, gla_o, swa_o, wa, wb, wo, g1.reshape(B, 1, D), n2.reshape(1, D),
      sc2.reshape(B, 1, D), sh2.reshape(B, 1, D), weff_t)


def _tree(op, xs):
    xs = list(xs)
    while len(xs) > 1:
        xs = [op(xs[i], xs[i + 1]) for i in range(0, len(xs) - 1, 2)] + ([xs[-1]] if len(xs) % 2 else [])
    return xs[0]


EXTRACT_CHAINS = 4


def _extract_best(ref, n):
    lanes = min(EXTRACT_CHAINS, n)
    best = [ref[i] for i in range(lanes)]
    for i in range(lanes, n):
        best[i % lanes] = jnp.maximum(best[i % lanes], ref[i])
    m = _tree(jnp.maximum, best)
    first = [jnp.where(ref[i] == m, i, n) for i in range(lanes)]
    for i in range(lanes, n):
        first[i % lanes] = jnp.minimum(first[i % lanes], jnp.where(ref[i] == m, i, n))
    pos = _tree(jnp.minimum, first)
    for i in range(n):
        ref[i] = jnp.where(pos == i, -jnp.inf, ref[i])
    return m, pos


TOPK_PAIRS = tuple((a, b) for a in range(PEER_TOPK) for b in range(PEER_TOPK)
                   if (a + 1) * (b + 1) <= PEER_TOPK)


def _topk_kernel(s_ref, e_ref, g_ref, work_ref, v_ref, i_ref, cand_ref, cidx_ref, best_ref, row_ref):
    K = PEER_TOPK
    nk = PEER_NKEYS
    H = PEER_HEADS
    work_ref[...] = s_ref[...].reshape(work_ref.shape)

    def stage1(k, _):
        m, pos = _extract_best(work_ref, nk)
        v_ref[k] = m
        i_ref[k] = pos
        return 0

    lax.fori_loop(0, K, stage1, 0)

    for ci, (a, b) in enumerate(TOPK_PAIRS):
        cand_ref[ci] = v_ref[a, 0:H, :] + v_ref[b, H:2 * H, :]
        cidx_ref[ci] = (i_ref[a, 0:H, :] * nk + i_ref[b, H:2 * H, :]) * WORDS_PER_EXPERT
    ncand = len(TOPK_PAIRS)

    def stage2(k, _):
        m, pos = _extract_best(cand_ref, ncand)
        row = _tree(jnp.maximum, [jnp.where(pos == ci, cidx_ref[ci], -1) for ci in range(ncand)])
        best_ref[k] = m
        row_ref[k] = row
        return 0

    lax.fori_loop(0, K, stage2, 0)
    best = best_ref[...]
    ex = jnp.exp(best - best[0:1])
    gates = ex / jnp.sum(ex, axis=0, keepdims=True)
    tm = s_ref.shape[1]
    g_ref[...] = gates.reshape(K * H, tm).T
    e_ref[...] = row_ref[...].reshape(K * H, tm).astype(F32).T.astype(I32)


def _topk(s_t, tm=128):
    SW, N = s_t.shape
    K = PEER_TOPK
    G = 2 * PEER_HEADS
    return pl.pallas_call(
        _topk_kernel,
        grid=(N // tm,),
        in_specs=[pl.BlockSpec((SW, tm), lambda i: (0, i))],
        out_specs=[
            pl.BlockSpec((tm, PEER_SLOTS), lambda i: (i, 0)),
            pl.BlockSpec((tm, PEER_SLOTS), lambda i: (i, 0)),
        ],
        out_shape=[
            jax.ShapeDtypeStruct((N, PEER_SLOTS), I32),
            jax.ShapeDtypeStruct((N, PEER_SLOTS), F32),
        ],
        scratch_shapes=[pltpu.VMEM((PEER_NKEYS, G, tm), F32),
                        pltpu.VMEM((K, G, tm), F32),
                        pltpu.VMEM((K, G, tm), I32),
                        pltpu.VMEM((len(TOPK_PAIRS), PEER_HEADS, tm), F32),
                        pltpu.VMEM((len(TOPK_PAIRS), PEER_HEADS, tm), I32),
                        pltpu.VMEM((K, PEER_HEADS, tm), F32),
                        pltpu.VMEM((K, PEER_HEADS, tm), I32)],
        compiler_params=_params(("arbitrary",)),
        name="peer_topk",
    )(s_t)


WORDS_PER_EXPERT = SUBLANES // 2


TILE_BLOCK = tuple((r % 2) * WORDS_PER_EXPERT + r // 2 for r in range(SUBLANES))
BLOCK_ROW = tuple(TILE_BLOCK.index(q) for q in range(SUBLANES))


def _bf16_bits(x):
    return lax.bitcast_convert_type(x.astype(BF16).astype(F32), jnp.uint32)


def _table_kernel(t_ref, o_ref):
    rows = t_ref.shape[0]
    for s in range(WORDS_PER_EXPERT):
        lo = _bf16_bits(t_ref[:, s * LANES:(s + 1) * LANES])
        hi = _bf16_bits(t_ref[:, (s + WORDS_PER_EXPERT) * LANES:(s + WORDS_PER_EXPERT + 1) * LANES])
        word = lax.shift_right_logical(lo, jnp.uint32(16)) | (hi & jnp.uint32(0xFFFF0000))
        o_ref[pl.ds(s, rows, stride=WORDS_PER_EXPERT), :] = lax.bitcast_convert_type(word, I32)


def _table(tables, layer, te=2048):
    _, n, d = tables.shape
    return pl.pallas_call(
        _table_kernel,
        grid=(n // te,),
        in_specs=[pl.BlockSpec((None, te, d), lambda i: (layer, i, 0))],
        out_specs=pl.BlockSpec((te * WORDS_PER_EXPERT, LANES), lambda i: (i, 0)),
        out_shape=jax.ShapeDtypeStruct((n * WORDS_PER_EXPERT, LANES), I32),
        compiler_params=_params(("arbitrary",)),
        name="peer_table",
    )(tables)


def _expert_tile(tab_ref, e4):
    words = tab_ref[pl.ds(pl.multiple_of(e4, WORDS_PER_EXPERT), WORDS_PER_EXPERT), :]
    return pltpu.bitcast(words, BF16).astype(F32)


def _to_tile_rows(x):
    return jnp.concatenate([x[q:q + 1, :] for q in TILE_BLOCK], axis=0)


def _from_tile_rows(x):
    return jnp.concatenate([x[r:r + 1, :] for r in BLOCK_ROW], axis=0)


def _gelu_tanh(x):
    return 0.5 * x * (1.0 + jnp.tanh(np.sqrt(2.0 / np.pi) * (x + 0.044715 * (x * x * x))))


RING_SLOTS = 2
SLOT_TOKENS = 8


def _for_each_token(e_ref, idx_ref, sem, tb, token_body):
    ngroups = tb // SLOT_TOKENS

    def fetch(q, slot):
        return pltpu.make_async_copy(e_ref.at[q], idx_ref.at[slot], sem.at[slot])

    for slot in range(RING_SLOTS):
        fetch(slot, slot).start()

    def ring(j, _):
        for slot in range(RING_SLOTS):
            q = j * RING_SLOTS + slot
            fetch(q, slot).wait()
            for tt in range(SLOT_TOKENS):
                token_body(q * SLOT_TOKENS + tt, lambda k, slot=slot, tt=tt: idx_ref[slot, tt, k])
            fetch(jnp.minimum(q + RING_SLOTS, ngroups - 1), slot).start()
        return 0

    lax.fori_loop(0, ngroups // RING_SLOTS, ring, 0)
    for slot in range(RING_SLOTS):
        fetch(ngroups - 1, slot).wait()


def _peer_u_kernel(e_ref, h_ref, g_ref, tab_ref, sel_ref, o_ref, slab_ref, idx_ref, sem, *, tb):
    groups_per_token = PEER_SLOTS // SUBLANES
    sel = sel_ref[...]

    def token(t, row):
        hb = _to_tile_rows(h_ref[t]).astype(BF16)
        for j in range(groups_per_token):
            prods = []
            for qn in range(SUBLANES):
                src = pl.ds(pl.multiple_of(row(j * SUBLANES + qn), WORDS_PER_EXPERT), WORDS_PER_EXPERT)
                prods.append(pltpu.bitcast(tab_ref[src, :], BF16) * hb)
            prods = jnp.concatenate(prods, axis=0)
            dst = pl.multiple_of(t * PEER_SLOTS + j * SUBLANES, SUBLANES)
            slab_ref[pl.ds(dst, SUBLANES), :] = jnp.dot(sel, prods, preferred_element_type=F32)

    _for_each_token(e_ref, idx_ref, sem, tb, token)
    rio = lax.broadcasted_iota(I32, (LANES, LANES), 0)
    lio = lax.broadcasted_iota(I32, (LANES, LANES), 1)
    diag = (rio == lio)[None]
    rows = SUBLANES * PEER_SLOTS
    for gi in range(tb // SUBLANES):
        tok = slice(gi * SUBLANES, (gi + 1) * SUBLANES)
        rs = jnp.sum(slab_ref[gi * rows:(gi + 1) * rows, :], axis=-1, keepdims=True)
        rs = rs.reshape(SUBLANES, PEER_SLOTS, 1)
        a = jnp.sum(jnp.where(diag, rs, 0.0), axis=1)
        o_ref[tok, :] = g_ref[tok, :] * _gelu_tanh(a)


def _peer_u(e4, h3, gates, tab, tb=128):
    N = h3.shape[0]
    ring_tokens = RING_SLOTS * SLOT_TOKENS
    assert tb % ring_tokens == 0 and tb % SUBLANES == 0
    sel = np.arange(SUBLANES)[:, None] == np.arange(SUBLANES * SUBLANES)[None, :] // SUBLANES
    return pl.pallas_call(
        functools.partial(_peer_u_kernel, tb=tb),
        grid=(N // tb,),
        in_specs=[
            pl.BlockSpec((tb // SLOT_TOKENS, SLOT_TOKENS, PEER_SLOTS), lambda i: (i, 0, 0)),
            pl.BlockSpec((tb, SUBLANES, LANES), lambda i: (i, 0, 0)),
            pl.BlockSpec((tb, PEER_SLOTS), lambda i: (i, 0)),
            pl.BlockSpec(tab.shape, lambda i: (0, 0), pipeline_mode=pl.Buffered(1)),
            pl.BlockSpec(sel.shape, lambda i: (0, 0)),
        ],
        out_specs=pl.BlockSpec((tb, PEER_SLOTS), lambda i: (i, 0)),
        out_shape=jax.ShapeDtypeStruct((N, PEER_SLOTS), F32),
        scratch_shapes=[pltpu.VMEM((tb * PEER_SLOTS, LANES), F32),
                        pltpu.SMEM((RING_SLOTS, SLOT_TOKENS, PEER_SLOTS), I32),
                        pltpu.SemaphoreType.DMA((RING_SLOTS,))],
        compiler_params=_params(("arbitrary",), VMEM_LIMIT_BYTES),
        name="peer_u",
    )(e4.reshape(N // SLOT_TOKENS, SLOT_TOKENS, PEER_SLOTS), h3, gates, tab, jnp.asarray(sel, BF16))


def _peer_v_kernel(e_ref, c_ref, x1_ref, g2_ref, tab_ref, o_ref, splat_ref, idx_ref, sem, *, tb):
    nacc = 3
    rio = lax.broadcasted_iota(I32, (LANES, LANES), 0)
    lio = lax.broadcasted_iota(I32, (LANES, LANES), 1)
    diag = (rio == lio)[None]
    for t8 in range(tb // SUBLANES):
        cc = c_ref[t8 * SUBLANES:(t8 + 1) * SUBLANES, :]
        s = jnp.sum(jnp.where(diag, cc[:, None, :], 0.0), axis=-1, keepdims=True)
        splat_ref[t8 * SUBLANES:(t8 + 1) * SUBLANES] = jnp.broadcast_to(s, (SUBLANES, LANES, LANES))

    def token(t, row):
        accs = [jnp.zeros((SUBLANES, LANES), F32) for _ in range(nacc)]
        for k in range(PEER_SLOTS):
            c = jnp.broadcast_to(splat_ref[t, k:k + 1, :], (SUBLANES, LANES))
            accs[k % nacc] = accs[k % nacc] + c * _expert_tile(tab_ref, row(k))
        y = _tree(jnp.add, accs)
        o_ref[t] = x1_ref[t] + g2_ref[0] * _from_tile_rows(y)

    _for_each_token(e_ref, idx_ref, sem, tb, token)


def _peer_v(e4, coef, x1_3, g2_3, tab, T, tb=128):
    N = x1_3.shape[0]
    assert tb % (RING_SLOTS * SLOT_TOKENS) == 0 and tb % SUBLANES == 0
    return pl.pallas_call(
        functools.partial(_peer_v_kernel, tb=tb),
        grid=(N // tb,),
        in_specs=[
            pl.BlockSpec((tb // SLOT_TOKENS, SLOT_TOKENS, PEER_SLOTS), lambda i: (i, 0, 0)),
            pl.BlockSpec((tb, PEER_SLOTS), lambda i: (i, 0)),
            pl.BlockSpec((tb, SUBLANES, LANES), lambda i: (i, 0, 0)),
            pl.BlockSpec((1, SUBLANES, LANES), lambda i: ((i * tb) // T, 0, 0)),
            pl.BlockSpec(tab.shape, lambda i: (0, 0), pipeline_mode=pl.Buffered(1)),
        ],
        out_specs=pl.BlockSpec((tb, SUBLANES, LANES), lambda i: (i, 0, 0)),
        out_shape=jax.ShapeDtypeStruct((N, SUBLANES, LANES), F32),
        scratch_shapes=[pltpu.VMEM((tb, LANES, LANES), F32),
                        pltpu.SMEM((RING_SLOTS, SLOT_TOKENS, PEER_SLOTS), I32),
                        pltpu.SemaphoreType.DMA((RING_SLOTS,))],
        compiler_params=_params(("arbitrary",), VMEM_LIMIT_BYTES),
        name="peer_v",
    )(e4.reshape(N // SLOT_TOKENS, SLOT_TOKENS, PEER_SLOTS), coef, x1_3, g2_3, tab)


def kernel(x, c, w_ada, b_ada, norm1_g, w_in, gla_gate_w2, gla_gate_b, gla_norm_g, swa_qnorm_g,
           swa_knorm_g, swa_sinks, rel_bias, w_up_a, w_up_b, w_out, norm2_g, peer_wq, peer_subkeys,
           peer_u, peer_v):
    B, T, D = x.shape
    N = B * T
    L = w_ada.shape[0]
    mod = _adaln(c, w_ada, b_ada)
    weff_t = _fold_peer_keys(peer_wq, peer_subkeys)
    bias = _swa_bias(rel_bias)
    xf = x.reshape(N, D)
    for l in range(L):
        sh1, sc1, g1, sh2, sc2, g2 = [mod[l, :, i * D:(i + 1) * D] for i in range(6)]
        proj = _in_proj(xf, norm1_g[l], sc1, sh1, _pack_w_in(w_in, l), T)
        proj3 = proj.reshape(B, T, PROJ_W)
        w2p = jnp.zeros((LANES, GLA_HEADS * GLA_DK), BF16).at[:GLA_RANK].set(gla_gate_w2[l].astype(BF16))
        gla_o = _gla(proj3, w2p, gla_gate_b[l].reshape(1, -1), gla_norm_g[l].reshape(1, -1))
        swa_o = _swa(proj3, bias, swa_qnorm_g[l].reshape(1, -1), swa_knorm_g[l].reshape(1, -1),
                     swa_sinks[l])
        x1, h2, s_t = _merge(xf, proj, gla_o.reshape(N, -1), swa_o.reshape(N, -1),
                             w_up_a[l].astype(BF16), w_up_b[l].astype(BF16), w_out[l].astype(BF16),
                             g1, norm2_g[l], sc2, sh2, weff_t[l], T)
        e4, gates = _topk(s_t)
        coef = _peer_u(e4, h2.reshape(N, SUBLANES, LANES), gates, _table(peer_u, l))
        g2_3 = g2.reshape(B, SUBLANES, LANES)
        xf = _peer_v(e4, coef, x1.reshape(N, SUBLANES, LANES), g2_3, _table(peer_v, l), T).reshape(N, D)
    return xf.reshape(B, T, D)
```

```python
import functools

import numpy as np
import jax
import jax.numpy as jnp
from jax import lax
from jax.experimental import pallas as pl
from jax.experimental.pallas import tpu as pltpu

F32 = jnp.float32
BF16 = jnp.bfloat16
I32 = jnp.int32
HIGHEST = lax.Precision.HIGHEST
EPS = 1e-6

GLA_HEADS = 4
GLA_DK = 128
GLA_DV = 256
GLA_RANK = 16
GLA_TAU = 16.0
GLA_CHUNK = 64
SWA_HEADS = 16
SWA_KV_HEADS = 2
SWA_HD = 64
SWA_BLOCK = 128
N_BUCKETS = 32
MAX_DISTANCE = 128
PEER_HEADS = 8
PEER_NKEYS = 128
PEER_TOPK = 16
PEER_SLOTS = PEER_HEADS * PEER_TOPK

SUBLANES = 8
LANES = 128
VMEM_LIMIT_BYTES = 56 * 1024 * 1024

NT_DIMS = (((1,), (1,)), ((), ()))
TN_DIMS = (((0,), (0,)), ((), ()))

COL_Q, COL_K, COL_V, COL_GR, COL_SQ, COL_GA, COL_GB = 0, 512, 1024, 2048, 3072, 4096, 5120
COL_SK, COL_SV, COL_GLR = 6144, 6272, 6400
PROJ_W = 6528


def _params(sem, vmem=None):
    return pltpu.CompilerParams(dimension_semantics=sem, vmem_limit_bytes=vmem)


def _adaln_kernel(c_ref, w_ref, b_ref, o_ref):
    c = c_ref[...]
    a = c * jax.nn.sigmoid(c)
    o_ref[0] = jnp.dot(a, w_ref[0], preferred_element_type=F32, precision=HIGHEST) + b_ref[0]


def _adaln(c, w_ada, b_ada):
    L, D, W = w_ada.shape
    B = c.shape[0]
    rows = -(-B // SUBLANES) * SUBLANES
    cp = jnp.zeros((rows, D), F32).at[:B].set(c)
    tn = W // 4
    out = pl.pallas_call(
        _adaln_kernel,
        grid=(L, W // tn),
        in_specs=[
            pl.BlockSpec((rows, D), lambda l, j: (0, 0)),
            pl.BlockSpec((1, D, tn), lambda l, j: (l, 0, j)),
            pl.BlockSpec((1, 1, tn), lambda l, j: (l, 0, j)),
        ],
        out_specs=pl.BlockSpec((1, rows, tn), lambda l, j: (l, 0, j)),
        out_shape=jax.ShapeDtypeStruct((L, rows, W), F32),
        compiler_params=_params(("arbitrary", "arbitrary")),
        name="adaln",
    )(cp, w_ada, b_ada.reshape(L, 1, W))
    return out[:, :B]


def _fold_kernel(sk_ref, wq_ref, o_ref, rows_ref):
    half = sk_ref.shape[-1]
    groups = 2 * PEER_HEADS
    for h in range(PEER_HEADS):
        for p in range(2):
            g = 2 * h + p
            res = lax.dot_general(sk_ref[0, p], wq_ref[0, :, g * half:(g + 1) * half], NT_DIMS,
                                  precision=HIGHEST, preferred_element_type=F32)
            for cb in range(rows_ref.shape[0]):
                rows_ref[cb, pl.ds(p * PEER_HEADS + h, PEER_NKEYS, stride=groups), :] = (
                    res[:, cb * LANES:(cb + 1) * LANES])
    for cb in range(rows_ref.shape[0]):
        o_ref[0, :, cb * LANES:(cb + 1) * LANES] = rows_ref[cb].astype(BF16)


def _fold_peer_keys(peer_wq, peer_subkeys):
    L, D, QW = peer_wq.shape
    half = peer_subkeys.shape[-1]
    rows = (QW // half) * PEER_NKEYS
    return pl.pallas_call(
        _fold_kernel,
        grid=(L,),
        in_specs=[
            pl.BlockSpec((1, 2, PEER_NKEYS, half), lambda l: (l, 0, 0, 0)),
            pl.BlockSpec((1, D, QW), lambda l: (l, 0, 0)),
        ],
        out_specs=pl.BlockSpec((1, rows, D), lambda l: (l, 0, 0)),
        out_shape=jax.ShapeDtypeStruct((L, rows, D), BF16),
        scratch_shapes=[pltpu.VMEM((D // LANES, rows, LANES), F32)],
        compiler_params=_params(("arbitrary",), VMEM_LIMIT_BYTES),
        name="peer_fold",
    )(peer_subkeys, peer_wq)


def _inproj_kernel(x_ref, g_ref, sc_ref, sh_ref, w_ref, o_ref):
    x = x_ref[...]
    ms = jnp.mean(x * x, axis=-1, keepdims=True)
    y = x * lax.rsqrt(ms + EPS) * g_ref[...]
    h = y * (1.0 + sc_ref[0]) + sh_ref[0]
    o_ref[...] = jnp.dot(h.astype(BF16), w_ref[...], preferred_element_type=F32).astype(BF16)


def _in_proj(xf, g, sc, sh, wp, T, tm=512):
    N, D = xf.shape
    B = sc.shape[0]
    ncol = 3
    tn = PROJ_W // ncol
    return pl.pallas_call(
        _inproj_kernel,
        grid=(ncol, N // tm),
        in_specs=[
            pl.BlockSpec((tm, D), lambda j, i: (i, 0)),
            pl.BlockSpec((1, D), lambda j, i: (0, 0)),
            pl.BlockSpec((1, 1, D), lambda j, i: ((i * tm) // T, 0, 0)),
            pl.BlockSpec((1, 1, D), lambda j, i: ((i * tm) // T, 0, 0)),
            pl.BlockSpec((D, tn), lambda j, i: (0, j)),
        ],
        out_specs=pl.BlockSpec((tm, tn), lambda j, i: (i, j)),
        out_shape=jax.ShapeDtypeStruct((N, PROJ_W), BF16),
        compiler_params=_params(("arbitrary", "arbitrary")),
        name="in_proj",
    )(xf, g.reshape(1, D), sc.reshape(B, 1, D), sh.reshape(B, 1, D), wp)


_SRC = dict(zip(("q", "k", "v", "glr", "gr", "sq", "sk", "sv", "ga", "gb"),
                np.cumsum([0, 512, 512, 1024, GLA_RANK, 1024, 1024, 128, 128, 1024])))
W_IN_SLABS = ((COL_Q, _SRC["q"], 512), (COL_K, _SRC["k"], 512), (COL_V, _SRC["v"], 1024),
              (COL_GR, _SRC["gr"], 1024), (COL_SQ, _SRC["sq"], 1024), (COL_GA, _SRC["ga"], 1024),
              (COL_GB, _SRC["gb"], 1024), (COL_SK, _SRC["sk"], 128), (COL_SV, _SRC["sv"], 128),
              (COL_GLR, _SRC["glr"], GLA_RANK))


def _pack_w_in_kernel(w_ref, o_ref):
    o_ref[:, COL_GLR:] = jnp.zeros((o_ref.shape[0], PROJ_W - COL_GLR), BF16)
    for dst, src, width in W_IN_SLABS:
        o_ref[:, dst:dst + width] = w_ref[:, int(src):int(src) + width].astype(BF16)


def _pack_w_in(w_in, layer, tr=128):
    _, rows, cols = w_in.shape
    return pl.pallas_call(
        _pack_w_in_kernel,
        grid=(rows // tr,),
        in_specs=[pl.BlockSpec((None, tr, cols), lambda i: (layer, i, 0))],
        out_specs=pl.BlockSpec((tr, PROJ_W), lambda i: (i, 0)),
        out_shape=jax.ShapeDtypeStruct((rows, PROJ_W), BF16),
        compiler_params=_params(("arbitrary",)),
        name="pack_w_in",
    )(w_in)


def _gla_kernel(q_ref, k_ref, v_ref, r_ref, glr_ref, w2_ref, b2_ref, ng_ref, o_ref, st_ref, *, nchunk):
    @pl.when(pl.program_id(1) == 0)
    def _():
        st_ref[...] = jnp.zeros_like(st_ref)

    C = GLA_CHUNK
    dk, dv = GLA_DK, GLA_DV
    row = lax.broadcasted_iota(I32, (C, C), 0)
    col = lax.broadcasted_iota(I32, (C, C), 1)
    tri = col <= row
    tri_f = tri.astype(F32)
    w2 = w2_ref[...]
    b2 = b2_ref[...]
    ng = ng_ref[...]
    for ci in range(nchunk):
        sl = pl.ds(ci * C, C)
        z = jnp.dot(glr_ref[sl, :], w2, preferred_element_type=F32) + b2
        log_a = (jnp.minimum(z, 0.0) - jnp.log(1.0 + jnp.exp(-jnp.abs(z)))) * (1.0 / GLA_TAU)
        b = jnp.dot(tri_f, log_a, preferred_element_type=F32, precision=HIGHEST)
        b_last = b[C - 1:C, :]
        q = q_ref[sl, :].astype(F32) * (dk ** -0.5)
        k = k_ref[sl, :].astype(F32)
        q_dec = (q * jnp.exp(b)).astype(BF16)
        k_inv = (k * jnp.exp(-b)).astype(BF16)
        k_tail = (k * jnp.exp(b_last - b)).astype(BF16)
        decay = jnp.exp(b_last)
        for h in range(GLA_HEADS):
            kc = slice(h * dk, (h + 1) * dk)
            vc = slice(h * dv, (h + 1) * dv)
            v = v_ref[sl, vc]
            attn = lax.dot_general(q_dec[:, kc], k_inv[:, kc], NT_DIMS, preferred_element_type=F32)
            attn = jnp.where(tri, attn, 0.0).astype(BF16)
            st = st_ref[h]
            o = (jnp.dot(attn, v, preferred_element_type=F32)
                 + lax.dot_general(q_dec[:, kc], st.astype(BF16), NT_DIMS, preferred_element_type=F32))
            st_ref[h] = st * decay[:, kc] + lax.dot_general(v, k_tail[:, kc], TN_DIMS,
                                                            preferred_element_type=F32)
            on = o * lax.rsqrt(jnp.mean(o * o, axis=-1, keepdims=True) + EPS) * ng
            r = r_ref[sl, vc].astype(F32)
            o_ref[sl, vc] = (on * (r * jax.nn.sigmoid(r))).astype(BF16)


def _gla(proj, w2p, b2, ng, tc=256):
    B, T, _ = proj.shape
    H = GLA_HEADS
    kw, vw = H * GLA_DK, H * GLA_DV
    return pl.pallas_call(
        functools.partial(_gla_kernel, nchunk=tc // GLA_CHUNK),
        grid=(B, T // tc),
        in_specs=[
            pl.BlockSpec((None, tc, kw), lambda b, c: (b, c, COL_Q // kw)),
            pl.BlockSpec((None, tc, kw), lambda b, c: (b, c, COL_K // kw)),
            pl.BlockSpec((None, tc, vw), lambda b, c: (b, c, COL_V // vw)),
            pl.BlockSpec((None, tc, vw), lambda b, c: (b, c, COL_GR // vw)),
            pl.BlockSpec((None, tc, LANES), lambda b, c: (b, c, COL_GLR // LANES)),
            pl.BlockSpec((LANES, kw), lambda b, c: (0, 0)),
            pl.BlockSpec((1, kw), lambda b, c: (0, 0)),
            pl.BlockSpec((1, GLA_DV), lambda b, c: (0, 0)),
        ],
        out_specs=pl.BlockSpec((None, tc, vw), lambda b, c: (b, c, 0)),
        out_shape=jax.ShapeDtypeStruct((B, T, vw), BF16),
        scratch_shapes=[pltpu.VMEM((H, GLA_DV, GLA_DK), F32)],
        compiler_params=_params(("arbitrary", "arbitrary")),
        name="gla",
    )(proj, proj, proj, proj, proj, w2p, b2, ng)


def _t5_bucket(dist):
    max_exact = N_BUCKETS // 2
    d = np.maximum(dist, 1).astype(np.float32)
    large = max_exact + (np.log(d / max_exact) / np.log(MAX_DISTANCE / max_exact)
                         * (N_BUCKETS - max_exact)).astype(np.int32)
    large = np.minimum(large, N_BUCKETS - 1)
    return np.where(dist < max_exact, dist, large).astype(np.int32)


NEG_BIG = -1e30


def _swa_bias(rel_bias):
    blk = SWA_BLOCK
    qi = np.arange(blk)[:, None]
    sj = np.arange(2 * blk)[None, :]
    dist = blk + qi - sj
    band = (dist >= 0) & (dist < blk)
    bucket = _t5_bucket(np.clip(dist, 0, None))
    onehot = jnp.asarray(bucket.reshape(-1, 1) == np.arange(N_BUCKETS)[None, :], F32)
    bias = jnp.dot(onehot, rel_bias.astype(F32), precision=HIGHEST).T.reshape(-1, blk, 2 * blk)
    masks = np.stack([band, band & (sj >= blk)])[:, None]
    return jnp.where(jnp.asarray(masks), bias[None], NEG_BIG)


def _segment_sums(x, seg):
    hi = x.astype(BF16)
    lo = (x - hi.astype(F32)).astype(BF16)
    return (jnp.dot(hi, seg, preferred_element_type=F32) + jnp.dot(lo, seg, preferred_element_type=F32))


def _head_rms_scale(x, seg, seg_t):
    inv = lax.rsqrt(_segment_sums(x * x, seg) * (1.0 / SWA_HD) + EPS)
    return _segment_sums(inv, seg_t)


def _swa_kernel(sink_ref, q_ref, kp_ref, kc_ref, vp_ref, vc_ref, bias_ref, qg_ref, kg_ref, seg_ref,
                segt_ref, o_ref):
    blk = SWA_BLOCK
    hd = SWA_HD
    group = SWA_HEADS // SWA_KV_HEADS
    kvw = SWA_KV_HEADS * hd
    seg = seg_ref[...]
    seg_t = segt_ref[...]
    q = q_ref[...].astype(F32)
    qn = (q * _head_rms_scale(q, seg, seg_t) * qg_ref[...] * (hd ** -0.5)).astype(BF16)
    k2 = jnp.concatenate([kp_ref[...], kc_ref[...]], axis=0).astype(F32)
    kn = (k2 * _head_rms_scale(k2, seg[:kvw], seg_t[:, :kvw]) * kg_ref[...]).astype(BF16)
    v2 = jnp.concatenate([vp_ref[...], vc_ref[...]], axis=0)
    lane = lax.broadcasted_iota(I32, (2 * blk, kvw), 1)
    outs = []
    for kh in range(SWA_KV_HEADS):
        kk = kn[:, kh * hd:(kh + 1) * hd]
        vsh = v2 if kh == 0 else jnp.concatenate([v2[:, kh * hd:], v2[:, :kh * hd]], axis=1)
        vv = jnp.where(lane < hd, vsh, jnp.ones_like(vsh))
        qs = jnp.concatenate([qn[:, (kh * group + g) * hd:(kh * group + g + 1) * hd]
                              for g in range(group)], axis=0)
        logits = lax.dot_general(qs, kk, NT_DIMS, preferred_element_type=F32)
        logits = logits + bias_ref[kh * group:(kh + 1) * group].reshape(group * blk, 2 * blk)
        for g in range(group):
            lg = logits[g * blk:(g + 1) * blk]
            sink = sink_ref[kh * group + g]
            m = jnp.maximum(jnp.max(lg, axis=-1, keepdims=True), sink)
            p = jnp.exp(lg - m).astype(BF16)
            pv = jnp.dot(p, vv, preferred_element_type=F32)
            denom = pv[:, hd:hd + 1] + jnp.exp(sink - m)
            outs.append(pv[:, :hd] / denom)
    o_ref[...] = jnp.concatenate(outs, axis=-1).astype(BF16)


def _swa(proj, bias, qg, kg, sinks):
    B, T, _ = proj.shape
    blk = SWA_BLOCK
    qw = SWA_HEADS * SWA_HD
    kvw = SWA_KV_HEADS * SWA_HD
    prev = lambda b, i: jnp.maximum(i - 1, 0)
    seg = (np.arange(qw)[:, None] // SWA_HD == np.arange(LANES)[None, :])
    seg = jnp.asarray(seg, BF16)
    return pl.pallas_call(
        _swa_kernel,
        grid=(B, T // blk),
        in_specs=[
            pl.BlockSpec(memory_space=pltpu.SMEM),
            pl.BlockSpec((None, blk, qw), lambda b, i: (b, i, COL_SQ // qw)),
            pl.BlockSpec((None, blk, kvw), lambda b, i: (b, prev(b, i), COL_SK // kvw)),
            pl.BlockSpec((None, blk, kvw), lambda b, i: (b, i, COL_SK // kvw)),
            pl.BlockSpec((None, blk, kvw), lambda b, i: (b, prev(b, i), COL_SV // kvw)),
            pl.BlockSpec((None, blk, kvw), lambda b, i: (b, i, COL_SV // kvw)),
            pl.BlockSpec((None, SWA_HEADS, blk, 2 * blk), lambda b, i: (jnp.where(i == 0, 1, 0), 0, 0, 0)),
            pl.BlockSpec((1, qw), lambda b, i: (0, 0)),
            pl.BlockSpec((1, kvw), lambda b, i: (0, 0)),
            pl.BlockSpec((qw, LANES), lambda b, i: (0, 0)),
            pl.BlockSpec((LANES, qw), lambda b, i: (0, 0)),
        ],
        out_specs=pl.BlockSpec((None, blk, qw), lambda b, i: (b, i, 0)),
        out_shape=jax.ShapeDtypeStruct((B, T, qw), BF16),
        compiler_params=_params(("arbitrary", "arbitrary")),
        name="swa",
    )(sinks, proj, proj, proj, proj, proj, bias, jnp.tile(qg, (1, SWA_HEADS)),
      jnp.tile(kg, (1, SWA_KV_HEADS)), seg, seg.T)


def _merge_kernel(x_ref, ga_ref, gb_ref, go_ref, so_ref, wa_ref, wb_ref, wo_ref, g1_ref,
                  n2_ref, sc2_ref, sh2_ref, we_ref, x1_ref, h2_ref, st_ref):
    ya = jnp.dot(go_ref[...], wa_ref[...], preferred_element_type=F32)
    yb = jnp.dot(so_ref[...], wb_ref[...], preferred_element_type=F32)
    m = jax.nn.sigmoid(ga_ref[...].astype(F32)) * ya + jax.nn.sigmoid(gb_ref[...].astype(F32)) * yb
    mixed = jnp.dot(m.astype(BF16), wo_ref[...], preferred_element_type=F32)
    x1 = x_ref[...] + g1_ref[0] * mixed
    x1_ref[...] = x1
    y = x1 * lax.rsqrt(jnp.mean(x1 * x1, axis=-1, keepdims=True) + EPS) * n2_ref[...]
    h2 = y * (1.0 + sc2_ref[0]) + sh2_ref[0]
    h2_ref[...] = h2
    st_ref[...] = lax.dot_general(we_ref[...], h2.astype(BF16), NT_DIMS, preferred_element_type=F32)


def _merge(xf, proj2, gla_o, swa_o, wa, wb, wo, g1, n2, sc2, sh2, weff_t, T, tm=256):
    N, D = xf.shape
    B = g1.shape[0]
    SW = weff_t.shape[0]
    bat = lambda i: ((i * tm) // T, 0, 0)
    full = lambda i: (0, 0)
    return pl.pallas_call(
        _merge_kernel,
        grid=(N // tm,),
        in_specs=[
            pl.BlockSpec((tm, D), lambda i: (i, 0)),
            pl.BlockSpec((tm, D), lambda i: (i, COL_GA // D)),
            pl.BlockSpec((tm, D), lambda i: (i, COL_GB // D)),
            pl.BlockSpec((tm, D), lambda i: (i, 0)),
            pl.BlockSpec((tm, D), lambda i: (i, 0)),
            pl.BlockSpec((D, D), full),
            pl.BlockSpec((D, D), full),
            pl.BlockSpec((D, D), full),
            pl.BlockSpec((1, 1, D), bat),
            pl.BlockSpec((1, D), full),
            pl.BlockSpec((1, 1, D), bat),
            pl.BlockSpec((1, 1, D), bat),
            pl.BlockSpec((SW, D), full),
        ],
        out_specs=[
            pl.BlockSpec((tm, D), lambda i: (i, 0)),
            pl.BlockSpec((tm, D), lambda i: (i, 0)),
            pl.BlockSpec((SW, tm), lambda i: (0, i)),
        ],
        out_shape=[
            jax.ShapeDtypeStruct((N, D), F32),
            jax.ShapeDtypeStruct((N, D), F32),
            jax.ShapeDtypeStruct((SW, N), F32),
        ],
        compiler_params=_params(("arbitrary",), VMEM_LIMIT_BYTES),
        name="merge",
    )(xf, proj2, proj2, gla_o, swa_o, wa, wb, wo, g1.reshape(B, 1, D), n2.reshape(1, D),
      sc2.reshape(B, 1, D), sh2.reshape(B, 1, D), weff_t)


def _tree(op, xs):
    xs = list(xs)
    while len(xs) > 1:
        xs = [op(xs[i], xs[i + 1]) for i in range(0, len(xs) - 1, 2)] + ([xs[-1]] if len(xs) % 2 else [])
    return xs[0]


EXTRACT_CHAINS = 4


def _extract_best(ref, n):
    lanes = min(EXTRACT_CHAINS, n)
    best = [ref[i] for i in range(lanes)]
    for i in range(lanes, n):
        best[i % lanes] = jnp.maximum(best[i % lanes], ref[i])
    m = _tree(jnp.maximum, best)
    first = [jnp.where(ref[i] == m, i, n) for i in range(lanes)]
    for i in range(lanes, n):
        first[i % lanes] = jnp.minimum(first[i % lanes], jnp.where(ref[i] == m, i, n))
    pos = _tree(jnp.minimum, first)
    for i in range(n):
        ref[i] = jnp.where(pos == i, -jnp.inf, ref[i])
    return m, pos


def _ordered(a, b):
    (va, ia), (vb, ib) = a, b
    keep = jnp.logical_or(va > vb, jnp.logical_and(va == vb, ia < ib))
    first = (jnp.where(keep, va, vb), jnp.where(keep, ia, ib))
    second = (jnp.where(keep, vb, va), jnp.where(keep, ib, ia))
    return first, second


def _bitonic_merge(xs):
    xs = list(xs)
    j = len(xs) // 2
    while j >= 1:
        for i in range(len(xs)):
            if i & j == 0:
                xs[i], xs[i | j] = _ordered(xs[i], xs[i | j])
        j //= 2
    return xs


def _bitonic_sort(xs):
    xs = list(xs)
    k = 2
    while k <= len(xs):
        j = k // 2
        while j >= 1:
            for i in range(len(xs)):
                if i & j == 0:
                    first, second = _ordered(xs[i], xs[i | j])
                    xs[i], xs[i | j] = (first, second) if i & k == 0 else (second, first)
            j //= 2
        k *= 2
    return xs


def _merge_best(xs, ys):
    n = len(xs)
    return _bitonic_merge([_ordered(xs[i], ys[n - 1 - i])[0] for i in range(n)])


TOPK_PAIRS = tuple((a, b) for a in range(PEER_TOPK) for b in range(PEER_TOPK)
                   if (a + 1) * (b + 1) <= PEER_TOPK)


def _topk_kernel(s_ref, e_ref, g_ref, sv_ref, si_ref, v_ref, i_ref, cand_ref, cidx_ref, best_ref, row_ref):
    K = PEER_TOPK
    nk = PEER_NKEYS
    H = PEER_HEADS
    G = 2 * H
    tm = s_ref.shape[1]
    nblocks = nk // K
    for hb in range(2):
        rows = slice(hb * H, (hb + 1) * H)
        for blk in range(nblocks):
            keys = [(s_ref[(blk * K + i) * G + hb * H:(blk * K + i) * G + (hb + 1) * H, :],
                     jnp.full((H, tm), blk * K + i, I32)) for i in range(K)]
            for i, (v, ix) in enumerate(_bitonic_sort(keys)):
                sv_ref[blk, i, rows, :] = v
                si_ref[blk, i, rows, :] = ix
        step = 1
        while step < nblocks:
            for blk in range(0, nblocks, 2 * step):
                xs = [(sv_ref[blk, i, rows, :], si_ref[blk, i, rows, :]) for i in range(K)]
                ys = [(sv_ref[blk + step, i, rows, :], si_ref[blk + step, i, rows, :]) for i in range(K)]
                for i, (v, ix) in enumerate(_merge_best(xs, ys)):
                    sv_ref[blk, i, rows, :] = v
                    si_ref[blk, i, rows, :] = ix
            step *= 2
    v_ref[...] = sv_ref[0]
    i_ref[...] = si_ref[0]

    for ci, (a, b) in enumerate(TOPK_PAIRS):
        cand_ref[ci] = v_ref[a, 0:H, :] + v_ref[b, H:2 * H, :]
        cidx_ref[ci] = (i_ref[a, 0:H, :] * nk + i_ref[b, H:2 * H, :]) * WORDS_PER_EXPERT
    ncand = len(TOPK_PAIRS)

    def stage2(k, _):
        m, pos = _extract_best(cand_ref, ncand)
        row = _tree(jnp.maximum, [jnp.where(pos == ci, cidx_ref[ci], -1) for ci in range(ncand)])
        best_ref[k] = m
        row_ref[k] = row
        return 0

    lax.fori_loop(0, K, stage2, 0)
    best = best_ref[...]
    ex = jnp.exp(best - best[0:1])
    gates = ex / jnp.sum(ex, axis=0, keepdims=True)
    g_ref[...] = gates.reshape(K * H, tm).T
    e_ref[...] = row_ref[...].reshape(K * H, tm).astype(F32).T.astype(I32)


def _topk(s_t, tm=128):
    SW, N = s_t.shape
    K = PEER_TOPK
    G = 2 * PEER_HEADS
    return pl.pallas_call(
        _topk_kernel,
        grid=(N // tm,),
        in_specs=[pl.BlockSpec((SW, tm), lambda i: (0, i))],
        out_specs=[
            pl.BlockSpec((tm, PEER_SLOTS), lambda i: (i, 0)),
            pl.BlockSpec((tm, PEER_SLOTS), lambda i: (i, 0)),
        ],
        out_shape=[
            jax.ShapeDtypeStruct((N, PEER_SLOTS), I32),
            jax.ShapeDtypeStruct((N, PEER_SLOTS), F32),
        ],
        scratch_shapes=[pltpu.VMEM((PEER_NKEYS // K, K, G, tm), F32),
                        pltpu.VMEM((PEER_NKEYS // K, K, G, tm), I32),
                        pltpu.VMEM((K, G, tm), F32),
                        pltpu.VMEM((K, G, tm), I32),
                        pltpu.VMEM((len(TOPK_PAIRS), PEER_HEADS, tm), F32),
                        pltpu.VMEM((len(TOPK_PAIRS), PEER_HEADS, tm), I32),
                        pltpu.VMEM((K, PEER_HEADS, tm), F32),
                        pltpu.VMEM((K, PEER_HEADS, tm), I32)],
        compiler_params=_params(("arbitrary",)),
        name="peer_topk",
    )(s_t)


WORDS_PER_EXPERT = SUBLANES // 2


TILE_BLOCK = tuple((r % 2) * WORDS_PER_EXPERT + r // 2 for r in range(SUBLANES))
BLOCK_ROW = tuple(TILE_BLOCK.index(q) for q in range(SUBLANES))


def _bf16_bits(x):
    return lax.bitcast_convert_type(x.astype(BF16).astype(F32), jnp.uint32)


def _table_kernel(t_ref, o_ref):
    rows = t_ref.shape[0]
    for s in range(WORDS_PER_EXPERT):
        lo = _bf16_bits(t_ref[:, s * LANES:(s + 1) * LANES])
        hi = _bf16_bits(t_ref[:, (s + WORDS_PER_EXPERT) * LANES:(s + WORDS_PER_EXPERT + 1) * LANES])
        word = lax.shift_right_logical(lo, jnp.uint32(16)) | (hi & jnp.uint32(0xFFFF0000))
        o_ref[pl.ds(s, rows, stride=WORDS_PER_EXPERT), :] = lax.bitcast_convert_type(word, I32)


def _table(tables, layer, te=2048):
    _, n, d = tables.shape
    return pl.pallas_call(
        _table_kernel,
        grid=(n // te,),
        in_specs=[pl.BlockSpec((None, te, d), lambda i: (layer, i, 0))],
        out_specs=pl.BlockSpec((te * WORDS_PER_EXPERT, LANES), lambda i: (i, 0)),
        out_shape=jax.ShapeDtypeStruct((n * WORDS_PER_EXPERT, LANES), I32),
        compiler_params=_params(("arbitrary",)),
        name="peer_table",
    )(tables)


def _expert_tile(tab_ref, e4):
    words = tab_ref[pl.ds(pl.multiple_of(e4, WORDS_PER_EXPERT), WORDS_PER_EXPERT), :]
    return pltpu.bitcast(words, BF16).astype(F32)


def _to_tile_rows(x):
    return jnp.concatenate([x[q:q + 1, :] for q in TILE_BLOCK], axis=0)


def _from_tile_rows(x):
    return jnp.concatenate([x[r:r + 1, :] for r in BLOCK_ROW], axis=0)


def _gelu_tanh(x):
    return 0.5 * x * (1.0 + jnp.tanh(np.sqrt(2.0 / np.pi) * (x + 0.044715 * (x * x * x))))


RING_SLOTS = 2
SLOT_TOKENS = 8


def _for_each_token(e_ref, idx_ref, sem, tb, token_body):
    ngroups = tb // SLOT_TOKENS

    def fetch(q, slot):
        return pltpu.make_async_copy(e_ref.at[q], idx_ref.at[slot], sem.at[slot])

    for slot in range(RING_SLOTS):
        fetch(slot, slot).start()

    def ring(j, _):
        for slot in range(RING_SLOTS):
            q = j * RING_SLOTS + slot
            fetch(q, slot).wait()
            for tt in range(SLOT_TOKENS):
                token_body(q * SLOT_TOKENS + tt, lambda k, slot=slot, tt=tt: idx_ref[slot, tt, k])
            fetch(jnp.minimum(q + RING_SLOTS, ngroups - 1), slot).start()
        return 0

    lax.fori_loop(0, ngroups // RING_SLOTS, ring, 0)
    for slot in range(RING_SLOTS):
        fetch(ngroups - 1, slot).wait()


def _peer_u_kernel(e_ref, h_ref, g_ref, tab_ref, sel_ref, o_ref, slab_ref, idx_ref, sem, *, tb):
    groups_per_token = PEER_SLOTS // SUBLANES
    sel = sel_ref[...]

    def token(t, row):
        hb = _to_tile_rows(h_ref[t]).astype(BF16)
        for j in range(groups_per_token):
            prods = []
            for qn in range(SUBLANES):
                src = pl.ds(pl.multiple_of(row(j * SUBLANES + qn), WORDS_PER_EXPERT), WORDS_PER_EXPERT)
                prods.append(pltpu.bitcast(tab_ref[src, :], BF16) * hb)
            prods = jnp.concatenate(prods, axis=0)
            dst = pl.multiple_of(t * PEER_SLOTS + j * SUBLANES, SUBLANES)
            slab_ref[pl.ds(dst, SUBLANES), :] = jnp.dot(sel, prods, preferred_element_type=F32)

    _for_each_token(e_ref, idx_ref, sem, tb, token)
    rio = lax.broadcasted_iota(I32, (LANES, LANES), 0)
    lio = lax.broadcasted_iota(I32, (LANES, LANES), 1)
    diag = (rio == lio)[None]
    rows = SUBLANES * PEER_SLOTS
    for gi in range(tb // SUBLANES):
        tok = slice(gi * SUBLANES, (gi + 1) * SUBLANES)
        rs = jnp.sum(slab_ref[gi * rows:(gi + 1) * rows, :], axis=-1, keepdims=True)
        rs = rs.reshape(SUBLANES, PEER_SLOTS, 1)
        a = jnp.sum(jnp.where(diag, rs, 0.0), axis=1)
        o_ref[tok, :] = g_ref[tok, :] * _gelu_tanh(a)


def _peer_u(e4, h3, gates, tab, tb=128):
    N = h3.shape[0]
    ring_tokens = RING_SLOTS * SLOT_TOKENS
    assert tb % ring_tokens == 0 and tb % SUBLANES == 0
    sel = np.arange(SUBLANES)[:, None] == np.arange(SUBLANES * SUBLANES)[None, :] // SUBLANES
    return pl.pallas_call(
        functools.partial(_peer_u_kernel, tb=tb),
        grid=(N // tb,),
        in_specs=[
            pl.BlockSpec((tb // SLOT_TOKENS, SLOT_TOKENS, PEER_SLOTS), lambda i: (i, 0, 0)),
            pl.BlockSpec((tb, SUBLANES, LANES), lambda i: (i, 0, 0)),
            pl.BlockSpec((tb, PEER_SLOTS), lambda i: (i, 0)),
            pl.BlockSpec(tab.shape, lambda i: (0, 0), pipeline_mode=pl.Buffered(1)),
            pl.BlockSpec(sel.shape, lambda i: (0, 0)),
        ],
        out_specs=pl.BlockSpec((tb, PEER_SLOTS), lambda i: (i, 0)),
        out_shape=jax.ShapeDtypeStruct((N, PEER_SLOTS), F32),
        scratch_shapes=[pltpu.VMEM((tb * PEER_SLOTS, LANES), F32),
                        pltpu.SMEM((RING_SLOTS, SLOT_TOKENS, PEER_SLOTS), I32),
                        pltpu.SemaphoreType.DMA((RING_SLOTS,))],
        compiler_params=_params(("arbitrary",), VMEM_LIMIT_BYTES),
        name="peer_u",
    )(e4.reshape(N // SLOT_TOKENS, SLOT_TOKENS, PEER_SLOTS), h3, gates, tab, jnp.asarray(sel, BF16))


def _peer_v_kernel(e_ref, c_ref, x1_ref, g2_ref, tab_ref, o_ref, splat_ref, idx_ref, sem, *, tb):
    nacc = 3
    rio = lax.broadcasted_iota(I32, (LANES, LANES), 0)
    lio = lax.broadcasted_iota(I32, (LANES, LANES), 1)
    diag = (rio == lio)[None]
    for t8 in range(tb // SUBLANES):
        cc = c_ref[t8 * SUBLANES:(t8 + 1) * SUBLANES, :]
        s = jnp.sum(jnp.where(diag, cc[:, None, :], 0.0), axis=-1, keepdims=True)
        splat_ref[t8 * SUBLANES:(t8 + 1) * SUBLANES] = jnp.broadcast_to(s, (SUBLANES, LANES, LANES))

    def token(t, row):
        accs = [jnp.zeros((SUBLANES, LANES), F32) for _ in range(nacc)]
        for k in range(PEER_SLOTS):
            c = jnp.broadcast_to(splat_ref[t, k:k + 1, :], (SUBLANES, LANES))
            accs[k % nacc] = accs[k % nacc] + c * _expert_tile(tab_ref, row(k))
        y = _tree(jnp.add, accs)
        o_ref[t] = x1_ref[t] + g2_ref[0] * _from_tile_rows(y)

    _for_each_token(e_ref, idx_ref, sem, tb, token)


def _peer_v(e4, coef, x1_3, g2_3, tab, T, tb=128):
    N = x1_3.shape[0]
    assert tb % (RING_SLOTS * SLOT_TOKENS) == 0 and tb % SUBLANES == 0
    return pl.pallas_call(
        functools.partial(_peer_v_kernel, tb=tb),
        grid=(N // tb,),
        in_specs=[
            pl.BlockSpec((tb // SLOT_TOKENS, SLOT_TOKENS, PEER_SLOTS), lambda i: (i, 0, 0)),
            pl.BlockSpec((tb, PEER_SLOTS), lambda i: (i, 0)),
            pl.BlockSpec((tb, SUBLANES, LANES), lambda i: (i, 0, 0)),
            pl.BlockSpec((1, SUBLANES, LANES), lambda i: ((i * tb) // T, 0, 0)),
            pl.BlockSpec(tab.shape, lambda i: (0, 0), pipeline_mode=pl.Buffered(1)),
        ],
        out_specs=pl.BlockSpec((tb, SUBLANES, LANES), lambda i: (i, 0, 0)),
        out_shape=jax.ShapeDtypeStruct((N, SUBLANES, LANES), F32),
        scratch_shapes=[pltpu.VMEM((tb, LANES, LANES), F32),
                        pltpu.SMEM((RING_SLOTS, SLOT_TOKENS, PEER_SLOTS), I32),
                        pltpu.SemaphoreType.DMA((RING_SLOTS,))],
        compiler_params=_params(("arbitrary",), VMEM_LIMIT_BYTES),
        name="peer_v",
    )(e4.reshape(N // SLOT_TOKENS, SLOT_TOKENS, PEER_SLOTS), coef, x1_3, g2_3, tab)


def kernel(x, c, w_ada, b_ada, norm1_g, w_in, gla_gate_w2, gla_gate_b, gla_norm_g, swa_qnorm_g,
           swa_knorm_g, swa_sinks, rel_bias, w_up_a, w_up_b, w_out, norm2_g, peer_wq, peer_subkeys,
           peer_u, peer_v):
    B, T, D = x.shape
    N = B * T
    L = w_ada.shape[0]
    mod = _adaln(c, w_ada, b_ada)
    weff_t = _fold_peer_keys(peer_wq, peer_subkeys)
    bias = _swa_bias(rel_bias)
    xf = x.reshape(N, D)
    for l in range(L):
        sh1, sc1, g1, sh2, sc2, g2 = [mod[l, :, i * D:(i + 1) * D] for i in range(6)]
        proj = _in_proj(xf, norm1_g[l], sc1, sh1, _pack_w_in(w_in, l), T)
        proj3 = proj.reshape(B, T, PROJ_W)
        w2p = jnp.zeros((LANES, GLA_HEADS * GLA_DK), BF16).at[:GLA_RANK].set(gla_gate_w2[l].astype(BF16))
        gla_o = _gla(proj3, w2p, gla_gate_b[l].reshape(1, -1), gla_norm_g[l].reshape(1, -1))
        swa_o = _swa(proj3, bias, swa_qnorm_g[l].reshape(1, -1), swa_knorm_g[l].reshape(1, -1),
                     swa_sinks[l])
        x1, h2, s_t = _merge(xf, proj, gla_o.reshape(N, -1), swa_o.reshape(N, -1),
                             w_up_a[l].astype(BF16), w_up_b[l].astype(BF16), w_out[l].astype(BF16),
                             g1, norm2_g[l], sc2, sh2, weff_t[l], T)
        e4, gates = _topk(s_t)
        coef = _peer_u(e4, h2.reshape(N, SUBLANES, LANES), gates, _table(peer_u, l))
        g2_3 = g2.reshape(B, SUBLANES, LANES)
        xf = _peer_v(e4, coef, x1.reshape(N, SUBLANES, LANES), g2_3, _table(peer_v, l), T).reshape(N, D)
    return xf.reshape(B, T, D)
```

```python
import functools

import numpy as np
import jax
import jax.numpy as jnp
from jax import lax
from jax.experimental import pallas as pl
from jax.experimental.pallas import tpu as pltpu

F32 = jnp.float32
BF16 = jnp.bfloat16
I32 = jnp.int32
HIGHEST = lax.Precision.HIGHEST
EPS = 1e-6

GLA_HEADS = 4
GLA_DK = 128
GLA_DV = 256
GLA_RANK = 16
GLA_TAU = 16.0
GLA_CHUNK = 64
SWA_HEADS = 16
SWA_KV_HEADS = 2
SWA_HD = 64
SWA_BLOCK = 128
N_BUCKETS = 32
MAX_DISTANCE = 128
PEER_HEADS = 8
PEER_NKEYS = 128
PEER_TOPK = 16
PEER_SLOTS = PEER_HEADS * PEER_TOPK

SUBLANES = 8
LANES = 128
VMEM_LIMIT_BYTES = 56 * 1024 * 1024

NT_DIMS = (((1,), (1,)), ((), ()))
TN_DIMS = (((0,), (0,)), ((), ()))

COL_Q, COL_K, COL_V, COL_GR, COL_SQ, COL_GA, COL_GB = 0, 512, 1024, 2048, 3072, 4096, 5120
COL_SK, COL_SV, COL_GLR = 6144, 6272, 6400
PROJ_W = 6528


def _params(sem, vmem=None):
    return pltpu.CompilerParams(dimension_semantics=sem, vmem_limit_bytes=vmem)


def _adaln_kernel(c_ref, w_ref, b_ref, o_ref):
    c = c_ref[...]
    a = c * jax.nn.sigmoid(c)
    o_ref[0] = jnp.dot(a, w_ref[0], preferred_element_type=F32, precision=HIGHEST) + b_ref[0]


def _adaln(c, w_ada, b_ada):
    L, D, W = w_ada.shape
    B = c.shape[0]
    rows = -(-B // SUBLANES) * SUBLANES
    cp = jnp.zeros((rows, D), F32).at[:B].set(c)
    tn = W // 4
    out = pl.pallas_call(
        _adaln_kernel,
        grid=(L, W // tn),
        in_specs=[
            pl.BlockSpec((rows, D), lambda l, j: (0, 0)),
            pl.BlockSpec((1, D, tn), lambda l, j: (l, 0, j)),
            pl.BlockSpec((1, 1, tn), lambda l, j: (l, 0, j)),
        ],
        out_specs=pl.BlockSpec((1, rows, tn), lambda l, j: (l, 0, j)),
        out_shape=jax.ShapeDtypeStruct((L, rows, W), F32),
        compiler_params=_params(("arbitrary", "arbitrary")),
        name="adaln",
    )(cp, w_ada, b_ada.reshape(L, 1, W))
    return out[:, :B]


def _fold_kernel(sk_ref, wq_ref, o_ref, rows_ref):
    half = sk_ref.shape[-1]
    groups = 2 * PEER_HEADS
    for h in range(PEER_HEADS):
        for p in range(2):
            g = 2 * h + p
            res = lax.dot_general(sk_ref[0, p], wq_ref[0, :, g * half:(g + 1) * half], NT_DIMS,
                                  precision=HIGHEST, preferred_element_type=F32)
            for cb in range(rows_ref.shape[0]):
                rows_ref[cb, pl.ds(p * PEER_HEADS + h, PEER_NKEYS, stride=groups), :] = (
                    res[:, cb * LANES:(cb + 1) * LANES])
    for cb in range(rows_ref.shape[0]):
        o_ref[0, :, cb * LANES:(cb + 1) * LANES] = rows_ref[cb].astype(BF16)


def _fold_peer_keys(peer_wq, peer_subkeys):
    L, D, QW = peer_wq.shape
    half = peer_subkeys.shape[-1]
    rows = (QW // half) * PEER_NKEYS
    return pl.pallas_call(
        _fold_kernel,
        grid=(L,),
        in_specs=[
            pl.BlockSpec((1, 2, PEER_NKEYS, half), lambda l: (l, 0, 0, 0)),
            pl.BlockSpec((1, D, QW), lambda l: (l, 0, 0)),
        ],
        out_specs=pl.BlockSpec((1, rows, D), lambda l: (l, 0, 0)),
        out_shape=jax.ShapeDtypeStruct((L, rows, D), BF16),
        scratch_shapes=[pltpu.VMEM((D // LANES, rows, LANES), F32)],
        compiler_params=_params(("arbitrary",), VMEM_LIMIT_BYTES),
        name="peer_fold",
    )(peer_subkeys, peer_wq)


def _inproj_kernel(x_ref, g_ref, sc_ref, sh_ref, w_ref, o_ref):
    x = x_ref[...]
    ms = jnp.mean(x * x, axis=-1, keepdims=True)
    y = x * lax.rsqrt(ms + EPS) * g_ref[...]
    h = y * (1.0 + sc_ref[0]) + sh_ref[0]
    o_ref[...] = jnp.dot(h.astype(BF16), w_ref[...], preferred_element_type=F32).astype(BF16)


def _in_proj(xf, g, sc, sh, wp, T, tm=512):
    N, D = xf.shape
    B = sc.shape[0]
    ncol = 3
    tn = PROJ_W // ncol
    return pl.pallas_call(
        _inproj_kernel,
        grid=(ncol, N // tm),
        in_specs=[
            pl.BlockSpec((tm, D), lambda j, i: (i, 0)),
            pl.BlockSpec((1, D), lambda j, i: (0, 0)),
            pl.BlockSpec((1, 1, D), lambda j, i: ((i * tm) // T, 0, 0)),
            pl.BlockSpec((1, 1, D), lambda j, i: ((i * tm) // T, 0, 0)),
            pl.BlockSpec((D, tn), lambda j, i: (0, j)),
        ],
        out_specs=pl.BlockSpec((tm, tn), lambda j, i: (i, j)),
        out_shape=jax.ShapeDtypeStruct((N, PROJ_W), BF16),
        compiler_params=_params(("arbitrary", "arbitrary")),
        name="in_proj",
    )(xf, g.reshape(1, D), sc.reshape(B, 1, D), sh.reshape(B, 1, D), wp)


_SRC = dict(zip(("q", "k", "v", "glr", "gr", "sq", "sk", "sv", "ga", "gb"),
                np.cumsum([0, 512, 512, 1024, GLA_RANK, 1024, 1024, 128, 128, 1024])))
W_IN_SLABS = ((COL_Q, _SRC["q"], 512), (COL_K, _SRC["k"], 512), (COL_V, _SRC["v"], 1024),
              (COL_GR, _SRC["gr"], 1024), (COL_SQ, _SRC["sq"], 1024), (COL_GA, _SRC["ga"], 1024),
              (COL_GB, _SRC["gb"], 1024), (COL_SK, _SRC["sk"], 128), (COL_SV, _SRC["sv"], 128),
              (COL_GLR, _SRC["glr"], GLA_RANK))


def _pack_w_in_kernel(w_ref, o_ref):
    o_ref[:, COL_GLR:] = jnp.zeros((o_ref.shape[0], PROJ_W - COL_GLR), BF16)
    for dst, src, width in W_IN_SLABS:
        o_ref[:, dst:dst + width] = w_ref[:, int(src):int(src) + width].astype(BF16)


def _pack_w_in(w_in, layer, tr=128):
    _, rows, cols = w_in.shape
    return pl.pallas_call(
        _pack_w_in_kernel,
        grid=(rows // tr,),
        in_specs=[pl.BlockSpec((None, tr, cols), lambda i: (layer, i, 0))],
        out_specs=pl.BlockSpec((tr, PROJ_W), lambda i: (i, 0)),
        out_shape=jax.ShapeDtypeStruct((rows, PROJ_W), BF16),
        compiler_params=_params(("arbitrary",)),
        name="pack_w_in",
    )(w_in)


def _gla_kernel(q_ref, k_ref, v_ref, r_ref, glr_ref, w2_ref, b2_ref, ng_ref, o_ref, st_ref, *, nchunk):
    @pl.when(pl.program_id(1) == 0)
    def _():
        st_ref[...] = jnp.zeros_like(st_ref)

    C = GLA_CHUNK
    dk, dv = GLA_DK, GLA_DV
    row = lax.broadcasted_iota(I32, (C, C), 0)
    col = lax.broadcasted_iota(I32, (C, C), 1)
    tri = col <= row
    tri_f = tri.astype(F32)
    w2 = w2_ref[...]
    b2 = b2_ref[...]
    ng = ng_ref[...]
    for ci in range(nchunk):
        sl = pl.ds(ci * C, C)
        z = jnp.dot(glr_ref[sl, :], w2, preferred_element_type=F32) + b2
        log_a = (jnp.minimum(z, 0.0) - jnp.log(1.0 + jnp.exp(-jnp.abs(z)))) * (1.0 / GLA_TAU)
        b = jnp.dot(tri_f, log_a, preferred_element_type=F32, precision=HIGHEST)
        b_last = b[C - 1:C, :]
        q = q_ref[sl, :].astype(F32) * (dk ** -0.5)
        k = k_ref[sl, :].astype(F32)
        q_dec = (q * jnp.exp(b)).astype(BF16)
        k_inv = (k * jnp.exp(-b)).astype(BF16)
        k_tail = (k * jnp.exp(b_last - b)).astype(BF16)
        decay = jnp.exp(b_last)
        for h in range(GLA_HEADS):
            kc = slice(h * dk, (h + 1) * dk)
            vc = slice(h * dv, (h + 1) * dv)
            v = v_ref[sl, vc]
            attn = lax.dot_general(q_dec[:, kc], k_inv[:, kc], NT_DIMS, preferred_element_type=F32)
            attn = jnp.where(tri, attn, 0.0).astype(BF16)
            st = st_ref[h]
            o = (jnp.dot(attn, v, preferred_element_type=F32)
                 + lax.dot_general(q_dec[:, kc], st.astype(BF16), NT_DIMS, preferred_element_type=F32))
            st_ref[h] = st * decay[:, kc] + lax.dot_general(v, k_tail[:, kc], TN_DIMS,
                                                            preferred_element_type=F32)
            on = o * lax.rsqrt(jnp.mean(o * o, axis=-1, keepdims=True) + EPS) * ng
            r = r_ref[sl, vc].astype(F32)
            o_ref[sl, vc] = (on * (r * jax.nn.sigmoid(r))).astype(BF16)


def _gla(proj, w2p, b2, ng, tc=256):
    B, T, _ = proj.shape
    H = GLA_HEADS
    kw, vw = H * GLA_DK, H * GLA_DV
    return pl.pallas_call(
        functools.partial(_gla_kernel, nchunk=tc // GLA_CHUNK),
        grid=(B, T // tc),
        in_specs=[
            pl.BlockSpec((None, tc, kw), lambda b, c: (b, c, COL_Q // kw)),
            pl.BlockSpec((None, tc, kw), lambda b, c: (b, c, COL_K // kw)),
            pl.BlockSpec((None, tc, vw), lambda b, c: (b, c, COL_V // vw)),
            pl.BlockSpec((None, tc, vw), lambda b, c: (b, c, COL_GR // vw)),
            pl.BlockSpec((None, tc, LANES), lambda b, c: (b, c, COL_GLR // LANES)),
            pl.BlockSpec((LANES, kw), lambda b, c: (0, 0)),
            pl.BlockSpec((1, kw), lambda b, c: (0, 0)),
            pl.BlockSpec((1, GLA_DV), lambda b, c: (0, 0)),
        ],
        out_specs=pl.BlockSpec((None, tc, vw), lambda b, c: (b, c, 0)),
        out_shape=jax.ShapeDtypeStruct((B, T, vw), BF16),
        scratch_shapes=[pltpu.VMEM((H, GLA_DV, GLA_DK), F32)],
        compiler_params=_params(("arbitrary", "arbitrary")),
        name="gla",
    )(proj, proj, proj, proj, proj, w2p, b2, ng)


def _t5_bucket(dist):
    max_exact = N_BUCKETS // 2
    d = np.maximum(dist, 1).astype(np.float32)
    large = max_exact + (np.log(d / max_exact) / np.log(MAX_DISTANCE / max_exact)
                         * (N_BUCKETS - max_exact)).astype(np.int32)
    large = np.minimum(large, N_BUCKETS - 1)
    return np.where(dist < max_exact, dist, large).astype(np.int32)


NEG_BIG = -1e30


def _swa_bias(rel_bias):
    blk = SWA_BLOCK
    qi = np.arange(blk)[:, None]
    sj = np.arange(2 * blk)[None, :]
    dist = blk + qi - sj
    band = (dist >= 0) & (dist < blk)
    bucket = _t5_bucket(np.clip(dist, 0, None))
    onehot = jnp.asarray(bucket.reshape(-1, 1) == np.arange(N_BUCKETS)[None, :], F32)
    bias = jnp.dot(onehot, rel_bias.astype(F32), precision=HIGHEST).T.reshape(-1, blk, 2 * blk)
    masks = np.stack([band, band & (sj >= blk)])[:, None]
    return jnp.where(jnp.asarray(masks), bias[None], NEG_BIG)


def _segment_sums(x, seg):
    hi = x.astype(BF16)
    lo = (x - hi.astype(F32)).astype(BF16)
    return (jnp.dot(hi, seg, preferred_element_type=F32) + jnp.dot(lo, seg, preferred_element_type=F32))


def _head_rms_scale(x, seg, seg_t):
    inv = lax.rsqrt(_segment_sums(x * x, seg) * (1.0 / SWA_HD) + EPS)
    return _segment_sums(inv, seg_t)


def _swa_kernel(sink_ref, q_ref, kp_ref, kc_ref, vp_ref, vc_ref, bias_ref, qg_ref, kg_ref, seg_ref,
                segt_ref, o_ref):
    blk = SWA_BLOCK
    hd = SWA_HD
    group = SWA_HEADS // SWA_KV_HEADS
    kvw = SWA_KV_HEADS * hd
    seg = seg_ref[...]
    seg_t = segt_ref[...]
    q = q_ref[...].astype(F32)
    qn = (q * _head_rms_scale(q, seg, seg_t) * qg_ref[...] * (hd ** -0.5)).astype(BF16)
    k2 = jnp.concatenate([kp_ref[...], kc_ref[...]], axis=0).astype(F32)
    kn = (k2 * _head_rms_scale(k2, seg[:kvw], seg_t[:, :kvw]) * kg_ref[...]).astype(BF16)
    v2 = jnp.concatenate([vp_ref[...], vc_ref[...]], axis=0)
    lane = lax.broadcasted_iota(I32, (2 * blk, kvw), 1)
    outs = []
    for kh in range(SWA_KV_HEADS):
        kk = kn[:, kh * hd:(kh + 1) * hd]
        vsh = v2 if kh == 0 else jnp.concatenate([v2[:, kh * hd:], v2[:, :kh * hd]], axis=1)
        vv = jnp.where(lane < hd, vsh, jnp.ones_like(vsh))
        qs = jnp.concatenate([qn[:, (kh * group + g) * hd:(kh * group + g + 1) * hd]
                              for g in range(group)], axis=0)
        logits = lax.dot_general(qs, kk, NT_DIMS, preferred_element_type=F32)
        logits = logits + bias_ref[kh * group:(kh + 1) * group].reshape(group * blk, 2 * blk)
        for g in range(group):
            lg = logits[g * blk:(g + 1) * blk]
            sink = sink_ref[kh * group + g]
            m = jnp.maximum(jnp.max(lg, axis=-1, keepdims=True), sink)
            p = jnp.exp(lg - m).astype(BF16)
            pv = jnp.dot(p, vv, preferred_element_type=F32)
            denom = pv[:, hd:hd + 1] + jnp.exp(sink - m)
            outs.append(pv[:, :hd] / denom)
    o_ref[...] = jnp.concatenate(outs, axis=-1).astype(BF16)


def _swa(proj, bias, qg, kg, sinks):
    B, T, _ = proj.shape
    blk = SWA_BLOCK
    qw = SWA_HEADS * SWA_HD
    kvw = SWA_KV_HEADS * SWA_HD
    prev = lambda b, i: jnp.maximum(i - 1, 0)
    seg = (np.arange(qw)[:, None] // SWA_HD == np.arange(LANES)[None, :])
    seg = jnp.asarray(seg, BF16)
    return pl.pallas_call(
        _swa_kernel,
        grid=(B, T // blk),
        in_specs=[
            pl.BlockSpec(memory_space=pltpu.SMEM),
            pl.BlockSpec((None, blk, qw), lambda b, i: (b, i, COL_SQ // qw)),
            pl.BlockSpec((None, blk, kvw), lambda b, i: (b, prev(b, i), COL_SK // kvw)),
            pl.BlockSpec((None, blk, kvw), lambda b, i: (b, i, COL_SK // kvw)),
            pl.BlockSpec((None, blk, kvw), lambda b, i: (b, prev(b, i), COL_SV // kvw)),
            pl.BlockSpec((None, blk, kvw), lambda b, i: (b, i, COL_SV // kvw)),
            pl.BlockSpec((None, SWA_HEADS, blk, 2 * blk), lambda b, i: (jnp.where(i == 0, 1, 0), 0, 0, 0)),
            pl.BlockSpec((1, qw), lambda b, i: (0, 0)),
            pl.BlockSpec((1, kvw), lambda b, i: (0, 0)),
            pl.BlockSpec((qw, LANES), lambda b, i: (0, 0)),
            pl.BlockSpec((LANES, qw), lambda b, i: (0, 0)),
        ],
        out_specs=pl.BlockSpec((None, blk, qw), lambda b, i: (b, i, 0)),
        out_shape=jax.ShapeDtypeStruct((B, T, qw), BF16),
        compiler_params=_params(("arbitrary", "arbitrary")),
        name="swa",
    )(sinks, proj, proj, proj, proj, proj, bias, jnp.tile(qg, (1, SWA_HEADS)),
      jnp.tile(kg, (1, SWA_KV_HEADS)), seg, seg.T)


def _merge_kernel(x_ref, ga_ref, gb_ref, go_ref, so_ref, wa_ref, wb_ref, wo_ref, g1_ref,
                  n2_ref, sc2_ref, sh2_ref, we_ref, x1_ref, h2_ref, st_ref):
    ya = jnp.dot(go_ref[...], wa_ref[...], preferred_element_type=F32)
    yb = jnp.dot(so_ref[...], wb_ref[...], preferred_element_type=F32)
    m = jax.nn.sigmoid(ga_ref[...].astype(F32)) * ya + jax.nn.sigmoid(gb_ref[...].astype(F32)) * yb
    mixed = jnp.dot(m.astype(BF16), wo_ref[...], preferred_element_type=F32)
    x1 = x_ref[...] + g1_ref[0] * mixed
    x1_ref[...] = x1
    y = x1 * lax.rsqrt(jnp.mean(x1 * x1, axis=-1, keepdims=True) + EPS) * n2_ref[...]
    h2 = y * (1.0 + sc2_ref[0]) + sh2_ref[0]
    h2_ref[...] = h2
    st_ref[...] = lax.dot_general(we_ref[...], h2.astype(BF16), NT_DIMS, preferred_element_type=F32)


def _merge(xf, proj2, gla_o, swa_o, wa, wb, wo, g1, n2, sc2, sh2, weff_t, T, tm=256):
    N, D = xf.shape
    B = g1.shape[0]
    SW = weff_t.shape[0]
    bat = lambda i: ((i * tm) // T, 0, 0)
    full = lambda i: (0, 0)
    return pl.pallas_call(
        _merge_kernel,
        grid=(N // tm,),
        in_specs=[
            pl.BlockSpec((tm, D), lambda i: (i, 0)),
            pl.BlockSpec((tm, D), lambda i: (i, COL_GA // D)),
            pl.BlockSpec((tm, D), lambda i: (i, COL_GB // D)),
            pl.BlockSpec((tm, D), lambda i: (i, 0)),
            pl.BlockSpec((tm, D), lambda i: (i, 0)),
            pl.BlockSpec((D, D), full),
            pl.BlockSpec((D, D), full),
            pl.BlockSpec((D, D), full),
            pl.BlockSpec((1, 1, D), bat),
            pl.BlockSpec((1, D), full),
            pl.BlockSpec((1, 1, D), bat),
            pl.BlockSpec((1, 1, D), bat),
            pl.BlockSpec((SW, D), full),
        ],
        out_specs=[
            pl.BlockSpec((tm, D), lambda i: (i, 0)),
            pl.BlockSpec((tm, D), lambda i: (i, 0)),
            pl.BlockSpec((SW, tm), lambda i: (0, i)),
        ],
        out_shape=[
            jax.ShapeDtypeStruct((N, D), F32),
            jax.ShapeDtypeStruct((N, D), F32),
            jax.ShapeDtypeStruct((SW, N), F32),
        ],
        compiler_params=_params(("arbitrary",), VMEM_LIMIT_BYTES),
        name="merge",
    )(xf, proj2, proj2, gla_o, swa_o, wa, wb, wo, g1.reshape(B, 1, D), n2.reshape(1, D),
      sc2.reshape(B, 1, D), sh2.reshape(B, 1, D), weff_t)


def _tree(op, xs):
    xs = list(xs)
    while len(xs) > 1:
        xs = [op(xs[i], xs[i + 1]) for i in range(0, len(xs) - 1, 2)] + ([xs[-1]] if len(xs) % 2 else [])
    return xs[0]


EXTRACT_CHAINS = 4


def _extract_best(ref, n):
    lanes = min(EXTRACT_CHAINS, n)
    best = [ref[i] for i in range(lanes)]
    for i in range(lanes, n):
        best[i % lanes] = jnp.maximum(best[i % lanes], ref[i])
    m = _tree(jnp.maximum, best)
    first = [jnp.where(ref[i] == m, i, n) for i in range(lanes)]
    for i in range(lanes, n):
        first[i % lanes] = jnp.minimum(first[i % lanes], jnp.where(ref[i] == m, i, n))
    pos = _tree(jnp.minimum, first)
    for i in range(n):
        ref[i] = jnp.where(pos == i, -jnp.inf, ref[i])
    return m, pos


def _ordered(a, b):
    (va, ia), (vb, ib) = a, b
    keep = jnp.logical_or(va > vb, jnp.logical_and(va == vb, ia < ib))
    first = (jnp.where(keep, va, vb), jnp.where(keep, ia, ib))
    second = (jnp.where(keep, vb, va), jnp.where(keep, ib, ia))
    return first, second


def _bitonic_merge(xs):
    xs = list(xs)
    j = len(xs) // 2
    while j >= 1:
        for i in range(len(xs)):
            if i & j == 0:
                xs[i], xs[i | j] = _ordered(xs[i], xs[i | j])
        j //= 2
    return xs


def _bitonic_sort(xs):
    xs = list(xs)
    k = 2
    while k <= len(xs):
        j = k // 2
        while j >= 1:
            for i in range(len(xs)):
                if i & j == 0:
                    first, second = _ordered(xs[i], xs[i | j])
                    xs[i], xs[i | j] = (first, second) if i & k == 0 else (second, first)
            j //= 2
        k *= 2
    return xs


def _merge_best(xs, ys):
    n = len(xs)
    return _bitonic_merge([_ordered(xs[i], ys[n - 1 - i])[0] for i in range(n)])


TOPK_PAIRS = tuple((a, b) for a in range(PEER_TOPK) for b in range(PEER_TOPK)
                   if (a + 1) * (b + 1) <= PEER_TOPK)


def _topk_kernel(s_ref, e_ref, g_ref, sv_ref, si_ref, v_ref, i_ref, cand_ref, cidx_ref, best_ref, row_ref):
    K = PEER_TOPK
    nk = PEER_NKEYS
    H = PEER_HEADS
    G = 2 * H
    tm = s_ref.shape[1]
    nblocks = nk // K
    for hb in range(2):
        rows = slice(hb * H, (hb + 1) * H)
        for blk in range(nblocks):
            keys = [(s_ref[(blk * K + i) * G + hb * H:(blk * K + i) * G + (hb + 1) * H, :],
                     jnp.full((H, tm), blk * K + i, I32)) for i in range(K)]
            for i, (v, ix) in enumerate(_bitonic_sort(keys)):
                sv_ref[blk, i, rows, :] = v
                si_ref[blk, i, rows, :] = ix
        step = 1
        while step < nblocks:
            for blk in range(0, nblocks, 2 * step):
                xs = [(sv_ref[blk, i, rows, :], si_ref[blk, i, rows, :]) for i in range(K)]
                ys = [(sv_ref[blk + step, i, rows, :], si_ref[blk + step, i, rows, :]) for i in range(K)]
                for i, (v, ix) in enumerate(_merge_best(xs, ys)):
                    sv_ref[blk, i, rows, :] = v
                    si_ref[blk, i, rows, :] = ix
            step *= 2
    v_ref[...] = sv_ref[0]
    i_ref[...] = si_ref[0]

    for ci, (a, b) in enumerate(TOPK_PAIRS):
        cand_ref[ci] = v_ref[a, 0:H, :] + v_ref[b, H:2 * H, :]
        cidx_ref[ci] = (i_ref[a, 0:H, :] * nk + i_ref[b, H:2 * H, :]) * WORDS_PER_EXPERT
    ncand = len(TOPK_PAIRS)

    def stage2(k, _):
        m, pos = _extract_best(cand_ref, ncand)
        row = _tree(jnp.maximum, [jnp.where(pos == ci, cidx_ref[ci], -1) for ci in range(ncand)])
        best_ref[k] = m
        row_ref[k] = row
        return 0

    lax.fori_loop(0, K, stage2, 0)
    best = best_ref[...]
    ex = jnp.exp(best - best[0:1])
    gates = ex / jnp.sum(ex, axis=0, keepdims=True)
    g_ref[...] = gates.reshape(K * H, tm).T
    e_ref[...] = row_ref[...].reshape(K * H, tm).astype(F32).T.astype(I32)


def _topk(s_t, tm=128):
    SW, N = s_t.shape
    K = PEER_TOPK
    G = 2 * PEER_HEADS
    return pl.pallas_call(
        _topk_kernel,
        grid=(N // tm,),
        in_specs=[pl.BlockSpec((SW, tm), lambda i: (0, i))],
        out_specs=[
            pl.BlockSpec((tm, PEER_SLOTS), lambda i: (i, 0)),
            pl.BlockSpec((tm, PEER_SLOTS), lambda i: (i, 0)),
        ],
        out_shape=[
            jax.ShapeDtypeStruct((N, PEER_SLOTS), I32),
            jax.ShapeDtypeStruct((N, PEER_SLOTS), F32),
        ],
        scratch_shapes=[pltpu.VMEM((PEER_NKEYS // K, K, G, tm), F32),
                        pltpu.VMEM((PEER_NKEYS // K, K, G, tm), I32),
                        pltpu.VMEM((K, G, tm), F32),
                        pltpu.VMEM((K, G, tm), I32),
                        pltpu.VMEM((len(TOPK_PAIRS), PEER_HEADS, tm), F32),
                        pltpu.VMEM((len(TOPK_PAIRS), PEER_HEADS, tm), I32),
                        pltpu.VMEM((K, PEER_HEADS, tm), F32),
                        pltpu.VMEM((K, PEER_HEADS, tm), I32)],
        compiler_params=_params(("arbitrary",)),
        name="peer_topk",
    )(s_t)


WORDS_PER_EXPERT = SUBLANES // 2


TILE_BLOCK = tuple((r % 2) * WORDS_PER_EXPERT + r // 2 for r in range(SUBLANES))
BLOCK_ROW = tuple(TILE_BLOCK.index(q) for q in range(SUBLANES))


def _bf16_bits(x):
    return lax.bitcast_convert_type(x.astype(BF16).astype(F32), jnp.uint32)


def _table_kernel(t_ref, o_ref):
    rows = t_ref.shape[0]
    for s in range(WORDS_PER_EXPERT):
        lo = _bf16_bits(t_ref[:, s * LANES:(s + 1) * LANES])
        hi = _bf16_bits(t_ref[:, (s + WORDS_PER_EXPERT) * LANES:(s + WORDS_PER_EXPERT + 1) * LANES])
        word = lax.shift_right_logical(lo, jnp.uint32(16)) | (hi & jnp.uint32(0xFFFF0000))
        o_ref[pl.ds(s, rows, stride=WORDS_PER_EXPERT), :] = lax.bitcast_convert_type(word, I32)


def _table(tables, layer, te=2048):
    _, n, d = tables.shape
    return pl.pallas_call(
        _table_kernel,
        grid=(n // te,),
        in_specs=[pl.BlockSpec((None, te, d), lambda i: (layer, i, 0))],
        out_specs=pl.BlockSpec((te * WORDS_PER_EXPERT, LANES), lambda i: (i, 0)),
        out_shape=jax.ShapeDtypeStruct((n * WORDS_PER_EXPERT, LANES), I32),
        compiler_params=_params(("arbitrary",)),
        name="peer_table",
    )(tables)


def _expert_tile(tab_ref, e4):
    words = tab_ref[pl.ds(pl.multiple_of(e4, WORDS_PER_EXPERT), WORDS_PER_EXPERT), :]
    return pltpu.bitcast(words, BF16).astype(F32)


def _to_tile_rows(x):
    return jnp.concatenate([x[q:q + 1, :] for q in TILE_BLOCK], axis=0)


def _from_tile_rows(x):
    return jnp.concatenate([x[r:r + 1, :] for r in BLOCK_ROW], axis=0)


def _gelu_tanh(x):
    return 0.5 * x * (1.0 + jnp.tanh(np.sqrt(2.0 / np.pi) * (x + 0.044715 * (x * x * x))))


RING_SLOTS = 2
SLOT_TOKENS = 16


def _for_each_token(e_ref, idx_ref, sem, tb, token_body):
    ngroups = tb // SLOT_TOKENS

    def fetch(q, slot):
        return pltpu.make_async_copy(e_ref.at[q], idx_ref.at[slot], sem.at[slot])

    for slot in range(RING_SLOTS):
        fetch(slot, slot).start()

    def ring(j, _):
        for slot in range(RING_SLOTS):
            q = j * RING_SLOTS + slot
            fetch(q, slot).wait()
            for tt in range(SLOT_TOKENS):
                token_body(q * SLOT_TOKENS + tt, lambda k, slot=slot, tt=tt: idx_ref[slot, tt, k])
            fetch(jnp.minimum(q + RING_SLOTS, ngroups - 1), slot).start()
        return 0

    lax.fori_loop(0, ngroups // RING_SLOTS, ring, 0)
    for slot in range(RING_SLOTS):
        fetch(ngroups - 1, slot).wait()


PAIR_GROUP = LANES // SUBLANES
SLOT_GROUPS = PEER_SLOTS // PAIR_GROUP


def _slot_tiles(tab_ref, row, j):
    tiles = []
    for kk in range(PAIR_GROUP):
        src = pl.ds(pl.multiple_of(row(j * PAIR_GROUP + kk), WORDS_PER_EXPERT), WORDS_PER_EXPERT)
        tiles.append(pltpu.bitcast(tab_ref[src, :], BF16))
    return jnp.concatenate(tiles, axis=0)


def _spread_matrices():
    lane = np.arange(LANES)
    return np.stack([lane[:, None] == PAIR_GROUP * j + lane[None, :] // SUBLANES
                     for j in range(SLOT_GROUPS)])


def _peer_u_kernel(e_ref, h_ref, g_ref, tab_ref, sel_ref, o_ref, slab_ref, idx_ref, sem, *, tb):
    groups_per_token = PEER_SLOTS // SUBLANES
    sel = sel_ref[...]

    def token(t, row):
        hb = _to_tile_rows(h_ref[t]).astype(BF16)
        for j in range(groups_per_token):
            prods = []
            for qn in range(SUBLANES):
                src = pl.ds(pl.multiple_of(row(j * SUBLANES + qn), WORDS_PER_EXPERT), WORDS_PER_EXPERT)
                prods.append(pltpu.bitcast(tab_ref[src, :], BF16) * hb)
            prods = jnp.concatenate(prods, axis=0)
            dst = pl.multiple_of(t * PEER_SLOTS + j * SUBLANES, SUBLANES)
            slab_ref[pl.ds(dst, SUBLANES), :] = jnp.dot(sel, prods, preferred_element_type=F32)

    _for_each_token(e_ref, idx_ref, sem, tb, token)
    rio = lax.broadcasted_iota(I32, (LANES, LANES), 0)
    lio = lax.broadcasted_iota(I32, (LANES, LANES), 1)
    diag = (rio == lio)[None]
    rows = SUBLANES * PEER_SLOTS
    for gi in range(tb // SUBLANES):
        tok = slice(gi * SUBLANES, (gi + 1) * SUBLANES)
        rs = jnp.sum(slab_ref[gi * rows:(gi + 1) * rows, :], axis=-1, keepdims=True)
        rs = rs.reshape(SUBLANES, PEER_SLOTS, 1)
        a = jnp.sum(jnp.where(diag, rs, 0.0), axis=1)
        o_ref[tok, :] = g_ref[tok, :] * _gelu_tanh(a)


def _peer_u(e4, h3, gates, tab, tb=128):
    N = h3.shape[0]
    assert tb % (RING_SLOTS * SLOT_TOKENS) == 0 and tb % SUBLANES == 0
    sel = np.arange(SUBLANES)[:, None] == np.arange(SUBLANES * SUBLANES)[None, :] // SUBLANES
    return pl.pallas_call(
        functools.partial(_peer_u_kernel, tb=tb),
        grid=(N // tb,),
        in_specs=[
            pl.BlockSpec((tb // SLOT_TOKENS, SLOT_TOKENS, PEER_SLOTS), lambda i: (i, 0, 0)),
            pl.BlockSpec((tb, SUBLANES, LANES), lambda i: (i, 0, 0)),
            pl.BlockSpec((tb, PEER_SLOTS), lambda i: (i, 0)),
            pl.BlockSpec(tab.shape, lambda i: (0, 0), pipeline_mode=pl.Buffered(1)),
            pl.BlockSpec(sel.shape, lambda i: (0, 0)),
        ],
        out_specs=pl.BlockSpec((tb, PEER_SLOTS), lambda i: (i, 0)),
        out_shape=jax.ShapeDtypeStruct((N, PEER_SLOTS), F32),
        scratch_shapes=[pltpu.VMEM((tb * PEER_SLOTS, LANES), F32),
                        pltpu.SMEM((RING_SLOTS, SLOT_TOKENS, PEER_SLOTS), I32),
                        pltpu.SemaphoreType.DMA((RING_SLOTS,))],
        compiler_params=_params(("arbitrary",), VMEM_LIMIT_BYTES),
        name="peer_u",
    )(e4.reshape(N // SLOT_TOKENS, SLOT_TOKENS, PEER_SLOTS), h3, gates, tab, jnp.asarray(sel, BF16))


def _peer_v_kernel(e_ref, c_ref, x1_ref, g2_ref, tab_ref, spread_ref, o_ref, hi_ref, lo_ref, idx_ref, sem, *,
                   tb):
    c = c_ref[...]
    c_hi = c.astype(BF16)
    c_lo = (c - c_hi.astype(F32)).astype(BF16)
    for j in range(SLOT_GROUPS):
        rows = pl.ds(j, tb, stride=SLOT_GROUPS)
        hi_ref[rows, :] = jnp.dot(c_hi, spread_ref[j], preferred_element_type=F32)
        lo_ref[rows, :] = jnp.dot(c_lo, spread_ref[j], preferred_element_type=F32)
    rio = lax.broadcasted_iota(I32, (SUBLANES, LANES), 0)
    lio = lax.broadcasted_iota(I32, (SUBLANES, LANES), 1)
    own_row = (lio & (SUBLANES - 1)) == rio

    def token(t, row):
        base = pl.multiple_of(t * SLOT_GROUPS, SLOT_GROUPS)
        hi = hi_ref[pl.ds(base, SLOT_GROUPS), :]
        lo = lo_ref[pl.ds(base, SLOT_GROUPS), :]
        acc = jnp.zeros((SUBLANES, LANES), F32)
        for j in range(SLOT_GROUPS):
            lhs = jnp.concatenate(
                [jnp.where(own_row, jnp.broadcast_to(part[j:j + 1, :], (SUBLANES, LANES)), 0.0).astype(BF16)
                 for part in (hi, lo)], axis=0)
            out = jnp.dot(lhs, _slot_tiles(tab_ref, row, j), preferred_element_type=F32)
            acc = acc + (out[:SUBLANES] + out[SUBLANES:])
        o_ref[t] = x1_ref[t] + g2_ref[0] * _from_tile_rows(acc)

    _for_each_token(e_ref, idx_ref, sem, tb, token)


def _peer_v(e4, coef, x1_3, g2_3, tab, T, tb=128):
    N = x1_3.shape[0]
    assert tb % (RING_SLOTS * SLOT_TOKENS) == 0 and tb % SUBLANES == 0
    return pl.pallas_call(
        functools.partial(_peer_v_kernel, tb=tb),
        grid=(N // tb,),
        in_specs=[
            pl.BlockSpec((tb // SLOT_TOKENS, SLOT_TOKENS, PEER_SLOTS), lambda i: (i, 0, 0)),
            pl.BlockSpec((tb, PEER_SLOTS), lambda i: (i, 0)),
            pl.BlockSpec((tb, SUBLANES, LANES), lambda i: (i, 0, 0)),
            pl.BlockSpec((1, SUBLANES, LANES), lambda i: ((i * tb) // T, 0, 0)),
            pl.BlockSpec(tab.shape, lambda i: (0, 0), pipeline_mode=pl.Buffered(1)),
            pl.BlockSpec((SLOT_GROUPS, LANES, LANES), lambda i: (0, 0, 0)),
        ],
        out_specs=pl.BlockSpec((tb, SUBLANES, LANES), lambda i: (i, 0, 0)),
        out_shape=jax.ShapeDtypeStruct((N, SUBLANES, LANES), F32),
        scratch_shapes=[pltpu.VMEM((tb * SLOT_GROUPS, LANES), F32),
                        pltpu.VMEM((tb * SLOT_GROUPS, LANES), F32),
                        pltpu.SMEM((RING_SLOTS, SLOT_TOKENS, PEER_SLOTS), I32),
                        pltpu.SemaphoreType.DMA((RING_SLOTS,))],
        compiler_params=_params(("arbitrary",), VMEM_LIMIT_BYTES),
        name="peer_v",
    )(e4.reshape(N // SLOT_TOKENS, SLOT_TOKENS, PEER_SLOTS), coef, x1_3, g2_3, tab,
      jnp.asarray(_spread_matrices(), BF16))


def kernel(x, c, w_ada, b_ada, norm1_g, w_in, gla_gate_w2, gla_gate_b, gla_norm_g, swa_qnorm_g,
           swa_knorm_g, swa_sinks, rel_bias, w_up_a, w_up_b, w_out, norm2_g, peer_wq, peer_subkeys,
           peer_u, peer_v):
    B, T, D = x.shape
    N = B * T
    L = w_ada.shape[0]
    mod = _adaln(c, w_ada, b_ada)
    weff_t = _fold_peer_keys(peer_wq, peer_subkeys)
    bias = _swa_bias(rel_bias)
    xf = x.reshape(N, D)
    for l in range(L):
        sh1, sc1, g1, sh2, sc2, g2 = [mod[l, :, i * D:(i + 1) * D] for i in range(6)]
        proj = _in_proj(xf, norm1_g[l], sc1, sh1, _pack_w_in(w_in, l), T)
        proj3 = proj.reshape(B, T, PROJ_W)
        w2p = jnp.zeros((LANES, GLA_HEADS * GLA_DK), BF16).at[:GLA_RANK].set(gla_gate_w2[l].astype(BF16))
        gla_o = _gla(proj3, w2p, gla_gate_b[l].reshape(1, -1), gla_norm_g[l].reshape(1, -1))
        swa_o = _swa(proj3, bias, swa_qnorm_g[l].reshape(1, -1), swa_knorm_g[l].reshape(1, -1),
                     swa_sinks[l])
        x1, h2, s_t = _merge(xf, proj, gla_o.reshape(N, -1), swa_o.reshape(N, -1),
                             w_up_a[l].astype(BF16), w_up_b[l].astype(BF16), w_out[l].astype(BF16),
                             g1, norm2_g[l], sc2, sh2, weff_t[l], T)
        e4, gates = _topk(s_t)
        coef = _peer_u(e4, h2.reshape(N, SUBLANES, LANES), gates, _table(peer_u, l))
        g2_3 = g2.reshape(B, SUBLANES, LANES)
        xf = _peer_v(e4, coef, x1.reshape(N, SUBLANES, LANES), g2_3, _table(peer_v, l), T).reshape(N, D)
    return xf.reshape(B, T, D)
```

```python
import functools

import numpy as np
import jax
import jax.numpy as jnp
from jax import lax
from jax.experimental import pallas as pl
from jax.experimental.pallas import tpu as pltpu

F32 = jnp.float32
BF16 = jnp.bfloat16
I32 = jnp.int32
HIGHEST = lax.Precision.HIGHEST
EPS = 1e-6

GLA_HEADS = 4
GLA_DK = 128
GLA_DV = 256
GLA_RANK = 16
GLA_TAU = 16.0
GLA_CHUNK = 64
SWA_HEADS = 16
SWA_KV_HEADS = 2
SWA_HD = 64
SWA_BLOCK = 128
N_BUCKETS = 32
MAX_DISTANCE = 128
PEER_HEADS = 8
PEER_NKEYS = 128
PEER_TOPK = 16
PEER_SLOTS = PEER_HEADS * PEER_TOPK

SUBLANES = 8
LANES = 128
VMEM_LIMIT_BYTES = 56 * 1024 * 1024

NT_DIMS = (((1,), (1,)), ((), ()))
TN_DIMS = (((0,), (0,)), ((), ()))

COL_Q, COL_K, COL_V, COL_GR, COL_SQ, COL_GA, COL_GB = 0, 512, 1024, 2048, 3072, 4096, 5120
COL_SK, COL_SV, COL_GLR = 6144, 6272, 6400
PROJ_W = 6528


def _params(sem, vmem=None):
    return pltpu.CompilerParams(dimension_semantics=sem, vmem_limit_bytes=vmem)


def _adaln_kernel(c_ref, w_ref, b_ref, o_ref):
    c = c_ref[...]
    a = c * jax.nn.sigmoid(c)
    o_ref[0] = jnp.dot(a, w_ref[0], preferred_element_type=F32, precision=HIGHEST) + b_ref[0]


def _adaln(c, w_ada, b_ada):
    L, D, W = w_ada.shape
    B = c.shape[0]
    rows = -(-B // SUBLANES) * SUBLANES
    cp = jnp.zeros((rows, D), F32).at[:B].set(c)
    tn = W // 4
    out = pl.pallas_call(
        _adaln_kernel,
        grid=(L, W // tn),
        in_specs=[
            pl.BlockSpec((rows, D), lambda l, j: (0, 0)),
            pl.BlockSpec((1, D, tn), lambda l, j: (l, 0, j)),
            pl.BlockSpec((1, 1, tn), lambda l, j: (l, 0, j)),
        ],
        out_specs=pl.BlockSpec((1, rows, tn), lambda l, j: (l, 0, j)),
        out_shape=jax.ShapeDtypeStruct((L, rows, W), F32),
        compiler_params=_params(("arbitrary", "arbitrary")),
        name="adaln",
    )(cp, w_ada, b_ada.reshape(L, 1, W))
    return out[:, :B]


def _fold_kernel(sk_ref, wq_ref, o_ref, rows_ref):
    half = sk_ref.shape[-1]
    groups = 2 * PEER_HEADS
    for h in range(PEER_HEADS):
        for p in range(2):
            g = 2 * h + p
            res = lax.dot_general(sk_ref[0, p], wq_ref[0, :, g * half:(g + 1) * half], NT_DIMS,
                                  precision=HIGHEST, preferred_element_type=F32)
            for cb in range(rows_ref.shape[0]):
                rows_ref[cb, pl.ds(p * PEER_HEADS + h, PEER_NKEYS, stride=groups), :] = (
                    res[:, cb * LANES:(cb + 1) * LANES])
    for cb in range(rows_ref.shape[0]):
        o_ref[0, :, cb * LANES:(cb + 1) * LANES] = rows_ref[cb].astype(BF16)


def _fold_peer_keys(peer_wq, peer_subkeys):
    L, D, QW = peer_wq.shape
    half = peer_subkeys.shape[-1]
    rows = (QW // half) * PEER_NKEYS
    return pl.pallas_call(
        _fold_kernel,
        grid=(L,),
        in_specs=[
            pl.BlockSpec((1, 2, PEER_NKEYS, half), lambda l: (l, 0, 0, 0)),
            pl.BlockSpec((1, D, QW), lambda l: (l, 0, 0)),
        ],
        out_specs=pl.BlockSpec((1, rows, D), lambda l: (l, 0, 0)),
        out_shape=jax.ShapeDtypeStruct((L, rows, D), BF16),
        scratch_shapes=[pltpu.VMEM((D // LANES, rows, LANES), F32)],
        compiler_params=_params(("arbitrary",), VMEM_LIMIT_BYTES),
        name="peer_fold",
    )(peer_subkeys, peer_wq)


def _inproj_kernel(x_ref, g_ref, sc_ref, sh_ref, w_ref, o_ref):
    x = x_ref[...]
    ms = jnp.mean(x * x, axis=-1, keepdims=True)
    y = x * lax.rsqrt(ms + EPS) * g_ref[...]
    h = y * (1.0 + sc_ref[0]) + sh_ref[0]
    o_ref[...] = jnp.dot(h.astype(BF16), w_ref[...], preferred_element_type=F32).astype(BF16)


def _in_proj(xf, g, sc, sh, wp, T, tm=512):
    N, D = xf.shape
    B = sc.shape[0]
    ncol = 3
    tn = PROJ_W // ncol
    return pl.pallas_call(
        _inproj_kernel,
        grid=(ncol, N // tm),
        in_specs=[
            pl.BlockSpec((tm, D), lambda j, i: (i, 0)),
            pl.BlockSpec((1, D), lambda j, i: (0, 0)),
            pl.BlockSpec((1, 1, D), lambda j, i: ((i * tm) // T, 0, 0)),
            pl.BlockSpec((1, 1, D), lambda j, i: ((i * tm) // T, 0, 0)),
            pl.BlockSpec((D, tn), lambda j, i: (0, j)),
        ],
        out_specs=pl.BlockSpec((tm, tn), lambda j, i: (i, j)),
        out_shape=jax.ShapeDtypeStruct((N, PROJ_W), BF16),
        compiler_params=_params(("arbitrary", "arbitrary")),
        name="in_proj",
    )(xf, g.reshape(1, D), sc.reshape(B, 1, D), sh.reshape(B, 1, D), wp)


_SRC = dict(zip(("q", "k", "v", "glr", "gr", "sq", "sk", "sv", "ga", "gb"),
                np.cumsum([0, 512, 512, 1024, GLA_RANK, 1024, 1024, 128, 128, 1024])))
W_IN_SLABS = ((COL_Q, _SRC["q"], 512), (COL_K, _SRC["k"], 512), (COL_V, _SRC["v"], 1024),
              (COL_GR, _SRC["gr"], 1024), (COL_SQ, _SRC["sq"], 1024), (COL_GA, _SRC["ga"], 1024),
              (COL_GB, _SRC["gb"], 1024), (COL_SK, _SRC["sk"], 128), (COL_SV, _SRC["sv"], 128),
              (COL_GLR, _SRC["glr"], GLA_RANK))


def _pack_w_in_kernel(w_ref, o_ref):
    o_ref[:, COL_GLR:] = jnp.zeros((o_ref.shape[0], PROJ_W - COL_GLR), BF16)
    for dst, src, width in W_IN_SLABS:
        o_ref[:, dst:dst + width] = w_ref[:, int(src):int(src) + width].astype(BF16)


def _pack_w_in(w_in, layer, tr=128):
    _, rows, cols = w_in.shape
    return pl.pallas_call(
        _pack_w_in_kernel,
        grid=(rows // tr,),
        in_specs=[pl.BlockSpec((None, tr, cols), lambda i: (layer, i, 0))],
        out_specs=pl.BlockSpec((tr, PROJ_W), lambda i: (i, 0)),
        out_shape=jax.ShapeDtypeStruct((rows, PROJ_W), BF16),
        compiler_params=_params(("arbitrary",)),
        name="pack_w_in",
    )(w_in)


def _gla_kernel(q_ref, k_ref, v_ref, r_ref, glr_ref, w2_ref, b2_ref, ng_ref, o_ref, st_ref, *, nchunk):
    @pl.when(pl.program_id(1) == 0)
    def _():
        st_ref[...] = jnp.zeros_like(st_ref)

    C = GLA_CHUNK
    dk, dv = GLA_DK, GLA_DV
    row = lax.broadcasted_iota(I32, (C, C), 0)
    col = lax.broadcasted_iota(I32, (C, C), 1)
    tri = col <= row
    tri_f = tri.astype(F32)
    w2 = w2_ref[...]
    b2 = b2_ref[...]
    ng = ng_ref[...]
    for ci in range(nchunk):
        sl = pl.ds(ci * C, C)
        z = jnp.dot(glr_ref[sl, :], w2, preferred_element_type=F32) + b2
        log_a = (jnp.minimum(z, 0.0) - jnp.log(1.0 + jnp.exp(-jnp.abs(z)))) * (1.0 / GLA_TAU)
        b = jnp.dot(tri_f, log_a, preferred_element_type=F32, precision=HIGHEST)
        b_last = b[C - 1:C, :]
        q = q_ref[sl, :].astype(F32) * (dk ** -0.5)
        k = k_ref[sl, :].astype(F32)
        q_dec = (q * jnp.exp(b)).astype(BF16)
        k_inv = (k * jnp.exp(-b)).astype(BF16)
        k_tail = (k * jnp.exp(b_last - b)).astype(BF16)
        decay = jnp.exp(b_last)
        for h in range(GLA_HEADS):
            kc = slice(h * dk, (h + 1) * dk)
            vc = slice(h * dv, (h + 1) * dv)
            v = v_ref[sl, vc]
            attn = lax.dot_general(q_dec[:, kc], k_inv[:, kc], NT_DIMS, preferred_element_type=F32)
            attn = jnp.where(tri, attn, 0.0).astype(BF16)
            st = st_ref[h]
            o = (jnp.dot(attn, v, preferred_element_type=F32)
                 + lax.dot_general(q_dec[:, kc], st.astype(BF16), NT_DIMS, preferred_element_type=F32))
            st_ref[h] = st * decay[:, kc] + lax.dot_general(v, k_tail[:, kc], TN_DIMS,
                                                            preferred_element_type=F32)
            on = o * lax.rsqrt(jnp.mean(o * o, axis=-1, keepdims=True) + EPS) * ng
            r = r_ref[sl, vc].astype(F32)
            o_ref[sl, vc] = (on * (r * jax.nn.sigmoid(r))).astype(BF16)


def _gla(proj, w2p, b2, ng, tc=256):
    B, T, _ = proj.shape
    H = GLA_HEADS
    kw, vw = H * GLA_DK, H * GLA_DV
    return pl.pallas_call(
        functools.partial(_gla_kernel, nchunk=tc // GLA_CHUNK),
        grid=(B, T // tc),
        in_specs=[
            pl.BlockSpec((None, tc, kw), lambda b, c: (b, c, COL_Q // kw)),
            pl.BlockSpec((None, tc, kw), lambda b, c: (b, c, COL_K // kw)),
            pl.BlockSpec((None, tc, vw), lambda b, c: (b, c, COL_V // vw)),
            pl.BlockSpec((None, tc, vw), lambda b, c: (b, c, COL_GR // vw)),
            pl.BlockSpec((None, tc, LANES), lambda b, c: (b, c, COL_GLR // LANES)),
            pl.BlockSpec((LANES, kw), lambda b, c: (0, 0)),
            pl.BlockSpec((1, kw), lambda b, c: (0, 0)),
            pl.BlockSpec((1, GLA_DV), lambda b, c: (0, 0)),
        ],
        out_specs=pl.BlockSpec((None, tc, vw), lambda b, c: (b, c, 0)),
        out_shape=jax.ShapeDtypeStruct((B, T, vw), BF16),
        scratch_shapes=[pltpu.VMEM((H, GLA_DV, GLA_DK), F32)],
        compiler_params=_params(("arbitrary", "arbitrary")),
        name="gla",
    )(proj, proj, proj, proj, proj, w2p, b2, ng)


def _t5_bucket(dist):
    max_exact = N_BUCKETS // 2
    d = np.maximum(dist, 1).astype(np.float32)
    large = max_exact + (np.log(d / max_exact) / np.log(MAX_DISTANCE / max_exact)
                         * (N_BUCKETS - max_exact)).astype(np.int32)
    large = np.minimum(large, N_BUCKETS - 1)
    return np.where(dist < max_exact, dist, large).astype(np.int32)


NEG_BIG = -1e30


def _swa_bias(rel_bias):
    blk = SWA_BLOCK
    qi = np.arange(blk)[:, None]
    sj = np.arange(2 * blk)[None, :]
    dist = blk + qi - sj
    band = (dist >= 0) & (dist < blk)
    bucket = _t5_bucket(np.clip(dist, 0, None))
    onehot = jnp.asarray(bucket.reshape(-1, 1) == np.arange(N_BUCKETS)[None, :], F32)
    bias = jnp.dot(onehot, rel_bias.astype(F32), precision=HIGHEST).T.reshape(-1, blk, 2 * blk)
    masks = np.stack([band, band & (sj >= blk)])[:, None]
    return jnp.where(jnp.asarray(masks), bias[None], NEG_BIG)


def _segment_sums(x, seg):
    hi = x.astype(BF16)
    lo = (x - hi.astype(F32)).astype(BF16)
    return (jnp.dot(hi, seg, preferred_element_type=F32) + jnp.dot(lo, seg, preferred_element_type=F32))


def _head_rms_scale(x, seg, seg_t):
    inv = lax.rsqrt(_segment_sums(x * x, seg) * (1.0 / SWA_HD) + EPS)
    return _segment_sums(inv, seg_t)


def _swa_kernel(sink_ref, q_ref, kp_ref, kc_ref, vp_ref, vc_ref, bias_ref, qg_ref, kg_ref, seg_ref,
                segt_ref, o_ref):
    blk = SWA_BLOCK
    hd = SWA_HD
    group = SWA_HEADS // SWA_KV_HEADS
    kvw = SWA_KV_HEADS * hd
    seg = seg_ref[...]
    seg_t = segt_ref[...]
    q = q_ref[...].astype(F32)
    qn = (q * _head_rms_scale(q, seg, seg_t) * qg_ref[...] * (hd ** -0.5)).astype(BF16)
    k2 = jnp.concatenate([kp_ref[...], kc_ref[...]], axis=0).astype(F32)
    kn = (k2 * _head_rms_scale(k2, seg[:kvw], seg_t[:, :kvw]) * kg_ref[...]).astype(BF16)
    v2 = jnp.concatenate([vp_ref[...], vc_ref[...]], axis=0)
    lane = lax.broadcasted_iota(I32, (2 * blk, kvw), 1)
    outs = []
    for kh in range(SWA_KV_HEADS):
        kk = kn[:, kh * hd:(kh + 1) * hd]
        vsh = v2 if kh == 0 else jnp.concatenate([v2[:, kh * hd:], v2[:, :kh * hd]], axis=1)
        vv = jnp.where(lane < hd, vsh, jnp.ones_like(vsh))
        qs = jnp.concatenate([qn[:, (kh * group + g) * hd:(kh * group + g + 1) * hd]
                              for g in range(group)], axis=0)
        logits = lax.dot_general(qs, kk, NT_DIMS, preferred_element_type=F32)
        logits = logits + bias_ref[kh * group:(kh + 1) * group].reshape(group * blk, 2 * blk)
        for g in range(group):
            lg = logits[g * blk:(g + 1) * blk]
            sink = sink_ref[kh * group + g]
            m = jnp.maximum(jnp.max(lg, axis=-1, keepdims=True), sink)
            p = jnp.exp(lg - m).astype(BF16)
            pv = jnp.dot(p, vv, preferred_element_type=F32)
            denom = pv[:, hd:hd + 1] + jnp.exp(sink - m)
            outs.append(pv[:, :hd] / denom)
    o_ref[...] = jnp.concatenate(outs, axis=-1).astype(BF16)


def _swa(proj, bias, qg, kg, sinks):
    B, T, _ = proj.shape
    blk = SWA_BLOCK
    qw = SWA_HEADS * SWA_HD
    kvw = SWA_KV_HEADS * SWA_HD
    prev = lambda b, i: jnp.maximum(i - 1, 0)
    seg = (np.arange(qw)[:, None] // SWA_HD == np.arange(LANES)[None, :])
    seg = jnp.asarray(seg, BF16)
    return pl.pallas_call(
        _swa_kernel,
        grid=(B, T // blk),
        in_specs=[
            pl.BlockSpec(memory_space=pltpu.SMEM),
            pl.BlockSpec((None, blk, qw), lambda b, i: (b, i, COL_SQ // qw)),
            pl.BlockSpec((None, blk, kvw), lambda b, i: (b, prev(b, i), COL_SK // kvw)),
            pl.BlockSpec((None, blk, kvw), lambda b, i: (b, i, COL_SK // kvw)),
            pl.BlockSpec((None, blk, kvw), lambda b, i: (b, prev(b, i), COL_SV // kvw)),
            pl.BlockSpec((None, blk, kvw), lambda b, i: (b, i, COL_SV // kvw)),
            pl.BlockSpec((None, SWA_HEADS, blk, 2 * blk), lambda b, i: (jnp.where(i == 0, 1, 0), 0, 0, 0)),
            pl.BlockSpec((1, qw), lambda b, i: (0, 0)),
            pl.BlockSpec((1, kvw), lambda b, i: (0, 0)),
            pl.BlockSpec((qw, LANES), lambda b, i: (0, 0)),
            pl.BlockSpec((LANES, qw), lambda b, i: (0, 0)),
        ],
        out_specs=pl.BlockSpec((None, blk, qw), lambda b, i: (b, i, 0)),
        out_shape=jax.ShapeDtypeStruct((B, T, qw), BF16),
        compiler_params=_params(("arbitrary", "arbitrary")),
        name="swa",
    )(sinks, proj, proj, proj, proj, proj, bias, jnp.tile(qg, (1, SWA_HEADS)),
      jnp.tile(kg, (1, SWA_KV_HEADS)), seg, seg.T)


def _merge_kernel(x_ref, ga_ref, gb_ref, go_ref, so_ref, wa_ref, wb_ref, wo_ref, g1_ref,
                  n2_ref, sc2_ref, sh2_ref, we_ref, x1_ref, h2_ref, st_ref):
    ya = jnp.dot(go_ref[...], wa_ref[...], preferred_element_type=F32)
    yb = jnp.dot(so_ref[...], wb_ref[...], preferred_element_type=F32)
    m = jax.nn.sigmoid(ga_ref[...].astype(F32)) * ya + jax.nn.sigmoid(gb_ref[...].astype(F32)) * yb
    mixed = jnp.dot(m.astype(BF16), wo_ref[...], preferred_element_type=F32)
    x1 = x_ref[...] + g1_ref[0] * mixed
    x1_ref[...] = x1
    y = x1 * lax.rsqrt(jnp.mean(x1 * x1, axis=-1, keepdims=True) + EPS) * n2_ref[...]
    h2 = y * (1.0 + sc2_ref[0]) + sh2_ref[0]
    h2_ref[...] = h2
    st_ref[...] = lax.dot_general(we_ref[...], h2.astype(BF16), NT_DIMS, preferred_element_type=F32)


def _merge(xf, proj2, gla_o, swa_o, wa, wb, wo, g1, n2, sc2, sh2, weff_t, T, tm=256):
    N, D = xf.shape
    B = g1.shape[0]
    SW = weff_t.shape[0]
    bat = lambda i: ((i * tm) // T, 0, 0)
    full = lambda i: (0, 0)
    return pl.pallas_call(
        _merge_kernel,
        grid=(N // tm,),
        in_specs=[
            pl.BlockSpec((tm, D), lambda i: (i, 0)),
            pl.BlockSpec((tm, D), lambda i: (i, COL_GA // D)),
            pl.BlockSpec((tm, D), lambda i: (i, COL_GB // D)),
            pl.BlockSpec((tm, D), lambda i: (i, 0)),
            pl.BlockSpec((tm, D), lambda i: (i, 0)),
            pl.BlockSpec((D, D), full),
            pl.BlockSpec((D, D), full),
            pl.BlockSpec((D, D), full),
            pl.BlockSpec((1, 1, D), bat),
            pl.BlockSpec((1, D), full),
            pl.BlockSpec((1, 1, D), bat),
            pl.BlockSpec((1, 1, D), bat),
            pl.BlockSpec((SW, D), full),
        ],
        out_specs=[
            pl.BlockSpec((tm, D), lambda i: (i, 0)),
            pl.BlockSpec((tm, D), lambda i: (i, 0)),
            pl.BlockSpec((SW, tm), lambda i: (0, i)),
        ],
        out_shape=[
            jax.ShapeDtypeStruct((N, D), F32),
            jax.ShapeDtypeStruct((N, D), F32),
            jax.ShapeDtypeStruct((SW, N), F32),
        ],
        compiler_params=_params(("arbitrary",), VMEM_LIMIT_BYTES),
        name="merge",
    )(xf, proj2, proj2, gla_o, swa_o, wa, wb, wo, g1.reshape(B, 1, D), n2.reshape(1, D),
      sc2.reshape(B, 1, D), sh2.reshape(B, 1, D), weff_t)


def _tree(op, xs):
    xs = list(xs)
    while len(xs) > 1:
        xs = [op(xs[i], xs[i + 1]) for i in range(0, len(xs) - 1, 2)] + ([xs[-1]] if len(xs) % 2 else [])
    return xs[0]


EXTRACT_CHAINS = 4


def _extract_best(ref, n):
    lanes = min(EXTRACT_CHAINS, n)
    best = [ref[i] for i in range(lanes)]
    for i in range(lanes, n):
        best[i % lanes] = jnp.maximum(best[i % lanes], ref[i])
    m = _tree(jnp.maximum, best)
    first = [jnp.where(ref[i] == m, i, n) for i in range(lanes)]
    for i in range(lanes, n):
        first[i % lanes] = jnp.minimum(first[i % lanes], jnp.where(ref[i] == m, i, n))
    pos = _tree(jnp.minimum, first)
    for i in range(n):
        ref[i] = jnp.where(pos == i, -jnp.inf, ref[i])
    return m, pos


def _ordered(a, b):
    (va, ia), (vb, ib) = a, b
    keep = jnp.logical_or(va > vb, jnp.logical_and(va == vb, ia < ib))
    first = (jnp.where(keep, va, vb), jnp.where(keep, ia, ib))
    second = (jnp.where(keep, vb, va), jnp.where(keep, ib, ia))
    return first, second


def _bitonic_merge(xs):
    xs = list(xs)
    j = len(xs) // 2
    while j >= 1:
        for i in range(len(xs)):
            if i & j == 0:
                xs[i], xs[i | j] = _ordered(xs[i], xs[i | j])
        j //= 2
    return xs


def _bitonic_sort(xs):
    xs = list(xs)
    k = 2
    while k <= len(xs):
        j = k // 2
        while j >= 1:
            for i in range(len(xs)):
                if i & j == 0:
                    first, second = _ordered(xs[i], xs[i | j])
                    xs[i], xs[i | j] = (first, second) if i & k == 0 else (second, first)
            j //= 2
        k *= 2
    return xs


def _merge_best(xs, ys):
    n = len(xs)
    return _bitonic_merge([_ordered(xs[i], ys[n - 1 - i])[0] for i in range(n)])


TOPK_PAIRS = tuple((a, b) for a in range(PEER_TOPK) for b in range(PEER_TOPK)
                   if (a + 1) * (b + 1) <= PEER_TOPK)


def _topk_kernel(s_ref, e_ref, g_ref, sv_ref, si_ref, v_ref, i_ref, cand_ref, cidx_ref, best_ref, row_ref):
    K = PEER_TOPK
    nk = PEER_NKEYS
    H = PEER_HEADS
    G = 2 * H
    tm = s_ref.shape[1]
    nblocks = nk // K
    for hb in range(2):
        rows = slice(hb * H, (hb + 1) * H)
        for blk in range(nblocks):
            keys = [(s_ref[(blk * K + i) * G + hb * H:(blk * K + i) * G + (hb + 1) * H, :],
                     jnp.full((H, tm), blk * K + i, I32)) for i in range(K)]
            for i, (v, ix) in enumerate(_bitonic_sort(keys)):
                sv_ref[blk, i, rows, :] = v
                si_ref[blk, i, rows, :] = ix
        step = 1
        while step < nblocks:
            for blk in range(0, nblocks, 2 * step):
                xs = [(sv_ref[blk, i, rows, :], si_ref[blk, i, rows, :]) for i in range(K)]
                ys = [(sv_ref[blk + step, i, rows, :], si_ref[blk + step, i, rows, :]) for i in range(K)]
                for i, (v, ix) in enumerate(_merge_best(xs, ys)):
                    sv_ref[blk, i, rows, :] = v
                    si_ref[blk, i, rows, :] = ix
            step *= 2
    v_ref[...] = sv_ref[0]
    i_ref[...] = si_ref[0]

    for ci, (a, b) in enumerate(TOPK_PAIRS):
        cand_ref[ci] = v_ref[a, 0:H, :] + v_ref[b, H:2 * H, :]
        cidx_ref[ci] = (i_ref[a, 0:H, :] * nk + i_ref[b, H:2 * H, :]) * WORDS_PER_EXPERT
    ncand = len(TOPK_PAIRS)

    def stage2(k, _):
        m, pos = _extract_best(cand_ref, ncand)
        row = _tree(jnp.maximum, [jnp.where(pos == ci, cidx_ref[ci], -1) for ci in range(ncand)])
        best_ref[k] = m
        row_ref[k] = row
        return 0

    lax.fori_loop(0, K, stage2, 0)
    best = best_ref[...]
    ex = jnp.exp(best - best[0:1])
    gates = ex / jnp.sum(ex, axis=0, keepdims=True)
    g_ref[...] = gates.reshape(K * H, tm).T
    e_ref[...] = row_ref[...].reshape(K * H, tm).astype(F32).T.astype(I32)


def _topk(s_t, tm=128):
    SW, N = s_t.shape
    K = PEER_TOPK
    G = 2 * PEER_HEADS
    return pl.pallas_call(
        _topk_kernel,
        grid=(N // tm,),
        in_specs=[pl.BlockSpec((SW, tm), lambda i: (0, i))],
        out_specs=[
            pl.BlockSpec((tm, PEER_SLOTS), lambda i: (i, 0)),
            pl.BlockSpec((tm, PEER_SLOTS), lambda i: (i, 0)),
        ],
        out_shape=[
            jax.ShapeDtypeStruct((N, PEER_SLOTS), I32),
            jax.ShapeDtypeStruct((N, PEER_SLOTS), F32),
        ],
        scratch_shapes=[pltpu.VMEM((PEER_NKEYS // K, K, G, tm), F32),
                        pltpu.VMEM((PEER_NKEYS // K, K, G, tm), I32),
                        pltpu.VMEM((K, G, tm), F32),
                        pltpu.VMEM((K, G, tm), I32),
                        pltpu.VMEM((len(TOPK_PAIRS), PEER_HEADS, tm), F32),
                        pltpu.VMEM((len(TOPK_PAIRS), PEER_HEADS, tm), I32),
                        pltpu.VMEM((K, PEER_HEADS, tm), F32),
                        pltpu.VMEM((K, PEER_HEADS, tm), I32)],
        compiler_params=_params(("arbitrary",)),
        name="peer_topk",
    )(s_t)


WORDS_PER_EXPERT = SUBLANES // 2


TILE_BLOCK = tuple((r % 2) * WORDS_PER_EXPERT + r // 2 for r in range(SUBLANES))
BLOCK_ROW = tuple(TILE_BLOCK.index(q) for q in range(SUBLANES))


def _bf16_bits(x):
    return lax.bitcast_convert_type(x.astype(BF16).astype(F32), jnp.uint32)


def _table_kernel(t_ref, o_ref):
    rows = t_ref.shape[0]
    for s in range(WORDS_PER_EXPERT):
        lo = _bf16_bits(t_ref[:, s * LANES:(s + 1) * LANES])
        hi = _bf16_bits(t_ref[:, (s + WORDS_PER_EXPERT) * LANES:(s + WORDS_PER_EXPERT + 1) * LANES])
        word = lax.shift_right_logical(lo, jnp.uint32(16)) | (hi & jnp.uint32(0xFFFF0000))
        o_ref[pl.ds(s, rows, stride=WORDS_PER_EXPERT), :] = lax.bitcast_convert_type(word, I32)


def _table(tables, layer, te=2048):
    _, n, d = tables.shape
    return pl.pallas_call(
        _table_kernel,
        grid=(n // te,),
        in_specs=[pl.BlockSpec((None, te, d), lambda i: (layer, i, 0))],
        out_specs=pl.BlockSpec((te * WORDS_PER_EXPERT, LANES), lambda i: (i, 0)),
        out_shape=jax.ShapeDtypeStruct((n * WORDS_PER_EXPERT, LANES), I32),
        compiler_params=_params(("arbitrary",)),
        name="peer_table",
    )(tables)


def _expert_tile(tab_ref, e4):
    words = tab_ref[pl.ds(pl.multiple_of(e4, WORDS_PER_EXPERT), WORDS_PER_EXPERT), :]
    return pltpu.bitcast(words, BF16).astype(F32)


def _to_tile_rows(x):
    return jnp.concatenate([x[q:q + 1, :] for q in TILE_BLOCK], axis=0)


def _from_tile_rows(x):
    return jnp.concatenate([x[r:r + 1, :] for r in BLOCK_ROW], axis=0)


def _gelu_tanh(x):
    return 0.5 * x * (1.0 + jnp.tanh(np.sqrt(2.0 / np.pi) * (x + 0.044715 * (x * x * x))))


RING_SLOTS = 2
SLOT_TOKENS = 32


def _for_each_token(e_ref, idx_ref, sem, tb, token_body):
    ngroups = tb // SLOT_TOKENS

    def fetch(q, slot):
        return pltpu.make_async_copy(e_ref.at[q], idx_ref.at[slot], sem.at[slot])

    for slot in range(RING_SLOTS):
        fetch(slot, slot).start()

    def ring(j, _):
        for slot in range(RING_SLOTS):
            q = j * RING_SLOTS + slot
            fetch(q, slot).wait()
            for tt in range(SLOT_TOKENS):
                token_body(q * SLOT_TOKENS + tt, lambda k, slot=slot, tt=tt: idx_ref[slot, tt, k])
            fetch(jnp.minimum(q + RING_SLOTS, ngroups - 1), slot).start()
        return 0

    lax.fori_loop(0, ngroups // RING_SLOTS, ring, 0)
    for slot in range(RING_SLOTS):
        fetch(ngroups - 1, slot).wait()


PAIR_GROUP = LANES // SUBLANES
SLOT_GROUPS = PEER_SLOTS // PAIR_GROUP


def _slot_tiles(tab_ref, row, j):
    tiles = []
    for kk in range(PAIR_GROUP):
        src = pl.ds(pl.multiple_of(row(j * PAIR_GROUP + kk), WORDS_PER_EXPERT), WORDS_PER_EXPERT)
        tiles.append(pltpu.bitcast(tab_ref[src, :], BF16))
    return jnp.concatenate(tiles, axis=0)


def _spread_matrices():
    lane = np.arange(LANES)
    return np.stack([lane[:, None] == PAIR_GROUP * j + lane[None, :] // SUBLANES
                     for j in range(SLOT_GROUPS)])


def _peer_u_kernel(e_ref, h_ref, g_ref, tab_ref, sel_ref, o_ref, slab_ref, idx_ref, sem, *, tb):
    groups_per_token = PEER_SLOTS // SUBLANES
    sel = sel_ref[...]

    def token(t, row):
        hb = _to_tile_rows(h_ref[t]).astype(BF16)
        for j in range(groups_per_token):
            prods = []
            for qn in range(SUBLANES):
                src = pl.ds(pl.multiple_of(row(j * SUBLANES + qn), WORDS_PER_EXPERT), WORDS_PER_EXPERT)
                prods.append(pltpu.bitcast(tab_ref[src, :], BF16) * hb)
            prods = jnp.concatenate(prods, axis=0)
            dst = pl.multiple_of(t * PEER_SLOTS + j * SUBLANES, SUBLANES)
            slab_ref[pl.ds(dst, SUBLANES), :] = jnp.dot(sel, prods, preferred_element_type=F32)

    _for_each_token(e_ref, idx_ref, sem, tb, token)
    rio = lax.broadcasted_iota(I32, (LANES, LANES), 0)
    lio = lax.broadcasted_iota(I32, (LANES, LANES), 1)
    diag = (rio == lio)[None]
    rows = SUBLANES * PEER_SLOTS
    for gi in range(tb // SUBLANES):
        tok = slice(gi * SUBLANES, (gi + 1) * SUBLANES)
        rs = jnp.sum(slab_ref[gi * rows:(gi + 1) * rows, :], axis=-1, keepdims=True)
        rs = rs.reshape(SUBLANES, PEER_SLOTS, 1)
        a = jnp.sum(jnp.where(diag, rs, 0.0), axis=1)
        o_ref[tok, :] = g_ref[tok, :] * _gelu_tanh(a)


def _peer_u(e4, h3, gates, tab, tb=128):
    N = h3.shape[0]
    assert tb % (RING_SLOTS * SLOT_TOKENS) == 0 and tb % SUBLANES == 0
    sel = np.arange(SUBLANES)[:, None] == np.arange(SUBLANES * SUBLANES)[None, :] // SUBLANES
    return pl.pallas_call(
        functools.partial(_peer_u_kernel, tb=tb),
        grid=(N // tb,),
        in_specs=[
            pl.BlockSpec((tb // SLOT_TOKENS, SLOT_TOKENS, PEER_SLOTS), lambda i: (i, 0, 0)),
            pl.BlockSpec((tb, SUBLANES, LANES), lambda i: (i, 0, 0)),
            pl.BlockSpec((tb, PEER_SLOTS), lambda i: (i, 0)),
            pl.BlockSpec(tab.shape, lambda i: (0, 0), pipeline_mode=pl.Buffered(1)),
            pl.BlockSpec(sel.shape, lambda i: (0, 0)),
        ],
        out_specs=pl.BlockSpec((tb, PEER_SLOTS), lambda i: (i, 0)),
        out_shape=jax.ShapeDtypeStruct((N, PEER_SLOTS), F32),
        scratch_shapes=[pltpu.VMEM((tb * PEER_SLOTS, LANES), F32),
                        pltpu.SMEM((RING_SLOTS, SLOT_TOKENS, PEER_SLOTS), I32),
                        pltpu.SemaphoreType.DMA((RING_SLOTS,))],
        compiler_params=_params(("arbitrary",), VMEM_LIMIT_BYTES),
        name="peer_u",
    )(e4.reshape(N // SLOT_TOKENS, SLOT_TOKENS, PEER_SLOTS), h3, gates, tab, jnp.asarray(sel, BF16))


def _peer_v_kernel(e_ref, c_ref, x1_ref, g2_ref, tab_ref, spread_ref, o_ref, hi_ref, lo_ref, idx_ref, sem, *,
                   tb):
    c = c_ref[...]
    c_hi = c.astype(BF16)
    c_lo = (c - c_hi.astype(F32)).astype(BF16)
    for j in range(SLOT_GROUPS):
        rows = pl.ds(j, tb, stride=SLOT_GROUPS)
        hi_ref[rows, :] = jnp.dot(c_hi, spread_ref[j], preferred_element_type=F32)
        lo_ref[rows, :] = jnp.dot(c_lo, spread_ref[j], preferred_element_type=F32)
    rio = lax.broadcasted_iota(I32, (SUBLANES, LANES), 0)
    lio = lax.broadcasted_iota(I32, (SUBLANES, LANES), 1)
    own_row = (lio & (SUBLANES - 1)) == rio

    def token(t, row):
        base = pl.multiple_of(t * SLOT_GROUPS, SLOT_GROUPS)
        hi = hi_ref[pl.ds(base, SLOT_GROUPS), :]
        lo = lo_ref[pl.ds(base, SLOT_GROUPS), :]
        acc = jnp.zeros((SUBLANES, LANES), F32)
        for j in range(SLOT_GROUPS):
            lhs = jnp.concatenate(
                [jnp.where(own_row, jnp.broadcast_to(part[j:j + 1, :], (SUBLANES, LANES)), 0.0).astype(BF16)
                 for part in (hi, lo)], axis=0)
            out = jnp.dot(lhs, _slot_tiles(tab_ref, row, j), preferred_element_type=F32)
            acc = acc + (out[:SUBLANES] + out[SUBLANES:])
        o_ref[t] = x1_ref[t] + g2_ref[0] * _from_tile_rows(acc)

    _for_each_token(e_ref, idx_ref, sem, tb, token)


def _peer_v(e4, coef, x1_3, g2_3, tab, T, tb=128):
    N = x1_3.shape[0]
    assert tb % (RING_SLOTS * SLOT_TOKENS) == 0 and tb % SUBLANES == 0
    return pl.pallas_call(
        functools.partial(_peer_v_kernel, tb=tb),
        grid=(N // tb,),
        in_specs=[
            pl.BlockSpec((tb // SLOT_TOKENS, SLOT_TOKENS, PEER_SLOTS), lambda i: (i, 0, 0)),
            pl.BlockSpec((tb, PEER_SLOTS), lambda i: (i, 0)),
            pl.BlockSpec((tb, SUBLANES, LANES), lambda i: (i, 0, 0)),
            pl.BlockSpec((1, SUBLANES, LANES), lambda i: ((i * tb) // T, 0, 0)),
            pl.BlockSpec(tab.shape, lambda i: (0, 0), pipeline_mode=pl.Buffered(1)),
            pl.BlockSpec((SLOT_GROUPS, LANES, LANES), lambda i: (0, 0, 0)),
        ],
        out_specs=pl.BlockSpec((tb, SUBLANES, LANES), lambda i: (i, 0, 0)),
        out_shape=jax.ShapeDtypeStruct((N, SUBLANES, LANES), F32),
        scratch_shapes=[pltpu.VMEM((tb * SLOT_GROUPS, LANES), F32),
                        pltpu.VMEM((tb * SLOT_GROUPS, LANES), F32),
                        pltpu.SMEM((RING_SLOTS, SLOT_TOKENS, PEER_SLOTS), I32),
                        pltpu.SemaphoreType.DMA((RING_SLOTS,))],
        compiler_params=_params(("arbitrary",), VMEM_LIMIT_BYTES),
        name="peer_v",
    )(e4.reshape(N // SLOT_TOKENS, SLOT_TOKENS, PEER_SLOTS), coef, x1_3, g2_3, tab,
      jnp.asarray(_spread_matrices(), BF16))


def kernel(x, c, w_ada, b_ada, norm1_g, w_in, gla_gate_w2, gla_gate_b, gla_norm_g, swa_qnorm_g,
           swa_knorm_g, swa_sinks, rel_bias, w_up_a, w_up_b, w_out, norm2_g, peer_wq, peer_subkeys,
           peer_u, peer_v):
    B, T, D = x.shape
    N = B * T
    L = w_ada.shape[0]
    mod = _adaln(c, w_ada, b_ada)
    weff_t = _fold_peer_keys(peer_wq, peer_subkeys)
    bias = _swa_bias(rel_bias)
    xf = x.reshape(N, D)
    for l in range(L):
        sh1, sc1, g1, sh2, sc2, g2 = [mod[l, :, i * D:(i + 1) * D] for i in range(6)]
        proj = _in_proj(xf, norm1_g[l], sc1, sh1, _pack_w_in(w_in, l), T)
        proj3 = proj.reshape(B, T, PROJ_W)
        w2p = jnp.zeros((LANES, GLA_HEADS * GLA_DK), BF16).at[:GLA_RANK].set(gla_gate_w2[l].astype(BF16))
        gla_o = _gla(proj3, w2p, gla_gate_b[l].reshape(1, -1), gla_norm_g[l].reshape(1, -1))
        swa_o = _swa(proj3, bias, swa_qnorm_g[l].reshape(1, -1), swa_knorm_g[l].reshape(1, -1),
                     swa_sinks[l])
        x1, h2, s_t = _merge(xf, proj, gla_o.reshape(N, -1), swa_o.reshape(N, -1),
                             w_up_a[l].astype(BF16), w_up_b[l].astype(BF16), w_out[l].astype(BF16),
                             g1, norm2_g[l], sc2, sh2, weff_t[l], T)
        e4, gates = _topk(s_t)
        coef = _peer_u(e4, h2.reshape(N, SUBLANES, LANES), gates, _table(peer_u, l))
        g2_3 = g2.reshape(B, SUBLANES, LANES)
        xf = _peer_v(e4, coef, x1.reshape(N, SUBLANES, LANES), g2_3, _table(peer_v, l), T).reshape(N, D)
    return xf.reshape(B, T, D)
```

```python
import functools

import numpy as np
import jax
import jax.numpy as jnp
from jax import lax
from jax.experimental import pallas as pl
from jax.experimental.pallas import tpu as pltpu

F32 = jnp.float32
BF16 = jnp.bfloat16
I32 = jnp.int32
HIGHEST = lax.Precision.HIGHEST
EPS = 1e-6

GLA_HEADS = 4
GLA_DK = 128
GLA_DV = 256
GLA_RANK = 16
GLA_TAU = 16.0
GLA_CHUNK = 64
SWA_HEADS = 16
SWA_KV_HEADS = 2
SWA_HD = 64
SWA_BLOCK = 128
N_BUCKETS = 32
MAX_DISTANCE = 128
PEER_HEADS = 8
PEER_NKEYS = 128
PEER_TOPK = 16
PEER_SLOTS = PEER_HEADS * PEER_TOPK

SUBLANES = 8
LANES = 128
VMEM_LIMIT_BYTES = 56 * 1024 * 1024

NT_DIMS = (((1,), (1,)), ((), ()))
TN_DIMS = (((0,), (0,)), ((), ()))

COL_Q, COL_K, COL_V, COL_GR, COL_SQ, COL_GA, COL_GB = 0, 512, 1024, 2048, 3072, 4096, 5120
COL_SK, COL_SV, COL_GLR = 6144, 6272, 6400
PROJ_W = 6528


def _params(sem, vmem=None):
    return pltpu.CompilerParams(dimension_semantics=sem, vmem_limit_bytes=vmem)


def _adaln_kernel(c_ref, w_ref, b_ref, o_ref):
    c = c_ref[...]
    a = c * jax.nn.sigmoid(c)
    o_ref[0] = jnp.dot(a, w_ref[0], preferred_element_type=F32, precision=HIGHEST) + b_ref[0]


def _adaln(c, w_ada, b_ada):
    L, D, W = w_ada.shape
    B = c.shape[0]
    rows = -(-B // SUBLANES) * SUBLANES
    cp = jnp.zeros((rows, D), F32).at[:B].set(c)
    tn = W // 4
    out = pl.pallas_call(
        _adaln_kernel,
        grid=(L, W // tn),
        in_specs=[
            pl.BlockSpec((rows, D), lambda l, j: (0, 0)),
            pl.BlockSpec((1, D, tn), lambda l, j: (l, 0, j)),
            pl.BlockSpec((1, 1, tn), lambda l, j: (l, 0, j)),
        ],
        out_specs=pl.BlockSpec((1, rows, tn), lambda l, j: (l, 0, j)),
        out_shape=jax.ShapeDtypeStruct((L, rows, W), F32),
        compiler_params=_params(("arbitrary", "arbitrary")),
        name="adaln",
    )(cp, w_ada, b_ada.reshape(L, 1, W))
    return out[:, :B]


def _fold_kernel(sk_ref, wq_ref, o_ref, rows_ref):
    half = sk_ref.shape[-1]
    groups = 2 * PEER_HEADS
    for h in range(PEER_HEADS):
        for p in range(2):
            g = 2 * h + p
            res = lax.dot_general(sk_ref[0, p], wq_ref[0, :, g * half:(g + 1) * half], NT_DIMS,
                                  precision=HIGHEST, preferred_element_type=F32)
            for cb in range(rows_ref.shape[0]):
                rows_ref[cb, pl.ds(p * PEER_HEADS + h, PEER_NKEYS, stride=groups), :] = (
                    res[:, cb * LANES:(cb + 1) * LANES])
    for cb in range(rows_ref.shape[0]):
        o_ref[0, :, cb * LANES:(cb + 1) * LANES] = rows_ref[cb].astype(BF16)


def _fold_peer_keys(peer_wq, peer_subkeys):
    L, D, QW = peer_wq.shape
    half = peer_subkeys.shape[-1]
    rows = (QW // half) * PEER_NKEYS
    return pl.pallas_call(
        _fold_kernel,
        grid=(L,),
        in_specs=[
            pl.BlockSpec((1, 2, PEER_NKEYS, half), lambda l: (l, 0, 0, 0)),
            pl.BlockSpec((1, D, QW), lambda l: (l, 0, 0)),
        ],
        out_specs=pl.BlockSpec((1, rows, D), lambda l: (l, 0, 0)),
        out_shape=jax.ShapeDtypeStruct((L, rows, D), BF16),
        scratch_shapes=[pltpu.VMEM((D // LANES, rows, LANES), F32)],
        compiler_params=_params(("arbitrary",), VMEM_LIMIT_BYTES),
        name="peer_fold",
    )(peer_subkeys, peer_wq)


def _inproj_kernel(x_ref, g_ref, sc_ref, sh_ref, w_ref, o_ref):
    x = x_ref[...]
    ms = jnp.mean(x * x, axis=-1, keepdims=True)
    y = x * lax.rsqrt(ms + EPS) * g_ref[...]
    h = y * (1.0 + sc_ref[0]) + sh_ref[0]
    o_ref[...] = jnp.dot(h.astype(BF16), w_ref[...], preferred_element_type=F32).astype(BF16)


def _in_proj(xf, g, sc, sh, wp, T, tm=512):
    N, D = xf.shape
    B = sc.shape[0]
    ncol = 3
    tn = PROJ_W // ncol
    return pl.pallas_call(
        _inproj_kernel,
        grid=(ncol, N // tm),
        in_specs=[
            pl.BlockSpec((tm, D), lambda j, i: (i, 0)),
            pl.BlockSpec((1, D), lambda j, i: (0, 0)),
            pl.BlockSpec((1, 1, D), lambda j, i: ((i * tm) // T, 0, 0)),
            pl.BlockSpec((1, 1, D), lambda j, i: ((i * tm) // T, 0, 0)),
            pl.BlockSpec((D, tn), lambda j, i: (0, j)),
        ],
        out_specs=pl.BlockSpec((tm, tn), lambda j, i: (i, j)),
        out_shape=jax.ShapeDtypeStruct((N, PROJ_W), BF16),
        compiler_params=_params(("arbitrary", "arbitrary")),
        name="in_proj",
    )(xf, g.reshape(1, D), sc.reshape(B, 1, D), sh.reshape(B, 1, D), wp)


_SRC = dict(zip(("q", "k", "v", "glr", "gr", "sq", "sk", "sv", "ga", "gb"),
                np.cumsum([0, 512, 512, 1024, GLA_RANK, 1024, 1024, 128, 128, 1024])))
W_IN_SLABS = ((COL_Q, _SRC["q"], 512), (COL_K, _SRC["k"], 512), (COL_V, _SRC["v"], 1024),
              (COL_GR, _SRC["gr"], 1024), (COL_SQ, _SRC["sq"], 1024), (COL_GA, _SRC["ga"], 1024),
              (COL_GB, _SRC["gb"], 1024), (COL_SK, _SRC["sk"], 128), (COL_SV, _SRC["sv"], 128),
              (COL_GLR, _SRC["glr"], GLA_RANK))


def _pack_w_in_kernel(w_ref, o_ref):
    o_ref[:, COL_GLR:] = jnp.zeros((o_ref.shape[0], PROJ_W - COL_GLR), BF16)
    for dst, src, width in W_IN_SLABS:
        o_ref[:, dst:dst + width] = w_ref[:, int(src):int(src) + width].astype(BF16)


def _pack_w_in(w_in, layer, tr=128):
    _, rows, cols = w_in.shape
    return pl.pallas_call(
        _pack_w_in_kernel,
        grid=(rows // tr,),
        in_specs=[pl.BlockSpec((None, tr, cols), lambda i: (layer, i, 0))],
        out_specs=pl.BlockSpec((tr, PROJ_W), lambda i: (i, 0)),
        out_shape=jax.ShapeDtypeStruct((rows, PROJ_W), BF16),
        compiler_params=_params(("arbitrary",)),
        name="pack_w_in",
    )(w_in)


def _gla_kernel(q_ref, k_ref, v_ref, r_ref, glr_ref, w2_ref, b2_ref, ng_ref, o_ref, st_ref, *, nchunk):
    @pl.when(pl.program_id(1) == 0)
    def _():
        st_ref[...] = jnp.zeros_like(st_ref)

    C = GLA_CHUNK
    dk, dv = GLA_DK, GLA_DV
    row = lax.broadcasted_iota(I32, (C, C), 0)
    col = lax.broadcasted_iota(I32, (C, C), 1)
    tri = col <= row
    tri_f = tri.astype(F32)
    w2 = w2_ref[...]
    b2 = b2_ref[...]
    ng = ng_ref[...]
    for ci in range(nchunk):
        sl = pl.ds(ci * C, C)
        z = jnp.dot(glr_ref[sl, :], w2, preferred_element_type=F32) + b2
        log_a = (jnp.minimum(z, 0.0) - jnp.log(1.0 + jnp.exp(-jnp.abs(z)))) * (1.0 / GLA_TAU)
        b = jnp.dot(tri_f, log_a, preferred_element_type=F32, precision=HIGHEST)
        b_last = b[C - 1:C, :]
        q = q_ref[sl, :].astype(F32) * (dk ** -0.5)
        k = k_ref[sl, :].astype(F32)
        q_dec = (q * jnp.exp(b)).astype(BF16)
        k_inv = (k * jnp.exp(-b)).astype(BF16)
        k_tail = (k * jnp.exp(b_last - b)).astype(BF16)
        decay = jnp.exp(b_last)
        for h in range(GLA_HEADS):
            kc = slice(h * dk, (h + 1) * dk)
            vc = slice(h * dv, (h + 1) * dv)
            v = v_ref[sl, vc]
            attn = lax.dot_general(q_dec[:, kc], k_inv[:, kc], NT_DIMS, preferred_element_type=F32)
            attn = jnp.where(tri, attn, 0.0).astype(BF16)
            st = st_ref[h]
            o = (jnp.dot(attn, v, preferred_element_type=F32)
                 + lax.dot_general(q_dec[:, kc], st.astype(BF16), NT_DIMS, preferred_element_type=F32))
            st_ref[h] = st * decay[:, kc] + lax.dot_general(v, k_tail[:, kc], TN_DIMS,
                                                            preferred_element_type=F32)
            on = o * lax.rsqrt(jnp.mean(o * o, axis=-1, keepdims=True) + EPS) * ng
            r = r_ref[sl, vc].astype(F32)
            o_ref[sl, vc] = (on * (r * jax.nn.sigmoid(r))).astype(BF16)


def _gla(proj, w2p, b2, ng, tc=256):
    B, T, _ = proj.shape
    H = GLA_HEADS
    kw, vw = H * GLA_DK, H * GLA_DV
    return pl.pallas_call(
        functools.partial(_gla_kernel, nchunk=tc // GLA_CHUNK),
        grid=(B, T // tc),
        in_specs=[
            pl.BlockSpec((None, tc, kw), lambda b, c: (b, c, COL_Q // kw)),
            pl.BlockSpec((None, tc, kw), lambda b, c: (b, c, COL_K // kw)),
            pl.BlockSpec((None, tc, vw), lambda b, c: (b, c, COL_V // vw)),
            pl.BlockSpec((None, tc, vw), lambda b, c: (b, c, COL_GR // vw)),
            pl.BlockSpec((None, tc, LANES), lambda b, c: (b, c, COL_GLR // LANES)),
            pl.BlockSpec((LANES, kw), lambda b, c: (0, 0)),
            pl.BlockSpec((1, kw), lambda b, c: (0, 0)),
            pl.BlockSpec((1, GLA_DV), lambda b, c: (0, 0)),
        ],
        out_specs=pl.BlockSpec((None, tc, vw), lambda b, c: (b, c, 0)),
        out_shape=jax.ShapeDtypeStruct((B, T, vw), BF16),
        scratch_shapes=[pltpu.VMEM((H, GLA_DV, GLA_DK), F32)],
        compiler_params=_params(("arbitrary", "arbitrary")),
        name="gla",
    )(proj, proj, proj, proj, proj, w2p, b2, ng)


def _t5_bucket(dist):
    max_exact = N_BUCKETS // 2
    d = np.maximum(dist, 1).astype(np.float32)
    large = max_exact + (np.log(d / max_exact) / np.log(MAX_DISTANCE / max_exact)
                         * (N_BUCKETS - max_exact)).astype(np.int32)
    large = np.minimum(large, N_BUCKETS - 1)
    return np.where(dist < max_exact, dist, large).astype(np.int32)


NEG_BIG = -1e30


def _swa_bias(rel_bias):
    blk = SWA_BLOCK
    qi = np.arange(blk)[:, None]
    sj = np.arange(2 * blk)[None, :]
    dist = blk + qi - sj
    band = (dist >= 0) & (dist < blk)
    bucket = _t5_bucket(np.clip(dist, 0, None))
    onehot = jnp.asarray(bucket.reshape(-1, 1) == np.arange(N_BUCKETS)[None, :], F32)
    bias = jnp.dot(onehot, rel_bias.astype(F32), precision=HIGHEST).T.reshape(-1, blk, 2 * blk)
    masks = np.stack([band, band & (sj >= blk)])[:, None]
    return jnp.where(jnp.asarray(masks), bias[None], NEG_BIG)


def _segment_sums(x, seg):
    hi = x.astype(BF16)
    lo = (x - hi.astype(F32)).astype(BF16)
    return (jnp.dot(hi, seg, preferred_element_type=F32) + jnp.dot(lo, seg, preferred_element_type=F32))


def _head_rms_scale(x, seg, seg_t):
    inv = lax.rsqrt(_segment_sums(x * x, seg) * (1.0 / SWA_HD) + EPS)
    return _segment_sums(inv, seg_t)


SWA_ROWS = 32


def _swa_kernel(sink_ref, q_ref, kp_ref, kc_ref, vp_ref, vc_ref, bias_ref, qg_ref, kg_ref, seg_ref,
                segt_ref, o_ref):
    blk = SWA_BLOCK
    hd = SWA_HD
    group = SWA_HEADS // SWA_KV_HEADS
    kvw = SWA_KV_HEADS * hd
    seg = seg_ref[...]
    seg_t = segt_ref[...]
    q = q_ref[...].astype(F32)
    qn = (q * _head_rms_scale(q, seg, seg_t) * qg_ref[...] * (hd ** -0.5)).astype(BF16)
    k2 = jnp.concatenate([kp_ref[...], kc_ref[...]], axis=0).astype(F32)
    kn = (k2 * _head_rms_scale(k2, seg[:kvw], seg_t[:, :kvw]) * kg_ref[...]).astype(BF16)
    v2 = jnp.concatenate([vp_ref[...], vc_ref[...]], axis=0)
    lane = lax.broadcasted_iota(I32, (2 * blk, kvw), 1)
    outs = []
    for kh in range(SWA_KV_HEADS):
        kk = kn[:, kh * hd:(kh + 1) * hd]
        vsh = v2 if kh == 0 else jnp.concatenate([v2[:, kh * hd:], v2[:, :kh * hd]], axis=1)
        vv = jnp.where(lane < hd, vsh, jnp.ones_like(vsh))
        qs = jnp.concatenate([qn[:, (kh * group + g) * hd:(kh * group + g + 1) * hd]
                              for g in range(group)], axis=0)
        logits = lax.dot_general(qs, kk, NT_DIMS, preferred_element_type=F32)
        logits = logits + bias_ref[kh * group:(kh + 1) * group].reshape(group * blk, 2 * blk)
        for g in range(group):
            sink = sink_ref[kh * group + g]
            parts = []
            for r0 in range(0, blk, SWA_ROWS):
                lg = logits[g * blk + r0:g * blk + r0 + SWA_ROWS]
                m = jnp.maximum(jnp.max(lg, axis=-1, keepdims=True), sink)
                p = jnp.exp(lg - m).astype(BF16)
                pv = jnp.dot(p, vv, preferred_element_type=F32)
                parts.append(pv[:, :hd] / (pv[:, hd:hd + 1] + jnp.exp(sink - m)))
            outs.append(jnp.concatenate(parts, axis=0))
    o_ref[...] = jnp.concatenate(outs, axis=-1).astype(BF16)


def _swa(proj, bias, qg, kg, sinks):
    B, T, _ = proj.shape
    blk = SWA_BLOCK
    qw = SWA_HEADS * SWA_HD
    kvw = SWA_KV_HEADS * SWA_HD
    prev = lambda b, i: jnp.maximum(i - 1, 0)
    seg = (np.arange(qw)[:, None] // SWA_HD == np.arange(LANES)[None, :])
    seg = jnp.asarray(seg, BF16)
    return pl.pallas_call(
        _swa_kernel,
        grid=(B, T // blk),
        in_specs=[
            pl.BlockSpec(memory_space=pltpu.SMEM),
            pl.BlockSpec((None, blk, qw), lambda b, i: (b, i, COL_SQ // qw)),
            pl.BlockSpec((None, blk, kvw), lambda b, i: (b, prev(b, i), COL_SK // kvw)),
            pl.BlockSpec((None, blk, kvw), lambda b, i: (b, i, COL_SK // kvw)),
            pl.BlockSpec((None, blk, kvw), lambda b, i: (b, prev(b, i), COL_SV // kvw)),
            pl.BlockSpec((None, blk, kvw), lambda b, i: (b, i, COL_SV // kvw)),
            pl.BlockSpec((None, SWA_HEADS, blk, 2 * blk), lambda b, i: (jnp.where(i == 0, 1, 0), 0, 0, 0)),
            pl.BlockSpec((1, qw), lambda b, i: (0, 0)),
            pl.BlockSpec((1, kvw), lambda b, i: (0, 0)),
            pl.BlockSpec((qw, LANES), lambda b, i: (0, 0)),
            pl.BlockSpec((LANES, qw), lambda b, i: (0, 0)),
        ],
        out_specs=pl.BlockSpec((None, blk, qw), lambda b, i: (b, i, 0)),
        out_shape=jax.ShapeDtypeStruct((B, T, qw), BF16),
        compiler_params=_params(("arbitrary", "arbitrary")),
        name="swa",
    )(sinks, proj, proj, proj, proj, proj, bias, jnp.tile(qg, (1, SWA_HEADS)),
      jnp.tile(kg, (1, SWA_KV_HEADS)), seg, seg.T)


def _merge_kernel(x_ref, ga_ref, gb_ref, go_ref, so_ref, wa_ref, wb_ref, wo_ref, g1_ref,
                  n2_ref, sc2_ref, sh2_ref, we_ref, x1_ref, h2_ref, st_ref):
    ya = jnp.dot(go_ref[...], wa_ref[...], preferred_element_type=F32)
    yb = jnp.dot(so_ref[...], wb_ref[...], preferred_element_type=F32)
    m = jax.nn.sigmoid(ga_ref[...].astype(F32)) * ya + jax.nn.sigmoid(gb_ref[...].astype(F32)) * yb
    mixed = jnp.dot(m.astype(BF16), wo_ref[...], preferred_element_type=F32)
    x1 = x_ref[...] + g1_ref[0] * mixed
    x1_ref[...] = x1
    y = x1 * lax.rsqrt(jnp.mean(x1 * x1, axis=-1, keepdims=True) + EPS) * n2_ref[...]
    h2 = y * (1.0 + sc2_ref[0]) + sh2_ref[0]
    h2_ref[...] = h2
    st_ref[...] = lax.dot_general(we_ref[...], h2.astype(BF16), NT_DIMS, preferred_element_type=F32)


def _merge(xf, proj2, gla_o, swa_o, wa, wb, wo, g1, n2, sc2, sh2, weff_t, T, tm=256):
    N, D = xf.shape
    B = g1.shape[0]
    SW = weff_t.shape[0]
    bat = lambda i: ((i * tm) // T, 0, 0)
    full = lambda i: (0, 0)
    return pl.pallas_call(
        _merge_kernel,
        grid=(N // tm,),
        in_specs=[
            pl.BlockSpec((tm, D), lambda i: (i, 0)),
            pl.BlockSpec((tm, D), lambda i: (i, COL_GA // D)),
            pl.BlockSpec((tm, D), lambda i: (i, COL_GB // D)),
            pl.BlockSpec((tm, D), lambda i: (i, 0)),
            pl.BlockSpec((tm, D), lambda i: (i, 0)),
            pl.BlockSpec((D, D), full),
            pl.BlockSpec((D, D), full),
            pl.BlockSpec((D, D), full),
            pl.BlockSpec((1, 1, D), bat),
            pl.BlockSpec((1, D), full),
            pl.BlockSpec((1, 1, D), bat),
            pl.BlockSpec((1, 1, D), bat),
            pl.BlockSpec((SW, D), full),
        ],
        out_specs=[
            pl.BlockSpec((tm, D), lambda i: (i, 0)),
            pl.BlockSpec((tm, D), lambda i: (i, 0)),
            pl.BlockSpec((SW, tm), lambda i: (0, i)),
        ],
        out_shape=[
            jax.ShapeDtypeStruct((N, D), F32),
            jax.ShapeDtypeStruct((N, D), F32),
            jax.ShapeDtypeStruct((SW, N), F32),
        ],
        compiler_params=_params(("arbitrary",), VMEM_LIMIT_BYTES),
        name="merge",
    )(xf, proj2, proj2, gla_o, swa_o, wa, wb, wo, g1.reshape(B, 1, D), n2.reshape(1, D),
      sc2.reshape(B, 1, D), sh2.reshape(B, 1, D), weff_t)


def _tree(op, xs):
    xs = list(xs)
    while len(xs) > 1:
        xs = [op(xs[i], xs[i + 1]) for i in range(0, len(xs) - 1, 2)] + ([xs[-1]] if len(xs) % 2 else [])
    return xs[0]


EXTRACT_CHAINS = 4


def _extract_best(ref, n):
    lanes = min(EXTRACT_CHAINS, n)
    best = [ref[i] for i in range(lanes)]
    for i in range(lanes, n):
        best[i % lanes] = jnp.maximum(best[i % lanes], ref[i])
    m = _tree(jnp.maximum, best)
    first = [jnp.where(ref[i] == m, i, n) for i in range(lanes)]
    for i in range(lanes, n):
        first[i % lanes] = jnp.minimum(first[i % lanes], jnp.where(ref[i] == m, i, n))
    pos = _tree(jnp.minimum, first)
    for i in range(n):
        ref[i] = jnp.where(pos == i, -jnp.inf, ref[i])
    return m, pos


def _ordered(a, b):
    (va, ia), (vb, ib) = a, b
    keep = jnp.logical_or(va > vb, jnp.logical_and(va == vb, ia < ib))
    first = (jnp.where(keep, va, vb), jnp.where(keep, ia, ib))
    second = (jnp.where(keep, vb, va), jnp.where(keep, ib, ia))
    return first, second


def _bitonic_merge(xs):
    xs = list(xs)
    j = len(xs) // 2
    while j >= 1:
        for i in range(len(xs)):
            if i & j == 0:
                xs[i], xs[i | j] = _ordered(xs[i], xs[i | j])
        j //= 2
    return xs


def _bitonic_sort(xs):
    xs = list(xs)
    k = 2
    while k <= len(xs):
        j = k // 2
        while j >= 1:
            for i in range(len(xs)):
                if i & j == 0:
                    first, second = _ordered(xs[i], xs[i | j])
                    xs[i], xs[i | j] = (first, second) if i & k == 0 else (second, first)
            j //= 2
        k *= 2
    return xs


def _merge_best(xs, ys):
    n = len(xs)
    return _bitonic_merge([_ordered(xs[i], ys[n - 1 - i])[0] for i in range(n)])


TOPK_PAIRS = tuple((a, b) for a in range(PEER_TOPK) for b in range(PEER_TOPK)
                   if (a + 1) * (b + 1) <= PEER_TOPK)


def _topk_kernel(s_ref, e_ref, g_ref, sv_ref, si_ref, v_ref, i_ref, cand_ref, cidx_ref, best_ref, row_ref):
    K = PEER_TOPK
    nk = PEER_NKEYS
    H = PEER_HEADS
    G = 2 * H
    tm = s_ref.shape[1]
    nblocks = nk // K
    for hb in range(2):
        rows = slice(hb * H, (hb + 1) * H)
        for blk in range(nblocks):
            keys = [(s_ref[(blk * K + i) * G + hb * H:(blk * K + i) * G + (hb + 1) * H, :],
                     jnp.full((H, tm), blk * K + i, I32)) for i in range(K)]
            for i, (v, ix) in enumerate(_bitonic_sort(keys)):
                sv_ref[blk, i, rows, :] = v
                si_ref[blk, i, rows, :] = ix
        step = 1
        while step < nblocks:
            for blk in range(0, nblocks, 2 * step):
                xs = [(sv_ref[blk, i, rows, :], si_ref[blk, i, rows, :]) for i in range(K)]
                ys = [(sv_ref[blk + step, i, rows, :], si_ref[blk + step, i, rows, :]) for i in range(K)]
                for i, (v, ix) in enumerate(_merge_best(xs, ys)):
                    sv_ref[blk, i, rows, :] = v
                    si_ref[blk, i, rows, :] = ix
            step *= 2
    v_ref[...] = sv_ref[0]
    i_ref[...] = si_ref[0]

    for ci, (a, b) in enumerate(TOPK_PAIRS):
        cand_ref[ci] = v_ref[a, 0:H, :] + v_ref[b, H:2 * H, :]
        cidx_ref[ci] = (i_ref[a, 0:H, :] * nk + i_ref[b, H:2 * H, :]) * WORDS_PER_EXPERT
    ncand = len(TOPK_PAIRS)

    def stage2(k, _):
        m, pos = _extract_best(cand_ref, ncand)
        row = _tree(jnp.maximum, [jnp.where(pos == ci, cidx_ref[ci], -1) for ci in range(ncand)])
        best_ref[k] = m
        row_ref[k] = row
        return 0

    lax.fori_loop(0, K, stage2, 0)
    best = best_ref[...]
    ex = jnp.exp(best - best[0:1])
    gates = ex / jnp.sum(ex, axis=0, keepdims=True)
    g_ref[...] = gates.reshape(K * H, tm).T
    e_ref[...] = row_ref[...].reshape(K * H, tm).astype(F32).T.astype(I32)


def _topk(s_t, tm=128):
    SW, N = s_t.shape
    K = PEER_TOPK
    G = 2 * PEER_HEADS
    return pl.pallas_call(
        _topk_kernel,
        grid=(N // tm,),
        in_specs=[pl.BlockSpec((SW, tm), lambda i: (0, i))],
        out_specs=[
            pl.BlockSpec((tm, PEER_SLOTS), lambda i: (i, 0)),
            pl.BlockSpec((tm, PEER_SLOTS), lambda i: (i, 0)),
        ],
        out_shape=[
            jax.ShapeDtypeStruct((N, PEER_SLOTS), I32),
            jax.ShapeDtypeStruct((N, PEER_SLOTS), F32),
        ],
        scratch_shapes=[pltpu.VMEM((PEER_NKEYS // K, K, G, tm), F32),
                        pltpu.VMEM((PEER_NKEYS // K, K, G, tm), I32),
                        pltpu.VMEM((K, G, tm), F32),
                        pltpu.VMEM((K, G, tm), I32),
                        pltpu.VMEM((len(TOPK_PAIRS), PEER_HEADS, tm), F32),
                        pltpu.VMEM((len(TOPK_PAIRS), PEER_HEADS, tm), I32),
                        pltpu.VMEM((K, PEER_HEADS, tm), F32),
                        pltpu.VMEM((K, PEER_HEADS, tm), I32)],
        compiler_params=_params(("arbitrary",)),
        name="peer_topk",
    )(s_t)


WORDS_PER_EXPERT = SUBLANES // 2


TILE_BLOCK = tuple((r % 2) * WORDS_PER_EXPERT + r // 2 for r in range(SUBLANES))
BLOCK_ROW = tuple(TILE_BLOCK.index(q) for q in range(SUBLANES))


def _bf16_bits(x):
    return lax.bitcast_convert_type(x.astype(BF16).astype(F32), jnp.uint32)


def _table_kernel(t_ref, o_ref):
    rows = t_ref.shape[0]
    for s in range(WORDS_PER_EXPERT):
        lo = _bf16_bits(t_ref[:, s * LANES:(s + 1) * LANES])
        hi = _bf16_bits(t_ref[:, (s + WORDS_PER_EXPERT) * LANES:(s + WORDS_PER_EXPERT + 1) * LANES])
        word = lax.shift_right_logical(lo, jnp.uint32(16)) | (hi & jnp.uint32(0xFFFF0000))
        o_ref[pl.ds(s, rows, stride=WORDS_PER_EXPERT), :] = lax.bitcast_convert_type(word, I32)


def _table(tables, layer, te=2048):
    _, n, d = tables.shape
    return pl.pallas_call(
        _table_kernel,
        grid=(n // te,),
        in_specs=[pl.BlockSpec((None, te, d), lambda i: (layer, i, 0))],
        out_specs=pl.BlockSpec((te * WORDS_PER_EXPERT, LANES), lambda i: (i, 0)),
        out_shape=jax.ShapeDtypeStruct((n * WORDS_PER_EXPERT, LANES), I32),
        compiler_params=_params(("arbitrary",)),
        name="peer_table",
    )(tables)


def _expert_tile(tab_ref, e4):
    words = tab_ref[pl.ds(pl.multiple_of(e4, WORDS_PER_EXPERT), WORDS_PER_EXPERT), :]
    return pltpu.bitcast(words, BF16).astype(F32)


def _to_tile_rows(x):
    return jnp.concatenate([x[q:q + 1, :] for q in TILE_BLOCK], axis=0)


def _from_tile_rows(x):
    return jnp.concatenate([x[r:r + 1, :] for r in BLOCK_ROW], axis=0)


def _gelu_tanh(x):
    return 0.5 * x * (1.0 + jnp.tanh(np.sqrt(2.0 / np.pi) * (x + 0.044715 * (x * x * x))))


RING_SLOTS = 2
SLOT_TOKENS = 32


def _for_each_token(e_ref, idx_ref, sem, tb, token_body):
    ngroups = tb // SLOT_TOKENS

    def fetch(q, slot):
        return pltpu.make_async_copy(e_ref.at[q], idx_ref.at[slot], sem.at[slot])

    for slot in range(RING_SLOTS):
        fetch(slot, slot).start()

    def ring(j, _):
        for slot in range(RING_SLOTS):
            q = j * RING_SLOTS + slot
            fetch(q, slot).wait()
            for tt in range(SLOT_TOKENS):
                token_body(q * SLOT_TOKENS + tt, lambda k, slot=slot, tt=tt: idx_ref[slot, tt, k])
            fetch(jnp.minimum(q + RING_SLOTS, ngroups - 1), slot).start()
        return 0

    lax.fori_loop(0, ngroups // RING_SLOTS, ring, 0)
    for slot in range(RING_SLOTS):
        fetch(ngroups - 1, slot).wait()


PAIR_GROUP = LANES // SUBLANES
SLOT_GROUPS = PEER_SLOTS // PAIR_GROUP


def _slot_tiles(tab_ref, row, j):
    tiles = []
    for kk in range(PAIR_GROUP):
        src = pl.ds(pl.multiple_of(row(j * PAIR_GROUP + kk), WORDS_PER_EXPERT), WORDS_PER_EXPERT)
        tiles.append(pltpu.bitcast(tab_ref[src, :], BF16))
    return jnp.concatenate(tiles, axis=0)


def _spread_matrices():
    lane = np.arange(LANES)
    return np.stack([lane[:, None] == PAIR_GROUP * j + lane[None, :] // SUBLANES
                     for j in range(SLOT_GROUPS)])


def _peer_u_kernel(e_ref, h_ref, g_ref, tab_ref, sel_ref, o_ref, slab_ref, idx_ref, sem, *, tb):
    groups_per_token = PEER_SLOTS // SUBLANES
    sel = sel_ref[...]

    def token(t, row):
        hb = _to_tile_rows(h_ref[t]).astype(BF16)
        for j in range(groups_per_token):
            prods = []
            for qn in range(SUBLANES):
                src = pl.ds(pl.multiple_of(row(j * SUBLANES + qn), WORDS_PER_EXPERT), WORDS_PER_EXPERT)
                prods.append(pltpu.bitcast(tab_ref[src, :], BF16) * hb)
            prods = jnp.concatenate(prods, axis=0)
            dst = pl.multiple_of(t * PEER_SLOTS + j * SUBLANES, SUBLANES)
            slab_ref[pl.ds(dst, SUBLANES), :] = jnp.dot(sel, prods, preferred_element_type=F32)

    _for_each_token(e_ref, idx_ref, sem, tb, token)
    rio = lax.broadcasted_iota(I32, (LANES, LANES), 0)
    lio = lax.broadcasted_iota(I32, (LANES, LANES), 1)
    diag = (rio == lio)[None]
    rows = SUBLANES * PEER_SLOTS
    for gi in range(tb // SUBLANES):
        tok = slice(gi * SUBLANES, (gi + 1) * SUBLANES)
        rs = jnp.sum(slab_ref[gi * rows:(gi + 1) * rows, :], axis=-1, keepdims=True)
        rs = rs.reshape(SUBLANES, PEER_SLOTS, 1)
        a = jnp.sum(jnp.where(diag, rs, 0.0), axis=1)
        o_ref[tok, :] = g_ref[tok, :] * _gelu_tanh(a)


def _peer_u(e4, h3, gates, tab, tb=128):
    N = h3.shape[0]
    assert tb % (RING_SLOTS * SLOT_TOKENS) == 0 and tb % SUBLANES == 0
    sel = np.arange(SUBLANES)[:, None] == np.arange(SUBLANES * SUBLANES)[None, :] // SUBLANES
    return pl.pallas_call(
        functools.partial(_peer_u_kernel, tb=tb),
        grid=(N // tb,),
        in_specs=[
            pl.BlockSpec((tb // SLOT_TOKENS, SLOT_TOKENS, PEER_SLOTS), lambda i: (i, 0, 0)),
            pl.BlockSpec((tb, SUBLANES, LANES), lambda i: (i, 0, 0)),
            pl.BlockSpec((tb, PEER_SLOTS), lambda i: (i, 0)),
            pl.BlockSpec(tab.shape, lambda i: (0, 0), pipeline_mode=pl.Buffered(1)),
            pl.BlockSpec(sel.shape, lambda i: (0, 0)),
        ],
        out_specs=pl.BlockSpec((tb, PEER_SLOTS), lambda i: (i, 0)),
        out_shape=jax.ShapeDtypeStruct((N, PEER_SLOTS), F32),
        scratch_shapes=[pltpu.VMEM((tb * PEER_SLOTS, LANES), F32),
                        pltpu.SMEM((RING_SLOTS, SLOT_TOKENS, PEER_SLOTS), I32),
                        pltpu.SemaphoreType.DMA((RING_SLOTS,))],
        compiler_params=_params(("arbitrary",), VMEM_LIMIT_BYTES),
        name="peer_u",
    )(e4.reshape(N // SLOT_TOKENS, SLOT_TOKENS, PEER_SLOTS), h3, gates, tab, jnp.asarray(sel, BF16))


def _peer_v_kernel(e_ref, c_ref, x1_ref, g2_ref, tab_ref, spread_ref, o_ref, hi_ref, lo_ref, idx_ref, sem, *,
                   tb):
    c = c_ref[...]
    c_hi = c.astype(BF16)
    c_lo = (c - c_hi.astype(F32)).astype(BF16)
    for j in range(SLOT_GROUPS):
        rows = pl.ds(j, tb, stride=SLOT_GROUPS)
        hi_ref[rows, :] = jnp.dot(c_hi, spread_ref[j], preferred_element_type=F32)
        lo_ref[rows, :] = jnp.dot(c_lo, spread_ref[j], preferred_element_type=F32)
    rio = lax.broadcasted_iota(I32, (SUBLANES, LANES), 0)
    lio = lax.broadcasted_iota(I32, (SUBLANES, LANES), 1)
    own_row = (lio & (SUBLANES - 1)) == rio

    def token(t, row):
        base = pl.multiple_of(t * SLOT_GROUPS, SLOT_GROUPS)
        hi = hi_ref[pl.ds(base, SLOT_GROUPS), :]
        lo = lo_ref[pl.ds(base, SLOT_GROUPS), :]
        acc = jnp.zeros((SUBLANES, LANES), F32)
        for j in range(SLOT_GROUPS):
            lhs = jnp.concatenate(
                [jnp.where(own_row, jnp.broadcast_to(part[j:j + 1, :], (SUBLANES, LANES)), 0.0).astype(BF16)
                 for part in (hi, lo)], axis=0)
            out = jnp.dot(lhs, _slot_tiles(tab_ref, row, j), preferred_element_type=F32)
            acc = acc + (out[:SUBLANES] + out[SUBLANES:])
        o_ref[t] = x1_ref[t] + g2_ref[0] * _from_tile_rows(acc)

    _for_each_token(e_ref, idx_ref, sem, tb, token)


def _peer_v(e4, coef, x1_3, g2_3, tab, T, tb=128):
    N = x1_3.shape[0]
    assert tb % (RING_SLOTS * SLOT_TOKENS) == 0 and tb % SUBLANES == 0
    return pl.pallas_call(
        functools.partial(_peer_v_kernel, tb=tb),
        grid=(N // tb,),
        in_specs=[
            pl.BlockSpec((tb // SLOT_TOKENS, SLOT_TOKENS, PEER_SLOTS), lambda i: (i, 0, 0)),
            pl.BlockSpec((tb, PEER_SLOTS), lambda i: (i, 0)),
            pl.BlockSpec((tb, SUBLANES, LANES), lambda i: (i, 0, 0)),
            pl.BlockSpec((1, SUBLANES, LANES), lambda i: ((i * tb) // T, 0, 0)),
            pl.BlockSpec(tab.shape, lambda i: (0, 0), pipeline_mode=pl.Buffered(1)),
            pl.BlockSpec((SLOT_GROUPS, LANES, LANES), lambda i: (0, 0, 0)),
        ],
        out_specs=pl.BlockSpec((tb, SUBLANES, LANES), lambda i: (i, 0, 0)),
        out_shape=jax.ShapeDtypeStruct((N, SUBLANES, LANES), F32),
        scratch_shapes=[pltpu.VMEM((tb * SLOT_GROUPS, LANES), F32),
                        pltpu.VMEM((tb * SLOT_GROUPS, LANES), F32),
                        pltpu.SMEM((RING_SLOTS, SLOT_TOKENS, PEER_SLOTS), I32),
                        pltpu.SemaphoreType.DMA((RING_SLOTS,))],
        compiler_params=_params(("arbitrary",), VMEM_LIMIT_BYTES),
        name="peer_v",
    )(e4.reshape(N // SLOT_TOKENS, SLOT_TOKENS, PEER_SLOTS), coef, x1_3, g2_3, tab,
      jnp.asarray(_spread_matrices(), BF16))


def kernel(x, c, w_ada, b_ada, norm1_g, w_in, gla_gate_w2, gla_gate_b, gla_norm_g, swa_qnorm_g,
           swa_knorm_g, swa_sinks, rel_bias, w_up_a, w_up_b, w_out, norm2_g, peer_wq, peer_subkeys,
           peer_u, peer_v):
    B, T, D = x.shape
    N = B * T
    L = w_ada.shape[0]
    mod = _adaln(c, w_ada, b_ada)
    weff_t = _fold_peer_keys(peer_wq, peer_subkeys)
    bias = _swa_bias(rel_bias)
    xf = x.reshape(N, D)
    for l in range(L):
        sh1, sc1, g1, sh2, sc2, g2 = [mod[l, :, i * D:(i + 1) * D] for i in range(6)]
        proj = _in_proj(xf, norm1_g[l], sc1, sh1, _pack_w_in(w_in, l), T)
        proj3 = proj.reshape(B, T, PROJ_W)
        w2p = jnp.zeros((LANES, GLA_HEADS * GLA_DK), BF16).at[:GLA_RANK].set(gla_gate_w2[l].astype(BF16))
        gla_o = _gla(proj3, w2p, gla_gate_b[l].reshape(1, -1), gla_norm_g[l].reshape(1, -1))
        swa_o = _swa(proj3, bias, swa_qnorm_g[l].reshape(1, -1), swa_knorm_g[l].reshape(1, -1),
                     swa_sinks[l])
        x1, h2, s_t = _merge(xf, proj, gla_o.reshape(N, -1), swa_o.reshape(N, -1),
                             w_up_a[l].astype(BF16), w_up_b[l].astype(BF16), w_out[l].astype(BF16),
                             g1, norm2_g[l], sc2, sh2, weff_t[l], T)
        e4, gates = _topk(s_t)
        coef = _peer_u(e4, h2.reshape(N, SUBLANES, LANES), gates, _table(peer_u, l))
        g2_3 = g2.reshape(B, SUBLANES, LANES)
        xf = _peer_v(e4, coef, x1.reshape(N, SUBLANES, LANES), g2_3, _table(peer_v, l), T).reshape(N, D)
    return xf.reshape(B, T, D)
```

```python
import functools

import numpy as np
import jax
import jax.numpy as jnp
from jax import lax
from jax.experimental import pallas as pl
from jax.experimental.pallas import tpu as pltpu

F32 = jnp.float32
BF16 = jnp.bfloat16
I32 = jnp.int32
HIGHEST = lax.Precision.HIGHEST
EPS = 1e-6

GLA_HEADS = 4
GLA_DK = 128
GLA_DV = 256
GLA_RANK = 16
GLA_TAU = 16.0
GLA_CHUNK = 64
SWA_HEADS = 16
SWA_KV_HEADS = 2
SWA_HD = 64
SWA_BLOCK = 128
N_BUCKETS = 32
MAX_DISTANCE = 128
PEER_HEADS = 8
PEER_NKEYS = 128
PEER_TOPK = 16
PEER_SLOTS = PEER_HEADS * PEER_TOPK

SUBLANES = 8
LANES = 128
VMEM_LIMIT_BYTES = 56 * 1024 * 1024

NT_DIMS = (((1,), (1,)), ((), ()))
TN_DIMS = (((0,), (0,)), ((), ()))

COL_Q, COL_K, COL_V, COL_GR, COL_SQ, COL_GA, COL_GB = 0, 512, 1024, 2048, 3072, 4096, 5120
COL_SK, COL_SV, COL_GLR = 6144, 6272, 6400
PROJ_W = 6528


def _params(sem, vmem=None):
    return pltpu.CompilerParams(dimension_semantics=sem, vmem_limit_bytes=vmem)


def _adaln_kernel(c_ref, w_ref, b_ref, o_ref):
    c = c_ref[...]
    a = c * jax.nn.sigmoid(c)
    o_ref[0] = jnp.dot(a, w_ref[0], preferred_element_type=F32, precision=HIGHEST) + b_ref[0]


def _adaln(c, w_ada, b_ada):
    L, D, W = w_ada.shape
    B = c.shape[0]
    rows = -(-B // SUBLANES) * SUBLANES
    cp = jnp.zeros((rows, D), F32).at[:B].set(c)
    tn = W // 4
    out = pl.pallas_call(
        _adaln_kernel,
        grid=(L, W // tn),
        in_specs=[
            pl.BlockSpec((rows, D), lambda l, j: (0, 0)),
            pl.BlockSpec((1, D, tn), lambda l, j: (l, 0, j)),
            pl.BlockSpec((1, 1, tn), lambda l, j: (l, 0, j)),
        ],
        out_specs=pl.BlockSpec((1, rows, tn), lambda l, j: (l, 0, j)),
        out_shape=jax.ShapeDtypeStruct((L, rows, W), F32),
        compiler_params=_params(("arbitrary", "arbitrary")),
        name="adaln",
    )(cp, w_ada, b_ada.reshape(L, 1, W))
    return out[:, :B]


def _fold_kernel(sk_ref, wq_ref, o_ref, rows_ref):
    half = sk_ref.shape[-1]
    groups = 2 * PEER_HEADS
    for h in range(PEER_HEADS):
        for p in range(2):
            g = 2 * h + p
            res = lax.dot_general(sk_ref[0, p], wq_ref[0, :, g * half:(g + 1) * half], NT_DIMS,
                                  precision=HIGHEST, preferred_element_type=F32)
            for cb in range(rows_ref.shape[0]):
                rows_ref[cb, pl.ds(p * PEER_HEADS + h, PEER_NKEYS, stride=groups), :] = (
                    res[:, cb * LANES:(cb + 1) * LANES])
    for cb in range(rows_ref.shape[0]):
        o_ref[0, :, cb * LANES:(cb + 1) * LANES] = rows_ref[cb].astype(BF16)


def _fold_peer_keys(peer_wq, peer_subkeys):
    L, D, QW = peer_wq.shape
    half = peer_subkeys.shape[-1]
    rows = (QW // half) * PEER_NKEYS
    return pl.pallas_call(
        _fold_kernel,
        grid=(L,),
        in_specs=[
            pl.BlockSpec((1, 2, PEER_NKEYS, half), lambda l: (l, 0, 0, 0)),
            pl.BlockSpec((1, D, QW), lambda l: (l, 0, 0)),
        ],
        out_specs=pl.BlockSpec((1, rows, D), lambda l: (l, 0, 0)),
        out_shape=jax.ShapeDtypeStruct((L, rows, D), BF16),
        scratch_shapes=[pltpu.VMEM((D // LANES, rows, LANES), F32)],
        compiler_params=_params(("arbitrary",), VMEM_LIMIT_BYTES),
        name="peer_fold",
    )(peer_subkeys, peer_wq)


def _inproj_kernel(x_ref, g_ref, sc_ref, sh_ref, w_ref, o_ref):
    x = x_ref[...]
    ms = jnp.mean(x * x, axis=-1, keepdims=True)
    y = x * lax.rsqrt(ms + EPS) * g_ref[...]
    h = y * (1.0 + sc_ref[0]) + sh_ref[0]
    o_ref[...] = jnp.dot(h.astype(BF16), w_ref[...], preferred_element_type=F32).astype(BF16)


def _in_proj(xf, g, sc, sh, wp, T, tm=512):
    N, D = xf.shape
    B = sc.shape[0]
    ncol = 3
    tn = PROJ_W // ncol
    return pl.pallas_call(
        _inproj_kernel,
        grid=(ncol, N // tm),
        in_specs=[
            pl.BlockSpec((tm, D), lambda j, i: (i, 0)),
            pl.BlockSpec((1, D), lambda j, i: (0, 0)),
            pl.BlockSpec((1, 1, D), lambda j, i: ((i * tm) // T, 0, 0)),
            pl.BlockSpec((1, 1, D), lambda j, i: ((i * tm) // T, 0, 0)),
            pl.BlockSpec((D, tn), lambda j, i: (0, j)),
        ],
        out_specs=pl.BlockSpec((tm, tn), lambda j, i: (i, j)),
        out_shape=jax.ShapeDtypeStruct((N, PROJ_W), BF16),
        compiler_params=_params(("arbitrary", "arbitrary")),
        name="in_proj",
    )(xf, g.reshape(1, D), sc.reshape(B, 1, D), sh.reshape(B, 1, D), wp)


_SRC = dict(zip(("q", "k", "v", "glr", "gr", "sq", "sk", "sv", "ga", "gb"),
                np.cumsum([0, 512, 512, 1024, GLA_RANK, 1024, 1024, 128, 128, 1024])))
W_IN_SLABS = ((COL_Q, _SRC["q"], 512), (COL_K, _SRC["k"], 512), (COL_V, _SRC["v"], 1024),
              (COL_GR, _SRC["gr"], 1024), (COL_SQ, _SRC["sq"], 1024), (COL_GA, _SRC["ga"], 1024),
              (COL_GB, _SRC["gb"], 1024), (COL_SK, _SRC["sk"], 128), (COL_SV, _SRC["sv"], 128),
              (COL_GLR, _SRC["glr"], GLA_RANK))


def _pack_w_in_kernel(w_ref, o_ref):
    o_ref[:, COL_GLR:] = jnp.zeros((o_ref.shape[0], PROJ_W - COL_GLR), BF16)
    for dst, src, width in W_IN_SLABS:
        o_ref[:, dst:dst + width] = w_ref[:, int(src):int(src) + width].astype(BF16)


def _pack_w_in(w_in, layer, tr=128):
    _, rows, cols = w_in.shape
    return pl.pallas_call(
        _pack_w_in_kernel,
        grid=(rows // tr,),
        in_specs=[pl.BlockSpec((None, tr, cols), lambda i: (layer, i, 0))],
        out_specs=pl.BlockSpec((tr, PROJ_W), lambda i: (i, 0)),
        out_shape=jax.ShapeDtypeStruct((rows, PROJ_W), BF16),
        compiler_params=_params(("arbitrary",)),
        name="pack_w_in",
    )(w_in)


def _gla_kernel(q_ref, k_ref, v_ref, r_ref, glr_ref, w2_ref, b2_ref, ng_ref, o_ref, st_ref, *, nchunk):
    @pl.when(pl.program_id(1) == 0)
    def _():
        st_ref[...] = jnp.zeros_like(st_ref)

    C = GLA_CHUNK
    dk, dv = GLA_DK, GLA_DV
    row = lax.broadcasted_iota(I32, (C, C), 0)
    col = lax.broadcasted_iota(I32, (C, C), 1)
    tri = col <= row
    tri_b = tri.astype(F32).astype(BF16)
    w2 = w2_ref[...]
    b2 = b2_ref[...]
    ng = ng_ref[...]
    for ci in range(nchunk):
        sl = pl.ds(ci * C, C)
        z = jnp.dot(glr_ref[sl, :], w2, preferred_element_type=F32) + b2
        log_a = (jnp.minimum(z, 0.0) - jnp.log(1.0 + jnp.exp(-jnp.abs(z)))) * (1.0 / GLA_TAU)
        la_hi = log_a.astype(BF16)
        la_lo = (log_a - la_hi.astype(F32)).astype(BF16)
        b = (jnp.dot(tri_b, la_hi, preferred_element_type=F32)
             + jnp.dot(tri_b, la_lo, preferred_element_type=F32))
        b_last = b[C - 1:C, :]
        q = q_ref[sl, :].astype(F32) * (dk ** -0.5)
        k = k_ref[sl, :].astype(F32)
        q_dec = (q * jnp.exp(b)).astype(BF16)
        k_inv = (k * jnp.exp(-b)).astype(BF16)
        k_tail = (k * jnp.exp(b_last - b)).astype(BF16)
        decay = jnp.exp(b_last)
        for h in range(GLA_HEADS):
            kc = slice(h * dk, (h + 1) * dk)
            vc = slice(h * dv, (h + 1) * dv)
            v = v_ref[sl, vc]
            attn = lax.dot_general(q_dec[:, kc], k_inv[:, kc], NT_DIMS, preferred_element_type=F32)
            attn = jnp.where(tri, attn, 0.0).astype(BF16)
            st = st_ref[h]
            o = (jnp.dot(attn, v, preferred_element_type=F32)
                 + lax.dot_general(q_dec[:, kc], st.astype(BF16), NT_DIMS, preferred_element_type=F32))
            st_ref[h] = st * decay[:, kc] + lax.dot_general(v, k_tail[:, kc], TN_DIMS,
                                                            preferred_element_type=F32)
            on = o * lax.rsqrt(jnp.mean(o * o, axis=-1, keepdims=True) + EPS) * ng
            r = r_ref[sl, vc].astype(F32)
            o_ref[sl, vc] = (on * (r * jax.nn.sigmoid(r))).astype(BF16)


def _gla(proj, w2p, b2, ng, tc=256):
    B, T, _ = proj.shape
    H = GLA_HEADS
    kw, vw = H * GLA_DK, H * GLA_DV
    return pl.pallas_call(
        functools.partial(_gla_kernel, nchunk=tc // GLA_CHUNK),
        grid=(B, T // tc),
        in_specs=[
            pl.BlockSpec((None, tc, kw), lambda b, c: (b, c, COL_Q // kw)),
            pl.BlockSpec((None, tc, kw), lambda b, c: (b, c, COL_K // kw)),
            pl.BlockSpec((None, tc, vw), lambda b, c: (b, c, COL_V // vw)),
            pl.BlockSpec((None, tc, vw), lambda b, c: (b, c, COL_GR // vw)),
            pl.BlockSpec((None, tc, LANES), lambda b, c: (b, c, COL_GLR // LANES)),
            pl.BlockSpec((LANES, kw), lambda b, c: (0, 0)),
            pl.BlockSpec((1, kw), lambda b, c: (0, 0)),
            pl.BlockSpec((1, GLA_DV), lambda b, c: (0, 0)),
        ],
        out_specs=pl.BlockSpec((None, tc, vw), lambda b, c: (b, c, 0)),
        out_shape=jax.ShapeDtypeStruct((B, T, vw), BF16),
        scratch_shapes=[pltpu.VMEM((H, GLA_DV, GLA_DK), F32)],
        compiler_params=_params(("arbitrary", "arbitrary")),
        name="gla",
    )(proj, proj, proj, proj, proj, w2p, b2, ng)


def _t5_bucket(dist):
    max_exact = N_BUCKETS // 2
    d = np.maximum(dist, 1).astype(np.float32)
    large = max_exact + (np.log(d / max_exact) / np.log(MAX_DISTANCE / max_exact)
                         * (N_BUCKETS - max_exact)).astype(np.int32)
    large = np.minimum(large, N_BUCKETS - 1)
    return np.where(dist < max_exact, dist, large).astype(np.int32)


NEG_BIG = -1e30


def _swa_bias(rel_bias):
    blk = SWA_BLOCK
    qi = np.arange(blk)[:, None]
    sj = np.arange(2 * blk)[None, :]
    dist = blk + qi - sj
    band = (dist >= 0) & (dist < blk)
    bucket = _t5_bucket(np.clip(dist, 0, None))
    onehot = jnp.asarray(bucket.reshape(-1, 1) == np.arange(N_BUCKETS)[None, :], F32)
    bias = jnp.dot(onehot, rel_bias.astype(F32), precision=HIGHEST).T.reshape(-1, blk, 2 * blk)
    masks = np.stack([band, band & (sj >= blk)])[:, None]
    return jnp.where(jnp.asarray(masks), bias[None], NEG_BIG)


def _segment_sums(x, seg):
    hi = x.astype(BF16)
    lo = (x - hi.astype(F32)).astype(BF16)
    return (jnp.dot(hi, seg, preferred_element_type=F32) + jnp.dot(lo, seg, preferred_element_type=F32))


def _head_rms_scale(x, seg, seg_t):
    inv = lax.rsqrt(_segment_sums(x * x, seg) * (1.0 / SWA_HD) + EPS)
    return _segment_sums(inv, seg_t)


SWA_ROWS = 32


def _swa_kernel(sink_ref, q_ref, kp_ref, kc_ref, vp_ref, vc_ref, bias_ref, qg_ref, kg_ref, seg_ref,
                segt_ref, o_ref):
    blk = SWA_BLOCK
    hd = SWA_HD
    group = SWA_HEADS // SWA_KV_HEADS
    kvw = SWA_KV_HEADS * hd
    seg = seg_ref[...]
    seg_t = segt_ref[...]
    q = q_ref[...].astype(F32)
    qn = (q * _head_rms_scale(q, seg, seg_t) * qg_ref[...] * (hd ** -0.5)).astype(BF16)
    k2 = jnp.concatenate([kp_ref[...], kc_ref[...]], axis=0).astype(F32)
    kn = (k2 * _head_rms_scale(k2, seg[:kvw], seg_t[:, :kvw]) * kg_ref[...]).astype(BF16)
    v2 = jnp.concatenate([vp_ref[...], vc_ref[...]], axis=0)
    lane = lax.broadcasted_iota(I32, (2 * blk, kvw), 1)
    outs = []
    for kh in range(SWA_KV_HEADS):
        kk = kn[:, kh * hd:(kh + 1) * hd]
        vsh = v2 if kh == 0 else jnp.concatenate([v2[:, kh * hd:], v2[:, :kh * hd]], axis=1)
        vv = jnp.where(lane < hd, vsh, jnp.ones_like(vsh))
        qs = jnp.concatenate([qn[:, (kh * group + g) * hd:(kh * group + g + 1) * hd]
                              for g in range(group)], axis=0)
        logits = lax.dot_general(qs, kk, NT_DIMS, preferred_element_type=F32)
        logits = logits + bias_ref[kh * group:(kh + 1) * group].reshape(group * blk, 2 * blk)
        for g in range(group):
            sink = sink_ref[kh * group + g]
            parts = []
            for r0 in range(0, blk, SWA_ROWS):
                lg = logits[g * blk + r0:g * blk + r0 + SWA_ROWS]
                m = jnp.maximum(jnp.max(lg, axis=-1, keepdims=True), sink)
                p = jnp.exp(lg - m).astype(BF16)
                pv = jnp.dot(p, vv, preferred_element_type=F32)
                parts.append(pv[:, :hd] / (pv[:, hd:hd + 1] + jnp.exp(sink - m)))
            outs.append(jnp.concatenate(parts, axis=0))
    o_ref[...] = jnp.concatenate(outs, axis=-1).astype(BF16)


def _swa(proj, bias, qg, kg, sinks):
    B, T, _ = proj.shape
    blk = SWA_BLOCK
    qw = SWA_HEADS * SWA_HD
    kvw = SWA_KV_HEADS * SWA_HD
    prev = lambda b, i: jnp.maximum(i - 1, 0)
    seg = (np.arange(qw)[:, None] // SWA_HD == np.arange(LANES)[None, :])
    seg = jnp.asarray(seg, BF16)
    return pl.pallas_call(
        _swa_kernel,
        grid=(B, T // blk),
        in_specs=[
            pl.BlockSpec(memory_space=pltpu.SMEM),
            pl.BlockSpec((None, blk, qw), lambda b, i: (b, i, COL_SQ // qw)),
            pl.BlockSpec((None, blk, kvw), lambda b, i: (b, prev(b, i), COL_SK // kvw)),
            pl.BlockSpec((None, blk, kvw), lambda b, i: (b, i, COL_SK // kvw)),
            pl.BlockSpec((None, blk, kvw), lambda b, i: (b, prev(b, i), COL_SV // kvw)),
            pl.BlockSpec((None, blk, kvw), lambda b, i: (b, i, COL_SV // kvw)),
            pl.BlockSpec((None, SWA_HEADS, blk, 2 * blk), lambda b, i: (jnp.where(i == 0, 1, 0), 0, 0, 0)),
            pl.BlockSpec((1, qw), lambda b, i: (0, 0)),
            pl.BlockSpec((1, kvw), lambda b, i: (0, 0)),
            pl.BlockSpec((qw, LANES), lambda b, i: (0, 0)),
            pl.BlockSpec((LANES, qw), lambda b, i: (0, 0)),
        ],
        out_specs=pl.BlockSpec((None, blk, qw), lambda b, i: (b, i, 0)),
        out_shape=jax.ShapeDtypeStruct((B, T, qw), BF16),
        compiler_params=_params(("arbitrary", "arbitrary")),
        name="swa",
    )(sinks, proj, proj, proj, proj, proj, bias, jnp.tile(qg, (1, SWA_HEADS)),
      jnp.tile(kg, (1, SWA_KV_HEADS)), seg, seg.T)


def _merge_kernel(x_ref, ga_ref, gb_ref, go_ref, so_ref, wa_ref, wb_ref, wo_ref, g1_ref,
                  n2_ref, sc2_ref, sh2_ref, we_ref, x1_ref, h2_ref, st_ref):
    ya = jnp.dot(go_ref[...], wa_ref[...], preferred_element_type=F32)
    yb = jnp.dot(so_ref[...], wb_ref[...], preferred_element_type=F32)
    m = jax.nn.sigmoid(ga_ref[...].astype(F32)) * ya + jax.nn.sigmoid(gb_ref[...].astype(F32)) * yb
    mixed = jnp.dot(m.astype(BF16), wo_ref[...], preferred_element_type=F32)
    x1 = x_ref[...] + g1_ref[0] * mixed
    x1_ref[...] = x1
    y = x1 * lax.rsqrt(jnp.mean(x1 * x1, axis=-1, keepdims=True) + EPS) * n2_ref[...]
    h2 = y * (1.0 + sc2_ref[0]) + sh2_ref[0]
    h2_ref[...] = h2
    st_ref[...] = lax.dot_general(we_ref[...], h2.astype(BF16), NT_DIMS, preferred_element_type=F32)


def _merge(xf, proj2, gla_o, swa_o, wa, wb, wo, g1, n2, sc2, sh2, weff_t, T, tm=256):
    N, D = xf.shape
    B = g1.shape[0]
    SW = weff_t.shape[0]
    bat = lambda i: ((i * tm) // T, 0, 0)
    full = lambda i: (0, 0)
    return pl.pallas_call(
        _merge_kernel,
        grid=(N // tm,),
        in_specs=[
            pl.BlockSpec((tm, D), lambda i: (i, 0)),
            pl.BlockSpec((tm, D), lambda i: (i, COL_GA // D)),
            pl.BlockSpec((tm, D), lambda i: (i, COL_GB // D)),
            pl.BlockSpec((tm, D), lambda i: (i, 0)),
            pl.BlockSpec((tm, D), lambda i: (i, 0)),
            pl.BlockSpec((D, D), full),
            pl.BlockSpec((D, D), full),
            pl.BlockSpec((D, D), full),
            pl.BlockSpec((1, 1, D), bat),
            pl.BlockSpec((1, D), full),
            pl.BlockSpec((1, 1, D), bat),
            pl.BlockSpec((1, 1, D), bat),
            pl.BlockSpec((SW, D), full),
        ],
        out_specs=[
            pl.BlockSpec((tm, D), lambda i: (i, 0)),
            pl.BlockSpec((tm, D), lambda i: (i, 0)),
            pl.BlockSpec((SW, tm), lambda i: (0, i)),
        ],
        out_shape=[
            jax.ShapeDtypeStruct((N, D), F32),
            jax.ShapeDtypeStruct((N, D), F32),
            jax.ShapeDtypeStruct((SW, N), F32),
        ],
        compiler_params=_params(("arbitrary",), VMEM_LIMIT_BYTES),
        name="merge",
    )(xf, proj2, proj2, gla_o, swa_o, wa, wb, wo, g1.reshape(B, 1, D), n2.reshape(1, D),
      sc2.reshape(B, 1, D), sh2.reshape(B, 1, D), weff_t)


def _tree(op, xs):
    xs = list(xs)
    while len(xs) > 1:
        xs = [op(xs[i], xs[i + 1]) for i in range(0, len(xs) - 1, 2)] + ([xs[-1]] if len(xs) % 2 else [])
    return xs[0]


EXTRACT_CHAINS = 4


def _extract_best(ref, n):
    lanes = min(EXTRACT_CHAINS, n)
    best = [ref[i] for i in range(lanes)]
    for i in range(lanes, n):
        best[i % lanes] = jnp.maximum(best[i % lanes], ref[i])
    m = _tree(jnp.maximum, best)
    first = [jnp.where(ref[i] == m, i, n) for i in range(lanes)]
    for i in range(lanes, n):
        first[i % lanes] = jnp.minimum(first[i % lanes], jnp.where(ref[i] == m, i, n))
    pos = _tree(jnp.minimum, first)
    for i in range(n):
        ref[i] = jnp.where(pos == i, -jnp.inf, ref[i])
    return m, pos


def _ordered(a, b):
    (va, ia), (vb, ib) = a, b
    keep = jnp.logical_or(va > vb, jnp.logical_and(va == vb, ia < ib))
    first = (jnp.where(keep, va, vb), jnp.where(keep, ia, ib))
    second = (jnp.where(keep, vb, va), jnp.where(keep, ib, ia))
    return first, second


def _bitonic_merge(xs):
    xs = list(xs)
    j = len(xs) // 2
    while j >= 1:
        for i in range(len(xs)):
            if i & j == 0:
                xs[i], xs[i | j] = _ordered(xs[i], xs[i | j])
        j //= 2
    return xs


def _bitonic_sort(xs):
    xs = list(xs)
    k = 2
    while k <= len(xs):
        j = k // 2
        while j >= 1:
            for i in range(len(xs)):
                if i & j == 0:
                    first, second = _ordered(xs[i], xs[i | j])
                    xs[i], xs[i | j] = (first, second) if i & k == 0 else (second, first)
            j //= 2
        k *= 2
    return xs


def _merge_best(xs, ys):
    n = len(xs)
    return _bitonic_merge([_ordered(xs[i], ys[n - 1 - i])[0] for i in range(n)])


TOPK_PAIRS = tuple((a, b) for a in range(PEER_TOPK) for b in range(PEER_TOPK)
                   if (a + 1) * (b + 1) <= PEER_TOPK)


def _topk_kernel(s_ref, e_ref, g_ref, sv_ref, si_ref, v_ref, i_ref, cand_ref, cidx_ref, best_ref, row_ref):
    K = PEER_TOPK
    nk = PEER_NKEYS
    H = PEER_HEADS
    G = 2 * H
    tm = s_ref.shape[1]
    nblocks = nk // K
    for hb in range(2):
        rows = slice(hb * H, (hb + 1) * H)
        for blk in range(nblocks):
            keys = [(s_ref[(blk * K + i) * G + hb * H:(blk * K + i) * G + (hb + 1) * H, :],
                     jnp.full((H, tm), blk * K + i, I32)) for i in range(K)]
            for i, (v, ix) in enumerate(_bitonic_sort(keys)):
                sv_ref[blk, i, rows, :] = v
                si_ref[blk, i, rows, :] = ix
        step = 1
        while step < nblocks:
            for blk in range(0, nblocks, 2 * step):
                xs = [(sv_ref[blk, i, rows, :], si_ref[blk, i, rows, :]) for i in range(K)]
                ys = [(sv_ref[blk + step, i, rows, :], si_ref[blk + step, i, rows, :]) for i in range(K)]
                for i, (v, ix) in enumerate(_merge_best(xs, ys)):
                    sv_ref[blk, i, rows, :] = v
                    si_ref[blk, i, rows, :] = ix
            step *= 2
    v_ref[...] = sv_ref[0]
    i_ref[...] = si_ref[0]

    for ci, (a, b) in enumerate(TOPK_PAIRS):
        cand_ref[ci] = v_ref[a, 0:H, :] + v_ref[b, H:2 * H, :]
        cidx_ref[ci] = (i_ref[a, 0:H, :] * nk + i_ref[b, H:2 * H, :]) * WORDS_PER_EXPERT
    ncand = len(TOPK_PAIRS)

    def stage2(k, _):
        m, pos = _extract_best(cand_ref, ncand)
        row = _tree(jnp.maximum, [jnp.where(pos == ci, cidx_ref[ci], -1) for ci in range(ncand)])
        best_ref[k] = m
        row_ref[k] = row
        return 0

    lax.fori_loop(0, K, stage2, 0)
    best = best_ref[...]
    ex = jnp.exp(best - best[0:1])
    gates = ex / jnp.sum(ex, axis=0, keepdims=True)
    g_ref[...] = gates.reshape(K * H, tm).T
    e_ref[...] = row_ref[...].reshape(K * H, tm).astype(F32).T.astype(I32)


def _topk(s_t, tm=128):
    SW, N = s_t.shape
    K = PEER_TOPK
    G = 2 * PEER_HEADS
    return pl.pallas_call(
        _topk_kernel,
        grid=(N // tm,),
        in_specs=[pl.BlockSpec((SW, tm), lambda i: (0, i))],
        out_specs=[
            pl.BlockSpec((tm, PEER_SLOTS), lambda i: (i, 0)),
            pl.BlockSpec((tm, PEER_SLOTS), lambda i: (i, 0)),
        ],
        out_shape=[
            jax.ShapeDtypeStruct((N, PEER_SLOTS), I32),
            jax.ShapeDtypeStruct((N, PEER_SLOTS), F32),
        ],
        scratch_shapes=[pltpu.VMEM((PEER_NKEYS // K, K, G, tm), F32),
                        pltpu.VMEM((PEER_NKEYS // K, K, G, tm), I32),
                        pltpu.VMEM((K, G, tm), F32),
                        pltpu.VMEM((K, G, tm), I32),
                        pltpu.VMEM((len(TOPK_PAIRS), PEER_HEADS, tm), F32),
                        pltpu.VMEM((len(TOPK_PAIRS), PEER_HEADS, tm), I32),
                        pltpu.VMEM((K, PEER_HEADS, tm), F32),
                        pltpu.VMEM((K, PEER_HEADS, tm), I32)],
        compiler_params=_params(("arbitrary",)),
        name="peer_topk",
    )(s_t)


WORDS_PER_EXPERT = SUBLANES // 2


TILE_BLOCK = tuple((r % 2) * WORDS_PER_EXPERT + r // 2 for r in range(SUBLANES))
BLOCK_ROW = tuple(TILE_BLOCK.index(q) for q in range(SUBLANES))


def _bf16_bits(x):
    return lax.bitcast_convert_type(x.astype(BF16).astype(F32), jnp.uint32)


def _table_kernel(t_ref, o_ref):
    rows = t_ref.shape[0]
    for s in range(WORDS_PER_EXPERT):
        lo = _bf16_bits(t_ref[:, s * LANES:(s + 1) * LANES])
        hi = _bf16_bits(t_ref[:, (s + WORDS_PER_EXPERT) * LANES:(s + WORDS_PER_EXPERT + 1) * LANES])
        word = lax.shift_right_logical(lo, jnp.uint32(16)) | (hi & jnp.uint32(0xFFFF0000))
        o_ref[pl.ds(s, rows, stride=WORDS_PER_EXPERT), :] = lax.bitcast_convert_type(word, I32)


def _table(tables, layer, te=2048):
    _, n, d = tables.shape
    return pl.pallas_call(
        _table_kernel,
        grid=(n // te,),
        in_specs=[pl.BlockSpec((None, te, d), lambda i: (layer, i, 0))],
        out_specs=pl.BlockSpec((te * WORDS_PER_EXPERT, LANES), lambda i: (i, 0)),
        out_shape=jax.ShapeDtypeStruct((n * WORDS_PER_EXPERT, LANES), I32),
        compiler_params=_params(("arbitrary",)),
        name="peer_table",
    )(tables)


def _expert_tile(tab_ref, e4):
    words = tab_ref[pl.ds(pl.multiple_of(e4, WORDS_PER_EXPERT), WORDS_PER_EXPERT), :]
    return pltpu.bitcast(words, BF16)


def _to_tile_rows(x):
    return jnp.concatenate([x[q:q + 1, :] for q in TILE_BLOCK], axis=0)


def _from_tile_rows(x):
    return jnp.concatenate([x[r:r + 1, :] for r in BLOCK_ROW], axis=0)


def _gelu_tanh(x):
    return 0.5 * x * (1.0 + jnp.tanh(np.sqrt(2.0 / np.pi) * (x + 0.044715 * (x * x * x))))


RING_SLOTS = 2
SLOT_TOKENS = 32


def _for_each_token(e_ref, idx_ref, sem, tb, token_body):
    ngroups = tb // SLOT_TOKENS

    def fetch(q, slot):
        return pltpu.make_async_copy(e_ref.at[q], idx_ref.at[slot], sem.at[slot])

    for slot in range(RING_SLOTS):
        fetch(slot, slot).start()

    def ring(j, _):
        for slot in range(RING_SLOTS):
            q = j * RING_SLOTS + slot
            fetch(q, slot).wait()
            for tt in range(SLOT_TOKENS):
                token_body(q * SLOT_TOKENS + tt, lambda k, slot=slot, tt=tt: idx_ref[slot, tt, k])
            fetch(jnp.minimum(q + RING_SLOTS, ngroups - 1), slot).start()
        return 0

    lax.fori_loop(0, ngroups // RING_SLOTS, ring, 0)
    for slot in range(RING_SLOTS):
        fetch(ngroups - 1, slot).wait()


PAIR_GROUP = LANES // SUBLANES
SLOT_GROUPS = PEER_SLOTS // PAIR_GROUP


def _slot_tiles(tab_ref, row, j):
    return jnp.concatenate([_expert_tile(tab_ref, row(j * PAIR_GROUP + kk)) for kk in range(PAIR_GROUP)],
                           axis=0)


def _spread_matrices():
    lane = np.arange(LANES)
    return np.stack([lane[:, None] == PAIR_GROUP * j + lane[None, :] // SUBLANES
                     for j in range(SLOT_GROUPS)])


def _peer_u_kernel(e_ref, h_ref, g_ref, tab_ref, sel_ref, o_ref, slab_ref, idx_ref, sem, *, tb):
    groups_per_token = PEER_SLOTS // SUBLANES
    sel = sel_ref[...]

    def token(t, row):
        hb = _to_tile_rows(h_ref[t]).astype(BF16)
        for j in range(groups_per_token):
            prods = jnp.concatenate([_expert_tile(tab_ref, row(j * SUBLANES + qn)) * hb
                                     for qn in range(SUBLANES)], axis=0)
            dst = pl.multiple_of(t * PEER_SLOTS + j * SUBLANES, SUBLANES)
            slab_ref[pl.ds(dst, SUBLANES), :] = jnp.dot(sel, prods, preferred_element_type=F32)

    _for_each_token(e_ref, idx_ref, sem, tb, token)
    rio = lax.broadcasted_iota(I32, (LANES, LANES), 0)
    lio = lax.broadcasted_iota(I32, (LANES, LANES), 1)
    diag = (rio == lio)[None]
    rows = SUBLANES * PEER_SLOTS
    for gi in range(tb // SUBLANES):
        tok = slice(gi * SUBLANES, (gi + 1) * SUBLANES)
        rs = jnp.sum(slab_ref[gi * rows:(gi + 1) * rows, :], axis=-1, keepdims=True)
        rs = rs.reshape(SUBLANES, PEER_SLOTS, 1)
        a = jnp.sum(jnp.where(diag, rs, 0.0), axis=1)
        o_ref[tok, :] = g_ref[tok, :] * _gelu_tanh(a)


def _peer_u(e4, h3, gates, tab, tb=128):
    N = h3.shape[0]
    assert tb % (RING_SLOTS * SLOT_TOKENS) == 0 and tb % SUBLANES == 0
    sel = np.arange(SUBLANES)[:, None] == np.arange(SUBLANES * SUBLANES)[None, :] // SUBLANES
    return pl.pallas_call(
        functools.partial(_peer_u_kernel, tb=tb),
        grid=(N // tb,),
        in_specs=[
            pl.BlockSpec((tb // SLOT_TOKENS, SLOT_TOKENS, PEER_SLOTS), lambda i: (i, 0, 0)),
            pl.BlockSpec((tb, SUBLANES, LANES), lambda i: (i, 0, 0)),
            pl.BlockSpec((tb, PEER_SLOTS), lambda i: (i, 0)),
            pl.BlockSpec(tab.shape, lambda i: (0, 0), pipeline_mode=pl.Buffered(1)),
            pl.BlockSpec(sel.shape, lambda i: (0, 0)),
        ],
        out_specs=pl.BlockSpec((tb, PEER_SLOTS), lambda i: (i, 0)),
        out_shape=jax.ShapeDtypeStruct((N, PEER_SLOTS), F32),
        scratch_shapes=[pltpu.VMEM((tb * PEER_SLOTS, LANES), F32),
                        pltpu.SMEM((RING_SLOTS, SLOT_TOKENS, PEER_SLOTS), I32),
                        pltpu.SemaphoreType.DMA((RING_SLOTS,))],
        compiler_params=_params(("arbitrary",), VMEM_LIMIT_BYTES),
        name="peer_u",
    )(e4.reshape(N // SLOT_TOKENS, SLOT_TOKENS, PEER_SLOTS), h3, gates, tab, jnp.asarray(sel, BF16))


def _peer_v_kernel(e_ref, c_ref, x1_ref, g2_ref, tab_ref, spread_ref, o_ref, hi_ref, lo_ref, idx_ref, sem, *,
                   tb):
    c = c_ref[...]
    c_hi = c.astype(BF16)
    c_lo = (c - c_hi.astype(F32)).astype(BF16)
    for j in range(SLOT_GROUPS):
        rows = pl.ds(j, tb, stride=SLOT_GROUPS)
        hi_ref[rows, :] = jnp.dot(c_hi, spread_ref[j], preferred_element_type=F32)
        lo_ref[rows, :] = jnp.dot(c_lo, spread_ref[j], preferred_element_type=F32)
    rio = lax.broadcasted_iota(I32, (SUBLANES, LANES), 0)
    lio = lax.broadcasted_iota(I32, (SUBLANES, LANES), 1)
    own_row = (lio & (SUBLANES - 1)) == rio

    def token(t, row):
        base = pl.multiple_of(t * SLOT_GROUPS, SLOT_GROUPS)
        hi = hi_ref[pl.ds(base, SLOT_GROUPS), :]
        lo = lo_ref[pl.ds(base, SLOT_GROUPS), :]
        acc = jnp.zeros((SUBLANES, LANES), F32)
        for j in range(SLOT_GROUPS):
            lhs = jnp.concatenate(
                [jnp.where(own_row, jnp.broadcast_to(part[j:j + 1, :], (SUBLANES, LANES)), 0.0).astype(BF16)
                 for part in (hi, lo)], axis=0)
            out = jnp.dot(lhs, _slot_tiles(tab_ref, row, j), preferred_element_type=F32)
            acc = acc + (out[:SUBLANES] + out[SUBLANES:])
        o_ref[t] = x1_ref[t] + g2_ref[0] * _from_tile_rows(acc)

    _for_each_token(e_ref, idx_ref, sem, tb, token)


def _peer_v(e4, coef, x1_3, g2_3, tab, T, tb=128):
    N = x1_3.shape[0]
    assert tb % (RING_SLOTS * SLOT_TOKENS) == 0 and tb % SUBLANES == 0
    return pl.pallas_call(
        functools.partial(_peer_v_kernel, tb=tb),
        grid=(N // tb,),
        in_specs=[
            pl.BlockSpec((tb // SLOT_TOKENS, SLOT_TOKENS, PEER_SLOTS), lambda i: (i, 0, 0)),
            pl.BlockSpec((tb, PEER_SLOTS), lambda i: (i, 0)),
            pl.BlockSpec((tb, SUBLANES, LANES), lambda i: (i, 0, 0)),
            pl.BlockSpec((1, SUBLANES, LANES), lambda i: ((i * tb) // T, 0, 0)),
            pl.BlockSpec(tab.shape, lambda i: (0, 0), pipeline_mode=pl.Buffered(1)),
            pl.BlockSpec((SLOT_GROUPS, LANES, LANES), lambda i: (0, 0, 0)),
        ],
        out_specs=pl.BlockSpec((tb, SUBLANES, LANES), lambda i: (i, 0, 0)),
        out_shape=jax.ShapeDtypeStruct((N, SUBLANES, LANES), F32),
        scratch_shapes=[pltpu.VMEM((tb * SLOT_GROUPS, LANES), F32),
                        pltpu.VMEM((tb * SLOT_GROUPS, LANES), F32),
                        pltpu.SMEM((RING_SLOTS, SLOT_TOKENS, PEER_SLOTS), I32),
                        pltpu.SemaphoreType.DMA((RING_SLOTS,))],
        compiler_params=_params(("arbitrary",), VMEM_LIMIT_BYTES),
        name="peer_v",
    )(e4.reshape(N // SLOT_TOKENS, SLOT_TOKENS, PEER_SLOTS), coef, x1_3, g2_3, tab,
      jnp.asarray(_spread_matrices(), BF16))


def kernel(x, c, w_ada, b_ada, norm1_g, w_in, gla_gate_w2, gla_gate_b, gla_norm_g, swa_qnorm_g,
           swa_knorm_g, swa_sinks, rel_bias, w_up_a, w_up_b, w_out, norm2_g, peer_wq, peer_subkeys,
           peer_u, peer_v):
    B, T, D = x.shape
    N = B * T
    L = w_ada.shape[0]
    mod = _adaln(c, w_ada, b_ada)
    weff_t = _fold_peer_keys(peer_wq, peer_subkeys)
    bias = _swa_bias(rel_bias)
    xf = x.reshape(N, D)
    for l in range(L):
        sh1, sc1, g1, sh2, sc2, g2 = [mod[l, :, i * D:(i + 1) * D] for i in range(6)]
        proj = _in_proj(xf, norm1_g[l], sc1, sh1, _pack_w_in(w_in, l), T)
        proj3 = proj.reshape(B, T, PROJ_W)
        w2p = jnp.zeros((LANES, GLA_HEADS * GLA_DK), BF16).at[:GLA_RANK].set(gla_gate_w2[l].astype(BF16))
        gla_o = _gla(proj3, w2p, gla_gate_b[l].reshape(1, -1), gla_norm_g[l].reshape(1, -1))
        swa_o = _swa(proj3, bias, swa_qnorm_g[l].reshape(1, -1), swa_knorm_g[l].reshape(1, -1),
                     swa_sinks[l])
        x1, h2, s_t = _merge(xf, proj, gla_o.reshape(N, -1), swa_o.reshape(N, -1),
                             w_up_a[l].astype(BF16), w_up_b[l].astype(BF16), w_out[l].astype(BF16),
                             g1, norm2_g[l], sc2, sh2, weff_t[l], T)
        e4, gates = _topk(s_t)
        coef = _peer_u(e4, h2.reshape(N, SUBLANES, LANES), gates, _table(peer_u, l))
        g2_3 = g2.reshape(B, SUBLANES, LANES)
        xf = _peer_v(e4, coef, x1.reshape(N, SUBLANES, LANES), g2_3, _table(peer_v, l), T).reshape(N, D)
    return xf.reshape(B, T, D)
```

```python
import functools

import numpy as np
import jax
import jax.numpy as jnp
from jax import lax
from jax.experimental import pallas as pl
from jax.experimental.pallas import tpu as pltpu

F32 = jnp.float32
BF16 = jnp.bfloat16
I32 = jnp.int32
HIGHEST = lax.Precision.HIGHEST
EPS = 1e-6

GLA_HEADS = 4
GLA_DK = 128
GLA_DV = 256
GLA_RANK = 16
GLA_TAU = 16.0
GLA_CHUNK = 64
SWA_HEADS = 16
SWA_KV_HEADS = 2
SWA_HD = 64
SWA_BLOCK = 128
N_BUCKETS = 32
MAX_DISTANCE = 128
PEER_HEADS = 8
PEER_NKEYS = 128
PEER_TOPK = 16
PEER_SLOTS = PEER_HEADS * PEER_TOPK

SUBLANES = 8
LANES = 128
VMEM_LIMIT_BYTES = 56 * 1024 * 1024

NT_DIMS = (((1,), (1,)), ((), ()))
TN_DIMS = (((0,), (0,)), ((), ()))

COL_Q, COL_K, COL_V, COL_GR, COL_SQ, COL_GA, COL_GB = 0, 512, 1024, 2048, 3072, 4096, 5120
COL_SK, COL_SV, COL_GLR = 6144, 6272, 6400
PROJ_W = 6528


def _params(sem, vmem=None):
    return pltpu.CompilerParams(dimension_semantics=sem, vmem_limit_bytes=vmem)


def _adaln_kernel(c_ref, w_ref, b_ref, o_ref):
    c = c_ref[...]
    a = c * jax.nn.sigmoid(c)
    o_ref[0] = jnp.dot(a, w_ref[0], preferred_element_type=F32, precision=HIGHEST) + b_ref[0]


def _adaln(c, w_ada, b_ada):
    L, D, W = w_ada.shape
    B = c.shape[0]
    rows = -(-B // SUBLANES) * SUBLANES
    cp = jnp.zeros((rows, D), F32).at[:B].set(c)
    tn = W // 4
    out = pl.pallas_call(
        _adaln_kernel,
        grid=(L, W // tn),
        in_specs=[
            pl.BlockSpec((rows, D), lambda l, j: (0, 0)),
            pl.BlockSpec((1, D, tn), lambda l, j: (l, 0, j)),
            pl.BlockSpec((1, 1, tn), lambda l, j: (l, 0, j)),
        ],
        out_specs=pl.BlockSpec((1, rows, tn), lambda l, j: (l, 0, j)),
        out_shape=jax.ShapeDtypeStruct((L, rows, W), F32),
        compiler_params=_params(("arbitrary", "arbitrary")),
        name="adaln",
    )(cp, w_ada, b_ada.reshape(L, 1, W))
    return out[:, :B]


def _fold_kernel(sk_ref, wq_ref, o_ref, rows_ref):
    half = sk_ref.shape[-1]
    groups = 2 * PEER_HEADS
    for h in range(PEER_HEADS):
        for p in range(2):
            g = 2 * h + p
            res = lax.dot_general(sk_ref[0, p], wq_ref[0, :, g * half:(g + 1) * half], NT_DIMS,
                                  precision=HIGHEST, preferred_element_type=F32)
            for cb in range(rows_ref.shape[0]):
                rows_ref[cb, pl.ds(p * PEER_HEADS + h, PEER_NKEYS, stride=groups), :] = (
                    res[:, cb * LANES:(cb + 1) * LANES])
    for cb in range(rows_ref.shape[0]):
        o_ref[0, :, cb * LANES:(cb + 1) * LANES] = rows_ref[cb].astype(BF16)


def _fold_peer_keys(peer_wq, peer_subkeys):
    L, D, QW = peer_wq.shape
    half = peer_subkeys.shape[-1]
    rows = (QW // half) * PEER_NKEYS
    return pl.pallas_call(
        _fold_kernel,
        grid=(L,),
        in_specs=[
            pl.BlockSpec((1, 2, PEER_NKEYS, half), lambda l: (l, 0, 0, 0)),
            pl.BlockSpec((1, D, QW), lambda l: (l, 0, 0)),
        ],
        out_specs=pl.BlockSpec((1, rows, D), lambda l: (l, 0, 0)),
        out_shape=jax.ShapeDtypeStruct((L, rows, D), BF16),
        scratch_shapes=[pltpu.VMEM((D // LANES, rows, LANES), F32)],
        compiler_params=_params(("arbitrary",), VMEM_LIMIT_BYTES),
        name="peer_fold",
    )(peer_subkeys, peer_wq)


def _inproj_kernel(x_ref, g_ref, sc_ref, sh_ref, w_ref, o_ref):
    x = x_ref[...]
    ms = jnp.mean(x * x, axis=-1, keepdims=True)
    y = x * lax.rsqrt(ms + EPS) * g_ref[...]
    h = y * (1.0 + sc_ref[0]) + sh_ref[0]
    o_ref[...] = jnp.dot(h.astype(BF16), w_ref[...], preferred_element_type=F32).astype(BF16)


def _in_proj(xf, g, sc, sh, wp, T, tm=512):
    N, D = xf.shape
    B = sc.shape[0]
    ncol = 3
    tn = PROJ_W // ncol
    return pl.pallas_call(
        _inproj_kernel,
        grid=(ncol, N // tm),
        in_specs=[
            pl.BlockSpec((tm, D), lambda j, i: (i, 0)),
            pl.BlockSpec((1, D), lambda j, i: (0, 0)),
            pl.BlockSpec((1, 1, D), lambda j, i: ((i * tm) // T, 0, 0)),
            pl.BlockSpec((1, 1, D), lambda j, i: ((i * tm) // T, 0, 0)),
            pl.BlockSpec((D, tn), lambda j, i: (0, j)),
        ],
        out_specs=pl.BlockSpec((tm, tn), lambda j, i: (i, j)),
        out_shape=jax.ShapeDtypeStruct((N, PROJ_W), BF16),
        compiler_params=_params(("arbitrary", "arbitrary")),
        name="in_proj",
    )(xf, g.reshape(1, D), sc.reshape(B, 1, D), sh.reshape(B, 1, D), wp)


_SRC = dict(zip(("q", "k", "v", "glr", "gr", "sq", "sk", "sv", "ga", "gb"),
                np.cumsum([0, 512, 512, 1024, GLA_RANK, 1024, 1024, 128, 128, 1024])))
W_IN_SLABS = ((COL_Q, _SRC["q"], 512), (COL_K, _SRC["k"], 512), (COL_V, _SRC["v"], 1024),
              (COL_GR, _SRC["gr"], 1024), (COL_SQ, _SRC["sq"], 1024), (COL_GA, _SRC["ga"], 1024),
              (COL_GB, _SRC["gb"], 1024), (COL_SK, _SRC["sk"], 128), (COL_SV, _SRC["sv"], 128),
              (COL_GLR, _SRC["glr"], GLA_RANK))


def _pack_w_in_kernel(w_ref, o_ref):
    o_ref[:, COL_GLR:] = jnp.zeros((o_ref.shape[0], PROJ_W - COL_GLR), BF16)
    for dst, src, width in W_IN_SLABS:
        o_ref[:, dst:dst + width] = w_ref[:, int(src):int(src) + width].astype(BF16)


def _pack_w_in(w_in, layer, tr=128):
    _, rows, cols = w_in.shape
    return pl.pallas_call(
        _pack_w_in_kernel,
        grid=(rows // tr,),
        in_specs=[pl.BlockSpec((None, tr, cols), lambda i: (layer, i, 0))],
        out_specs=pl.BlockSpec((tr, PROJ_W), lambda i: (i, 0)),
        out_shape=jax.ShapeDtypeStruct((rows, PROJ_W), BF16),
        compiler_params=_params(("arbitrary",)),
        name="pack_w_in",
    )(w_in)


def _gla_kernel(q_ref, k_ref, v_ref, r_ref, glr_ref, w2_ref, b2_ref, ng_ref, o_ref, st_ref, *, nchunk):
    @pl.when(pl.program_id(1) == 0)
    def _():
        st_ref[...] = jnp.zeros_like(st_ref)

    C = GLA_CHUNK
    dk, dv = GLA_DK, GLA_DV
    row = lax.broadcasted_iota(I32, (C, C), 0)
    col = lax.broadcasted_iota(I32, (C, C), 1)
    tri = col <= row
    tri_b = tri.astype(F32).astype(BF16)
    w2 = w2_ref[...]
    b2 = b2_ref[...]
    ng = ng_ref[...]
    for ci in range(nchunk):
        sl = pl.ds(ci * C, C)
        z = jnp.dot(glr_ref[sl, :], w2, preferred_element_type=F32) + b2
        log_a = (jnp.minimum(z, 0.0) - jnp.log(1.0 + jnp.exp(-jnp.abs(z)))) * (1.0 / GLA_TAU)
        la_hi = log_a.astype(BF16)
        la_lo = (log_a - la_hi.astype(F32)).astype(BF16)
        b = (jnp.dot(tri_b, la_hi, preferred_element_type=F32)
             + jnp.dot(tri_b, la_lo, preferred_element_type=F32))
        b_last = b[C - 1:C, :]
        q = q_ref[sl, :].astype(F32) * (dk ** -0.5)
        k = k_ref[sl, :].astype(F32)
        q_dec = (q * jnp.exp(b)).astype(BF16)
        k_inv = (k * jnp.exp(-b)).astype(BF16)
        k_tail = (k * jnp.exp(b_last - b)).astype(BF16)
        decay = jnp.exp(b_last)
        for h in range(GLA_HEADS):
            kc = slice(h * dk, (h + 1) * dk)
            vc = slice(h * dv, (h + 1) * dv)
            v = v_ref[sl, vc]
            attn = lax.dot_general(q_dec[:, kc], k_inv[:, kc], NT_DIMS, preferred_element_type=F32)
            attn = jnp.where(tri, attn, 0.0).astype(BF16)
            st = st_ref[h]
            o = (jnp.dot(attn, v, preferred_element_type=F32)
                 + lax.dot_general(q_dec[:, kc], st.astype(BF16), NT_DIMS, preferred_element_type=F32))
            st_ref[h] = st * decay[:, kc] + lax.dot_general(v, k_tail[:, kc], TN_DIMS,
                                                            preferred_element_type=F32)
            on = o * lax.rsqrt(jnp.mean(o * o, axis=-1, keepdims=True) + EPS) * ng
            r = r_ref[sl, vc].astype(F32)
            o_ref[sl, vc] = (on * (r * jax.nn.sigmoid(r))).astype(BF16)


def _gla(proj, w2p, b2, ng, tc=256):
    B, T, _ = proj.shape
    H = GLA_HEADS
    kw, vw = H * GLA_DK, H * GLA_DV
    return pl.pallas_call(
        functools.partial(_gla_kernel, nchunk=tc // GLA_CHUNK),
        grid=(B, T // tc),
        in_specs=[
            pl.BlockSpec((None, tc, kw), lambda b, c: (b, c, COL_Q // kw)),
            pl.BlockSpec((None, tc, kw), lambda b, c: (b, c, COL_K // kw)),
            pl.BlockSpec((None, tc, vw), lambda b, c: (b, c, COL_V // vw)),
            pl.BlockSpec((None, tc, vw), lambda b, c: (b, c, COL_GR // vw)),
            pl.BlockSpec((None, tc, LANES), lambda b, c: (b, c, COL_GLR // LANES)),
            pl.BlockSpec((LANES, kw), lambda b, c: (0, 0)),
            pl.BlockSpec((1, kw), lambda b, c: (0, 0)),
            pl.BlockSpec((1, GLA_DV), lambda b, c: (0, 0)),
        ],
        out_specs=pl.BlockSpec((None, tc, vw), lambda b, c: (b, c, 0)),
        out_shape=jax.ShapeDtypeStruct((B, T, vw), BF16),
        scratch_shapes=[pltpu.VMEM((H, GLA_DV, GLA_DK), F32)],
        compiler_params=_params(("arbitrary", "arbitrary")),
        name="gla",
    )(proj, proj, proj, proj, proj, w2p, b2, ng)


def _t5_bucket(dist):
    max_exact = N_BUCKETS // 2
    d = np.maximum(dist, 1).astype(np.float32)
    large = max_exact + (np.log(d / max_exact) / np.log(MAX_DISTANCE / max_exact)
                         * (N_BUCKETS - max_exact)).astype(np.int32)
    large = np.minimum(large, N_BUCKETS - 1)
    return np.where(dist < max_exact, dist, large).astype(np.int32)


NEG_BIG = -1e30


def _swa_bias(rel_bias):
    blk = SWA_BLOCK
    qi = np.arange(blk)[:, None]
    sj = np.arange(2 * blk)[None, :]
    dist = blk + qi - sj
    band = (dist >= 0) & (dist < blk)
    bucket = _t5_bucket(np.clip(dist, 0, None))
    onehot = jnp.asarray(bucket.reshape(-1, 1) == np.arange(N_BUCKETS)[None, :], F32)
    bias = jnp.dot(onehot, rel_bias.astype(F32), precision=HIGHEST).T.reshape(-1, blk, 2 * blk)
    masks = np.stack([band, band & (sj >= blk)])[:, None]
    return jnp.where(jnp.asarray(masks), bias[None], NEG_BIG)


def _segment_sums(x, seg):
    hi = x.astype(BF16)
    lo = (x - hi.astype(F32)).astype(BF16)
    return (jnp.dot(hi, seg, preferred_element_type=F32) + jnp.dot(lo, seg, preferred_element_type=F32))


def _head_rms_scale(x, seg, seg_t):
    inv = lax.rsqrt(_segment_sums(x * x, seg) * (1.0 / SWA_HD) + EPS)
    return _segment_sums(inv, seg_t)


SWA_ROWS = 32


def _swa_kernel(sink_ref, q_ref, kp_ref, kc_ref, vp_ref, vc_ref, bias_ref, qg_ref, kg_ref, seg_ref,
                segt_ref, o_ref):
    blk = SWA_BLOCK
    hd = SWA_HD
    group = SWA_HEADS // SWA_KV_HEADS
    kvw = SWA_KV_HEADS * hd
    seg = seg_ref[...]
    seg_t = segt_ref[...]
    q = q_ref[...].astype(F32)
    qn = (q * _head_rms_scale(q, seg, seg_t) * qg_ref[...] * (hd ** -0.5)).astype(BF16)
    k2 = jnp.concatenate([kp_ref[...], kc_ref[...]], axis=0).astype(F32)
    kn = (k2 * _head_rms_scale(k2, seg[:kvw], seg_t[:, :kvw]) * kg_ref[...]).astype(BF16)
    v2 = jnp.concatenate([vp_ref[...], vc_ref[...]], axis=0)
    lane = lax.broadcasted_iota(I32, (2 * blk, kvw), 1)
    outs = []
    for kh in range(SWA_KV_HEADS):
        kk = kn[:, kh * hd:(kh + 1) * hd]
        vsh = v2 if kh == 0 else jnp.concatenate([v2[:, kh * hd:], v2[:, :kh * hd]], axis=1)
        vv = jnp.where(lane < hd, vsh, jnp.ones_like(vsh))
        qs = jnp.concatenate([qn[:, (kh * group + g) * hd:(kh * group + g + 1) * hd]
                              for g in range(group)], axis=0)
        logits = lax.dot_general(qs, kk, NT_DIMS, preferred_element_type=F32)
        logits = logits + bias_ref[kh * group:(kh + 1) * group].reshape(group * blk, 2 * blk)
        for g in range(group):
            sink = sink_ref[kh * group + g]
            parts = []
            for r0 in range(0, blk, SWA_ROWS):
                lg = logits[g * blk + r0:g * blk + r0 + SWA_ROWS]
                m = jnp.maximum(jnp.max(lg, axis=-1, keepdims=True), sink)
                p = jnp.exp(lg - m).astype(BF16)
                pv = jnp.dot(p, vv, preferred_element_type=F32)
                parts.append(pv[:, :hd] / (pv[:, hd:hd + 1] + jnp.exp(sink - m)))
            outs.append(jnp.concatenate(parts, axis=0))
    o_ref[...] = jnp.concatenate(outs, axis=-1).astype(BF16)


def _swa(proj, bias, qg, kg, sinks):
    B, T, _ = proj.shape
    blk = SWA_BLOCK
    qw = SWA_HEADS * SWA_HD
    kvw = SWA_KV_HEADS * SWA_HD
    prev = lambda b, i: jnp.maximum(i - 1, 0)
    seg = (np.arange(qw)[:, None] // SWA_HD == np.arange(LANES)[None, :])
    seg = jnp.asarray(seg, BF16)
    return pl.pallas_call(
        _swa_kernel,
        grid=(B, T // blk),
        in_specs=[
            pl.BlockSpec(memory_space=pltpu.SMEM),
            pl.BlockSpec((None, blk, qw), lambda b, i: (b, i, COL_SQ // qw)),
            pl.BlockSpec((None, blk, kvw), lambda b, i: (b, prev(b, i), COL_SK // kvw)),
            pl.BlockSpec((None, blk, kvw), lambda b, i: (b, i, COL_SK // kvw)),
            pl.BlockSpec((None, blk, kvw), lambda b, i: (b, prev(b, i), COL_SV // kvw)),
            pl.BlockSpec((None, blk, kvw), lambda b, i: (b, i, COL_SV // kvw)),
            pl.BlockSpec((None, SWA_HEADS, blk, 2 * blk), lambda b, i: (jnp.where(i == 0, 1, 0), 0, 0, 0)),
            pl.BlockSpec((1, qw), lambda b, i: (0, 0)),
            pl.BlockSpec((1, kvw), lambda b, i: (0, 0)),
            pl.BlockSpec((qw, LANES), lambda b, i: (0, 0)),
            pl.BlockSpec((LANES, qw), lambda b, i: (0, 0)),
        ],
        out_specs=pl.BlockSpec((None, blk, qw), lambda b, i: (b, i, 0)),
        out_shape=jax.ShapeDtypeStruct((B, T, qw), BF16),
        compiler_params=_params(("arbitrary", "arbitrary")),
        name="swa",
    )(sinks, proj, proj, proj, proj, proj, bias, jnp.tile(qg, (1, SWA_HEADS)),
      jnp.tile(kg, (1, SWA_KV_HEADS)), seg, seg.T)


def _store_token_tiles(ref, x):
    tokens = x.shape[0]
    for q in range(SUBLANES):
        ref[pl.ds(q, tokens, stride=SUBLANES), :] = x[:, q * LANES:(q + 1) * LANES]


def _load_token_tiles(ref, tokens):
    return jnp.concatenate([ref[pl.ds(q, tokens, stride=SUBLANES), :] for q in range(SUBLANES)], axis=1)


def _merge_kernel(x_ref, ga_ref, gb_ref, go_ref, so_ref, wa_ref, wb_ref, wo_ref, g1_ref,
                  n2_ref, sc2_ref, sh2_ref, we_ref, x1_ref, h2_ref, st_ref):
    ya = jnp.dot(go_ref[...], wa_ref[...], preferred_element_type=F32)
    yb = jnp.dot(so_ref[...], wb_ref[...], preferred_element_type=F32)
    m = jax.nn.sigmoid(ga_ref[...].astype(F32)) * ya + jax.nn.sigmoid(gb_ref[...].astype(F32)) * yb
    mixed = jnp.dot(m.astype(BF16), wo_ref[...], preferred_element_type=F32)
    x1 = x_ref[...] + g1_ref[0] * mixed
    y = x1 * lax.rsqrt(jnp.mean(x1 * x1, axis=-1, keepdims=True) + EPS) * n2_ref[...]
    h2 = y * (1.0 + sc2_ref[0]) + sh2_ref[0]
    _store_token_tiles(x1_ref, x1)
    _store_token_tiles(h2_ref, h2)
    st_ref[...] = lax.dot_general(we_ref[...], h2.astype(BF16), NT_DIMS, preferred_element_type=F32)


def _merge(xf, proj2, gla_o, swa_o, wa, wb, wo, g1, n2, sc2, sh2, weff_t, T, tm=256):
    N, D = xf.shape
    B = g1.shape[0]
    SW = weff_t.shape[0]
    bat = lambda i: ((i * tm) // T, 0, 0)
    full = lambda i: (0, 0)
    return pl.pallas_call(
        _merge_kernel,
        grid=(N // tm,),
        in_specs=[
            pl.BlockSpec((tm, D), lambda i: (i, 0)),
            pl.BlockSpec((tm, D), lambda i: (i, COL_GA // D)),
            pl.BlockSpec((tm, D), lambda i: (i, COL_GB // D)),
            pl.BlockSpec((tm, D), lambda i: (i, 0)),
            pl.BlockSpec((tm, D), lambda i: (i, 0)),
            pl.BlockSpec((D, D), full),
            pl.BlockSpec((D, D), full),
            pl.BlockSpec((D, D), full),
            pl.BlockSpec((1, 1, D), bat),
            pl.BlockSpec((1, D), full),
            pl.BlockSpec((1, 1, D), bat),
            pl.BlockSpec((1, 1, D), bat),
            pl.BlockSpec((SW, D), full),
        ],
        out_specs=[
            pl.BlockSpec((tm * SUBLANES, LANES), lambda i: (i, 0)),
            pl.BlockSpec((tm * SUBLANES, LANES), lambda i: (i, 0)),
            pl.BlockSpec((SW, tm), lambda i: (0, i)),
        ],
        out_shape=[
            jax.ShapeDtypeStruct((N * SUBLANES, LANES), F32),
            jax.ShapeDtypeStruct((N * SUBLANES, LANES), F32),
            jax.ShapeDtypeStruct((SW, N), F32),
        ],
        compiler_params=_params(("arbitrary",), VMEM_LIMIT_BYTES),
        name="merge",
    )(xf, proj2, proj2, gla_o, swa_o, wa, wb, wo, g1.reshape(B, 1, D), n2.reshape(1, D),
      sc2.reshape(B, 1, D), sh2.reshape(B, 1, D), weff_t)


def _tree(op, xs):
    xs = list(xs)
    while len(xs) > 1:
        xs = [op(xs[i], xs[i + 1]) for i in range(0, len(xs) - 1, 2)] + ([xs[-1]] if len(xs) % 2 else [])
    return xs[0]


EXTRACT_CHAINS = 4


def _extract_best(ref, n):
    lanes = min(EXTRACT_CHAINS, n)
    best = [ref[i] for i in range(lanes)]
    for i in range(lanes, n):
        best[i % lanes] = jnp.maximum(best[i % lanes], ref[i])
    m = _tree(jnp.maximum, best)
    first = [jnp.where(ref[i] == m, i, n) for i in range(lanes)]
    for i in range(lanes, n):
        first[i % lanes] = jnp.minimum(first[i % lanes], jnp.where(ref[i] == m, i, n))
    pos = _tree(jnp.minimum, first)
    for i in range(n):
        ref[i] = jnp.where(pos == i, -jnp.inf, ref[i])
    return m, pos


def _ordered(a, b):
    (va, ia), (vb, ib) = a, b
    keep = jnp.logical_or(va > vb, jnp.logical_and(va == vb, ia < ib))
    first = (jnp.where(keep, va, vb), jnp.where(keep, ia, ib))
    second = (jnp.where(keep, vb, va), jnp.where(keep, ib, ia))
    return first, second


def _bitonic_merge(xs):
    xs = list(xs)
    j = len(xs) // 2
    while j >= 1:
        for i in range(len(xs)):
            if i & j == 0:
                xs[i], xs[i | j] = _ordered(xs[i], xs[i | j])
        j //= 2
    return xs


def _bitonic_sort(xs):
    xs = list(xs)
    k = 2
    while k <= len(xs):
        j = k // 2
        while j >= 1:
            for i in range(len(xs)):
                if i & j == 0:
                    first, second = _ordered(xs[i], xs[i | j])
                    xs[i], xs[i | j] = (first, second) if i & k == 0 else (second, first)
            j //= 2
        k *= 2
    return xs


def _merge_best(xs, ys):
    n = len(xs)
    return _bitonic_merge([_ordered(xs[i], ys[n - 1 - i])[0] for i in range(n)])


TOPK_PAIRS = tuple((a, b) for a in range(PEER_TOPK) for b in range(PEER_TOPK)
                   if (a + 1) * (b + 1) <= PEER_TOPK)


def _topk_kernel(s_ref, e_ref, g_ref, sv_ref, si_ref, v_ref, i_ref, cand_ref, cidx_ref, best_ref, row_ref):
    K = PEER_TOPK
    nk = PEER_NKEYS
    H = PEER_HEADS
    G = 2 * H
    tm = s_ref.shape[1]
    nblocks = nk // K
    for hb in range(2):
        rows = slice(hb * H, (hb + 1) * H)
        for blk in range(nblocks):
            keys = [(s_ref[(blk * K + i) * G + hb * H:(blk * K + i) * G + (hb + 1) * H, :],
                     jnp.full((H, tm), blk * K + i, I32)) for i in range(K)]
            for i, (v, ix) in enumerate(_bitonic_sort(keys)):
                sv_ref[blk, i, rows, :] = v
                si_ref[blk, i, rows, :] = ix
        step = 1
        while step < nblocks:
            for blk in range(0, nblocks, 2 * step):
                xs = [(sv_ref[blk, i, rows, :], si_ref[blk, i, rows, :]) for i in range(K)]
                ys = [(sv_ref[blk + step, i, rows, :], si_ref[blk + step, i, rows, :]) for i in range(K)]
                for i, (v, ix) in enumerate(_merge_best(xs, ys)):
                    sv_ref[blk, i, rows, :] = v
                    si_ref[blk, i, rows, :] = ix
            step *= 2
    v_ref[...] = sv_ref[0]
    i_ref[...] = si_ref[0]

    for ci, (a, b) in enumerate(TOPK_PAIRS):
        cand_ref[ci] = v_ref[a, 0:H, :] + v_ref[b, H:2 * H, :]
        cidx_ref[ci] = (i_ref[a, 0:H, :] * nk + i_ref[b, H:2 * H, :]) * WORDS_PER_EXPERT
    ncand = len(TOPK_PAIRS)

    def stage2(k, _):
        m, pos = _extract_best(cand_ref, ncand)
        row = _tree(jnp.maximum, [jnp.where(pos == ci, cidx_ref[ci], -1) for ci in range(ncand)])
        best_ref[k] = m
        row_ref[k] = row
        return 0

    lax.fori_loop(0, K, stage2, 0)
    best = best_ref[...]
    ex = jnp.exp(best - best[0:1])
    gates = ex / jnp.sum(ex, axis=0, keepdims=True)
    g_ref[...] = gates.reshape(K * H, tm).T
    e_ref[...] = row_ref[...].reshape(K * H, tm).astype(F32).T.astype(I32)


def _topk(s_t, tm=128):
    SW, N = s_t.shape
    K = PEER_TOPK
    G = 2 * PEER_HEADS
    return pl.pallas_call(
        _topk_kernel,
        grid=(N // tm,),
        in_specs=[pl.BlockSpec((SW, tm), lambda i: (0, i))],
        out_specs=[
            pl.BlockSpec((tm, PEER_SLOTS), lambda i: (i, 0)),
            pl.BlockSpec((tm, PEER_SLOTS), lambda i: (i, 0)),
        ],
        out_shape=[
            jax.ShapeDtypeStruct((N, PEER_SLOTS), I32),
            jax.ShapeDtypeStruct((N, PEER_SLOTS), F32),
        ],
        scratch_shapes=[pltpu.VMEM((PEER_NKEYS // K, K, G, tm), F32),
                        pltpu.VMEM((PEER_NKEYS // K, K, G, tm), I32),
                        pltpu.VMEM((K, G, tm), F32),
                        pltpu.VMEM((K, G, tm), I32),
                        pltpu.VMEM((len(TOPK_PAIRS), PEER_HEADS, tm), F32),
                        pltpu.VMEM((len(TOPK_PAIRS), PEER_HEADS, tm), I32),
                        pltpu.VMEM((K, PEER_HEADS, tm), F32),
                        pltpu.VMEM((K, PEER_HEADS, tm), I32)],
        compiler_params=_params(("arbitrary",)),
        name="peer_topk",
    )(s_t)


WORDS_PER_EXPERT = SUBLANES // 2


TILE_BLOCK = tuple((r % 2) * WORDS_PER_EXPERT + r // 2 for r in range(SUBLANES))
BLOCK_ROW = tuple(TILE_BLOCK.index(q) for q in range(SUBLANES))


def _bf16_bits(x):
    return lax.bitcast_convert_type(x.astype(BF16).astype(F32), jnp.uint32)


def _table_kernel(t_ref, o_ref):
    rows = t_ref.shape[0]
    for s in range(WORDS_PER_EXPERT):
        lo = _bf16_bits(t_ref[:, s * LANES:(s + 1) * LANES])
        hi = _bf16_bits(t_ref[:, (s + WORDS_PER_EXPERT) * LANES:(s + WORDS_PER_EXPERT + 1) * LANES])
        word = lax.shift_right_logical(lo, jnp.uint32(16)) | (hi & jnp.uint32(0xFFFF0000))
        o_ref[pl.ds(s, rows, stride=WORDS_PER_EXPERT), :] = lax.bitcast_convert_type(word, I32)


def _table(tables, layer, te=2048):
    _, n, d = tables.shape
    return pl.pallas_call(
        _table_kernel,
        grid=(n // te,),
        in_specs=[pl.BlockSpec((None, te, d), lambda i: (layer, i, 0))],
        out_specs=pl.BlockSpec((te * WORDS_PER_EXPERT, LANES), lambda i: (i, 0)),
        out_shape=jax.ShapeDtypeStruct((n * WORDS_PER_EXPERT, LANES), I32),
        compiler_params=_params(("arbitrary",)),
        name="peer_table",
    )(tables)


def _expert_tile(tab_ref, e4):
    words = tab_ref[pl.ds(pl.multiple_of(e4, WORDS_PER_EXPERT), WORDS_PER_EXPERT), :]
    return pltpu.bitcast(words, BF16)


def _to_tile_rows(x):
    return jnp.concatenate([x[q:q + 1, :] for q in TILE_BLOCK], axis=0)


def _from_tile_rows(x):
    return jnp.concatenate([x[r:r + 1, :] for r in BLOCK_ROW], axis=0)


def _gelu_tanh(x):
    return 0.5 * x * (1.0 + jnp.tanh(np.sqrt(2.0 / np.pi) * (x + 0.044715 * (x * x * x))))


RING_SLOTS = 2
SLOT_TOKENS = 32


def _for_each_token(e_ref, idx_ref, sem, tb, token_body):
    ngroups = tb // SLOT_TOKENS

    def fetch(q, slot):
        return pltpu.make_async_copy(e_ref.at[q], idx_ref.at[slot], sem.at[slot])

    for slot in range(RING_SLOTS):
        fetch(slot, slot).start()

    def ring(j, _):
        for slot in range(RING_SLOTS):
            q = j * RING_SLOTS + slot
            fetch(q, slot).wait()
            for tt in range(SLOT_TOKENS):
                token_body(q * SLOT_TOKENS + tt, lambda k, slot=slot, tt=tt: idx_ref[slot, tt, k])
            fetch(jnp.minimum(q + RING_SLOTS, ngroups - 1), slot).start()
        return 0

    lax.fori_loop(0, ngroups // RING_SLOTS, ring, 0)
    for slot in range(RING_SLOTS):
        fetch(ngroups - 1, slot).wait()


PAIR_GROUP = LANES // SUBLANES
SLOT_GROUPS = PEER_SLOTS // PAIR_GROUP


def _slot_tiles(tab_ref, row, j):
    return jnp.concatenate([_expert_tile(tab_ref, row(j * PAIR_GROUP + kk)) for kk in range(PAIR_GROUP)],
                           axis=0)


def _spread_matrices():
    lane = np.arange(LANES)
    return np.stack([lane[:, None] == PAIR_GROUP * j + lane[None, :] // SUBLANES
                     for j in range(SLOT_GROUPS)])


def _peer_u_kernel(e_ref, h_ref, g_ref, tab_ref, sel_ref, o_ref, slab_ref, idx_ref, sem, *, tb):
    groups_per_token = PEER_SLOTS // SUBLANES
    sel = sel_ref[...]

    def token(t, row):
        hb = _to_tile_rows(h_ref[t]).astype(BF16)
        for j in range(groups_per_token):
            prods = jnp.concatenate([_expert_tile(tab_ref, row(j * SUBLANES + qn)) * hb
                                     for qn in range(SUBLANES)], axis=0)
            dst = pl.multiple_of(t * PEER_SLOTS + j * SUBLANES, SUBLANES)
            slab_ref[pl.ds(dst, SUBLANES), :] = jnp.dot(sel, prods, preferred_element_type=F32)

    _for_each_token(e_ref, idx_ref, sem, tb, token)
    rio = lax.broadcasted_iota(I32, (LANES, LANES), 0)
    lio = lax.broadcasted_iota(I32, (LANES, LANES), 1)
    diag = (rio == lio)[None]
    rows = SUBLANES * PEER_SLOTS
    for gi in range(tb // SUBLANES):
        tok = slice(gi * SUBLANES, (gi + 1) * SUBLANES)
        rs = jnp.sum(slab_ref[gi * rows:(gi + 1) * rows, :], axis=-1, keepdims=True)
        rs = rs.reshape(SUBLANES, PEER_SLOTS, 1)
        a = jnp.sum(jnp.where(diag, rs, 0.0), axis=1)
        o_ref[tok, :] = g_ref[tok, :] * _gelu_tanh(a)


def _peer_u(e4, h3, gates, tab, tb=128):
    N = h3.shape[0]
    assert tb % (RING_SLOTS * SLOT_TOKENS) == 0 and tb % SUBLANES == 0
    sel = np.arange(SUBLANES)[:, None] == np.arange(SUBLANES * SUBLANES)[None, :] // SUBLANES
    return pl.pallas_call(
        functools.partial(_peer_u_kernel, tb=tb),
        grid=(N // tb,),
        in_specs=[
            pl.BlockSpec((tb // SLOT_TOKENS, SLOT_TOKENS, PEER_SLOTS), lambda i: (i, 0, 0)),
            pl.BlockSpec((tb, SUBLANES, LANES), lambda i: (i, 0, 0)),
            pl.BlockSpec((tb, PEER_SLOTS), lambda i: (i, 0)),
            pl.BlockSpec(tab.shape, lambda i: (0, 0), pipeline_mode=pl.Buffered(1)),
            pl.BlockSpec(sel.shape, lambda i: (0, 0)),
        ],
        out_specs=pl.BlockSpec((tb, PEER_SLOTS), lambda i: (i, 0)),
        out_shape=jax.ShapeDtypeStruct((N, PEER_SLOTS), F32),
        scratch_shapes=[pltpu.VMEM((tb * PEER_SLOTS, LANES), F32),
                        pltpu.SMEM((RING_SLOTS, SLOT_TOKENS, PEER_SLOTS), I32),
                        pltpu.SemaphoreType.DMA((RING_SLOTS,))],
        compiler_params=_params(("arbitrary",), VMEM_LIMIT_BYTES),
        name="peer_u",
    )(e4.reshape(N // SLOT_TOKENS, SLOT_TOKENS, PEER_SLOTS), h3, gates, tab, jnp.asarray(sel, BF16))


def _peer_v_kernel(e_ref, c_ref, x1_ref, g2_ref, tab_ref, spread_ref, o_ref, hi_ref, lo_ref, out_ref, idx_ref,
                   sem, *, tb):
    c = c_ref[...]
    c_hi = c.astype(BF16)
    c_lo = (c - c_hi.astype(F32)).astype(BF16)
    for j in range(SLOT_GROUPS):
        rows = pl.ds(j, tb, stride=SLOT_GROUPS)
        hi_ref[rows, :] = jnp.dot(c_hi, spread_ref[j], preferred_element_type=F32)
        lo_ref[rows, :] = jnp.dot(c_lo, spread_ref[j], preferred_element_type=F32)
    rio = lax.broadcasted_iota(I32, (SUBLANES, LANES), 0)
    lio = lax.broadcasted_iota(I32, (SUBLANES, LANES), 1)
    own_row = (lio & (SUBLANES - 1)) == rio

    def token(t, row):
        base = pl.multiple_of(t * SLOT_GROUPS, SLOT_GROUPS)
        hi = hi_ref[pl.ds(base, SLOT_GROUPS), :]
        lo = lo_ref[pl.ds(base, SLOT_GROUPS), :]
        acc = jnp.zeros((SUBLANES, LANES), F32)
        for j in range(SLOT_GROUPS):
            lhs = jnp.concatenate(
                [jnp.where(own_row, jnp.broadcast_to(part[j:j + 1, :], (SUBLANES, LANES)), 0.0).astype(BF16)
                 for part in (hi, lo)], axis=0)
            out = jnp.dot(lhs, _slot_tiles(tab_ref, row, j), preferred_element_type=F32)
            acc = acc + (out[:SUBLANES] + out[SUBLANES:])
        tile = pl.ds(pl.multiple_of(t * SUBLANES, SUBLANES), SUBLANES)
        out_ref[tile, :] = x1_ref[t] + g2_ref[0] * _from_tile_rows(acc)

    _for_each_token(e_ref, idx_ref, sem, tb, token)
    o_ref[...] = _load_token_tiles(out_ref, tb)


def _peer_v(e4, coef, x1_3, g2_3, tab, T, tb=128):
    N = x1_3.shape[0]
    assert tb % (RING_SLOTS * SLOT_TOKENS) == 0 and tb % SUBLANES == 0
    return pl.pallas_call(
        functools.partial(_peer_v_kernel, tb=tb),
        grid=(N // tb,),
        in_specs=[
            pl.BlockSpec((tb // SLOT_TOKENS, SLOT_TOKENS, PEER_SLOTS), lambda i: (i, 0, 0)),
            pl.BlockSpec((tb, PEER_SLOTS), lambda i: (i, 0)),
            pl.BlockSpec((tb, SUBLANES, LANES), lambda i: (i, 0, 0)),
            pl.BlockSpec((1, SUBLANES, LANES), lambda i: ((i * tb) // T, 0, 0)),
            pl.BlockSpec(tab.shape, lambda i: (0, 0), pipeline_mode=pl.Buffered(1)),
            pl.BlockSpec((SLOT_GROUPS, LANES, LANES), lambda i: (0, 0, 0)),
        ],
        out_specs=pl.BlockSpec((tb, SUBLANES * LANES), lambda i: (i, 0)),
        out_shape=jax.ShapeDtypeStruct((N, SUBLANES * LANES), F32),
        scratch_shapes=[pltpu.VMEM((tb * SLOT_GROUPS, LANES), F32),
                        pltpu.VMEM((tb * SLOT_GROUPS, LANES), F32),
                        pltpu.VMEM((tb * SUBLANES, LANES), F32),
                        pltpu.SMEM((RING_SLOTS, SLOT_TOKENS, PEER_SLOTS), I32),
                        pltpu.SemaphoreType.DMA((RING_SLOTS,))],
        compiler_params=_params(("arbitrary",), VMEM_LIMIT_BYTES),
        name="peer_v",
    )(e4.reshape(N // SLOT_TOKENS, SLOT_TOKENS, PEER_SLOTS), coef, x1_3, g2_3, tab,
      jnp.asarray(_spread_matrices(), BF16))


def kernel(x, c, w_ada, b_ada, norm1_g, w_in, gla_gate_w2, gla_gate_b, gla_norm_g, swa_qnorm_g,
           swa_knorm_g, swa_sinks, rel_bias, w_up_a, w_up_b, w_out, norm2_g, peer_wq, peer_subkeys,
           peer_u, peer_v):
    B, T, D = x.shape
    N = B * T
    L = w_ada.shape[0]
    mod = _adaln(c, w_ada, b_ada)
    weff_t = _fold_peer_keys(peer_wq, peer_subkeys)
    bias = _swa_bias(rel_bias)
    xf = x.reshape(N, D)
    for l in range(L):
        sh1, sc1, g1, sh2, sc2, g2 = [mod[l, :, i * D:(i + 1) * D] for i in range(6)]
        proj = _in_proj(xf, norm1_g[l], sc1, sh1, _pack_w_in(w_in, l), T)
        proj3 = proj.reshape(B, T, PROJ_W)
        w2p = jnp.zeros((LANES, GLA_HEADS * GLA_DK), BF16).at[:GLA_RANK].set(gla_gate_w2[l].astype(BF16))
        gla_o = _gla(proj3, w2p, gla_gate_b[l].reshape(1, -1), gla_norm_g[l].reshape(1, -1))
        swa_o = _swa(proj3, bias, swa_qnorm_g[l].reshape(1, -1), swa_knorm_g[l].reshape(1, -1),
                     swa_sinks[l])
        x1, h2, s_t = _merge(xf, proj, gla_o.reshape(N, -1), swa_o.reshape(N, -1),
                             w_up_a[l].astype(BF16), w_up_b[l].astype(BF16), w_out[l].astype(BF16),
                             g1, norm2_g[l], sc2, sh2, weff_t[l], T)
        e4, gates = _topk(s_t)
        coef = _peer_u(e4, h2.reshape(N, SUBLANES, LANES), gates, _table(peer_u, l))
        g2_3 = g2.reshape(B, SUBLANES, LANES)
        xf = _peer_v(e4, coef, x1.reshape(N, SUBLANES, LANES), g2_3, _table(peer_v, l), T)
    return xf.reshape(B, T, D)
```

```python
import functools

import numpy as np
import jax
import jax.numpy as jnp
from jax import lax
from jax.experimental import pallas as pl
from jax.experimental.pallas import tpu as pltpu

F32 = jnp.float32
BF16 = jnp.bfloat16
I32 = jnp.int32
HIGHEST = lax.Precision.HIGHEST
EPS = 1e-6

GLA_HEADS = 4
GLA_DK = 128
GLA_DV = 256
GLA_RANK = 16
GLA_TAU = 16.0
GLA_CHUNK = 64
SWA_HEADS = 16
SWA_KV_HEADS = 2
SWA_HD = 64
SWA_BLOCK = 128
N_BUCKETS = 32
MAX_DISTANCE = 128
PEER_HEADS = 8
PEER_NKEYS = 128
PEER_TOPK = 16
PEER_SLOTS = PEER_HEADS * PEER_TOPK

SUBLANES = 8
LANES = 128
VMEM_LIMIT_BYTES = 56 * 1024 * 1024

NT_DIMS = (((1,), (1,)), ((), ()))
TN_DIMS = (((0,), (0,)), ((), ()))

COL_Q, COL_K, COL_V, COL_GR, COL_SQ, COL_GA, COL_GB = 0, 512, 1024, 2048, 3072, 4096, 5120
COL_SK, COL_SV, COL_GLR = 6144, 6272, 6400
PROJ_W = 6528


def _params(sem, vmem=None):
    return pltpu.CompilerParams(dimension_semantics=sem, vmem_limit_bytes=vmem)


def _adaln_kernel(c_ref, w_ref, b_ref, o_ref):
    c = c_ref[...]
    a = c * jax.nn.sigmoid(c)
    o_ref[0] = jnp.dot(a, w_ref[0], preferred_element_type=F32, precision=HIGHEST) + b_ref[0]


def _adaln(c, w_ada, b_ada):
    L, D, W = w_ada.shape
    B = c.shape[0]
    rows = -(-B // SUBLANES) * SUBLANES
    cp = jnp.zeros((rows, D), F32).at[:B].set(c)
    tn = W // 4
    out = pl.pallas_call(
        _adaln_kernel,
        grid=(L, W // tn),
        in_specs=[
            pl.BlockSpec((rows, D), lambda l, j: (0, 0)),
            pl.BlockSpec((1, D, tn), lambda l, j: (l, 0, j)),
            pl.BlockSpec((1, 1, tn), lambda l, j: (l, 0, j)),
        ],
        out_specs=pl.BlockSpec((1, rows, tn), lambda l, j: (l, 0, j)),
        out_shape=jax.ShapeDtypeStruct((L, rows, W), F32),
        compiler_params=_params(("arbitrary", "arbitrary")),
        name="adaln",
    )(cp, w_ada, b_ada.reshape(L, 1, W))
    return out[:, :B]


def _fold_kernel(sk_ref, wq_ref, o_ref, rows_ref):
    half = sk_ref.shape[-1]
    groups = 2 * PEER_HEADS
    for h in range(PEER_HEADS):
        for p in range(2):
            g = 2 * h + p
            res = lax.dot_general(sk_ref[0, p], wq_ref[0, :, g * half:(g + 1) * half], NT_DIMS,
                                  precision=HIGHEST, preferred_element_type=F32)
            for cb in range(rows_ref.shape[0]):
                rows_ref[cb, pl.ds(p * PEER_HEADS + h, PEER_NKEYS, stride=groups), :] = (
                    res[:, cb * LANES:(cb + 1) * LANES])
    for cb in range(rows_ref.shape[0]):
        o_ref[0, :, cb * LANES:(cb + 1) * LANES] = rows_ref[cb].astype(BF16)


def _fold_peer_keys(peer_wq, peer_subkeys):
    L, D, QW = peer_wq.shape
    half = peer_subkeys.shape[-1]
    rows = (QW // half) * PEER_NKEYS
    return pl.pallas_call(
        _fold_kernel,
        grid=(L,),
        in_specs=[
            pl.BlockSpec((1, 2, PEER_NKEYS, half), lambda l: (l, 0, 0, 0)),
            pl.BlockSpec((1, D, QW), lambda l: (l, 0, 0)),
        ],
        out_specs=pl.BlockSpec((1, rows, D), lambda l: (l, 0, 0)),
        out_shape=jax.ShapeDtypeStruct((L, rows, D), BF16),
        scratch_shapes=[pltpu.VMEM((D // LANES, rows, LANES), F32)],
        compiler_params=_params(("arbitrary",), VMEM_LIMIT_BYTES),
        name="peer_fold",
    )(peer_subkeys, peer_wq)


def _inproj_kernel(x_ref, g_ref, sc_ref, sh_ref, w_ref, o_ref):
    x = x_ref[...]
    ms = jnp.mean(x * x, axis=-1, keepdims=True)
    y = x * lax.rsqrt(ms + EPS) * g_ref[...]
    h = y * (1.0 + sc_ref[0]) + sh_ref[0]
    o_ref[...] = jnp.dot(h.astype(BF16), w_ref[...], preferred_element_type=F32).astype(BF16)


def _in_proj(xf, g, sc, sh, wp, T, tm=512):
    N, D = xf.shape
    B = sc.shape[0]
    ncol = 3
    tn = PROJ_W // ncol
    return pl.pallas_call(
        _inproj_kernel,
        grid=(ncol, N // tm),
        in_specs=[
            pl.BlockSpec((tm, D), lambda j, i: (i, 0)),
            pl.BlockSpec((1, D), lambda j, i: (0, 0)),
            pl.BlockSpec((1, 1, D), lambda j, i: ((i * tm) // T, 0, 0)),
            pl.BlockSpec((1, 1, D), lambda j, i: ((i * tm) // T, 0, 0)),
            pl.BlockSpec((D, tn), lambda j, i: (0, j)),
        ],
        out_specs=pl.BlockSpec((tm, tn), lambda j, i: (i, j)),
        out_shape=jax.ShapeDtypeStruct((N, PROJ_W), BF16),
        compiler_params=_params(("arbitrary", "arbitrary")),
        name="in_proj",
    )(xf, g.reshape(1, D), sc.reshape(B, 1, D), sh.reshape(B, 1, D), wp)


_SRC = dict(zip(("q", "k", "v", "glr", "gr", "sq", "sk", "sv", "ga", "gb"),
                np.cumsum([0, 512, 512, 1024, GLA_RANK, 1024, 1024, 128, 128, 1024])))
W_IN_SLABS = ((COL_Q, _SRC["q"], 512), (COL_K, _SRC["k"], 512), (COL_V, _SRC["v"], 1024),
              (COL_GR, _SRC["gr"], 1024), (COL_SQ, _SRC["sq"], 1024), (COL_GA, _SRC["ga"], 1024),
              (COL_GB, _SRC["gb"], 1024), (COL_SK, _SRC["sk"], 128), (COL_SV, _SRC["sv"], 128),
              (COL_GLR, _SRC["glr"], GLA_RANK))


def _pack_w_in_kernel(w_ref, o_ref):
    o_ref[:, COL_GLR:] = jnp.zeros((o_ref.shape[0], PROJ_W - COL_GLR), BF16)
    for dst, src, width in W_IN_SLABS:
        o_ref[:, dst:dst + width] = w_ref[:, int(src):int(src) + width].astype(BF16)


def _pack_w_in(w_in, layer, tr=128):
    _, rows, cols = w_in.shape
    return pl.pallas_call(
        _pack_w_in_kernel,
        grid=(rows // tr,),
        in_specs=[pl.BlockSpec((None, tr, cols), lambda i: (layer, i, 0))],
        out_specs=pl.BlockSpec((tr, PROJ_W), lambda i: (i, 0)),
        out_shape=jax.ShapeDtypeStruct((rows, PROJ_W), BF16),
        compiler_params=_params(("arbitrary",)),
        name="pack_w_in",
    )(w_in)


def _gla_kernel(q_ref, k_ref, v_ref, r_ref, glr_ref, w2_ref, b2_ref, ng_ref, o_ref, st_ref, *, nchunk):
    @pl.when(pl.program_id(1) == 0)
    def _():
        st_ref[...] = jnp.zeros_like(st_ref)

    C = GLA_CHUNK
    dk, dv = GLA_DK, GLA_DV
    row = lax.broadcasted_iota(I32, (C, C), 0)
    col = lax.broadcasted_iota(I32, (C, C), 1)
    tri = col <= row
    tri_b = tri.astype(F32).astype(BF16)
    w2 = w2_ref[...]
    b2 = b2_ref[...]
    ng = ng_ref[...]
    for ci in range(nchunk):
        sl = pl.ds(ci * C, C)
        z = jnp.dot(glr_ref[sl, :], w2, preferred_element_type=F32) + b2
        log_a = (jnp.minimum(z, 0.0) - jnp.log(1.0 + jnp.exp(-jnp.abs(z)))) * (1.0 / GLA_TAU)
        la_hi = log_a.astype(BF16)
        la_lo = (log_a - la_hi.astype(F32)).astype(BF16)
        b = (jnp.dot(tri_b, la_hi, preferred_element_type=F32)
             + jnp.dot(tri_b, la_lo, preferred_element_type=F32))
        b_last = b[C - 1:C, :]
        q = q_ref[sl, :].astype(F32) * (dk ** -0.5)
        k = k_ref[sl, :].astype(F32)
        q_dec = (q * jnp.exp(b)).astype(BF16)
        k_inv = (k * jnp.exp(-b)).astype(BF16)
        k_tail = (k * jnp.exp(b_last - b)).astype(BF16)
        decay = jnp.exp(b_last)
        for h in range(GLA_HEADS):
            kc = slice(h * dk, (h + 1) * dk)
            vc = slice(h * dv, (h + 1) * dv)
            v = v_ref[sl, vc]
            attn = lax.dot_general(q_dec[:, kc], k_inv[:, kc], NT_DIMS, preferred_element_type=F32)
            attn = jnp.where(tri, attn, 0.0).astype(BF16)
            st = st_ref[h]
            o = (jnp.dot(attn, v, preferred_element_type=F32)
                 + lax.dot_general(q_dec[:, kc], st.astype(BF16), NT_DIMS, preferred_element_type=F32))
            st_ref[h] = st * decay[:, kc] + lax.dot_general(v, k_tail[:, kc], TN_DIMS,
                                                            preferred_element_type=F32)
            on = o * lax.rsqrt(jnp.mean(o * o, axis=-1, keepdims=True) + EPS) * ng
            r = r_ref[sl, vc].astype(F32)
            o_ref[sl, vc] = (on * (r * jax.nn.sigmoid(r))).astype(BF16)


def _gla(proj, w2p, b2, ng, tc=256):
    B, T, _ = proj.shape
    H = GLA_HEADS
    kw, vw = H * GLA_DK, H * GLA_DV
    return pl.pallas_call(
        functools.partial(_gla_kernel, nchunk=tc // GLA_CHUNK),
        grid=(B, T // tc),
        in_specs=[
            pl.BlockSpec((None, tc, kw), lambda b, c: (b, c, COL_Q // kw)),
            pl.BlockSpec((None, tc, kw), lambda b, c: (b, c, COL_K // kw)),
            pl.BlockSpec((None, tc, vw), lambda b, c: (b, c, COL_V // vw)),
            pl.BlockSpec((None, tc, vw), lambda b, c: (b, c, COL_GR // vw)),
            pl.BlockSpec((None, tc, LANES), lambda b, c: (b, c, COL_GLR // LANES)),
            pl.BlockSpec((LANES, kw), lambda b, c: (0, 0)),
            pl.BlockSpec((1, kw), lambda b, c: (0, 0)),
            pl.BlockSpec((1, GLA_DV), lambda b, c: (0, 0)),
        ],
        out_specs=pl.BlockSpec((None, tc, vw), lambda b, c: (b, c, 0)),
        out_shape=jax.ShapeDtypeStruct((B, T, vw), BF16),
        scratch_shapes=[pltpu.VMEM((H, GLA_DV, GLA_DK), F32)],
        compiler_params=_params(("arbitrary", "arbitrary")),
        name="gla",
    )(proj, proj, proj, proj, proj, w2p, b2, ng)


def _t5_bucket(dist):
    max_exact = N_BUCKETS // 2
    d = np.maximum(dist, 1).astype(np.float32)
    large = max_exact + (np.log(d / max_exact) / np.log(MAX_DISTANCE / max_exact)
                         * (N_BUCKETS - max_exact)).astype(np.int32)
    large = np.minimum(large, N_BUCKETS - 1)
    return np.where(dist < max_exact, dist, large).astype(np.int32)


NEG_BIG = -1e30


def _swa_bias(rel_bias):
    blk = SWA_BLOCK
    qi = np.arange(blk)[:, None]
    sj = np.arange(2 * blk)[None, :]
    dist = blk + qi - sj
    band = (dist >= 0) & (dist < blk)
    bucket = _t5_bucket(np.clip(dist, 0, None))
    onehot = jnp.asarray(bucket.reshape(-1, 1) == np.arange(N_BUCKETS)[None, :], F32)
    bias = jnp.dot(onehot, rel_bias.astype(F32), precision=HIGHEST).T.reshape(-1, blk, 2 * blk)
    masks = np.stack([band, band & (sj >= blk)])[:, None]
    return jnp.where(jnp.asarray(masks), bias[None], NEG_BIG)


def _segment_sums(x, seg):
    hi = x.astype(BF16)
    lo = (x - hi.astype(F32)).astype(BF16)
    return (jnp.dot(hi, seg, preferred_element_type=F32) + jnp.dot(lo, seg, preferred_element_type=F32))


def _head_rms_scale(x, seg, seg_t):
    inv = lax.rsqrt(_segment_sums(x * x, seg) * (1.0 / SWA_HD) + EPS)
    return _segment_sums(inv, seg_t)


SWA_ROWS = 32


def _swa_kernel(sink_ref, q_ref, kp_ref, kc_ref, vp_ref, vc_ref, bias_ref, qg_ref, kg_ref, seg_ref,
                segt_ref, o_ref):
    blk = SWA_BLOCK
    hd = SWA_HD
    group = SWA_HEADS // SWA_KV_HEADS
    kvw = SWA_KV_HEADS * hd
    seg = seg_ref[...]
    seg_t = segt_ref[...]
    q = q_ref[...].astype(F32)
    qn = (q * _head_rms_scale(q, seg, seg_t) * qg_ref[...] * (hd ** -0.5)).astype(BF16)
    k2 = jnp.concatenate([kp_ref[...], kc_ref[...]], axis=0).astype(F32)
    kn = (k2 * _head_rms_scale(k2, seg[:kvw], seg_t[:, :kvw]) * kg_ref[...]).astype(BF16)
    v2 = jnp.concatenate([vp_ref[...], vc_ref[...]], axis=0)
    lane = lax.broadcasted_iota(I32, (2 * blk, kvw), 1)
    outs = []
    for kh in range(SWA_KV_HEADS):
        kk = kn[:, kh * hd:(kh + 1) * hd]
        vsh = v2 if kh == 0 else jnp.concatenate([v2[:, kh * hd:], v2[:, :kh * hd]], axis=1)
        vv = jnp.where(lane < hd, vsh, jnp.ones_like(vsh))
        qs = jnp.concatenate([qn[:, (kh * group + g) * hd:(kh * group + g + 1) * hd]
                              for g in range(group)], axis=0)
        logits = lax.dot_general(qs, kk, NT_DIMS, preferred_element_type=F32)
        logits = logits + bias_ref[kh * group:(kh + 1) * group].reshape(group * blk, 2 * blk)
        for g in range(group):
            sink = sink_ref[kh * group + g]
            parts = []
            for r0 in range(0, blk, SWA_ROWS):
                lg = logits[g * blk + r0:g * blk + r0 + SWA_ROWS]
                m = jnp.maximum(jnp.max(lg, axis=-1, keepdims=True), sink)
                p = jnp.exp(lg - m).astype(BF16)
                pv = jnp.dot(p, vv, preferred_element_type=F32)
                parts.append(pv[:, :hd] / (pv[:, hd:hd + 1] + jnp.exp(sink - m)))
            outs.append(jnp.concatenate(parts, axis=0))
    o_ref[...] = jnp.concatenate(outs, axis=-1).astype(BF16)


def _swa(proj, bias, qg, kg, sinks):
    B, T, _ = proj.shape
    blk = SWA_BLOCK
    qw = SWA_HEADS * SWA_HD
    kvw = SWA_KV_HEADS * SWA_HD
    prev = lambda b, i: jnp.maximum(i - 1, 0)
    seg = (np.arange(qw)[:, None] // SWA_HD == np.arange(LANES)[None, :])
    seg = jnp.asarray(seg, BF16)
    return pl.pallas_call(
        _swa_kernel,
        grid=(B, T // blk),
        in_specs=[
            pl.BlockSpec(memory_space=pltpu.SMEM),
            pl.BlockSpec((None, blk, qw), lambda b, i: (b, i, COL_SQ // qw)),
            pl.BlockSpec((None, blk, kvw), lambda b, i: (b, prev(b, i), COL_SK // kvw)),
            pl.BlockSpec((None, blk, kvw), lambda b, i: (b, i, COL_SK // kvw)),
            pl.BlockSpec((None, blk, kvw), lambda b, i: (b, prev(b, i), COL_SV // kvw)),
            pl.BlockSpec((None, blk, kvw), lambda b, i: (b, i, COL_SV // kvw)),
            pl.BlockSpec((None, SWA_HEADS, blk, 2 * blk), lambda b, i: (jnp.where(i == 0, 1, 0), 0, 0, 0)),
            pl.BlockSpec((1, qw), lambda b, i: (0, 0)),
            pl.BlockSpec((1, kvw), lambda b, i: (0, 0)),
            pl.BlockSpec((qw, LANES), lambda b, i: (0, 0)),
            pl.BlockSpec((LANES, qw), lambda b, i: (0, 0)),
        ],
        out_specs=pl.BlockSpec((None, blk, qw), lambda b, i: (b, i, 0)),
        out_shape=jax.ShapeDtypeStruct((B, T, qw), BF16),
        compiler_params=_params(("arbitrary", "arbitrary")),
        name="swa",
    )(sinks, proj, proj, proj, proj, proj, bias, jnp.tile(qg, (1, SWA_HEADS)),
      jnp.tile(kg, (1, SWA_KV_HEADS)), seg, seg.T)


def _store_token_tiles(ref, x):
    tokens = x.shape[0]
    for q in range(SUBLANES):
        ref[pl.ds(q, tokens, stride=SUBLANES), :] = x[:, q * LANES:(q + 1) * LANES]


def _load_token_tiles(ref, tokens):
    return jnp.concatenate([ref[pl.ds(q, tokens, stride=SUBLANES), :] for q in range(SUBLANES)], axis=1)


def _merge_kernel(x_ref, ga_ref, gb_ref, go_ref, so_ref, wa_ref, wb_ref, wo_ref, g1_ref,
                  n2_ref, sc2_ref, sh2_ref, we_ref, x1_ref, h2_ref, st_ref):
    ya = jnp.dot(go_ref[...], wa_ref[...], preferred_element_type=F32)
    yb = jnp.dot(so_ref[...], wb_ref[...], preferred_element_type=F32)
    m = jax.nn.sigmoid(ga_ref[...].astype(F32)) * ya + jax.nn.sigmoid(gb_ref[...].astype(F32)) * yb
    mixed = jnp.dot(m.astype(BF16), wo_ref[...], preferred_element_type=F32)
    x1 = x_ref[...] + g1_ref[0] * mixed
    y = x1 * lax.rsqrt(jnp.mean(x1 * x1, axis=-1, keepdims=True) + EPS) * n2_ref[...]
    h2 = y * (1.0 + sc2_ref[0]) + sh2_ref[0]
    _store_token_tiles(x1_ref, x1)
    _store_token_tiles(h2_ref, h2)
    st_ref[...] = lax.dot_general(we_ref[...], h2.astype(BF16), NT_DIMS, preferred_element_type=F32)


def _merge(xf, proj2, gla_o, swa_o, wa, wb, wo, g1, n2, sc2, sh2, weff_t, T, tm=256):
    N, D = xf.shape
    B = g1.shape[0]
    SW = weff_t.shape[0]
    bat = lambda i: ((i * tm) // T, 0, 0)
    full = lambda i: (0, 0)
    return pl.pallas_call(
        _merge_kernel,
        grid=(N // tm,),
        in_specs=[
            pl.BlockSpec((tm, D), lambda i: (i, 0)),
            pl.BlockSpec((tm, D), lambda i: (i, COL_GA // D)),
            pl.BlockSpec((tm, D), lambda i: (i, COL_GB // D)),
            pl.BlockSpec((tm, D), lambda i: (i, 0)),
            pl.BlockSpec((tm, D), lambda i: (i, 0)),
            pl.BlockSpec((D, D), full),
            pl.BlockSpec((D, D), full),
            pl.BlockSpec((D, D), full),
            pl.BlockSpec((1, 1, D), bat),
            pl.BlockSpec((1, D), full),
            pl.BlockSpec((1, 1, D), bat),
            pl.BlockSpec((1, 1, D), bat),
            pl.BlockSpec((SW, D), full),
        ],
        out_specs=[
            pl.BlockSpec((tm * SUBLANES, LANES), lambda i: (i, 0)),
            pl.BlockSpec((tm * SUBLANES, LANES), lambda i: (i, 0)),
            pl.BlockSpec((SW, tm), lambda i: (0, i)),
        ],
        out_shape=[
            jax.ShapeDtypeStruct((N * SUBLANES, LANES), F32),
            jax.ShapeDtypeStruct((N * SUBLANES, LANES), F32),
            jax.ShapeDtypeStruct((SW, N), F32),
        ],
        compiler_params=_params(("arbitrary",), VMEM_LIMIT_BYTES),
        name="merge",
    )(xf, proj2, proj2, gla_o, swa_o, wa, wb, wo, g1.reshape(B, 1, D), n2.reshape(1, D),
      sc2.reshape(B, 1, D), sh2.reshape(B, 1, D), weff_t)


def _tree(op, xs):
    xs = list(xs)
    while len(xs) > 1:
        xs = [op(xs[i], xs[i + 1]) for i in range(0, len(xs) - 1, 2)] + ([xs[-1]] if len(xs) % 2 else [])
    return xs[0]


EXTRACT_CHAINS = 4


def _extract_best(ref, n):
    lanes = min(EXTRACT_CHAINS, n)
    best = [ref[i] for i in range(lanes)]
    for i in range(lanes, n):
        best[i % lanes] = jnp.maximum(best[i % lanes], ref[i])
    m = _tree(jnp.maximum, best)
    first = [jnp.where(ref[i] == m, i, n) for i in range(lanes)]
    for i in range(lanes, n):
        first[i % lanes] = jnp.minimum(first[i % lanes], jnp.where(ref[i] == m, i, n))
    pos = _tree(jnp.minimum, first)
    for i in range(n):
        ref[i] = jnp.where(pos == i, -jnp.inf, ref[i])
    return m, pos


def _ordered(a, b):
    (va, ia), (vb, ib) = a, b
    keep = jnp.logical_or(va > vb, jnp.logical_and(va == vb, ia < ib))
    first = (jnp.where(keep, va, vb), jnp.where(keep, ia, ib))
    second = (jnp.where(keep, vb, va), jnp.where(keep, ib, ia))
    return first, second


def _bitonic_merge(xs):
    xs = list(xs)
    j = len(xs) // 2
    while j >= 1:
        for i in range(len(xs)):
            if i & j == 0:
                xs[i], xs[i | j] = _ordered(xs[i], xs[i | j])
        j //= 2
    return xs


def _bitonic_sort(xs):
    xs = list(xs)
    k = 2
    while k <= len(xs):
        j = k // 2
        while j >= 1:
            for i in range(len(xs)):
                if i & j == 0:
                    first, second = _ordered(xs[i], xs[i | j])
                    xs[i], xs[i | j] = (first, second) if i & k == 0 else (second, first)
            j //= 2
        k *= 2
    return xs


def _merge_best(xs, ys):
    n = len(xs)
    return _bitonic_merge([_ordered(xs[i], ys[n - 1 - i])[0] for i in range(n)])


TOPK_PAIRS = tuple((a, b) for a in range(PEER_TOPK) for b in range(PEER_TOPK)
                   if (a + 1) * (b + 1) <= PEER_TOPK)


def _topk_kernel(s_ref, e_ref, g_ref, sv_ref, si_ref, v_ref, i_ref, cand_ref, cidx_ref, best_ref, row_ref):
    K = PEER_TOPK
    nk = PEER_NKEYS
    H = PEER_HEADS
    G = 2 * H
    tm = s_ref.shape[1]
    nblocks = nk // K
    for hb in range(2):
        rows = slice(hb * H, (hb + 1) * H)
        for blk in range(nblocks):
            keys = [(s_ref[(blk * K + i) * G + hb * H:(blk * K + i) * G + (hb + 1) * H, :],
                     jnp.full((H, tm), blk * K + i, I32)) for i in range(K)]
            for i, (v, ix) in enumerate(_bitonic_sort(keys)):
                sv_ref[blk, i, rows, :] = v
                si_ref[blk, i, rows, :] = ix
        step = 1
        while step < nblocks:
            for blk in range(0, nblocks, 2 * step):
                xs = [(sv_ref[blk, i, rows, :], si_ref[blk, i, rows, :]) for i in range(K)]
                ys = [(sv_ref[blk + step, i, rows, :], si_ref[blk + step, i, rows, :]) for i in range(K)]
                for i, (v, ix) in enumerate(_merge_best(xs, ys)):
                    sv_ref[blk, i, rows, :] = v
                    si_ref[blk, i, rows, :] = ix
            step *= 2
    v_ref[...] = sv_ref[0]
    i_ref[...] = si_ref[0]

    for ci, (a, b) in enumerate(TOPK_PAIRS):
        cand_ref[ci] = v_ref[a, 0:H, :] + v_ref[b, H:2 * H, :]
        cidx_ref[ci] = (i_ref[a, 0:H, :] * nk + i_ref[b, H:2 * H, :]) * WORDS_PER_EXPERT
    ncand = len(TOPK_PAIRS)

    def stage2(k, _):
        m, pos = _extract_best(cand_ref, ncand)
        row = _tree(jnp.maximum, [jnp.where(pos == ci, cidx_ref[ci], -1) for ci in range(ncand)])
        best_ref[k] = m
        row_ref[k] = row
        return 0

    lax.fori_loop(0, K, stage2, 0)
    best = best_ref[...]
    ex = jnp.exp(best - best[0:1])
    gates = ex / jnp.sum(ex, axis=0, keepdims=True)
    g_ref[...] = gates.reshape(K * H, tm).T
    e_ref[...] = row_ref[...].reshape(K * H, tm).astype(F32).T.astype(I32)


def _topk(s_t, tm=128):
    SW, N = s_t.shape
    K = PEER_TOPK
    G = 2 * PEER_HEADS
    return pl.pallas_call(
        _topk_kernel,
        grid=(N // tm,),
        in_specs=[pl.BlockSpec((SW, tm), lambda i: (0, i))],
        out_specs=[
            pl.BlockSpec((tm, PEER_SLOTS), lambda i: (i, 0)),
            pl.BlockSpec((tm, PEER_SLOTS), lambda i: (i, 0)),
        ],
        out_shape=[
            jax.ShapeDtypeStruct((N, PEER_SLOTS), I32),
            jax.ShapeDtypeStruct((N, PEER_SLOTS), F32),
        ],
        scratch_shapes=[pltpu.VMEM((PEER_NKEYS // K, K, G, tm), F32),
                        pltpu.VMEM((PEER_NKEYS // K, K, G, tm), I32),
                        pltpu.VMEM((K, G, tm), F32),
                        pltpu.VMEM((K, G, tm), I32),
                        pltpu.VMEM((len(TOPK_PAIRS), PEER_HEADS, tm), F32),
                        pltpu.VMEM((len(TOPK_PAIRS), PEER_HEADS, tm), I32),
                        pltpu.VMEM((K, PEER_HEADS, tm), F32),
                        pltpu.VMEM((K, PEER_HEADS, tm), I32)],
        compiler_params=_params(("arbitrary",)),
        name="peer_topk",
    )(s_t)


WORDS_PER_EXPERT = SUBLANES // 2


TILE_BLOCK = tuple((r % 2) * WORDS_PER_EXPERT + r // 2 for r in range(SUBLANES))
BLOCK_ROW = tuple(TILE_BLOCK.index(q) for q in range(SUBLANES))


def _bf16_bits(x):
    return lax.bitcast_convert_type(x.astype(BF16).astype(F32), jnp.uint32)


def _table_kernel(t_ref, o_ref):
    rows = t_ref.shape[0]
    for s in range(WORDS_PER_EXPERT):
        lo = _bf16_bits(t_ref[:, s * LANES:(s + 1) * LANES])
        hi = _bf16_bits(t_ref[:, (s + WORDS_PER_EXPERT) * LANES:(s + WORDS_PER_EXPERT + 1) * LANES])
        word = lax.shift_right_logical(lo, jnp.uint32(16)) | (hi & jnp.uint32(0xFFFF0000))
        o_ref[pl.ds(s, rows, stride=WORDS_PER_EXPERT), :] = lax.bitcast_convert_type(word, I32)


def _table(tables, layer, te=2048):
    _, n, d = tables.shape
    return pl.pallas_call(
        _table_kernel,
        grid=(n // te,),
        in_specs=[pl.BlockSpec((None, te, d), lambda i: (layer, i, 0))],
        out_specs=pl.BlockSpec((te * WORDS_PER_EXPERT, LANES), lambda i: (i, 0)),
        out_shape=jax.ShapeDtypeStruct((n * WORDS_PER_EXPERT, LANES), I32),
        compiler_params=_params(("arbitrary",)),
        name="peer_table",
    )(tables)


def _expert_tile(tab_ref, e4):
    words = tab_ref[pl.ds(pl.multiple_of(e4, WORDS_PER_EXPERT), WORDS_PER_EXPERT), :]
    return pltpu.bitcast(words, BF16)


def _to_tile_rows(x):
    return jnp.concatenate([x[q:q + 1, :] for q in TILE_BLOCK], axis=0)


def _from_tile_rows(x):
    return jnp.concatenate([x[r:r + 1, :] for r in BLOCK_ROW], axis=0)


def _gelu_tanh(x):
    return 0.5 * x * (1.0 + jnp.tanh(np.sqrt(2.0 / np.pi) * (x + 0.044715 * (x * x * x))))


RING_SLOTS = 2
SLOT_TOKENS = 32


def _for_each_token(e_ref, idx_ref, sem, tb, token_body):
    ngroups = tb // SLOT_TOKENS

    def fetch(q, slot):
        return pltpu.make_async_copy(e_ref.at[q], idx_ref.at[slot], sem.at[slot])

    for slot in range(RING_SLOTS):
        fetch(slot, slot).start()

    def ring(j, _):
        for slot in range(RING_SLOTS):
            q = j * RING_SLOTS + slot
            fetch(q, slot).wait()
            for tt in range(SLOT_TOKENS):
                token_body(q * SLOT_TOKENS + tt, lambda k, slot=slot, tt=tt: idx_ref[slot, tt, k])
            fetch(jnp.minimum(q + RING_SLOTS, ngroups - 1), slot).start()
        return 0

    lax.fori_loop(0, ngroups // RING_SLOTS, ring, 0)
    for slot in range(RING_SLOTS):
        fetch(ngroups - 1, slot).wait()


PAIR_GROUP = LANES // SUBLANES
SLOT_GROUPS = PEER_SLOTS // PAIR_GROUP


def _slot_tiles(tab_ref, row, j):
    return jnp.concatenate([_expert_tile(tab_ref, row(j * PAIR_GROUP + kk)) for kk in range(PAIR_GROUP)],
                           axis=0)


def _spread_matrices():
    lane = np.arange(LANES)
    return np.stack([lane[:, None] == PAIR_GROUP * j + lane[None, :] // SUBLANES
                     for j in range(SLOT_GROUPS)])


def _peer_u_kernel(e_ref, h_ref, g_ref, tab_ref, sel_ref, o_ref, slab_ref, idx_ref, sem, *, tb):
    groups_per_token = PEER_SLOTS // SUBLANES
    sel = sel_ref[...]

    def token(t, row):
        hb = _to_tile_rows(h_ref[t]).astype(BF16)
        for j in range(groups_per_token):
            prods = jnp.concatenate([_expert_tile(tab_ref, row(j * SUBLANES + qn)) * hb
                                     for qn in range(SUBLANES)], axis=0)
            dst = pl.multiple_of(t * PEER_SLOTS + j * SUBLANES, SUBLANES)
            slab_ref[pl.ds(dst, SUBLANES), :] = jnp.dot(sel, prods, preferred_element_type=F32)

    _for_each_token(e_ref, idx_ref, sem, tb, token)
    rio = lax.broadcasted_iota(I32, (LANES, LANES), 0)
    lio = lax.broadcasted_iota(I32, (LANES, LANES), 1)
    diag = (rio == lio)[None]
    rows = SUBLANES * PEER_SLOTS
    for gi in range(tb // SUBLANES):
        tok = slice(gi * SUBLANES, (gi + 1) * SUBLANES)
        rs = jnp.sum(slab_ref[gi * rows:(gi + 1) * rows, :], axis=-1, keepdims=True)
        rs = rs.reshape(SUBLANES, PEER_SLOTS, 1)
        a = jnp.sum(jnp.where(diag, rs, 0.0), axis=1)
        o_ref[tok, :] = g_ref[tok, :] * _gelu_tanh(a)


def _peer_u(e4, h3, gates, tab, tb=256):
    N = h3.shape[0]
    assert tb % (RING_SLOTS * SLOT_TOKENS) == 0 and tb % SUBLANES == 0
    sel = np.arange(SUBLANES)[:, None] == np.arange(SUBLANES * SUBLANES)[None, :] // SUBLANES
    return pl.pallas_call(
        functools.partial(_peer_u_kernel, tb=tb),
        grid=(N // tb,),
        in_specs=[
            pl.BlockSpec((tb // SLOT_TOKENS, SLOT_TOKENS, PEER_SLOTS), lambda i: (i, 0, 0)),
            pl.BlockSpec((tb, SUBLANES, LANES), lambda i: (i, 0, 0)),
            pl.BlockSpec((tb, PEER_SLOTS), lambda i: (i, 0)),
            pl.BlockSpec(tab.shape, lambda i: (0, 0), pipeline_mode=pl.Buffered(1)),
            pl.BlockSpec(sel.shape, lambda i: (0, 0)),
        ],
        out_specs=pl.BlockSpec((tb, PEER_SLOTS), lambda i: (i, 0)),
        out_shape=jax.ShapeDtypeStruct((N, PEER_SLOTS), F32),
        scratch_shapes=[pltpu.VMEM((tb * PEER_SLOTS, LANES), F32),
                        pltpu.SMEM((RING_SLOTS, SLOT_TOKENS, PEER_SLOTS), I32),
                        pltpu.SemaphoreType.DMA((RING_SLOTS,))],
        compiler_params=_params(("arbitrary",), VMEM_LIMIT_BYTES),
        name="peer_u",
    )(e4.reshape(N // SLOT_TOKENS, SLOT_TOKENS, PEER_SLOTS), h3, gates, tab, jnp.asarray(sel, BF16))


def _peer_v_kernel(e_ref, c_ref, x1_ref, g2_ref, tab_ref, spread_ref, o_ref, hi_ref, lo_ref, out_ref, idx_ref,
                   sem, *, tb):
    c = c_ref[...]
    c_hi = c.astype(BF16)
    c_lo = (c - c_hi.astype(F32)).astype(BF16)
    for j in range(SLOT_GROUPS):
        rows = pl.ds(j, tb, stride=SLOT_GROUPS)
        hi_ref[rows, :] = jnp.dot(c_hi, spread_ref[j], preferred_element_type=F32)
        lo_ref[rows, :] = jnp.dot(c_lo, spread_ref[j], preferred_element_type=F32)
    rio = lax.broadcasted_iota(I32, (SUBLANES, LANES), 0)
    lio = lax.broadcasted_iota(I32, (SUBLANES, LANES), 1)
    own_row = (lio & (SUBLANES - 1)) == rio

    def token(t, row):
        base = pl.multiple_of(t * SLOT_GROUPS, SLOT_GROUPS)
        hi = hi_ref[pl.ds(base, SLOT_GROUPS), :]
        lo = lo_ref[pl.ds(base, SLOT_GROUPS), :]
        acc = jnp.zeros((SUBLANES, LANES), F32)
        for j in range(SLOT_GROUPS):
            lhs = jnp.concatenate(
                [jnp.where(own_row, jnp.broadcast_to(part[j:j + 1, :], (SUBLANES, LANES)), 0.0).astype(BF16)
                 for part in (hi, lo)], axis=0)
            out = jnp.dot(lhs, _slot_tiles(tab_ref, row, j), preferred_element_type=F32)
            acc = acc + (out[:SUBLANES] + out[SUBLANES:])
        tile = pl.ds(pl.multiple_of(t * SUBLANES, SUBLANES), SUBLANES)
        out_ref[tile, :] = x1_ref[t] + g2_ref[0] * _from_tile_rows(acc)

    _for_each_token(e_ref, idx_ref, sem, tb, token)
    o_ref[...] = _load_token_tiles(out_ref, tb)


def _peer_v(e4, coef, x1_3, g2_3, tab, T, tb=256):
    N = x1_3.shape[0]
    assert tb % (RING_SLOTS * SLOT_TOKENS) == 0 and tb % SUBLANES == 0
    return pl.pallas_call(
        functools.partial(_peer_v_kernel, tb=tb),
        grid=(N // tb,),
        in_specs=[
            pl.BlockSpec((tb // SLOT_TOKENS, SLOT_TOKENS, PEER_SLOTS), lambda i: (i, 0, 0)),
            pl.BlockSpec((tb, PEER_SLOTS), lambda i: (i, 0)),
            pl.BlockSpec((tb, SUBLANES, LANES), lambda i: (i, 0, 0)),
            pl.BlockSpec((1, SUBLANES, LANES), lambda i: ((i * tb) // T, 0, 0)),
            pl.BlockSpec(tab.shape, lambda i: (0, 0), pipeline_mode=pl.Buffered(1)),
            pl.BlockSpec((SLOT_GROUPS, LANES, LANES), lambda i: (0, 0, 0)),
        ],
        out_specs=pl.BlockSpec((tb, SUBLANES * LANES), lambda i: (i, 0)),
        out_shape=jax.ShapeDtypeStruct((N, SUBLANES * LANES), F32),
        scratch_shapes=[pltpu.VMEM((tb * SLOT_GROUPS, LANES), F32),
                        pltpu.VMEM((tb * SLOT_GROUPS, LANES), F32),
                        pltpu.VMEM((tb * SUBLANES, LANES), F32),
                        pltpu.SMEM((RING_SLOTS, SLOT_TOKENS, PEER_SLOTS), I32),
                        pltpu.SemaphoreType.DMA((RING_SLOTS,))],
        compiler_params=_params(("arbitrary",), VMEM_LIMIT_BYTES),
        name="peer_v",
    )(e4.reshape(N // SLOT_TOKENS, SLOT_TOKENS, PEER_SLOTS), coef, x1_3, g2_3, tab,
      jnp.asarray(_spread_matrices(), BF16))


def kernel(x, c, w_ada, b_ada, norm1_g, w_in, gla_gate_w2, gla_gate_b, gla_norm_g, swa_qnorm_g,
           swa_knorm_g, swa_sinks, rel_bias, w_up_a, w_up_b, w_out, norm2_g, peer_wq, peer_subkeys,
           peer_u, peer_v):
    B, T, D = x.shape
    N = B * T
    L = w_ada.shape[0]
    mod = _adaln(c, w_ada, b_ada)
    weff_t = _fold_peer_keys(peer_wq, peer_subkeys)
    bias = _swa_bias(rel_bias)
    xf = x.reshape(N, D)
    for l in range(L):
        sh1, sc1, g1, sh2, sc2, g2 = [mod[l, :, i * D:(i + 1) * D] for i in range(6)]
        proj = _in_proj(xf, norm1_g[l], sc1, sh1, _pack_w_in(w_in, l), T)
        proj3 = proj.reshape(B, T, PROJ_W)
        w2p = jnp.zeros((LANES, GLA_HEADS * GLA_DK), BF16).at[:GLA_RANK].set(gla_gate_w2[l].astype(BF16))
        gla_o = _gla(proj3, w2p, gla_gate_b[l].reshape(1, -1), gla_norm_g[l].reshape(1, -1))
        swa_o = _swa(proj3, bias, swa_qnorm_g[l].reshape(1, -1), swa_knorm_g[l].reshape(1, -1),
                     swa_sinks[l])
        x1, h2, s_t = _merge(xf, proj, gla_o.reshape(N, -1), swa_o.reshape(N, -1),
                             w_up_a[l].astype(BF16), w_up_b[l].astype(BF16), w_out[l].astype(BF16),
                             g1, norm2_g[l], sc2, sh2, weff_t[l], T)
        e4, gates = _topk(s_t)
        coef = _peer_u(e4, h2.reshape(N, SUBLANES, LANES), gates, _table(peer_u, l))
        g2_3 = g2.reshape(B, SUBLANES, LANES)
        xf = _peer_v(e4, coef, x1.reshape(N, SUBLANES, LANES), g2_3, _table(peer_v, l), T)
    return xf.reshape(B, T, D)
```

```python
import functools

import numpy as np
import jax
import jax.numpy as jnp
from jax import lax
from jax.experimental import pallas as pl
from jax.experimental.pallas import tpu as pltpu

F32 = jnp.float32
BF16 = jnp.bfloat16
I32 = jnp.int32
HIGHEST = lax.Precision.HIGHEST
EPS = 1e-6

GLA_HEADS = 4
GLA_DK = 128
GLA_DV = 256
GLA_RANK = 16
GLA_TAU = 16.0
GLA_CHUNK = 64
SWA_HEADS = 16
SWA_KV_HEADS = 2
SWA_HD = 64
SWA_BLOCK = 128
N_BUCKETS = 32
MAX_DISTANCE = 128
PEER_HEADS = 8
PEER_NKEYS = 128
PEER_TOPK = 16
PEER_SLOTS = PEER_HEADS * PEER_TOPK

SUBLANES = 8
LANES = 128
VMEM_LIMIT_BYTES = 56 * 1024 * 1024

NT_DIMS = (((1,), (1,)), ((), ()))
TN_DIMS = (((0,), (0,)), ((), ()))

COL_Q, COL_K, COL_V, COL_GR, COL_SQ, COL_GA, COL_GB = 0, 512, 1024, 2048, 3072, 4096, 5120
COL_SK, COL_SV, COL_GLR = 6144, 6272, 6400
PROJ_W = 6528


def _params(sem, vmem=None):
    return pltpu.CompilerParams(dimension_semantics=sem, vmem_limit_bytes=vmem)


def _adaln_kernel(c_ref, w_ref, b_ref, o_ref):
    c = c_ref[...]
    a = c * jax.nn.sigmoid(c)
    o_ref[0] = jnp.dot(a, w_ref[0], preferred_element_type=F32, precision=HIGHEST) + b_ref[0]


def _adaln(c, w_ada, b_ada):
    L, D, W = w_ada.shape
    B = c.shape[0]
    rows = -(-B // SUBLANES) * SUBLANES
    cp = jnp.zeros((rows, D), F32).at[:B].set(c)
    tn = W // 4
    out = pl.pallas_call(
        _adaln_kernel,
        grid=(L, W // tn),
        in_specs=[
            pl.BlockSpec((rows, D), lambda l, j: (0, 0)),
            pl.BlockSpec((1, D, tn), lambda l, j: (l, 0, j)),
            pl.BlockSpec((1, 1, tn), lambda l, j: (l, 0, j)),
        ],
        out_specs=pl.BlockSpec((1, rows, tn), lambda l, j: (l, 0, j)),
        out_shape=jax.ShapeDtypeStruct((L, rows, W), F32),
        compiler_params=_params(("arbitrary", "arbitrary")),
        name="adaln",
    )(cp, w_ada, b_ada.reshape(L, 1, W))
    return out[:, :B]


def _fold_kernel(sk_ref, wq_ref, o_ref, rows_ref):
    half = sk_ref.shape[-1]
    groups = 2 * PEER_HEADS
    for h in range(PEER_HEADS):
        for p in range(2):
            g = 2 * h + p
            res = lax.dot_general(sk_ref[0, p], wq_ref[0, :, g * half:(g + 1) * half], NT_DIMS,
                                  precision=HIGHEST, preferred_element_type=F32)
            for cb in range(rows_ref.shape[0]):
                rows_ref[cb, pl.ds(p * PEER_HEADS + h, PEER_NKEYS, stride=groups), :] = (
                    res[:, cb * LANES:(cb + 1) * LANES])
    for cb in range(rows_ref.shape[0]):
        o_ref[0, :, cb * LANES:(cb + 1) * LANES] = rows_ref[cb].astype(BF16)


def _fold_peer_keys(peer_wq, peer_subkeys):
    L, D, QW = peer_wq.shape
    half = peer_subkeys.shape[-1]
    rows = (QW // half) * PEER_NKEYS
    return pl.pallas_call(
        _fold_kernel,
        grid=(L,),
        in_specs=[
            pl.BlockSpec((1, 2, PEER_NKEYS, half), lambda l: (l, 0, 0, 0)),
            pl.BlockSpec((1, D, QW), lambda l: (l, 0, 0)),
        ],
        out_specs=pl.BlockSpec((1, rows, D), lambda l: (l, 0, 0)),
        out_shape=jax.ShapeDtypeStruct((L, rows, D), BF16),
        scratch_shapes=[pltpu.VMEM((D // LANES, rows, LANES), F32)],
        compiler_params=_params(("arbitrary",), VMEM_LIMIT_BYTES),
        name="peer_fold",
    )(peer_subkeys, peer_wq)


def _inproj_kernel(x_ref, g_ref, sc_ref, sh_ref, w_ref, o_ref):
    x = x_ref[...]
    ms = jnp.mean(x * x, axis=-1, keepdims=True)
    y = x * lax.rsqrt(ms + EPS) * g_ref[...]
    h = y * (1.0 + sc_ref[0]) + sh_ref[0]
    o_ref[...] = jnp.dot(h.astype(BF16), w_ref[...], preferred_element_type=F32).astype(BF16)


def _in_proj(xf, g, sc, sh, wp, T, tm=512):
    N, D = xf.shape
    B = sc.shape[0]
    ncol = 3
    tn = PROJ_W // ncol
    return pl.pallas_call(
        _inproj_kernel,
        grid=(ncol, N // tm),
        in_specs=[
            pl.BlockSpec((tm, D), lambda j, i: (i, 0)),
            pl.BlockSpec((1, D), lambda j, i: (0, 0)),
            pl.BlockSpec((1, 1, D), lambda j, i: ((i * tm) // T, 0, 0)),
            pl.BlockSpec((1, 1, D), lambda j, i: ((i * tm) // T, 0, 0)),
            pl.BlockSpec((D, tn), lambda j, i: (0, j)),
        ],
        out_specs=pl.BlockSpec((tm, tn), lambda j, i: (i, j)),
        out_shape=jax.ShapeDtypeStruct((N, PROJ_W), BF16),
        compiler_params=_params(("arbitrary", "arbitrary")),
        name="in_proj",
    )(xf, g.reshape(1, D), sc.reshape(B, 1, D), sh.reshape(B, 1, D), wp)


_SRC = dict(zip(("q", "k", "v", "glr", "gr", "sq", "sk", "sv", "ga", "gb"),
                np.cumsum([0, 512, 512, 1024, GLA_RANK, 1024, 1024, 128, 128, 1024])))
W_IN_SLABS = ((COL_Q, _SRC["q"], 512), (COL_K, _SRC["k"], 512), (COL_V, _SRC["v"], 1024),
              (COL_GR, _SRC["gr"], 1024), (COL_SQ, _SRC["sq"], 1024), (COL_GA, _SRC["ga"], 1024),
              (COL_GB, _SRC["gb"], 1024), (COL_SK, _SRC["sk"], 128), (COL_SV, _SRC["sv"], 128),
              (COL_GLR, _SRC["glr"], GLA_RANK))


def _pack_w_in_kernel(w_ref, o_ref):
    o_ref[:, COL_GLR:] = jnp.zeros((o_ref.shape[0], PROJ_W - COL_GLR), BF16)
    for dst, src, width in W_IN_SLABS:
        o_ref[:, dst:dst + width] = w_ref[:, int(src):int(src) + width].astype(BF16)


def _pack_w_in(w_in, layer, tr=128):
    _, rows, cols = w_in.shape
    return pl.pallas_call(
        _pack_w_in_kernel,
        grid=(rows // tr,),
        in_specs=[pl.BlockSpec((None, tr, cols), lambda i: (layer, i, 0))],
        out_specs=pl.BlockSpec((tr, PROJ_W), lambda i: (i, 0)),
        out_shape=jax.ShapeDtypeStruct((rows, PROJ_W), BF16),
        compiler_params=_params(("arbitrary",)),
        name="pack_w_in",
    )(w_in)


def _gla_kernel(q_ref, k_ref, v_ref, r_ref, glr_ref, w2_ref, b2_ref, ng_ref, o_ref, st_ref, *, nchunk):
    @pl.when(pl.program_id(1) == 0)
    def _():
        st_ref[...] = jnp.zeros_like(st_ref)

    C = GLA_CHUNK
    dk, dv = GLA_DK, GLA_DV
    row = lax.broadcasted_iota(I32, (C, C), 0)
    col = lax.broadcasted_iota(I32, (C, C), 1)
    tri = col <= row
    tri_b = tri.astype(F32).astype(BF16)
    w2 = w2_ref[...]
    b2 = b2_ref[...]
    ng = ng_ref[...]
    for ci in range(nchunk):
        sl = pl.ds(ci * C, C)
        z = jnp.dot(glr_ref[sl, :], w2, preferred_element_type=F32) + b2
        log_a = (jnp.minimum(z, 0.0) - jnp.log(1.0 + jnp.exp(-jnp.abs(z)))) * (1.0 / GLA_TAU)
        la_hi = log_a.astype(BF16)
        la_lo = (log_a - la_hi.astype(F32)).astype(BF16)
        b = (jnp.dot(tri_b, la_hi, preferred_element_type=F32)
             + jnp.dot(tri_b, la_lo, preferred_element_type=F32))
        b_last = b[C - 1:C, :]
        q = q_ref[sl, :].astype(F32) * (dk ** -0.5)
        k = k_ref[sl, :].astype(F32)
        q_dec = (q * jnp.exp(b)).astype(BF16)
        k_inv = (k * jnp.exp(-b)).astype(BF16)
        k_tail = (k * jnp.exp(b_last - b)).astype(BF16)
        decay = jnp.exp(b_last)
        for h in range(GLA_HEADS):
            kc = slice(h * dk, (h + 1) * dk)
            vc = slice(h * dv, (h + 1) * dv)
            v = v_ref[sl, vc]
            attn = lax.dot_general(q_dec[:, kc], k_inv[:, kc], NT_DIMS, preferred_element_type=F32)
            attn = jnp.where(tri, attn, 0.0).astype(BF16)
            st = st_ref[h]
            o = (jnp.dot(attn, v, preferred_element_type=F32)
                 + lax.dot_general(q_dec[:, kc], st.astype(BF16), NT_DIMS, preferred_element_type=F32))
            st_ref[h] = st * decay[:, kc] + lax.dot_general(v, k_tail[:, kc], TN_DIMS,
                                                            preferred_element_type=F32)
            on = o * lax.rsqrt(jnp.mean(o * o, axis=-1, keepdims=True) + EPS) * ng
            r = r_ref[sl, vc].astype(F32)
            o_ref[sl, vc] = (on * (r * jax.nn.sigmoid(r))).astype(BF16)


def _gla(proj, w2p, b2, ng, tc=256):
    B, T, _ = proj.shape
    H = GLA_HEADS
    kw, vw = H * GLA_DK, H * GLA_DV
    return pl.pallas_call(
        functools.partial(_gla_kernel, nchunk=tc // GLA_CHUNK),
        grid=(B, T // tc),
        in_specs=[
            pl.BlockSpec((None, tc, kw), lambda b, c: (b, c, COL_Q // kw)),
            pl.BlockSpec((None, tc, kw), lambda b, c: (b, c, COL_K // kw)),
            pl.BlockSpec((None, tc, vw), lambda b, c: (b, c, COL_V // vw)),
            pl.BlockSpec((None, tc, vw), lambda b, c: (b, c, COL_GR // vw)),
            pl.BlockSpec((None, tc, LANES), lambda b, c: (b, c, COL_GLR // LANES)),
            pl.BlockSpec((LANES, kw), lambda b, c: (0, 0)),
            pl.BlockSpec((1, kw), lambda b, c: (0, 0)),
            pl.BlockSpec((1, GLA_DV), lambda b, c: (0, 0)),
        ],
        out_specs=pl.BlockSpec((None, tc, vw), lambda b, c: (b, c, 0)),
        out_shape=jax.ShapeDtypeStruct((B, T, vw), BF16),
        scratch_shapes=[pltpu.VMEM((H, GLA_DV, GLA_DK), F32)],
        compiler_params=_params(("arbitrary", "arbitrary")),
        name="gla",
    )(proj, proj, proj, proj, proj, w2p, b2, ng)


def _t5_bucket(dist):
    max_exact = N_BUCKETS // 2
    d = np.maximum(dist, 1).astype(np.float32)
    large = max_exact + (np.log(d / max_exact) / np.log(MAX_DISTANCE / max_exact)
                         * (N_BUCKETS - max_exact)).astype(np.int32)
    large = np.minimum(large, N_BUCKETS - 1)
    return np.where(dist < max_exact, dist, large).astype(np.int32)


NEG_BIG = -1e30


def _swa_bias(rel_bias):
    blk = SWA_BLOCK
    qi = np.arange(blk)[:, None]
    sj = np.arange(2 * blk)[None, :]
    dist = blk + qi - sj
    band = (dist >= 0) & (dist < blk)
    bucket = _t5_bucket(np.clip(dist, 0, None))
    onehot = jnp.asarray(bucket.reshape(-1, 1) == np.arange(N_BUCKETS)[None, :], F32)
    bias = jnp.dot(onehot, rel_bias.astype(F32), precision=HIGHEST).T.reshape(-1, blk, 2 * blk)
    masks = np.stack([band, band & (sj >= blk)])[:, None]
    return jnp.where(jnp.asarray(masks), bias[None], NEG_BIG)


def _segment_sums(x, seg):
    hi = x.astype(BF16)
    lo = (x - hi.astype(F32)).astype(BF16)
    return (jnp.dot(hi, seg, preferred_element_type=F32) + jnp.dot(lo, seg, preferred_element_type=F32))


def _head_rms_scale(x, seg, seg_t):
    inv = lax.rsqrt(_segment_sums(x * x, seg) * (1.0 / SWA_HD) + EPS)
    return _segment_sums(inv, seg_t)


SWA_ROWS = 32


def _swa_kernel(sink_ref, q_ref, kp_ref, kc_ref, vp_ref, vc_ref, bias_ref, qg_ref, kg_ref, seg_ref,
                segt_ref, o_ref):
    blk = SWA_BLOCK
    hd = SWA_HD
    group = SWA_HEADS // SWA_KV_HEADS
    kvw = SWA_KV_HEADS * hd
    seg = seg_ref[...]
    seg_t = segt_ref[...]
    q = q_ref[...].astype(F32)
    qn = (q * _head_rms_scale(q, seg, seg_t) * qg_ref[...] * (hd ** -0.5)).astype(BF16)
    k2 = jnp.concatenate([kp_ref[...], kc_ref[...]], axis=0).astype(F32)
    kn = (k2 * _head_rms_scale(k2, seg[:kvw], seg_t[:, :kvw]) * kg_ref[...]).astype(BF16)
    v2 = jnp.concatenate([vp_ref[...], vc_ref[...]], axis=0)
    lane = lax.broadcasted_iota(I32, (2 * blk, kvw), 1)
    outs = []
    for kh in range(SWA_KV_HEADS):
        kk = kn[:, kh * hd:(kh + 1) * hd]
        vsh = v2 if kh == 0 else jnp.concatenate([v2[:, kh * hd:], v2[:, :kh * hd]], axis=1)
        vv = jnp.where(lane < hd, vsh, jnp.ones_like(vsh))
        qs = jnp.concatenate([qn[:, (kh * group + g) * hd:(kh * group + g + 1) * hd]
                              for g in range(group)], axis=0)
        logits = lax.dot_general(qs, kk, NT_DIMS, preferred_element_type=F32)
        logits = logits + bias_ref[kh * group:(kh + 1) * group].reshape(group * blk, 2 * blk)
        for g in range(group):
            sink = sink_ref[kh * group + g]
            parts = []
            for r0 in range(0, blk, SWA_ROWS):
                lg = logits[g * blk + r0:g * blk + r0 + SWA_ROWS]
                m = jnp.maximum(jnp.max(lg, axis=-1, keepdims=True), sink)
                p = jnp.exp(lg - m).astype(BF16)
                pv = jnp.dot(p, vv, preferred_element_type=F32)
                parts.append(pv[:, :hd] / (pv[:, hd:hd + 1] + jnp.exp(sink - m)))
            outs.append(jnp.concatenate(parts, axis=0))
    o_ref[...] = jnp.concatenate(outs, axis=-1).astype(BF16)


def _swa(proj, bias, qg, kg, sinks):
    B, T, _ = proj.shape
    blk = SWA_BLOCK
    qw = SWA_HEADS * SWA_HD
    kvw = SWA_KV_HEADS * SWA_HD
    prev = lambda b, i: jnp.maximum(i - 1, 0)
    seg = (np.arange(qw)[:, None] // SWA_HD == np.arange(LANES)[None, :])
    seg = jnp.asarray(seg, BF16)
    return pl.pallas_call(
        _swa_kernel,
        grid=(B, T // blk),
        in_specs=[
            pl.BlockSpec(memory_space=pltpu.SMEM),
            pl.BlockSpec((None, blk, qw), lambda b, i: (b, i, COL_SQ // qw)),
            pl.BlockSpec((None, blk, kvw), lambda b, i: (b, prev(b, i), COL_SK // kvw)),
            pl.BlockSpec((None, blk, kvw), lambda b, i: (b, i, COL_SK // kvw)),
            pl.BlockSpec((None, blk, kvw), lambda b, i: (b, prev(b, i), COL_SV // kvw)),
            pl.BlockSpec((None, blk, kvw), lambda b, i: (b, i, COL_SV // kvw)),
            pl.BlockSpec((None, SWA_HEADS, blk, 2 * blk), lambda b, i: (jnp.where(i == 0, 1, 0), 0, 0, 0)),
            pl.BlockSpec((1, qw), lambda b, i: (0, 0)),
            pl.BlockSpec((1, kvw), lambda b, i: (0, 0)),
            pl.BlockSpec((qw, LANES), lambda b, i: (0, 0)),
            pl.BlockSpec((LANES, qw), lambda b, i: (0, 0)),
        ],
        out_specs=pl.BlockSpec((None, blk, qw), lambda b, i: (b, i, 0)),
        out_shape=jax.ShapeDtypeStruct((B, T, qw), BF16),
        compiler_params=_params(("arbitrary", "arbitrary")),
        name="swa",
    )(sinks, proj, proj, proj, proj, proj, bias, jnp.tile(qg, (1, SWA_HEADS)),
      jnp.tile(kg, (1, SWA_KV_HEADS)), seg, seg.T)


def _store_token_tiles(ref, x):
    tokens = x.shape[0]
    for q in range(SUBLANES):
        ref[pl.ds(q, tokens, stride=SUBLANES), :] = x[:, q * LANES:(q + 1) * LANES]


def _load_token_tiles(ref, tokens):
    return jnp.concatenate([ref[pl.ds(q, tokens, stride=SUBLANES), :] for q in range(SUBLANES)], axis=1)


def _merge_kernel(x_ref, ga_ref, gb_ref, go_ref, so_ref, wa_ref, wb_ref, wo_ref, g1_ref,
                  n2_ref, sc2_ref, sh2_ref, we_ref, x1_ref, h2_ref, st_ref):
    ya = jnp.dot(go_ref[...], wa_ref[...], preferred_element_type=F32)
    yb = jnp.dot(so_ref[...], wb_ref[...], preferred_element_type=F32)
    m = jax.nn.sigmoid(ga_ref[...].astype(F32)) * ya + jax.nn.sigmoid(gb_ref[...].astype(F32)) * yb
    mixed = jnp.dot(m.astype(BF16), wo_ref[...], preferred_element_type=F32)
    x1 = x_ref[...] + g1_ref[0] * mixed
    y = x1 * lax.rsqrt(jnp.mean(x1 * x1, axis=-1, keepdims=True) + EPS) * n2_ref[...]
    h2 = y * (1.0 + sc2_ref[0]) + sh2_ref[0]
    _store_token_tiles(x1_ref, x1)
    _store_token_tiles(h2_ref, h2)
    st_ref[...] = lax.dot_general(we_ref[...], h2.astype(BF16), NT_DIMS, preferred_element_type=F32)


def _merge(xf, proj2, gla_o, swa_o, wa, wb, wo, g1, n2, sc2, sh2, weff_t, T, tm=256):
    N, D = xf.shape
    B = g1.shape[0]
    SW = weff_t.shape[0]
    bat = lambda i: ((i * tm) // T, 0, 0)
    full = lambda i: (0, 0)
    return pl.pallas_call(
        _merge_kernel,
        grid=(N // tm,),
        in_specs=[
            pl.BlockSpec((tm, D), lambda i: (i, 0)),
            pl.BlockSpec((tm, D), lambda i: (i, COL_GA // D)),
            pl.BlockSpec((tm, D), lambda i: (i, COL_GB // D)),
            pl.BlockSpec((tm, D), lambda i: (i, 0)),
            pl.BlockSpec((tm, D), lambda i: (i, 0)),
            pl.BlockSpec((D, D), full),
            pl.BlockSpec((D, D), full),
            pl.BlockSpec((D, D), full),
            pl.BlockSpec((1, 1, D), bat),
            pl.BlockSpec((1, D), full),
            pl.BlockSpec((1, 1, D), bat),
            pl.BlockSpec((1, 1, D), bat),
            pl.BlockSpec((SW, D), full),
        ],
        out_specs=[
            pl.BlockSpec((tm * SUBLANES, LANES), lambda i: (i, 0)),
            pl.BlockSpec((tm * SUBLANES, LANES), lambda i: (i, 0)),
            pl.BlockSpec((SW, tm), lambda i: (0, i)),
        ],
        out_shape=[
            jax.ShapeDtypeStruct((N * SUBLANES, LANES), F32),
            jax.ShapeDtypeStruct((N * SUBLANES, LANES), F32),
            jax.ShapeDtypeStruct((SW, N), F32),
        ],
        compiler_params=_params(("arbitrary",), VMEM_LIMIT_BYTES),
        name="merge",
    )(xf, proj2, proj2, gla_o, swa_o, wa, wb, wo, g1.reshape(B, 1, D), n2.reshape(1, D),
      sc2.reshape(B, 1, D), sh2.reshape(B, 1, D), weff_t)


def _tree(op, xs):
    xs = list(xs)
    while len(xs) > 1:
        xs = [op(xs[i], xs[i + 1]) for i in range(0, len(xs) - 1, 2)] + ([xs[-1]] if len(xs) % 2 else [])
    return xs[0]


EXTRACT_CHAINS = 4


def _extract_best(ref, n):
    lanes = min(EXTRACT_CHAINS, n)
    best = [ref[i] for i in range(lanes)]
    for i in range(lanes, n):
        best[i % lanes] = jnp.maximum(best[i % lanes], ref[i])
    m = _tree(jnp.maximum, best)
    first = [jnp.where(ref[i] == m, i, n) for i in range(lanes)]
    for i in range(lanes, n):
        first[i % lanes] = jnp.minimum(first[i % lanes], jnp.where(ref[i] == m, i, n))
    pos = _tree(jnp.minimum, first)
    for i in range(n):
        ref[i] = jnp.where(pos == i, -jnp.inf, ref[i])
    return m, pos


def _ordered(a, b):
    (va, ia), (vb, ib) = a, b
    keep = jnp.logical_or(va > vb, jnp.logical_and(va == vb, ia < ib))
    first = (jnp.where(keep, va, vb), jnp.where(keep, ia, ib))
    second = (jnp.where(keep, vb, va), jnp.where(keep, ib, ia))
    return first, second


def _bitonic_merge(xs):
    xs = list(xs)
    j = len(xs) // 2
    while j >= 1:
        for i in range(len(xs)):
            if i & j == 0:
                xs[i], xs[i | j] = _ordered(xs[i], xs[i | j])
        j //= 2
    return xs


def _bitonic_sort(xs):
    xs = list(xs)
    k = 2
    while k <= len(xs):
        j = k // 2
        while j >= 1:
            for i in range(len(xs)):
                if i & j == 0:
                    first, second = _ordered(xs[i], xs[i | j])
                    xs[i], xs[i | j] = (first, second) if i & k == 0 else (second, first)
            j //= 2
        k *= 2
    return xs


def _merge_best(xs, ys):
    n = len(xs)
    return _bitonic_merge([_ordered(xs[i], ys[n - 1 - i])[0] for i in range(n)])


TOPK_PAIRS = tuple((a, b) for a in range(PEER_TOPK) for b in range(PEER_TOPK)
                   if (a + 1) * (b + 1) <= PEER_TOPK)


def _topk_kernel(s_ref, e_ref, g_ref, sv_ref, si_ref, v_ref, i_ref, cand_ref, cidx_ref, best_ref, row_ref):
    K = PEER_TOPK
    nk = PEER_NKEYS
    H = PEER_HEADS
    G = 2 * H
    tm = s_ref.shape[1]
    nblocks = nk // K
    for hb in range(2):
        rows = slice(hb * H, (hb + 1) * H)
        for blk in range(nblocks):
            keys = [(s_ref[(blk * K + i) * G + hb * H:(blk * K + i) * G + (hb + 1) * H, :],
                     jnp.full((H, tm), blk * K + i, I32)) for i in range(K)]
            for i, (v, ix) in enumerate(_bitonic_sort(keys)):
                sv_ref[blk, i, rows, :] = v
                si_ref[blk, i, rows, :] = ix
        step = 1
        while step < nblocks:
            for blk in range(0, nblocks, 2 * step):
                xs = [(sv_ref[blk, i, rows, :], si_ref[blk, i, rows, :]) for i in range(K)]
                ys = [(sv_ref[blk + step, i, rows, :], si_ref[blk + step, i, rows, :]) for i in range(K)]
                for i, (v, ix) in enumerate(_merge_best(xs, ys)):
                    sv_ref[blk, i, rows, :] = v
                    si_ref[blk, i, rows, :] = ix
            step *= 2
    v_ref[...] = sv_ref[0]
    i_ref[...] = si_ref[0]

    for ci, (a, b) in enumerate(TOPK_PAIRS):
        cand_ref[ci] = v_ref[a, 0:H, :] + v_ref[b, H:2 * H, :]
        cidx_ref[ci] = (i_ref[a, 0:H, :] * nk + i_ref[b, H:2 * H, :]) * WORDS_PER_EXPERT
    ncand = len(TOPK_PAIRS)

    def stage2(k, _):
        m, pos = _extract_best(cand_ref, ncand)
        row = _tree(jnp.maximum, [jnp.where(pos == ci, cidx_ref[ci], -1) for ci in range(ncand)])
        best_ref[k] = m
        row_ref[k] = row
        return 0

    lax.fori_loop(0, K, stage2, 0)
    best = best_ref[...]
    ex = jnp.exp(best - best[0:1])
    gates = ex / jnp.sum(ex, axis=0, keepdims=True)
    g_ref[...] = gates.reshape(K * H, tm).T
    e_ref[...] = row_ref[...].reshape(K * H, tm).astype(F32).T.astype(I32)


def _topk(s_t, tm=128):
    SW, N = s_t.shape
    K = PEER_TOPK
    G = 2 * PEER_HEADS
    return pl.pallas_call(
        _topk_kernel,
        grid=(N // tm,),
        in_specs=[pl.BlockSpec((SW, tm), lambda i: (0, i))],
        out_specs=[
            pl.BlockSpec((tm, PEER_SLOTS), lambda i: (i, 0)),
            pl.BlockSpec((tm, PEER_SLOTS), lambda i: (i, 0)),
        ],
        out_shape=[
            jax.ShapeDtypeStruct((N, PEER_SLOTS), I32),
            jax.ShapeDtypeStruct((N, PEER_SLOTS), F32),
        ],
        scratch_shapes=[pltpu.VMEM((PEER_NKEYS // K, K, G, tm), F32),
                        pltpu.VMEM((PEER_NKEYS // K, K, G, tm), I32),
                        pltpu.VMEM((K, G, tm), F32),
                        pltpu.VMEM((K, G, tm), I32),
                        pltpu.VMEM((len(TOPK_PAIRS), PEER_HEADS, tm), F32),
                        pltpu.VMEM((len(TOPK_PAIRS), PEER_HEADS, tm), I32),
                        pltpu.VMEM((K, PEER_HEADS, tm), F32),
                        pltpu.VMEM((K, PEER_HEADS, tm), I32)],
        compiler_params=_params(("arbitrary",)),
        name="peer_topk",
    )(s_t)


WORDS_PER_EXPERT = SUBLANES // 2


TILE_BLOCK = tuple((r % 2) * WORDS_PER_EXPERT + r // 2 for r in range(SUBLANES))
BLOCK_ROW = tuple(TILE_BLOCK.index(q) for q in range(SUBLANES))


def _bf16_bits(x):
    return lax.bitcast_convert_type(x.astype(BF16).astype(F32), jnp.uint32)


def _table_kernel(t_ref, o_ref):
    rows = t_ref.shape[0]
    for s in range(WORDS_PER_EXPERT):
        lo = _bf16_bits(t_ref[:, s * LANES:(s + 1) * LANES])
        hi = _bf16_bits(t_ref[:, (s + WORDS_PER_EXPERT) * LANES:(s + WORDS_PER_EXPERT + 1) * LANES])
        word = lax.shift_right_logical(lo, jnp.uint32(16)) | (hi & jnp.uint32(0xFFFF0000))
        o_ref[pl.ds(s, rows, stride=WORDS_PER_EXPERT), :] = lax.bitcast_convert_type(word, I32)


def _table(tables, layer, te=2048):
    _, n, d = tables.shape
    return pl.pallas_call(
        _table_kernel,
        grid=(n // te,),
        in_specs=[pl.BlockSpec((None, te, d), lambda i: (layer, i, 0))],
        out_specs=pl.BlockSpec((te * WORDS_PER_EXPERT, LANES), lambda i: (i, 0)),
        out_shape=jax.ShapeDtypeStruct((n * WORDS_PER_EXPERT, LANES), I32),
        compiler_params=_params(("arbitrary",)),
        name="peer_table",
    )(tables)


def _expert_tile(tab_ref, e4):
    words = tab_ref[pl.ds(pl.multiple_of(e4, WORDS_PER_EXPERT), WORDS_PER_EXPERT), :]
    return pltpu.bitcast(words, BF16)


def _to_tile_rows(x):
    return jnp.concatenate([x[q:q + 1, :] for q in TILE_BLOCK], axis=0)


def _from_tile_rows(x):
    return jnp.concatenate([x[r:r + 1, :] for r in BLOCK_ROW], axis=0)


def _gelu_tanh(x):
    return 0.5 * x * (1.0 + jnp.tanh(np.sqrt(2.0 / np.pi) * (x + 0.044715 * (x * x * x))))


RING_SLOTS = 2
SLOT_TOKENS = 32
SLAB_TOKENS = 128


def _for_each_token(e_ref, idx_ref, sem, first, count, token_body):
    group0 = first // SLOT_TOKENS
    ngroups = count // SLOT_TOKENS

    def fetch(q, slot):
        return pltpu.make_async_copy(e_ref.at[group0 + q], idx_ref.at[slot], sem.at[slot])

    for slot in range(RING_SLOTS):
        fetch(slot, slot).start()

    def ring(j, _):
        for slot in range(RING_SLOTS):
            q = j * RING_SLOTS + slot
            fetch(q, slot).wait()
            for tt in range(SLOT_TOKENS):
                token_body(first + q * SLOT_TOKENS + tt, lambda k, slot=slot, tt=tt: idx_ref[slot, tt, k])
            fetch(jnp.minimum(q + RING_SLOTS, ngroups - 1), slot).start()
        return 0

    lax.fori_loop(0, ngroups // RING_SLOTS, ring, 0)
    for slot in range(RING_SLOTS):
        fetch(ngroups - 1, slot).wait()


PAIR_GROUP = LANES // SUBLANES
SLOT_GROUPS = PEER_SLOTS // PAIR_GROUP


def _slot_tiles(tab_ref, row, j):
    return jnp.concatenate([_expert_tile(tab_ref, row(j * PAIR_GROUP + kk)) for kk in range(PAIR_GROUP)],
                           axis=0)


def _spread_matrices():
    lane = np.arange(LANES)
    return np.stack([lane[:, None] == PAIR_GROUP * j + lane[None, :] // SUBLANES
                     for j in range(SLOT_GROUPS)])


def _peer_u_kernel(e_ref, h_ref, g_ref, tab_ref, sel_ref, o_ref, slab_ref, idx_ref, sem, *, tb):
    groups_per_token = PEER_SLOTS // SUBLANES
    sel = sel_ref[...]
    rio = lax.broadcasted_iota(I32, (LANES, LANES), 0)
    lio = lax.broadcasted_iota(I32, (LANES, LANES), 1)
    diag = (rio == lio)[None]
    rows = SUBLANES * PEER_SLOTS

    def sub_block(sb, _):
        first = pl.multiple_of(sb * SLAB_TOKENS, SLAB_TOKENS)

        def token(t, row):
            hb = _to_tile_rows(h_ref[t]).astype(BF16)
            for j in range(groups_per_token):
                prods = jnp.concatenate([_expert_tile(tab_ref, row(j * SUBLANES + qn)) * hb
                                         for qn in range(SUBLANES)], axis=0)
                dst = pl.multiple_of((t - first) * PEER_SLOTS + j * SUBLANES, SUBLANES)
                slab_ref[pl.ds(dst, SUBLANES), :] = jnp.dot(sel, prods, preferred_element_type=F32)

        _for_each_token(e_ref, idx_ref, sem, first, SLAB_TOKENS, token)
        for gi in range(SLAB_TOKENS // SUBLANES):
            tok = pl.ds(pl.multiple_of(first + gi * SUBLANES, SUBLANES), SUBLANES)
            rs = jnp.sum(slab_ref[gi * rows:(gi + 1) * rows, :], axis=-1, keepdims=True)
            rs = rs.reshape(SUBLANES, PEER_SLOTS, 1)
            a = jnp.sum(jnp.where(diag, rs, 0.0), axis=1)
            o_ref[tok, :] = g_ref[tok, :] * _gelu_tanh(a)
        return 0

    lax.fori_loop(0, tb // SLAB_TOKENS, sub_block, 0)


def _peer_u(e4, h3, gates, tab, tb=512):
    N = h3.shape[0]
    assert tb % SLAB_TOKENS == 0 and SLAB_TOKENS % (RING_SLOTS * SLOT_TOKENS) == 0
    sel = np.arange(SUBLANES)[:, None] == np.arange(SUBLANES * SUBLANES)[None, :] // SUBLANES
    return pl.pallas_call(
        functools.partial(_peer_u_kernel, tb=tb),
        grid=(N // tb,),
        in_specs=[
            pl.BlockSpec((tb // SLOT_TOKENS, SLOT_TOKENS, PEER_SLOTS), lambda i: (i, 0, 0)),
            pl.BlockSpec((tb, SUBLANES, LANES), lambda i: (i, 0, 0)),
            pl.BlockSpec((tb, PEER_SLOTS), lambda i: (i, 0)),
            pl.BlockSpec(tab.shape, lambda i: (0, 0), pipeline_mode=pl.Buffered(1)),
            pl.BlockSpec(sel.shape, lambda i: (0, 0)),
        ],
        out_specs=pl.BlockSpec((tb, PEER_SLOTS), lambda i: (i, 0)),
        out_shape=jax.ShapeDtypeStruct((N, PEER_SLOTS), F32),
        scratch_shapes=[pltpu.VMEM((SLAB_TOKENS * PEER_SLOTS, LANES), F32),
                        pltpu.SMEM((RING_SLOTS, SLOT_TOKENS, PEER_SLOTS), I32),
                        pltpu.SemaphoreType.DMA((RING_SLOTS,))],
        compiler_params=_params(("arbitrary",), VMEM_LIMIT_BYTES),
        name="peer_u",
    )(e4.reshape(N // SLOT_TOKENS, SLOT_TOKENS, PEER_SLOTS), h3, gates, tab, jnp.asarray(sel, BF16))


def _peer_v_kernel(e_ref, c_ref, x1_ref, g2_ref, tab_ref, spread_ref, o_ref, hi_ref, lo_ref, out_ref, idx_ref,
                   sem, *, tb):
    c = c_ref[...]
    c_hi = c.astype(BF16)
    c_lo = (c - c_hi.astype(F32)).astype(BF16)
    for j in range(SLOT_GROUPS):
        rows = pl.ds(j, tb, stride=SLOT_GROUPS)
        hi_ref[rows, :] = jnp.dot(c_hi, spread_ref[j], preferred_element_type=F32)
        lo_ref[rows, :] = jnp.dot(c_lo, spread_ref[j], preferred_element_type=F32)
    rio = lax.broadcasted_iota(I32, (SUBLANES, LANES), 0)
    lio = lax.broadcasted_iota(I32, (SUBLANES, LANES), 1)
    own_row = (lio & (SUBLANES - 1)) == rio

    def token(t, row):
        base = pl.multiple_of(t * SLOT_GROUPS, SLOT_GROUPS)
        hi = hi_ref[pl.ds(base, SLOT_GROUPS), :]
        lo = lo_ref[pl.ds(base, SLOT_GROUPS), :]
        acc = jnp.zeros((SUBLANES, LANES), F32)
        for j in range(SLOT_GROUPS):
            lhs = jnp.concatenate(
                [jnp.where(own_row, jnp.broadcast_to(part[j:j + 1, :], (SUBLANES, LANES)), 0.0).astype(BF16)
                 for part in (hi, lo)], axis=0)
            out = jnp.dot(lhs, _slot_tiles(tab_ref, row, j), preferred_element_type=F32)
            acc = acc + (out[:SUBLANES] + out[SUBLANES:])
        tile = pl.ds(pl.multiple_of(t * SUBLANES, SUBLANES), SUBLANES)
        out_ref[tile, :] = x1_ref[t] + g2_ref[0] * _from_tile_rows(acc)

    _for_each_token(e_ref, idx_ref, sem, 0, tb, token)
    o_ref[...] = _load_token_tiles(out_ref, tb)


def _peer_v(e4, coef, x1_3, g2_3, tab, T, tb=512):
    N = x1_3.shape[0]
    assert tb % (RING_SLOTS * SLOT_TOKENS) == 0 and tb % SUBLANES == 0
    return pl.pallas_call(
        functools.partial(_peer_v_kernel, tb=tb),
        grid=(N // tb,),
        in_specs=[
            pl.BlockSpec((tb // SLOT_TOKENS, SLOT_TOKENS, PEER_SLOTS), lambda i: (i, 0, 0)),
            pl.BlockSpec((tb, PEER_SLOTS), lambda i: (i, 0)),
            pl.BlockSpec((tb, SUBLANES, LANES), lambda i: (i, 0, 0)),
            pl.BlockSpec((1, SUBLANES, LANES), lambda i: ((i * tb) // T, 0, 0)),
            pl.BlockSpec(tab.shape, lambda i: (0, 0), pipeline_mode=pl.Buffered(1)),
            pl.BlockSpec((SLOT_GROUPS, LANES, LANES), lambda i: (0, 0, 0)),
        ],
        out_specs=pl.BlockSpec((tb, SUBLANES * LANES), lambda i: (i, 0)),
        out_shape=jax.ShapeDtypeStruct((N, SUBLANES * LANES), F32),
        scratch_shapes=[pltpu.VMEM((tb * SLOT_GROUPS, LANES), F32),
                        pltpu.VMEM((tb * SLOT_GROUPS, LANES), F32),
                        pltpu.VMEM((tb * SUBLANES, LANES), F32),
                        pltpu.SMEM((RING_SLOTS, SLOT_TOKENS, PEER_SLOTS), I32),
                        pltpu.SemaphoreType.DMA((RING_SLOTS,))],
        compiler_params=_params(("arbitrary",), VMEM_LIMIT_BYTES),
        name="peer_v",
    )(e4.reshape(N // SLOT_TOKENS, SLOT_TOKENS, PEER_SLOTS), coef, x1_3, g2_3, tab,
      jnp.asarray(_spread_matrices(), BF16))


def kernel(x, c, w_ada, b_ada, norm1_g, w_in, gla_gate_w2, gla_gate_b, gla_norm_g, swa_qnorm_g,
           swa_knorm_g, swa_sinks, rel_bias, w_up_a, w_up_b, w_out, norm2_g, peer_wq, peer_subkeys,
           peer_u, peer_v):
    B, T, D = x.shape
    N = B * T
    L = w_ada.shape[0]
    mod = _adaln(c, w_ada, b_ada)
    weff_t = _fold_peer_keys(peer_wq, peer_subkeys)
    bias = _swa_bias(rel_bias)
    xf = x.reshape(N, D)
    for l in range(L):
        sh1, sc1, g1, sh2, sc2, g2 = [mod[l, :, i * D:(i + 1) * D] for i in range(6)]
        proj = _in_proj(xf, norm1_g[l], sc1, sh1, _pack_w_in(w_in, l), T)
        proj3 = proj.reshape(B, T, PROJ_W)
        w2p = jnp.zeros((LANES, GLA_HEADS * GLA_DK), BF16).at[:GLA_RANK].set(gla_gate_w2[l].astype(BF16))
        gla_o = _gla(proj3, w2p, gla_gate_b[l].reshape(1, -1), gla_norm_g[l].reshape(1, -1))
        swa_o = _swa(proj3, bias, swa_qnorm_g[l].reshape(1, -1), swa_knorm_g[l].reshape(1, -1),
                     swa_sinks[l])
        x1, h2, s_t = _merge(xf, proj, gla_o.reshape(N, -1), swa_o.reshape(N, -1),
                             w_up_a[l].astype(BF16), w_up_b[l].astype(BF16), w_out[l].astype(BF16),
                             g1, norm2_g[l], sc2, sh2, weff_t[l], T)
        e4, gates = _topk(s_t)
        coef = _peer_u(e4, h2.reshape(N, SUBLANES, LANES), gates, _table(peer_u, l))
        g2_3 = g2.reshape(B, SUBLANES, LANES)
        xf = _peer_v(e4, coef, x1.reshape(N, SUBLANES, LANES), g2_3, _table(peer_v, l), T)
    return xf.reshape(B, T, D)
```

```python
import functools

import numpy as np
import jax
import jax.numpy as jnp
from jax import lax
from jax.experimental import pallas as pl
from jax.experimental.pallas import tpu as pltpu

F32 = jnp.float32
BF16 = jnp.bfloat16
I32 = jnp.int32
HIGHEST = lax.Precision.HIGHEST
EPS = 1e-6

GLA_HEADS = 4
GLA_DK = 128
GLA_DV = 256
GLA_RANK = 16
GLA_TAU = 16.0
GLA_CHUNK = 64
SWA_HEADS = 16
SWA_KV_HEADS = 2
SWA_HD = 64
SWA_BLOCK = 128
N_BUCKETS = 32
MAX_DISTANCE = 128
PEER_HEADS = 8
PEER_NKEYS = 128
PEER_TOPK = 16
PEER_SLOTS = PEER_HEADS * PEER_TOPK

SUBLANES = 8
LANES = 128
VMEM_LIMIT_BYTES = 56 * 1024 * 1024

NT_DIMS = (((1,), (1,)), ((), ()))
TN_DIMS = (((0,), (0,)), ((), ()))

COL_Q, COL_K, COL_V, COL_GR, COL_SQ, COL_GA, COL_GB = 0, 512, 1024, 2048, 3072, 4096, 5120
COL_SK, COL_SV, COL_GLR = 6144, 6272, 6400
PROJ_W = 6528


def _params(sem, vmem=None):
    return pltpu.CompilerParams(dimension_semantics=sem, vmem_limit_bytes=vmem)


def _adaln_kernel(c_ref, w_ref, b_ref, o_ref):
    c = c_ref[...]
    a = c * jax.nn.sigmoid(c)
    o_ref[0] = jnp.dot(a, w_ref[0], preferred_element_type=F32, precision=HIGHEST) + b_ref[0]


def _adaln(c, w_ada, b_ada):
    L, D, W = w_ada.shape
    B = c.shape[0]
    rows = -(-B // SUBLANES) * SUBLANES
    cp = jnp.zeros((rows, D), F32).at[:B].set(c)
    tn = W // 4
    out = pl.pallas_call(
        _adaln_kernel,
        grid=(L, W // tn),
        in_specs=[
            pl.BlockSpec((rows, D), lambda l, j: (0, 0)),
            pl.BlockSpec((1, D, tn), lambda l, j: (l, 0, j)),
            pl.BlockSpec((1, 1, tn), lambda l, j: (l, 0, j)),
        ],
        out_specs=pl.BlockSpec((1, rows, tn), lambda l, j: (l, 0, j)),
        out_shape=jax.ShapeDtypeStruct((L, rows, W), F32),
        compiler_params=_params(("arbitrary", "arbitrary")),
        name="adaln",
    )(cp, w_ada, b_ada.reshape(L, 1, W))
    return out[:, :B]


def _fold_kernel(sk_ref, wq_ref, o_ref, rows_ref):
    half = sk_ref.shape[-1]
    groups = 2 * PEER_HEADS
    for h in range(PEER_HEADS):
        for p in range(2):
            g = 2 * h + p
            res = lax.dot_general(sk_ref[0, p], wq_ref[0, :, g * half:(g + 1) * half], NT_DIMS,
                                  precision=HIGHEST, preferred_element_type=F32)
            for cb in range(rows_ref.shape[0]):
                rows_ref[cb, pl.ds(p * PEER_HEADS + h, PEER_NKEYS, stride=groups), :] = (
                    res[:, cb * LANES:(cb + 1) * LANES])
    for cb in range(rows_ref.shape[0]):
        o_ref[0, :, cb * LANES:(cb + 1) * LANES] = rows_ref[cb].astype(BF16)


def _fold_peer_keys(peer_wq, peer_subkeys):
    L, D, QW = peer_wq.shape
    half = peer_subkeys.shape[-1]
    rows = (QW // half) * PEER_NKEYS
    return pl.pallas_call(
        _fold_kernel,
        grid=(L,),
        in_specs=[
            pl.BlockSpec((1, 2, PEER_NKEYS, half), lambda l: (l, 0, 0, 0)),
            pl.BlockSpec((1, D, QW), lambda l: (l, 0, 0)),
        ],
        out_specs=pl.BlockSpec((1, rows, D), lambda l: (l, 0, 0)),
        out_shape=jax.ShapeDtypeStruct((L, rows, D), BF16),
        scratch_shapes=[pltpu.VMEM((D // LANES, rows, LANES), F32)],
        compiler_params=_params(("arbitrary",), VMEM_LIMIT_BYTES),
        name="peer_fold",
    )(peer_subkeys, peer_wq)


def _inproj_kernel(x_ref, g_ref, sc_ref, sh_ref, w_ref, o_ref):
    x = x_ref[...]
    ms = jnp.mean(x * x, axis=-1, keepdims=True)
    y = x * lax.rsqrt(ms + EPS) * g_ref[...]
    h = y * (1.0 + sc_ref[0]) + sh_ref[0]
    o_ref[...] = jnp.dot(h.astype(BF16), w_ref[...], preferred_element_type=F32).astype(BF16)


def _in_proj(xf, g, sc, sh, wp, T, tm=512):
    N, D = xf.shape
    B = sc.shape[0]
    ncol = 3
    tn = PROJ_W // ncol
    return pl.pallas_call(
        _inproj_kernel,
        grid=(ncol, N // tm),
        in_specs=[
            pl.BlockSpec((tm, D), lambda j, i: (i, 0)),
            pl.BlockSpec((1, D), lambda j, i: (0, 0)),
            pl.BlockSpec((1, 1, D), lambda j, i: ((i * tm) // T, 0, 0)),
            pl.BlockSpec((1, 1, D), lambda j, i: ((i * tm) // T, 0, 0)),
            pl.BlockSpec((D, tn), lambda j, i: (0, j)),
        ],
        out_specs=pl.BlockSpec((tm, tn), lambda j, i: (i, j)),
        out_shape=jax.ShapeDtypeStruct((N, PROJ_W), BF16),
        compiler_params=_params(("arbitrary", "arbitrary")),
        name="in_proj",
    )(xf, g.reshape(1, D), sc.reshape(B, 1, D), sh.reshape(B, 1, D), wp)


_SRC = dict(zip(("q", "k", "v", "glr", "gr", "sq", "sk", "sv", "ga", "gb"),
                np.cumsum([0, 512, 512, 1024, GLA_RANK, 1024, 1024, 128, 128, 1024])))
W_IN_SLABS = ((COL_Q, _SRC["q"], 512), (COL_K, _SRC["k"], 512), (COL_V, _SRC["v"], 1024),
              (COL_GR, _SRC["gr"], 1024), (COL_SQ, _SRC["sq"], 1024), (COL_GA, _SRC["ga"], 1024),
              (COL_GB, _SRC["gb"], 1024), (COL_SK, _SRC["sk"], 128), (COL_SV, _SRC["sv"], 128),
              (COL_GLR, _SRC["glr"], GLA_RANK))


def _pack_w_in_kernel(w_ref, o_ref):
    o_ref[:, COL_GLR:] = jnp.zeros((o_ref.shape[0], PROJ_W - COL_GLR), BF16)
    for dst, src, width in W_IN_SLABS:
        o_ref[:, dst:dst + width] = w_ref[:, int(src):int(src) + width].astype(BF16)


def _pack_w_in(w_in, layer, tr=128):
    _, rows, cols = w_in.shape
    return pl.pallas_call(
        _pack_w_in_kernel,
        grid=(rows // tr,),
        in_specs=[pl.BlockSpec((None, tr, cols), lambda i: (layer, i, 0))],
        out_specs=pl.BlockSpec((tr, PROJ_W), lambda i: (i, 0)),
        out_shape=jax.ShapeDtypeStruct((rows, PROJ_W), BF16),
        compiler_params=_params(("arbitrary",)),
        name="pack_w_in",
    )(w_in)


def _gla_kernel(q_ref, k_ref, v_ref, r_ref, glr_ref, w2_ref, b2_ref, ng_ref, o_ref, st_ref, *, nchunk):
    @pl.when(pl.program_id(1) == 0)
    def _():
        st_ref[...] = jnp.zeros_like(st_ref)

    C = GLA_CHUNK
    dk, dv = GLA_DK, GLA_DV
    row = lax.broadcasted_iota(I32, (C, C), 0)
    col = lax.broadcasted_iota(I32, (C, C), 1)
    tri = col <= row
    tri_b = tri.astype(F32).astype(BF16)
    w2 = w2_ref[...]
    b2 = b2_ref[...]
    ng = ng_ref[...]
    for ci in range(nchunk):
        sl = pl.ds(ci * C, C)
        z = jnp.dot(glr_ref[sl, :], w2, preferred_element_type=F32) + b2
        log_a = (jnp.minimum(z, 0.0) - jnp.log(1.0 + jnp.exp(-jnp.abs(z)))) * (1.0 / GLA_TAU)
        la_hi = log_a.astype(BF16)
        la_lo = (log_a - la_hi.astype(F32)).astype(BF16)
        b = (jnp.dot(tri_b, la_hi, preferred_element_type=F32)
             + jnp.dot(tri_b, la_lo, preferred_element_type=F32))
        b_last = b[C - 1:C, :]
        q = q_ref[sl, :].astype(F32) * (dk ** -0.5)
        k = k_ref[sl, :].astype(F32)
        q_dec = (q * jnp.exp(b)).astype(BF16)
        k_inv = (k * jnp.exp(-b)).astype(BF16)
        k_tail = (k * jnp.exp(b_last - b)).astype(BF16)
        decay = jnp.exp(b_last)
        for h in range(GLA_HEADS):
            kc = slice(h * dk, (h + 1) * dk)
            vc = slice(h * dv, (h + 1) * dv)
            v = v_ref[sl, vc]
            attn = lax.dot_general(q_dec[:, kc], k_inv[:, kc], NT_DIMS, preferred_element_type=F32)
            attn = jnp.where(tri, attn, 0.0).astype(BF16)
            st = st_ref[h]
            o = (jnp.dot(attn, v, preferred_element_type=F32)
                 + lax.dot_general(q_dec[:, kc], st.astype(BF16), NT_DIMS, preferred_element_type=F32))
            st_ref[h] = st * decay[:, kc] + lax.dot_general(v, k_tail[:, kc], TN_DIMS,
                                                            preferred_element_type=F32)
            on = o * lax.rsqrt(jnp.mean(o * o, axis=-1, keepdims=True) + EPS) * ng
            r = r_ref[sl, vc].astype(F32)
            o_ref[sl, vc] = (on * (r * jax.nn.sigmoid(r))).astype(BF16)


def _gla(proj, w2p, b2, ng, tc=256):
    B, T, _ = proj.shape
    H = GLA_HEADS
    kw, vw = H * GLA_DK, H * GLA_DV
    return pl.pallas_call(
        functools.partial(_gla_kernel, nchunk=tc // GLA_CHUNK),
        grid=(B, T // tc),
        in_specs=[
            pl.BlockSpec((None, tc, kw), lambda b, c: (b, c, COL_Q // kw)),
            pl.BlockSpec((None, tc, kw), lambda b, c: (b, c, COL_K // kw)),
            pl.BlockSpec((None, tc, vw), lambda b, c: (b, c, COL_V // vw)),
            pl.BlockSpec((None, tc, vw), lambda b, c: (b, c, COL_GR // vw)),
            pl.BlockSpec((None, tc, LANES), lambda b, c: (b, c, COL_GLR // LANES)),
            pl.BlockSpec((LANES, kw), lambda b, c: (0, 0)),
            pl.BlockSpec((1, kw), lambda b, c: (0, 0)),
            pl.BlockSpec((1, GLA_DV), lambda b, c: (0, 0)),
        ],
        out_specs=pl.BlockSpec((None, tc, vw), lambda b, c: (b, c, 0)),
        out_shape=jax.ShapeDtypeStruct((B, T, vw), BF16),
        scratch_shapes=[pltpu.VMEM((H, GLA_DV, GLA_DK), F32)],
        compiler_params=_params(("arbitrary", "arbitrary")),
        name="gla",
    )(proj, proj, proj, proj, proj, w2p, b2, ng)


def _t5_bucket(dist):
    max_exact = N_BUCKETS // 2
    d = np.maximum(dist, 1).astype(np.float32)
    large = max_exact + (np.log(d / max_exact) / np.log(MAX_DISTANCE / max_exact)
                         * (N_BUCKETS - max_exact)).astype(np.int32)
    large = np.minimum(large, N_BUCKETS - 1)
    return np.where(dist < max_exact, dist, large).astype(np.int32)


NEG_BIG = -1e30


def _swa_bias(rel_bias):
    blk = SWA_BLOCK
    qi = np.arange(blk)[:, None]
    sj = np.arange(2 * blk)[None, :]
    dist = blk + qi - sj
    band = (dist >= 0) & (dist < blk)
    bucket = _t5_bucket(np.clip(dist, 0, None))
    onehot = jnp.asarray(bucket.reshape(-1, 1) == np.arange(N_BUCKETS)[None, :], F32)
    bias = jnp.dot(onehot, rel_bias.astype(F32), precision=HIGHEST).T.reshape(-1, blk, 2 * blk)
    masks = np.stack([band, band & (sj >= blk)])[:, None]
    return jnp.where(jnp.asarray(masks), bias[None], NEG_BIG)


def _segment_sums(x, seg):
    hi = x.astype(BF16)
    lo = (x - hi.astype(F32)).astype(BF16)
    return (jnp.dot(hi, seg, preferred_element_type=F32) + jnp.dot(lo, seg, preferred_element_type=F32))


def _head_rms_scale(x, seg, seg_t):
    inv = lax.rsqrt(_segment_sums(x * x, seg) * (1.0 / SWA_HD) + EPS)
    return _segment_sums(inv, seg_t)


SWA_ROWS = 32


def _swa_kernel(sink_ref, q_ref, kp_ref, kc_ref, vp_ref, vc_ref, bias_ref, qg_ref, kg_ref, seg_ref,
                segt_ref, o_ref):
    blk = SWA_BLOCK
    hd = SWA_HD
    group = SWA_HEADS // SWA_KV_HEADS
    kvw = SWA_KV_HEADS * hd
    seg = seg_ref[...]
    seg_t = segt_ref[...]
    q = q_ref[...].astype(F32)
    qn = (q * _head_rms_scale(q, seg, seg_t) * qg_ref[...] * (hd ** -0.5)).astype(BF16)
    k2 = jnp.concatenate([kp_ref[...], kc_ref[...]], axis=0).astype(F32)
    kn = (k2 * _head_rms_scale(k2, seg[:kvw], seg_t[:, :kvw]) * kg_ref[...]).astype(BF16)
    v2 = jnp.concatenate([vp_ref[...], vc_ref[...]], axis=0)
    lane = lax.broadcasted_iota(I32, (2 * blk, kvw), 1)
    outs = []
    for kh in range(SWA_KV_HEADS):
        kk = kn[:, kh * hd:(kh + 1) * hd]
        vsh = v2 if kh == 0 else jnp.concatenate([v2[:, kh * hd:], v2[:, :kh * hd]], axis=1)
        vv = jnp.where(lane < hd, vsh, jnp.ones_like(vsh))
        qs = jnp.concatenate([qn[:, (kh * group + g) * hd:(kh * group + g + 1) * hd]
                              for g in range(group)], axis=0)
        logits = lax.dot_general(qs, kk, NT_DIMS, preferred_element_type=F32)
        logits = logits + bias_ref[kh * group:(kh + 1) * group].reshape(group * blk, 2 * blk)
        for g in range(group):
            sink = sink_ref[kh * group + g]
            parts = []
            for r0 in range(0, blk, SWA_ROWS):
                lg = logits[g * blk + r0:g * blk + r0 + SWA_ROWS]
                m = jnp.maximum(jnp.max(lg, axis=-1, keepdims=True), sink)
                p = jnp.exp(lg - m).astype(BF16)
                pv = jnp.dot(p, vv, preferred_element_type=F32)
                parts.append(pv[:, :hd] / (pv[:, hd:hd + 1] + jnp.exp(sink - m)))
            outs.append(jnp.concatenate(parts, axis=0))
    o_ref[...] = jnp.concatenate(outs, axis=-1).astype(BF16)


def _swa(proj, bias, qg, kg, sinks):
    B, T, _ = proj.shape
    blk = SWA_BLOCK
    qw = SWA_HEADS * SWA_HD
    kvw = SWA_KV_HEADS * SWA_HD
    prev = lambda b, i: jnp.maximum(i - 1, 0)
    seg = (np.arange(qw)[:, None] // SWA_HD == np.arange(LANES)[None, :])
    seg = jnp.asarray(seg, BF16)
    return pl.pallas_call(
        _swa_kernel,
        grid=(B, T // blk),
        in_specs=[
            pl.BlockSpec(memory_space=pltpu.SMEM),
            pl.BlockSpec((None, blk, qw), lambda b, i: (b, i, COL_SQ // qw)),
            pl.BlockSpec((None, blk, kvw), lambda b, i: (b, prev(b, i), COL_SK // kvw)),
            pl.BlockSpec((None, blk, kvw), lambda b, i: (b, i, COL_SK // kvw)),
            pl.BlockSpec((None, blk, kvw), lambda b, i: (b, prev(b, i), COL_SV // kvw)),
            pl.BlockSpec((None, blk, kvw), lambda b, i: (b, i, COL_SV // kvw)),
            pl.BlockSpec((None, SWA_HEADS, blk, 2 * blk), lambda b, i: (jnp.where(i == 0, 1, 0), 0, 0, 0)),
            pl.BlockSpec((1, qw), lambda b, i: (0, 0)),
            pl.BlockSpec((1, kvw), lambda b, i: (0, 0)),
            pl.BlockSpec((qw, LANES), lambda b, i: (0, 0)),
            pl.BlockSpec((LANES, qw), lambda b, i: (0, 0)),
        ],
        out_specs=pl.BlockSpec((None, blk, qw), lambda b, i: (b, i, 0)),
        out_shape=jax.ShapeDtypeStruct((B, T, qw), BF16),
        compiler_params=_params(("arbitrary", "arbitrary")),
        name="swa",
    )(sinks, proj, proj, proj, proj, proj, bias, jnp.tile(qg, (1, SWA_HEADS)),
      jnp.tile(kg, (1, SWA_KV_HEADS)), seg, seg.T)


def _store_token_tiles(ref, x):
    tokens = x.shape[0]
    for q in range(SUBLANES):
        ref[pl.ds(q, tokens, stride=SUBLANES), :] = x[:, q * LANES:(q + 1) * LANES]


def _load_token_tiles(ref, tokens):
    return jnp.concatenate([ref[pl.ds(q, tokens, stride=SUBLANES), :] for q in range(SUBLANES)], axis=1)


def _merge_kernel(x_ref, ga_ref, gb_ref, go_ref, so_ref, wa_ref, wb_ref, wo_ref, g1_ref,
                  n2_ref, sc2_ref, sh2_ref, we_ref, x1_ref, h2_ref, st_ref):
    ya = jnp.dot(go_ref[...], wa_ref[...], preferred_element_type=F32)
    yb = jnp.dot(so_ref[...], wb_ref[...], preferred_element_type=F32)
    m = jax.nn.sigmoid(ga_ref[...].astype(F32)) * ya + jax.nn.sigmoid(gb_ref[...].astype(F32)) * yb
    mixed = jnp.dot(m.astype(BF16), wo_ref[...], preferred_element_type=F32)
    x1 = x_ref[...] + g1_ref[0] * mixed
    y = x1 * lax.rsqrt(jnp.mean(x1 * x1, axis=-1, keepdims=True) + EPS) * n2_ref[...]
    h2 = y * (1.0 + sc2_ref[0]) + sh2_ref[0]
    _store_token_tiles(x1_ref, x1)
    _store_token_tiles(h2_ref, h2)
    st_ref[...] = lax.dot_general(we_ref[...], h2.astype(BF16), NT_DIMS, preferred_element_type=F32)


def _merge(xf, proj2, gla_o, swa_o, wa, wb, wo, g1, n2, sc2, sh2, weff_t, T, tm=256):
    N, D = xf.shape
    B = g1.shape[0]
    SW = weff_t.shape[0]
    bat = lambda i: ((i * tm) // T, 0, 0)
    full = lambda i: (0, 0)
    return pl.pallas_call(
        _merge_kernel,
        grid=(N // tm,),
        in_specs=[
            pl.BlockSpec((tm, D), lambda i: (i, 0)),
            pl.BlockSpec((tm, D), lambda i: (i, COL_GA // D)),
            pl.BlockSpec((tm, D), lambda i: (i, COL_GB // D)),
            pl.BlockSpec((tm, D), lambda i: (i, 0)),
            pl.BlockSpec((tm, D), lambda i: (i, 0)),
            pl.BlockSpec((D, D), full),
            pl.BlockSpec((D, D), full),
            pl.BlockSpec((D, D), full),
            pl.BlockSpec((1, 1, D), bat),
            pl.BlockSpec((1, D), full),
            pl.BlockSpec((1, 1, D), bat),
            pl.BlockSpec((1, 1, D), bat),
            pl.BlockSpec((SW, D), full),
        ],
        out_specs=[
            pl.BlockSpec((tm * SUBLANES, LANES), lambda i: (i, 0)),
            pl.BlockSpec((tm * SUBLANES, LANES), lambda i: (i, 0)),
            pl.BlockSpec((SW, tm), lambda i: (0, i)),
        ],
        out_shape=[
            jax.ShapeDtypeStruct((N * SUBLANES, LANES), F32),
            jax.ShapeDtypeStruct((N * SUBLANES, LANES), F32),
            jax.ShapeDtypeStruct((SW, N), F32),
        ],
        compiler_params=_params(("arbitrary",), VMEM_LIMIT_BYTES),
        name="merge",
    )(xf, proj2, proj2, gla_o, swa_o, wa, wb, wo, g1.reshape(B, 1, D), n2.reshape(1, D),
      sc2.reshape(B, 1, D), sh2.reshape(B, 1, D), weff_t)


def _tree(op, xs):
    xs = list(xs)
    while len(xs) > 1:
        xs = [op(xs[i], xs[i + 1]) for i in range(0, len(xs) - 1, 2)] + ([xs[-1]] if len(xs) % 2 else [])
    return xs[0]


EXTRACT_CHAINS = 4


def _extract_best(ref, n):
    lanes = min(EXTRACT_CHAINS, n)
    best = [ref[i] for i in range(lanes)]
    for i in range(lanes, n):
        best[i % lanes] = jnp.maximum(best[i % lanes], ref[i])
    m = _tree(jnp.maximum, best)
    first = [jnp.where(ref[i] == m, i, n) for i in range(lanes)]
    for i in range(lanes, n):
        first[i % lanes] = jnp.minimum(first[i % lanes], jnp.where(ref[i] == m, i, n))
    pos = _tree(jnp.minimum, first)
    for i in range(n):
        ref[i] = jnp.where(pos == i, -jnp.inf, ref[i])
    return m, pos


def _ordered(a, b):
    (va, ia), (vb, ib) = a, b
    keep = jnp.logical_or(va > vb, jnp.logical_and(va == vb, ia < ib))
    first = (jnp.where(keep, va, vb), jnp.where(keep, ia, ib))
    second = (jnp.where(keep, vb, va), jnp.where(keep, ib, ia))
    return first, second


def _bitonic_merge(xs):
    xs = list(xs)
    j = len(xs) // 2
    while j >= 1:
        for i in range(len(xs)):
            if i & j == 0:
                xs[i], xs[i | j] = _ordered(xs[i], xs[i | j])
        j //= 2
    return xs


def _bitonic_sort(xs):
    xs = list(xs)
    k = 2
    while k <= len(xs):
        j = k // 2
        while j >= 1:
            for i in range(len(xs)):
                if i & j == 0:
                    first, second = _ordered(xs[i], xs[i | j])
                    xs[i], xs[i | j] = (first, second) if i & k == 0 else (second, first)
            j //= 2
        k *= 2
    return xs


def _merge_best(xs, ys):
    n = len(xs)
    return _bitonic_merge([_ordered(xs[i], ys[n - 1 - i])[0] for i in range(n)])


TOPK_PAIRS = tuple((a, b) for a in range(PEER_TOPK) for b in range(PEER_TOPK)
                   if (a + 1) * (b + 1) <= PEER_TOPK)


def _topk_kernel(s_ref, e_ref, g_ref, sv_ref, si_ref, v_ref, i_ref, cand_ref, cidx_ref, best_ref, row_ref):
    K = PEER_TOPK
    nk = PEER_NKEYS
    H = PEER_HEADS
    G = 2 * H
    tm = s_ref.shape[1]
    nblocks = nk // K
    for hb in range(2):
        rows = slice(hb * H, (hb + 1) * H)
        for blk in range(nblocks):
            keys = [(s_ref[(blk * K + i) * G + hb * H:(blk * K + i) * G + (hb + 1) * H, :],
                     jnp.full((H, tm), blk * K + i, I32)) for i in range(K)]
            for i, (v, ix) in enumerate(_bitonic_sort(keys)):
                sv_ref[blk, i, rows, :] = v
                si_ref[blk, i, rows, :] = ix
        step = 1
        while step < nblocks:
            for blk in range(0, nblocks, 2 * step):
                xs = [(sv_ref[blk, i, rows, :], si_ref[blk, i, rows, :]) for i in range(K)]
                ys = [(sv_ref[blk + step, i, rows, :], si_ref[blk + step, i, rows, :]) for i in range(K)]
                for i, (v, ix) in enumerate(_merge_best(xs, ys)):
                    sv_ref[blk, i, rows, :] = v
                    si_ref[blk, i, rows, :] = ix
            step *= 2
    v_ref[...] = sv_ref[0]
    i_ref[...] = si_ref[0]

    for ci, (a, b) in enumerate(TOPK_PAIRS):
        cand_ref[ci] = v_ref[a, 0:H, :] + v_ref[b, H:2 * H, :]
        cidx_ref[ci] = (i_ref[a, 0:H, :] * nk + i_ref[b, H:2 * H, :]) * WORDS_PER_EXPERT
    ncand = len(TOPK_PAIRS)

    def stage2(k, _):
        m, pos = _extract_best(cand_ref, ncand)
        row = _tree(jnp.maximum, [jnp.where(pos == ci, cidx_ref[ci], -1) for ci in range(ncand)])
        best_ref[k] = m
        row_ref[k] = row
        return 0

    lax.fori_loop(0, K, stage2, 0)
    best = best_ref[...]
    ex = jnp.exp(best - best[0:1])
    gates = ex / jnp.sum(ex, axis=0, keepdims=True)
    g_ref[...] = gates.reshape(K * H, tm).T
    e_ref[...] = row_ref[...].reshape(K * H, tm).astype(F32).T.astype(I32)


def _topk(s_t, tm=128):
    SW, N = s_t.shape
    K = PEER_TOPK
    G = 2 * PEER_HEADS
    return pl.pallas_call(
        _topk_kernel,
        grid=(N // tm,),
        in_specs=[pl.BlockSpec((SW, tm), lambda i: (0, i))],
        out_specs=[
            pl.BlockSpec((tm, PEER_SLOTS), lambda i: (i, 0)),
            pl.BlockSpec((tm, PEER_SLOTS), lambda i: (i, 0)),
        ],
        out_shape=[
            jax.ShapeDtypeStruct((N, PEER_SLOTS), I32),
            jax.ShapeDtypeStruct((N, PEER_SLOTS), F32),
        ],
        scratch_shapes=[pltpu.VMEM((PEER_NKEYS // K, K, G, tm), F32),
                        pltpu.VMEM((PEER_NKEYS // K, K, G, tm), I32),
                        pltpu.VMEM((K, G, tm), F32),
                        pltpu.VMEM((K, G, tm), I32),
                        pltpu.VMEM((len(TOPK_PAIRS), PEER_HEADS, tm), F32),
                        pltpu.VMEM((len(TOPK_PAIRS), PEER_HEADS, tm), I32),
                        pltpu.VMEM((K, PEER_HEADS, tm), F32),
                        pltpu.VMEM((K, PEER_HEADS, tm), I32)],
        compiler_params=_params(("arbitrary",)),
        name="peer_topk",
    )(s_t)


WORDS_PER_EXPERT = SUBLANES // 2


TILE_BLOCK = tuple((r % 2) * WORDS_PER_EXPERT + r // 2 for r in range(SUBLANES))
BLOCK_ROW = tuple(TILE_BLOCK.index(q) for q in range(SUBLANES))


def _bf16_bits(x):
    return lax.bitcast_convert_type(x.astype(BF16).astype(F32), jnp.uint32)


def _table_kernel(t_ref, o_ref):
    rows = t_ref.shape[0]
    for s in range(WORDS_PER_EXPERT):
        lo = _bf16_bits(t_ref[:, s * LANES:(s + 1) * LANES])
        hi = _bf16_bits(t_ref[:, (s + WORDS_PER_EXPERT) * LANES:(s + WORDS_PER_EXPERT + 1) * LANES])
        word = lax.shift_right_logical(lo, jnp.uint32(16)) | (hi & jnp.uint32(0xFFFF0000))
        o_ref[pl.ds(s, rows, stride=WORDS_PER_EXPERT), :] = lax.bitcast_convert_type(word, I32)


def _table(tables, layer, te=2048):
    _, n, d = tables.shape
    return pl.pallas_call(
        _table_kernel,
        grid=(n // te,),
        in_specs=[pl.BlockSpec((None, te, d), lambda i: (layer, i, 0))],
        out_specs=pl.BlockSpec((te * WORDS_PER_EXPERT, LANES), lambda i: (i, 0)),
        out_shape=jax.ShapeDtypeStruct((n * WORDS_PER_EXPERT, LANES), I32),
        compiler_params=_params(("arbitrary",)),
        name="peer_table",
    )(tables)


def _expert_tile(tab_ref, e4):
    words = tab_ref[pl.ds(pl.multiple_of(e4, WORDS_PER_EXPERT), WORDS_PER_EXPERT), :]
    return pltpu.bitcast(words, BF16)


def _to_tile_rows(x):
    return jnp.concatenate([x[q:q + 1, :] for q in TILE_BLOCK], axis=0)


def _from_tile_rows(x):
    return jnp.concatenate([x[r:r + 1, :] for r in BLOCK_ROW], axis=0)


def _gelu_tanh(x):
    return 0.5 * x * (1.0 + jnp.tanh(np.sqrt(2.0 / np.pi) * (x + 0.044715 * (x * x * x))))


RING_SLOTS = 2
SLOT_TOKENS = 32
SLAB_TOKENS = 256


def _for_each_token(e_ref, idx_ref, sem, first, count, token_body):
    group0 = first // SLOT_TOKENS
    ngroups = count // SLOT_TOKENS

    def fetch(q, slot):
        return pltpu.make_async_copy(e_ref.at[group0 + q], idx_ref.at[slot], sem.at[slot])

    for slot in range(RING_SLOTS):
        fetch(slot, slot).start()

    def ring(j, _):
        for slot in range(RING_SLOTS):
            q = j * RING_SLOTS + slot
            fetch(q, slot).wait()
            for tt in range(SLOT_TOKENS):
                token_body(first + q * SLOT_TOKENS + tt, lambda k, slot=slot, tt=tt: idx_ref[slot, tt, k])
            fetch(jnp.minimum(q + RING_SLOTS, ngroups - 1), slot).start()
        return 0

    lax.fori_loop(0, ngroups // RING_SLOTS, ring, 0)
    for slot in range(RING_SLOTS):
        fetch(ngroups - 1, slot).wait()


PAIR_GROUP = LANES // SUBLANES
SLOT_GROUPS = PEER_SLOTS // PAIR_GROUP


def _slot_tiles(tab_ref, row, j):
    return jnp.concatenate([_expert_tile(tab_ref, row(j * PAIR_GROUP + kk)) for kk in range(PAIR_GROUP)],
                           axis=0)


def _spread_matrices():
    lane = np.arange(LANES)
    return np.stack([lane[:, None] == PAIR_GROUP * j + lane[None, :] // SUBLANES
                     for j in range(SLOT_GROUPS)])


def _peer_u_kernel(e_ref, h_ref, g_ref, tab_ref, sel_ref, o_ref, slab_ref, idx_ref, sem, *, tb):
    groups_per_token = PEER_SLOTS // SUBLANES
    sel = sel_ref[...]
    rio = lax.broadcasted_iota(I32, (LANES, LANES), 0)
    lio = lax.broadcasted_iota(I32, (LANES, LANES), 1)
    diag = (rio == lio)[None]
    rows = SUBLANES * PEER_SLOTS

    def sub_block(sb, _):
        first = pl.multiple_of(sb * SLAB_TOKENS, SLAB_TOKENS)

        def token(t, row):
            hb = _to_tile_rows(h_ref[t]).astype(BF16)
            for j in range(groups_per_token):
                prods = jnp.concatenate([_expert_tile(tab_ref, row(j * SUBLANES + qn)) * hb
                                         for qn in range(SUBLANES)], axis=0)
                dst = pl.multiple_of((t - first) * PEER_SLOTS + j * SUBLANES, SUBLANES)
                slab_ref[pl.ds(dst, SUBLANES), :] = jnp.dot(sel, prods, preferred_element_type=F32)

        _for_each_token(e_ref, idx_ref, sem, first, SLAB_TOKENS, token)
        for gi in range(SLAB_TOKENS // SUBLANES):
            tok = pl.ds(pl.multiple_of(first + gi * SUBLANES, SUBLANES), SUBLANES)
            rs = jnp.sum(slab_ref[gi * rows:(gi + 1) * rows, :], axis=-1, keepdims=True)
            rs = rs.reshape(SUBLANES, PEER_SLOTS, 1)
            a = jnp.sum(jnp.where(diag, rs, 0.0), axis=1)
            o_ref[tok, :] = g_ref[tok, :] * _gelu_tanh(a)
        return 0

    lax.fori_loop(0, tb // SLAB_TOKENS, sub_block, 0)


def _peer_u(e4, h3, gates, tab, tb=256):
    N = h3.shape[0]
    assert tb % SLAB_TOKENS == 0 and SLAB_TOKENS % (RING_SLOTS * SLOT_TOKENS) == 0
    sel = np.arange(SUBLANES)[:, None] == np.arange(SUBLANES * SUBLANES)[None, :] // SUBLANES
    return pl.pallas_call(
        functools.partial(_peer_u_kernel, tb=tb),
        grid=(N // tb,),
        in_specs=[
            pl.BlockSpec((tb // SLOT_TOKENS, SLOT_TOKENS, PEER_SLOTS), lambda i: (i, 0, 0)),
            pl.BlockSpec((tb, SUBLANES, LANES), lambda i: (i, 0, 0)),
            pl.BlockSpec((tb, PEER_SLOTS), lambda i: (i, 0)),
            pl.BlockSpec(tab.shape, lambda i: (0, 0), pipeline_mode=pl.Buffered(1)),
            pl.BlockSpec(sel.shape, lambda i: (0, 0)),
        ],
        out_specs=pl.BlockSpec((tb, PEER_SLOTS), lambda i: (i, 0)),
        out_shape=jax.ShapeDtypeStruct((N, PEER_SLOTS), F32),
        scratch_shapes=[pltpu.VMEM((SLAB_TOKENS * PEER_SLOTS, LANES), F32),
                        pltpu.SMEM((RING_SLOTS, SLOT_TOKENS, PEER_SLOTS), I32),
                        pltpu.SemaphoreType.DMA((RING_SLOTS,))],
        compiler_params=_params(("arbitrary",), VMEM_LIMIT_BYTES),
        name="peer_u",
    )(e4.reshape(N // SLOT_TOKENS, SLOT_TOKENS, PEER_SLOTS), h3, gates, tab, jnp.asarray(sel, BF16))


def _peer_v_kernel(e_ref, c_ref, x1_ref, g2_ref, tab_ref, spread_ref, o_ref, hi_ref, lo_ref, out_ref, idx_ref,
                   sem, *, tb):
    c = c_ref[...]
    c_hi = c.astype(BF16)
    c_lo = (c - c_hi.astype(F32)).astype(BF16)
    for j in range(SLOT_GROUPS):
        rows = pl.ds(j, tb, stride=SLOT_GROUPS)
        hi_ref[rows, :] = jnp.dot(c_hi, spread_ref[j], preferred_element_type=F32)
        lo_ref[rows, :] = jnp.dot(c_lo, spread_ref[j], preferred_element_type=F32)
    rio = lax.broadcasted_iota(I32, (SUBLANES, LANES), 0)
    lio = lax.broadcasted_iota(I32, (SUBLANES, LANES), 1)
    own_row = (lio & (SUBLANES - 1)) == rio

    def token(t, row):
        base = pl.multiple_of(t * SLOT_GROUPS, SLOT_GROUPS)
        hi = hi_ref[pl.ds(base, SLOT_GROUPS), :]
        lo = lo_ref[pl.ds(base, SLOT_GROUPS), :]
        acc = jnp.zeros((SUBLANES, LANES), F32)
        for j in range(SLOT_GROUPS):
            lhs = jnp.concatenate(
                [jnp.where(own_row, jnp.broadcast_to(part[j:j + 1, :], (SUBLANES, LANES)), 0.0).astype(BF16)
                 for part in (hi, lo)], axis=0)
            out = jnp.dot(lhs, _slot_tiles(tab_ref, row, j), preferred_element_type=F32)
            acc = acc + (out[:SUBLANES] + out[SUBLANES:])
        tile = pl.ds(pl.multiple_of(t * SUBLANES, SUBLANES), SUBLANES)
        out_ref[tile, :] = x1_ref[t] + g2_ref[0] * _from_tile_rows(acc)

    _for_each_token(e_ref, idx_ref, sem, 0, tb, token)
    o_ref[...] = _load_token_tiles(out_ref, tb)


def _peer_v(e4, coef, x1_3, g2_3, tab, T, tb=512):
    N = x1_3.shape[0]
    assert tb % (RING_SLOTS * SLOT_TOKENS) == 0 and tb % SUBLANES == 0
    return pl.pallas_call(
        functools.partial(_peer_v_kernel, tb=tb),
        grid=(N // tb,),
        in_specs=[
            pl.BlockSpec((tb // SLOT_TOKENS, SLOT_TOKENS, PEER_SLOTS), lambda i: (i, 0, 0)),
            pl.BlockSpec((tb, PEER_SLOTS), lambda i: (i, 0)),
            pl.BlockSpec((tb, SUBLANES, LANES), lambda i: (i, 0, 0)),
            pl.BlockSpec((1, SUBLANES, LANES), lambda i: ((i * tb) // T, 0, 0)),
            pl.BlockSpec(tab.shape, lambda i: (0, 0), pipeline_mode=pl.Buffered(1)),
            pl.BlockSpec((SLOT_GROUPS, LANES, LANES), lambda i: (0, 0, 0)),
        ],
        out_specs=pl.BlockSpec((tb, SUBLANES * LANES), lambda i: (i, 0)),
        out_shape=jax.ShapeDtypeStruct((N, SUBLANES * LANES), F32),
        scratch_shapes=[pltpu.VMEM((tb * SLOT_GROUPS, LANES), F32),
                        pltpu.VMEM((tb * SLOT_GROUPS, LANES), F32),
                        pltpu.VMEM((tb * SUBLANES, LANES), F32),
                        pltpu.SMEM((RING_SLOTS, SLOT_TOKENS, PEER_SLOTS), I32),
                        pltpu.SemaphoreType.DMA((RING_SLOTS,))],
        compiler_params=_params(("arbitrary",), VMEM_LIMIT_BYTES),
        name="peer_v",
    )(e4.reshape(N // SLOT_TOKENS, SLOT_TOKENS, PEER_SLOTS), coef, x1_3, g2_3, tab,
      jnp.asarray(_spread_matrices(), BF16))


def kernel(x, c, w_ada, b_ada, norm1_g, w_in, gla_gate_w2, gla_gate_b, gla_norm_g, swa_qnorm_g,
           swa_knorm_g, swa_sinks, rel_bias, w_up_a, w_up_b, w_out, norm2_g, peer_wq, peer_subkeys,
           peer_u, peer_v):
    B, T, D = x.shape
    N = B * T
    L = w_ada.shape[0]
    mod = _adaln(c, w_ada, b_ada)
    weff_t = _fold_peer_keys(peer_wq, peer_subkeys)
    bias = _swa_bias(rel_bias)
    xf = x.reshape(N, D)
    for l in range(L):
        sh1, sc1, g1, sh2, sc2, g2 = [mod[l, :, i * D:(i + 1) * D] for i in range(6)]
        proj = _in_proj(xf, norm1_g[l], sc1, sh1, _pack_w_in(w_in, l), T)
        proj3 = proj.reshape(B, T, PROJ_W)
        w2p = jnp.zeros((LANES, GLA_HEADS * GLA_DK), BF16).at[:GLA_RANK].set(gla_gate_w2[l].astype(BF16))
        gla_o = _gla(proj3, w2p, gla_gate_b[l].reshape(1, -1), gla_norm_g[l].reshape(1, -1))
        swa_o = _swa(proj3, bias, swa_qnorm_g[l].reshape(1, -1), swa_knorm_g[l].reshape(1, -1),
                     swa_sinks[l])
        x1, h2, s_t = _merge(xf, proj, gla_o.reshape(N, -1), swa_o.reshape(N, -1),
                             w_up_a[l].astype(BF16), w_up_b[l].astype(BF16), w_out[l].astype(BF16),
                             g1, norm2_g[l], sc2, sh2, weff_t[l], T)
        e4, gates = _topk(s_t)
        coef = _peer_u(e4, h2.reshape(N, SUBLANES, LANES), gates, _table(peer_u, l))
        g2_3 = g2.reshape(B, SUBLANES, LANES)
        xf = _peer_v(e4, coef, x1.reshape(N, SUBLANES, LANES), g2_3, _table(peer_v, l), T)
    return xf.reshape(B, T, D)
```

```python
import functools

import numpy as np
import jax
import jax.numpy as jnp
from jax import lax
from jax.experimental import pallas as pl
from jax.experimental.pallas import tpu as pltpu

F32 = jnp.float32
BF16 = jnp.bfloat16
I32 = jnp.int32
HIGHEST = lax.Precision.HIGHEST
EPS = 1e-6

GLA_HEADS = 4
GLA_DK = 128
GLA_DV = 256
GLA_RANK = 16
GLA_TAU = 16.0
GLA_CHUNK = 64
SWA_HEADS = 16
SWA_KV_HEADS = 2
SWA_HD = 64
SWA_BLOCK = 128
N_BUCKETS = 32
MAX_DISTANCE = 128
PEER_HEADS = 8
PEER_NKEYS = 128
PEER_TOPK = 16
PEER_SLOTS = PEER_HEADS * PEER_TOPK

SUBLANES = 8
LANES = 128
VMEM_LIMIT_BYTES = 56 * 1024 * 1024

NT_DIMS = (((1,), (1,)), ((), ()))
TN_DIMS = (((0,), (0,)), ((), ()))

COL_Q, COL_K, COL_V, COL_GR, COL_SQ, COL_GA, COL_GB = 0, 512, 1024, 2048, 3072, 4096, 5120
COL_SK, COL_SV, COL_GLR = 6144, 6272, 6400
PROJ_W = 6528


def _params(sem, vmem=None):
    return pltpu.CompilerParams(dimension_semantics=sem, vmem_limit_bytes=vmem)


def _adaln_kernel(c_ref, w_ref, b_ref, o_ref):
    c = c_ref[...]
    a = c * jax.nn.sigmoid(c)
    o_ref[0] = jnp.dot(a, w_ref[0], preferred_element_type=F32, precision=HIGHEST) + b_ref[0]


def _adaln(c, w_ada, b_ada):
    L, D, W = w_ada.shape
    B = c.shape[0]
    rows = -(-B // SUBLANES) * SUBLANES
    cp = jnp.zeros((rows, D), F32).at[:B].set(c)
    tn = W // 4
    out = pl.pallas_call(
        _adaln_kernel,
        grid=(L, W // tn),
        in_specs=[
            pl.BlockSpec((rows, D), lambda l, j: (0, 0)),
            pl.BlockSpec((1, D, tn), lambda l, j: (l, 0, j)),
            pl.BlockSpec((1, 1, tn), lambda l, j: (l, 0, j)),
        ],
        out_specs=pl.BlockSpec((1, rows, tn), lambda l, j: (l, 0, j)),
        out_shape=jax.ShapeDtypeStruct((L, rows, W), F32),
        compiler_params=_params(("arbitrary", "arbitrary")),
        name="adaln",
    )(cp, w_ada, b_ada.reshape(L, 1, W))
    return out[:, :B]


def _fold_kernel(sk_ref, wq_ref, o_ref, rows_ref):
    half = sk_ref.shape[-1]
    groups = 2 * PEER_HEADS
    for h in range(PEER_HEADS):
        for p in range(2):
            g = 2 * h + p
            res = lax.dot_general(sk_ref[0, p], wq_ref[0, :, g * half:(g + 1) * half], NT_DIMS,
                                  precision=HIGHEST, preferred_element_type=F32)
            for cb in range(rows_ref.shape[0]):
                rows_ref[cb, pl.ds(p * PEER_HEADS + h, PEER_NKEYS, stride=groups), :] = (
                    res[:, cb * LANES:(cb + 1) * LANES])
    for cb in range(rows_ref.shape[0]):
        o_ref[0, :, cb * LANES:(cb + 1) * LANES] = rows_ref[cb].astype(BF16)


def _fold_peer_keys(peer_wq, peer_subkeys):
    L, D, QW = peer_wq.shape
    half = peer_subkeys.shape[-1]
    rows = (QW // half) * PEER_NKEYS
    return pl.pallas_call(
        _fold_kernel,
        grid=(L,),
        in_specs=[
            pl.BlockSpec((1, 2, PEER_NKEYS, half), lambda l: (l, 0, 0, 0)),
            pl.BlockSpec((1, D, QW), lambda l: (l, 0, 0)),
        ],
        out_specs=pl.BlockSpec((1, rows, D), lambda l: (l, 0, 0)),
        out_shape=jax.ShapeDtypeStruct((L, rows, D), BF16),
        scratch_shapes=[pltpu.VMEM((D // LANES, rows, LANES), F32)],
        compiler_params=_params(("arbitrary",), VMEM_LIMIT_BYTES),
        name="peer_fold",
    )(peer_subkeys, peer_wq)


def _inproj_kernel(x_ref, g_ref, sc_ref, sh_ref, w_ref, o_ref):
    x = x_ref[...]
    ms = jnp.mean(x * x, axis=-1, keepdims=True)
    y = x * lax.rsqrt(ms + EPS) * g_ref[...]
    h = y * (1.0 + sc_ref[0]) + sh_ref[0]
    o_ref[...] = jnp.dot(h.astype(BF16), w_ref[...], preferred_element_type=F32).astype(BF16)


def _in_proj(xf, g, sc, sh, wp, T, tm=512):
    N, D = xf.shape
    B = sc.shape[0]
    ncol = 3
    tn = PROJ_W // ncol
    return pl.pallas_call(
        _inproj_kernel,
        grid=(ncol, N // tm),
        in_specs=[
            pl.BlockSpec((tm, D), lambda j, i: (i, 0)),
            pl.BlockSpec((1, D), lambda j, i: (0, 0)),
            pl.BlockSpec((1, 1, D), lambda j, i: ((i * tm) // T, 0, 0)),
            pl.BlockSpec((1, 1, D), lambda j, i: ((i * tm) // T, 0, 0)),
            pl.BlockSpec((D, tn), lambda j, i: (0, j)),
        ],
        out_specs=pl.BlockSpec((tm, tn), lambda j, i: (i, j)),
        out_shape=jax.ShapeDtypeStruct((N, PROJ_W), BF16),
        compiler_params=_params(("arbitrary", "arbitrary")),
        name="in_proj",
    )(xf, g.reshape(1, D), sc.reshape(B, 1, D), sh.reshape(B, 1, D), wp)


_SRC = dict(zip(("q", "k", "v", "glr", "gr", "sq", "sk", "sv", "ga", "gb"),
                np.cumsum([0, 512, 512, 1024, GLA_RANK, 1024, 1024, 128, 128, 1024])))
W_IN_SLABS = ((COL_Q, _SRC["q"], 512), (COL_K, _SRC["k"], 512), (COL_V, _SRC["v"], 1024),
              (COL_GR, _SRC["gr"], 1024), (COL_SQ, _SRC["sq"], 1024), (COL_GA, _SRC["ga"], 1024),
              (COL_GB, _SRC["gb"], 1024), (COL_SK, _SRC["sk"], 128), (COL_SV, _SRC["sv"], 128),
              (COL_GLR, _SRC["glr"], GLA_RANK))


def _pack_w_in_kernel(w_ref, o_ref):
    o_ref[:, COL_GLR:] = jnp.zeros((o_ref.shape[0], PROJ_W - COL_GLR), BF16)
    for dst, src, width in W_IN_SLABS:
        o_ref[:, dst:dst + width] = w_ref[:, int(src):int(src) + width].astype(BF16)


def _pack_w_in(w_in, layer, tr=128):
    _, rows, cols = w_in.shape
    return pl.pallas_call(
        _pack_w_in_kernel,
        grid=(rows // tr,),
        in_specs=[pl.BlockSpec((None, tr, cols), lambda i: (layer, i, 0))],
        out_specs=pl.BlockSpec((tr, PROJ_W), lambda i: (i, 0)),
        out_shape=jax.ShapeDtypeStruct((rows, PROJ_W), BF16),
        compiler_params=_params(("arbitrary",)),
        name="pack_w_in",
    )(w_in)


def _gla_kernel(q_ref, k_ref, v_ref, r_ref, glr_ref, w2_ref, b2_ref, ng_ref, o_ref, st_ref, *, nchunk):
    @pl.when(pl.program_id(1) == 0)
    def _():
        st_ref[...] = jnp.zeros_like(st_ref)

    C = GLA_CHUNK
    dk, dv = GLA_DK, GLA_DV
    row = lax.broadcasted_iota(I32, (C, C), 0)
    col = lax.broadcasted_iota(I32, (C, C), 1)
    tri = col <= row
    tri_b = tri.astype(F32).astype(BF16)
    w2 = w2_ref[...]
    b2 = b2_ref[...]
    ng = ng_ref[...]
    for ci in range(nchunk):
        sl = pl.ds(ci * C, C)
        z = jnp.dot(glr_ref[sl, :], w2, preferred_element_type=F32) + b2
        log_a = (jnp.minimum(z, 0.0) - jnp.log(1.0 + jnp.exp(-jnp.abs(z)))) * (1.0 / GLA_TAU)
        la_hi = log_a.astype(BF16)
        la_lo = (log_a - la_hi.astype(F32)).astype(BF16)
        b = (jnp.dot(tri_b, la_hi, preferred_element_type=F32)
             + jnp.dot(tri_b, la_lo, preferred_element_type=F32))
        b_last = b[C - 1:C, :]
        q = q_ref[sl, :].astype(F32) * (dk ** -0.5)
        k = k_ref[sl, :].astype(F32)
        q_dec = (q * jnp.exp(b)).astype(BF16)
        k_inv = (k * jnp.exp(-b)).astype(BF16)
        k_tail = (k * jnp.exp(b_last - b)).astype(BF16)
        decay = jnp.exp(b_last)
        for h in range(GLA_HEADS):
            kc = slice(h * dk, (h + 1) * dk)
            vc = slice(h * dv, (h + 1) * dv)
            v = v_ref[sl, vc]
            attn = lax.dot_general(q_dec[:, kc], k_inv[:, kc], NT_DIMS, preferred_element_type=F32)
            attn = jnp.where(tri, attn, 0.0).astype(BF16)
            st = st_ref[h]
            o = (jnp.dot(attn, v, preferred_element_type=F32)
                 + lax.dot_general(q_dec[:, kc], st.astype(BF16), NT_DIMS, preferred_element_type=F32))
            st_ref[h] = st * decay[:, kc] + lax.dot_general(v, k_tail[:, kc], TN_DIMS,
                                                            preferred_element_type=F32)
            on = o * lax.rsqrt(jnp.mean(o * o, axis=-1, keepdims=True) + EPS) * ng
            r = r_ref[sl, vc].astype(F32)
            o_ref[sl, vc] = (on * (r * jax.nn.sigmoid(r))).astype(BF16)


def _gla(proj, w2p, b2, ng, tc=256):
    B, T, _ = proj.shape
    H = GLA_HEADS
    kw, vw = H * GLA_DK, H * GLA_DV
    return pl.pallas_call(
        functools.partial(_gla_kernel, nchunk=tc // GLA_CHUNK),
        grid=(B, T // tc),
        in_specs=[
            pl.BlockSpec((None, tc, kw), lambda b, c: (b, c, COL_Q // kw)),
            pl.BlockSpec((None, tc, kw), lambda b, c: (b, c, COL_K // kw)),
            pl.BlockSpec((None, tc, vw), lambda b, c: (b, c, COL_V // vw)),
            pl.BlockSpec((None, tc, vw), lambda b, c: (b, c, COL_GR // vw)),
            pl.BlockSpec((None, tc, LANES), lambda b, c: (b, c, COL_GLR // LANES)),
            pl.BlockSpec((LANES, kw), lambda b, c: (0, 0)),
            pl.BlockSpec((1, kw), lambda b, c: (0, 0)),
            pl.BlockSpec((1, GLA_DV), lambda b, c: (0, 0)),
        ],
        out_specs=pl.BlockSpec((None, tc, vw), lambda b, c: (b, c, 0)),
        out_shape=jax.ShapeDtypeStruct((B, T, vw), BF16),
        scratch_shapes=[pltpu.VMEM((H, GLA_DV, GLA_DK), F32)],
        compiler_params=_params(("arbitrary", "arbitrary")),
        name="gla",
    )(proj, proj, proj, proj, proj, w2p, b2, ng)


def _t5_bucket(dist):
    max_exact = N_BUCKETS // 2
    d = np.maximum(dist, 1).astype(np.float32)
    large = max_exact + (np.log(d / max_exact) / np.log(MAX_DISTANCE / max_exact)
                         * (N_BUCKETS - max_exact)).astype(np.int32)
    large = np.minimum(large, N_BUCKETS - 1)
    return np.where(dist < max_exact, dist, large).astype(np.int32)


NEG_BIG = -1e30


def _swa_bias(rel_bias):
    blk = SWA_BLOCK
    qi = np.arange(blk)[:, None]
    sj = np.arange(2 * blk)[None, :]
    dist = blk + qi - sj
    band = (dist >= 0) & (dist < blk)
    bucket = _t5_bucket(np.clip(dist, 0, None))
    onehot = jnp.asarray(bucket.reshape(-1, 1) == np.arange(N_BUCKETS)[None, :], F32)
    bias = jnp.dot(onehot, rel_bias.astype(F32), precision=HIGHEST).T.reshape(-1, blk, 2 * blk)
    masks = np.stack([band, band & (sj >= blk)])[:, None]
    return jnp.where(jnp.asarray(masks), bias[None], NEG_BIG)


def _segment_sums(x, seg):
    hi = x.astype(BF16)
    lo = (x - hi.astype(F32)).astype(BF16)
    return (jnp.dot(hi, seg, preferred_element_type=F32) + jnp.dot(lo, seg, preferred_element_type=F32))


def _head_rms_scale(x, seg, seg_t):
    inv = lax.rsqrt(_segment_sums(x * x, seg) * (1.0 / SWA_HD) + EPS)
    return _segment_sums(inv, seg_t)


SWA_ROWS = 32


def _swa_kernel(sink_ref, q_ref, kp_ref, kc_ref, vp_ref, vc_ref, bias_ref, qg_ref, kg_ref, seg_ref,
                segt_ref, o_ref):
    blk = SWA_BLOCK
    hd = SWA_HD
    group = SWA_HEADS // SWA_KV_HEADS
    kvw = SWA_KV_HEADS * hd
    seg = seg_ref[...]
    seg_t = segt_ref[...]
    q = q_ref[...].astype(F32)
    qn = (q * _head_rms_scale(q, seg, seg_t) * qg_ref[...] * (hd ** -0.5)).astype(BF16)
    k2 = jnp.concatenate([kp_ref[...], kc_ref[...]], axis=0).astype(F32)
    kn = (k2 * _head_rms_scale(k2, seg[:kvw], seg_t[:, :kvw]) * kg_ref[...]).astype(BF16)
    v2 = jnp.concatenate([vp_ref[...], vc_ref[...]], axis=0)
    lane = lax.broadcasted_iota(I32, (2 * blk, kvw), 1)
    outs = []
    for kh in range(SWA_KV_HEADS):
        kk = kn[:, kh * hd:(kh + 1) * hd]
        vsh = v2 if kh == 0 else jnp.concatenate([v2[:, kh * hd:], v2[:, :kh * hd]], axis=1)
        vv = jnp.where(lane < hd, vsh, jnp.ones_like(vsh))
        qs = jnp.concatenate([qn[:, (kh * group + g) * hd:(kh * group + g + 1) * hd]
                              for g in range(group)], axis=0)
        logits = lax.dot_general(qs, kk, NT_DIMS, preferred_element_type=F32)
        logits = logits + bias_ref[kh * group:(kh + 1) * group].reshape(group * blk, 2 * blk)
        for g in range(group):
            sink = sink_ref[kh * group + g]
            parts = []
            for r0 in range(0, blk, SWA_ROWS):
                lg = logits[g * blk + r0:g * blk + r0 + SWA_ROWS]
                m = jnp.maximum(jnp.max(lg, axis=-1, keepdims=True), sink)
                p = jnp.exp(lg - m).astype(BF16)
                pv = jnp.dot(p, vv, preferred_element_type=F32)
                parts.append(pv[:, :hd] / (pv[:, hd:hd + 1] + jnp.exp(sink - m)))
            outs.append(jnp.concatenate(parts, axis=0))
    o_ref[...] = jnp.concatenate(outs, axis=-1).astype(BF16)


def _swa(proj, bias, qg, kg, sinks):
    B, T, _ = proj.shape
    blk = SWA_BLOCK
    qw = SWA_HEADS * SWA_HD
    kvw = SWA_KV_HEADS * SWA_HD
    prev = lambda b, i: jnp.maximum(i - 1, 0)
    seg = (np.arange(qw)[:, None] // SWA_HD == np.arange(LANES)[None, :])
    seg = jnp.asarray(seg, BF16)
    return pl.pallas_call(
        _swa_kernel,
        grid=(B, T // blk),
        in_specs=[
            pl.BlockSpec(memory_space=pltpu.SMEM),
            pl.BlockSpec((None, blk, qw), lambda b, i: (b, i, COL_SQ // qw)),
            pl.BlockSpec((None, blk, kvw), lambda b, i: (b, prev(b, i), COL_SK // kvw)),
            pl.BlockSpec((None, blk, kvw), lambda b, i: (b, i, COL_SK // kvw)),
            pl.BlockSpec((None, blk, kvw), lambda b, i: (b, prev(b, i), COL_SV // kvw)),
            pl.BlockSpec((None, blk, kvw), lambda b, i: (b, i, COL_SV // kvw)),
            pl.BlockSpec((None, SWA_HEADS, blk, 2 * blk), lambda b, i: (jnp.where(i == 0, 1, 0), 0, 0, 0)),
            pl.BlockSpec((1, qw), lambda b, i: (0, 0)),
            pl.BlockSpec((1, kvw), lambda b, i: (0, 0)),
            pl.BlockSpec((qw, LANES), lambda b, i: (0, 0)),
            pl.BlockSpec((LANES, qw), lambda b, i: (0, 0)),
        ],
        out_specs=pl.BlockSpec((None, blk, qw), lambda b, i: (b, i, 0)),
        out_shape=jax.ShapeDtypeStruct((B, T, qw), BF16),
        compiler_params=_params(("arbitrary", "arbitrary")),
        name="swa",
    )(sinks, proj, proj, proj, proj, proj, bias, jnp.tile(qg, (1, SWA_HEADS)),
      jnp.tile(kg, (1, SWA_KV_HEADS)), seg, seg.T)


def _store_token_tiles(ref, x):
    tokens = x.shape[0]
    for q in range(SUBLANES):
        ref[pl.ds(q, tokens, stride=SUBLANES), :] = x[:, q * LANES:(q + 1) * LANES]


def _load_token_tiles(ref, tokens):
    return jnp.concatenate([ref[pl.ds(q, tokens, stride=SUBLANES), :] for q in range(SUBLANES)], axis=1)


def _merge_kernel(x_ref, ga_ref, gb_ref, go_ref, so_ref, wa_ref, wb_ref, wo_ref, g1_ref,
                  n2_ref, sc2_ref, sh2_ref, we_ref, x1_ref, h2_ref, st_ref):
    ya = jnp.dot(go_ref[...], wa_ref[...], preferred_element_type=F32)
    yb = jnp.dot(so_ref[...], wb_ref[...], preferred_element_type=F32)
    m = jax.nn.sigmoid(ga_ref[...].astype(F32)) * ya + jax.nn.sigmoid(gb_ref[...].astype(F32)) * yb
    mixed = jnp.dot(m.astype(BF16), wo_ref[...], preferred_element_type=F32)
    x1 = x_ref[...] + g1_ref[0] * mixed
    y = x1 * lax.rsqrt(jnp.mean(x1 * x1, axis=-1, keepdims=True) + EPS) * n2_ref[...]
    h2 = y * (1.0 + sc2_ref[0]) + sh2_ref[0]
    _store_token_tiles(x1_ref, x1)
    _store_token_tiles(h2_ref, h2)
    st_ref[...] = lax.dot_general(we_ref[...], h2.astype(BF16), NT_DIMS, preferred_element_type=F32)


def _merge(xf, proj2, gla_o, swa_o, wa, wb, wo, g1, n2, sc2, sh2, weff_t, T, tm=256):
    N, D = xf.shape
    B = g1.shape[0]
    SW = weff_t.shape[0]
    bat = lambda i: ((i * tm) // T, 0, 0)
    full = lambda i: (0, 0)
    return pl.pallas_call(
        _merge_kernel,
        grid=(N // tm,),
        in_specs=[
            pl.BlockSpec((tm, D), lambda i: (i, 0)),
            pl.BlockSpec((tm, D), lambda i: (i, COL_GA // D)),
            pl.BlockSpec((tm, D), lambda i: (i, COL_GB // D)),
            pl.BlockSpec((tm, D), lambda i: (i, 0)),
            pl.BlockSpec((tm, D), lambda i: (i, 0)),
            pl.BlockSpec((D, D), full),
            pl.BlockSpec((D, D), full),
            pl.BlockSpec((D, D), full),
            pl.BlockSpec((1, 1, D), bat),
            pl.BlockSpec((1, D), full),
            pl.BlockSpec((1, 1, D), bat),
            pl.BlockSpec((1, 1, D), bat),
            pl.BlockSpec((SW, D), full),
        ],
        out_specs=[
            pl.BlockSpec((tm * SUBLANES, LANES), lambda i: (i, 0)),
            pl.BlockSpec((tm * SUBLANES, LANES), lambda i: (i, 0)),
            pl.BlockSpec((SW, tm), lambda i: (0, i)),
        ],
        out_shape=[
            jax.ShapeDtypeStruct((N * SUBLANES, LANES), F32),
            jax.ShapeDtypeStruct((N * SUBLANES, LANES), F32),
            jax.ShapeDtypeStruct((SW, N), F32),
        ],
        compiler_params=_params(("arbitrary",), VMEM_LIMIT_BYTES),
        name="merge",
    )(xf, proj2, proj2, gla_o, swa_o, wa, wb, wo, g1.reshape(B, 1, D), n2.reshape(1, D),
      sc2.reshape(B, 1, D), sh2.reshape(B, 1, D), weff_t)


def _tree(op, xs):
    xs = list(xs)
    while len(xs) > 1:
        xs = [op(xs[i], xs[i + 1]) for i in range(0, len(xs) - 1, 2)] + ([xs[-1]] if len(xs) % 2 else [])
    return xs[0]


EXTRACT_CHAINS = 4


def _extract_best(ref, n):
    lanes = min(EXTRACT_CHAINS, n)
    best = [ref[i] for i in range(lanes)]
    for i in range(lanes, n):
        best[i % lanes] = jnp.maximum(best[i % lanes], ref[i])
    m = _tree(jnp.maximum, best)
    first = [jnp.where(ref[i] == m, i, n) for i in range(lanes)]
    for i in range(lanes, n):
        first[i % lanes] = jnp.minimum(first[i % lanes], jnp.where(ref[i] == m, i, n))
    pos = _tree(jnp.minimum, first)
    for i in range(n):
        ref[i] = jnp.where(pos == i, -jnp.inf, ref[i])
    return m, pos


def _ordered(a, b):
    (va, ia), (vb, ib) = a, b
    keep = jnp.logical_or(va > vb, jnp.logical_and(va == vb, ia < ib))
    first = (jnp.where(keep, va, vb), jnp.where(keep, ia, ib))
    second = (jnp.where(keep, vb, va), jnp.where(keep, ib, ia))
    return first, second


def _bitonic_merge(xs):
    xs = list(xs)
    j = len(xs) // 2
    while j >= 1:
        for i in range(len(xs)):
            if i & j == 0:
                xs[i], xs[i | j] = _ordered(xs[i], xs[i | j])
        j //= 2
    return xs


def _bitonic_sort(xs):
    xs = list(xs)
    k = 2
    while k <= len(xs):
        j = k // 2
        while j >= 1:
            for i in range(len(xs)):
                if i & j == 0:
                    first, second = _ordered(xs[i], xs[i | j])
                    xs[i], xs[i | j] = (first, second) if i & k == 0 else (second, first)
            j //= 2
        k *= 2
    return xs


def _merge_best(xs, ys):
    n = len(xs)
    return _bitonic_merge([_ordered(xs[i], ys[n - 1 - i])[0] for i in range(n)])


TOPK_PAIRS = tuple((a, b) for a in range(PEER_TOPK) for b in range(PEER_TOPK)
                   if (a + 1) * (b + 1) <= PEER_TOPK)


def _topk_kernel(s_ref, e_ref, g_ref, sv_ref, si_ref, v_ref, i_ref, cand_ref, cidx_ref, best_ref, row_ref):
    K = PEER_TOPK
    nk = PEER_NKEYS
    H = PEER_HEADS
    G = 2 * H
    tm = s_ref.shape[1]
    nblocks = nk // K
    for hb in range(2):
        rows = slice(hb * H, (hb + 1) * H)
        for blk in range(nblocks):
            keys = [(s_ref[(blk * K + i) * G + hb * H:(blk * K + i) * G + (hb + 1) * H, :],
                     jnp.full((H, tm), blk * K + i, I32)) for i in range(K)]
            for i, (v, ix) in enumerate(_bitonic_sort(keys)):
                sv_ref[blk, i, rows, :] = v
                si_ref[blk, i, rows, :] = ix
        step = 1
        while step < nblocks:
            for blk in range(0, nblocks, 2 * step):
                xs = [(sv_ref[blk, i, rows, :], si_ref[blk, i, rows, :]) for i in range(K)]
                ys = [(sv_ref[blk + step, i, rows, :], si_ref[blk + step, i, rows, :]) for i in range(K)]
                for i, (v, ix) in enumerate(_merge_best(xs, ys)):
                    sv_ref[blk, i, rows, :] = v
                    si_ref[blk, i, rows, :] = ix
            step *= 2
    v_ref[...] = sv_ref[0]
    i_ref[...] = si_ref[0]

    for ci, (a, b) in enumerate(TOPK_PAIRS):
        cand_ref[ci] = v_ref[a, 0:H, :] + v_ref[b, H:2 * H, :]
        cidx_ref[ci] = (i_ref[a, 0:H, :] * nk + i_ref[b, H:2 * H, :]) * WORDS_PER_EXPERT
    ncand = len(TOPK_PAIRS)

    def stage2(k, _):
        m, pos = _extract_best(cand_ref, ncand)
        row = _tree(jnp.maximum, [jnp.where(pos == ci, cidx_ref[ci], -1) for ci in range(ncand)])
        best_ref[k] = m
        row_ref[k] = row
        return 0

    lax.fori_loop(0, K, stage2, 0)
    best = best_ref[...]
    ex = jnp.exp(best - best[0:1])
    gates = ex / jnp.sum(ex, axis=0, keepdims=True)
    g_ref[...] = gates.reshape(K * H, tm).T
    e_ref[...] = row_ref[...].reshape(K * H, tm).astype(F32).T.astype(I32)


def _topk(s_t, tm=128):
    SW, N = s_t.shape
    K = PEER_TOPK
    G = 2 * PEER_HEADS
    return pl.pallas_call(
        _topk_kernel,
        grid=(N // tm,),
        in_specs=[pl.BlockSpec((SW, tm), lambda i: (0, i))],
        out_specs=[
            pl.BlockSpec((tm, PEER_SLOTS), lambda i: (i, 0)),
            pl.BlockSpec((tm, PEER_SLOTS), lambda i: (i, 0)),
        ],
        out_shape=[
            jax.ShapeDtypeStruct((N, PEER_SLOTS), I32),
            jax.ShapeDtypeStruct((N, PEER_SLOTS), F32),
        ],
        scratch_shapes=[pltpu.VMEM((PEER_NKEYS // K, K, G, tm), F32),
                        pltpu.VMEM((PEER_NKEYS // K, K, G, tm), I32),
                        pltpu.VMEM((K, G, tm), F32),
                        pltpu.VMEM((K, G, tm), I32),
                        pltpu.VMEM((len(TOPK_PAIRS), PEER_HEADS, tm), F32),
                        pltpu.VMEM((len(TOPK_PAIRS), PEER_HEADS, tm), I32),
                        pltpu.VMEM((K, PEER_HEADS, tm), F32),
                        pltpu.VMEM((K, PEER_HEADS, tm), I32)],
        compiler_params=_params(("arbitrary",)),
        name="peer_topk",
    )(s_t)


WORDS_PER_EXPERT = SUBLANES // 2


TILE_BLOCK = tuple((r % 2) * WORDS_PER_EXPERT + r // 2 for r in range(SUBLANES))
BLOCK_ROW = tuple(TILE_BLOCK.index(q) for q in range(SUBLANES))


def _bf16_bits(x):
    return lax.bitcast_convert_type(x.astype(BF16).astype(F32), jnp.uint32)


def _table_kernel(t_ref, o_ref):
    rows = t_ref.shape[0]
    for s in range(WORDS_PER_EXPERT):
        lo = _bf16_bits(t_ref[:, s * LANES:(s + 1) * LANES])
        hi = _bf16_bits(t_ref[:, (s + WORDS_PER_EXPERT) * LANES:(s + WORDS_PER_EXPERT + 1) * LANES])
        word = lax.shift_right_logical(lo, jnp.uint32(16)) | (hi & jnp.uint32(0xFFFF0000))
        o_ref[pl.ds(s, rows, stride=WORDS_PER_EXPERT), :] = lax.bitcast_convert_type(word, I32)


def _table(tables, layer, te=2048):
    _, n, d = tables.shape
    return pl.pallas_call(
        _table_kernel,
        grid=(n // te,),
        in_specs=[pl.BlockSpec((None, te, d), lambda i: (layer, i, 0))],
        out_specs=pl.BlockSpec((te * WORDS_PER_EXPERT, LANES), lambda i: (i, 0)),
        out_shape=jax.ShapeDtypeStruct((n * WORDS_PER_EXPERT, LANES), I32),
        compiler_params=_params(("arbitrary",)),
        name="peer_table",
    )(tables)


def _expert_tile(tab_ref, e4):
    words = tab_ref[pl.ds(pl.multiple_of(e4, WORDS_PER_EXPERT), WORDS_PER_EXPERT), :]
    return pltpu.bitcast(words, BF16)


def _to_tile_rows(x):
    return jnp.concatenate([x[q:q + 1, :] for q in TILE_BLOCK], axis=0)


def _from_tile_rows(x):
    return jnp.concatenate([x[r:r + 1, :] for r in BLOCK_ROW], axis=0)


def _gelu_tanh(x):
    return 0.5 * x * (1.0 + jnp.tanh(np.sqrt(2.0 / np.pi) * (x + 0.044715 * (x * x * x))))


RING_SLOTS = 2
SLOT_TOKENS = 32
SLAB_TOKENS = 256


def _for_each_token(e_ref, idx_ref, sem, first, count, token_body, warmup=None):
    group0 = first // SLOT_TOKENS
    ngroups = count // SLOT_TOKENS

    def fetch(q, slot):
        return pltpu.make_async_copy(e_ref.at[group0 + q], idx_ref.at[slot], sem.at[slot])

    for slot in range(RING_SLOTS):
        fetch(slot, slot).start()
    if warmup is not None:
        warmup()

    def ring(j, _):
        for slot in range(RING_SLOTS):
            q = j * RING_SLOTS + slot
            fetch(q, slot).wait()
            for tt in range(SLOT_TOKENS):
                token_body(first + q * SLOT_TOKENS + tt, lambda k, slot=slot, tt=tt: idx_ref[slot, tt, k])
            fetch(jnp.minimum(q + RING_SLOTS, ngroups - 1), slot).start()
        return 0

    lax.fori_loop(0, ngroups // RING_SLOTS, ring, 0)
    for slot in range(RING_SLOTS):
        fetch(ngroups - 1, slot).wait()


PAIR_GROUP = LANES // SUBLANES
SLOT_GROUPS = PEER_SLOTS // PAIR_GROUP


def _slot_tiles(tab_ref, row, j):
    return jnp.concatenate([_expert_tile(tab_ref, row(j * PAIR_GROUP + kk)) for kk in range(PAIR_GROUP)],
                           axis=0)


def _spread_matrices():
    lane = np.arange(LANES)
    return np.stack([lane[:, None] == PAIR_GROUP * j + lane[None, :] // SUBLANES
                     for j in range(SLOT_GROUPS)])


def _peer_u_kernel(e_ref, h_ref, g_ref, tab_ref, sel_ref, o_ref, slab_ref, idx_ref, sem, *, tb):
    groups_per_token = PEER_SLOTS // SUBLANES
    sel = sel_ref[...]
    rio = lax.broadcasted_iota(I32, (LANES, LANES), 0)
    lio = lax.broadcasted_iota(I32, (LANES, LANES), 1)
    diag = (rio == lio)[None]
    rows = SUBLANES * PEER_SLOTS

    def sub_block(sb, _):
        first = pl.multiple_of(sb * SLAB_TOKENS, SLAB_TOKENS)

        def token(t, row):
            hb = _to_tile_rows(h_ref[t]).astype(BF16)
            for j in range(groups_per_token):
                prods = jnp.concatenate([_expert_tile(tab_ref, row(j * SUBLANES + qn)) * hb
                                         for qn in range(SUBLANES)], axis=0)
                dst = pl.multiple_of((t - first) * PEER_SLOTS + j * SUBLANES, SUBLANES)
                slab_ref[pl.ds(dst, SUBLANES), :] = jnp.dot(sel, prods, preferred_element_type=F32)

        _for_each_token(e_ref, idx_ref, sem, first, SLAB_TOKENS, token)
        for gi in range(SLAB_TOKENS // SUBLANES):
            tok = pl.ds(pl.multiple_of(first + gi * SUBLANES, SUBLANES), SUBLANES)
            rs = jnp.sum(slab_ref[gi * rows:(gi + 1) * rows, :], axis=-1, keepdims=True)
            rs = rs.reshape(SUBLANES, PEER_SLOTS, 1)
            a = jnp.sum(jnp.where(diag, rs, 0.0), axis=1)
            o_ref[tok, :] = g_ref[tok, :] * _gelu_tanh(a)
        return 0

    lax.fori_loop(0, tb // SLAB_TOKENS, sub_block, 0)


def _peer_u(e4, h3, gates, tab, tb=256):
    N = h3.shape[0]
    assert tb % SLAB_TOKENS == 0 and SLAB_TOKENS % (RING_SLOTS * SLOT_TOKENS) == 0
    sel = np.arange(SUBLANES)[:, None] == np.arange(SUBLANES * SUBLANES)[None, :] // SUBLANES
    return pl.pallas_call(
        functools.partial(_peer_u_kernel, tb=tb),
        grid=(N // tb,),
        in_specs=[
            pl.BlockSpec((tb // SLOT_TOKENS, SLOT_TOKENS, PEER_SLOTS), lambda i: (i, 0, 0)),
            pl.BlockSpec((tb, SUBLANES, LANES), lambda i: (i, 0, 0)),
            pl.BlockSpec((tb, PEER_SLOTS), lambda i: (i, 0)),
            pl.BlockSpec(tab.shape, lambda i: (0, 0), pipeline_mode=pl.Buffered(1)),
            pl.BlockSpec(sel.shape, lambda i: (0, 0)),
        ],
        out_specs=pl.BlockSpec((tb, PEER_SLOTS), lambda i: (i, 0)),
        out_shape=jax.ShapeDtypeStruct((N, PEER_SLOTS), F32),
        scratch_shapes=[pltpu.VMEM((SLAB_TOKENS * PEER_SLOTS, LANES), F32),
                        pltpu.SMEM((RING_SLOTS, SLOT_TOKENS, PEER_SLOTS), I32),
                        pltpu.SemaphoreType.DMA((RING_SLOTS,))],
        compiler_params=_params(("arbitrary",), VMEM_LIMIT_BYTES),
        name="peer_u",
    )(e4.reshape(N // SLOT_TOKENS, SLOT_TOKENS, PEER_SLOTS), h3, gates, tab, jnp.asarray(sel, BF16))


def _peer_v_kernel(e_ref, c_ref, x1_ref, g2_ref, tab_ref, spread_ref, o_ref, hi_ref, lo_ref, out_ref, idx_ref,
                   sem, *, tb):
    def spread_coefficients():
        c = c_ref[...]
        c_hi = c.astype(BF16)
        c_lo = (c - c_hi.astype(F32)).astype(BF16)
        for j in range(SLOT_GROUPS):
            rows = pl.ds(j, tb, stride=SLOT_GROUPS)
            hi_ref[rows, :] = jnp.dot(c_hi, spread_ref[j], preferred_element_type=F32)
            lo_ref[rows, :] = jnp.dot(c_lo, spread_ref[j], preferred_element_type=F32)

    rio = lax.broadcasted_iota(I32, (SUBLANES, LANES), 0)
    lio = lax.broadcasted_iota(I32, (SUBLANES, LANES), 1)
    own_row = (lio & (SUBLANES - 1)) == rio

    def token(t, row):
        base = pl.multiple_of(t * SLOT_GROUPS, SLOT_GROUPS)
        hi = hi_ref[pl.ds(base, SLOT_GROUPS), :]
        lo = lo_ref[pl.ds(base, SLOT_GROUPS), :]
        acc = jnp.zeros((SUBLANES, LANES), F32)
        for j in range(SLOT_GROUPS):
            lhs = jnp.concatenate(
                [jnp.where(own_row, jnp.broadcast_to(part[j:j + 1, :], (SUBLANES, LANES)), 0.0).astype(BF16)
                 for part in (hi, lo)], axis=0)
            out = jnp.dot(lhs, _slot_tiles(tab_ref, row, j), preferred_element_type=F32)
            acc = acc + (out[:SUBLANES] + out[SUBLANES:])
        tile = pl.ds(pl.multiple_of(t * SUBLANES, SUBLANES), SUBLANES)
        out_ref[tile, :] = x1_ref[t] + g2_ref[0] * _from_tile_rows(acc)

    _for_each_token(e_ref, idx_ref, sem, 0, tb, token, warmup=spread_coefficients)
    o_ref[...] = _load_token_tiles(out_ref, tb)


def _peer_v(e4, coef, x1_3, g2_3, tab, T, tb=512):
    N = x1_3.shape[0]
    assert tb % (RING_SLOTS * SLOT_TOKENS) == 0 and tb % SUBLANES == 0
    return pl.pallas_call(
        functools.partial(_peer_v_kernel, tb=tb),
        grid=(N // tb,),
        in_specs=[
            pl.BlockSpec((tb // SLOT_TOKENS, SLOT_TOKENS, PEER_SLOTS), lambda i: (i, 0, 0)),
            pl.BlockSpec((tb, PEER_SLOTS), lambda i: (i, 0)),
            pl.BlockSpec((tb, SUBLANES, LANES), lambda i: (i, 0, 0)),
            pl.BlockSpec((1, SUBLANES, LANES), lambda i: ((i * tb) // T, 0, 0)),
            pl.BlockSpec(tab.shape, lambda i: (0, 0), pipeline_mode=pl.Buffered(1)),
            pl.BlockSpec((SLOT_GROUPS, LANES, LANES), lambda i: (0, 0, 0)),
        ],
        out_specs=pl.BlockSpec((tb, SUBLANES * LANES), lambda i: (i, 0)),
        out_shape=jax.ShapeDtypeStruct((N, SUBLANES * LANES), F32),
        scratch_shapes=[pltpu.VMEM((tb * SLOT_GROUPS, LANES), F32),
                        pltpu.VMEM((tb * SLOT_GROUPS, LANES), F32),
                        pltpu.VMEM((tb * SUBLANES, LANES), F32),
                        pltpu.SMEM((RING_SLOTS, SLOT_TOKENS, PEER_SLOTS), I32),
                        pltpu.SemaphoreType.DMA((RING_SLOTS,))],
        compiler_params=_params(("arbitrary",), VMEM_LIMIT_BYTES),
        name="peer_v",
    )(e4.reshape(N // SLOT_TOKENS, SLOT_TOKENS, PEER_SLOTS), coef, x1_3, g2_3, tab,
      jnp.asarray(_spread_matrices(), BF16))


def kernel(x, c, w_ada, b_ada, norm1_g, w_in, gla_gate_w2, gla_gate_b, gla_norm_g, swa_qnorm_g,
           swa_knorm_g, swa_sinks, rel_bias, w_up_a, w_up_b, w_out, norm2_g, peer_wq, peer_subkeys,
           peer_u, peer_v):
    B, T, D = x.shape
    N = B * T
    L = w_ada.shape[0]
    mod = _adaln(c, w_ada, b_ada)
    weff_t = _fold_peer_keys(peer_wq, peer_subkeys)
    bias = _swa_bias(rel_bias)
    xf = x.reshape(N, D)
    for l in range(L):
        sh1, sc1, g1, sh2, sc2, g2 = [mod[l, :, i * D:(i + 1) * D] for i in range(6)]
        proj = _in_proj(xf, norm1_g[l], sc1, sh1, _pack_w_in(w_in, l), T)
        proj3 = proj.reshape(B, T, PROJ_W)
        w2p = jnp.zeros((LANES, GLA_HEADS * GLA_DK), BF16).at[:GLA_RANK].set(gla_gate_w2[l].astype(BF16))
        gla_o = _gla(proj3, w2p, gla_gate_b[l].reshape(1, -1), gla_norm_g[l].reshape(1, -1))
        swa_o = _swa(proj3, bias, swa_qnorm_g[l].reshape(1, -1), swa_knorm_g[l].reshape(1, -1),
                     swa_sinks[l])
        x1, h2, s_t = _merge(xf, proj, gla_o.reshape(N, -1), swa_o.reshape(N, -1),
                             w_up_a[l].astype(BF16), w_up_b[l].astype(BF16), w_out[l].astype(BF16),
                             g1, norm2_g[l], sc2, sh2, weff_t[l], T)
        e4, gates = _topk(s_t)
        coef = _peer_u(e4, h2.reshape(N, SUBLANES, LANES), gates, _table(peer_u, l))
        g2_3 = g2.reshape(B, SUBLANES, LANES)
        xf = _peer_v(e4, coef, x1.reshape(N, SUBLANES, LANES), g2_3, _table(peer_v, l), T)
    return xf.reshape(B, T, D)
```

```python
import functools

import numpy as np
import jax
import jax.numpy as jnp
from jax import lax
from jax.experimental import pallas as pl
from jax.experimental.pallas import tpu as pltpu

F32 = jnp.float32
BF16 = jnp.bfloat16
I32 = jnp.int32
HIGHEST = lax.Precision.HIGHEST
EPS = 1e-6

GLA_HEADS = 4
GLA_DK = 128
GLA_DV = 256
GLA_RANK = 16
GLA_TAU = 16.0
GLA_CHUNK = 64
SWA_HEADS = 16
SWA_KV_HEADS = 2
SWA_HD = 64
SWA_BLOCK = 128
N_BUCKETS = 32
MAX_DISTANCE = 128
PEER_HEADS = 8
PEER_NKEYS = 128
PEER_TOPK = 16
PEER_SLOTS = PEER_HEADS * PEER_TOPK

SUBLANES = 8
LANES = 128
VMEM_LIMIT_BYTES = 56 * 1024 * 1024

NT_DIMS = (((1,), (1,)), ((), ()))
TN_DIMS = (((0,), (0,)), ((), ()))

COL_Q, COL_K, COL_V, COL_GR, COL_SQ, COL_GA, COL_GB = 0, 512, 1024, 2048, 3072, 4096, 5120
COL_SK, COL_SV, COL_GLR = 6144, 6272, 6400
PROJ_W = 6528


def _params(sem, vmem=None):
    return pltpu.CompilerParams(dimension_semantics=sem, vmem_limit_bytes=vmem)


def _adaln_kernel(c_ref, w_ref, b_ref, o_ref):
    c = c_ref[...]
    a = c * jax.nn.sigmoid(c)
    o_ref[0] = jnp.dot(a, w_ref[0], preferred_element_type=F32, precision=HIGHEST) + b_ref[0]


def _adaln(c, w_ada, b_ada):
    L, D, W = w_ada.shape
    B = c.shape[0]
    rows = -(-B // SUBLANES) * SUBLANES
    cp = jnp.zeros((rows, D), F32).at[:B].set(c)
    tn = W // 4
    out = pl.pallas_call(
        _adaln_kernel,
        grid=(L, W // tn),
        in_specs=[
            pl.BlockSpec((rows, D), lambda l, j: (0, 0)),
            pl.BlockSpec((1, D, tn), lambda l, j: (l, 0, j)),
            pl.BlockSpec((1, 1, tn), lambda l, j: (l, 0, j)),
        ],
        out_specs=pl.BlockSpec((1, rows, tn), lambda l, j: (l, 0, j)),
        out_shape=jax.ShapeDtypeStruct((L, rows, W), F32),
        compiler_params=_params(("arbitrary", "arbitrary")),
        name="adaln",
    )(cp, w_ada, b_ada.reshape(L, 1, W))
    return out[:, :B]


def _fold_kernel(sk_ref, wq_ref, o_ref, rows_ref):
    half = sk_ref.shape[-1]
    groups = 2 * PEER_HEADS
    for h in range(PEER_HEADS):
        for p in range(2):
            g = 2 * h + p
            res = lax.dot_general(sk_ref[0, p], wq_ref[0, :, g * half:(g + 1) * half], NT_DIMS,
                                  precision=HIGHEST, preferred_element_type=F32)
            for cb in range(rows_ref.shape[0]):
                rows_ref[cb, pl.ds(p * PEER_HEADS + h, PEER_NKEYS, stride=groups), :] = (
                    res[:, cb * LANES:(cb + 1) * LANES])
    for cb in range(rows_ref.shape[0]):
        o_ref[0, :, cb * LANES:(cb + 1) * LANES] = rows_ref[cb].astype(BF16)


def _fold_peer_keys(peer_wq, peer_subkeys):
    L, D, QW = peer_wq.shape
    half = peer_subkeys.shape[-1]
    rows = (QW // half) * PEER_NKEYS
    return pl.pallas_call(
        _fold_kernel,
        grid=(L,),
        in_specs=[
            pl.BlockSpec((1, 2, PEER_NKEYS, half), lambda l: (l, 0, 0, 0)),
            pl.BlockSpec((1, D, QW), lambda l: (l, 0, 0)),
        ],
        out_specs=pl.BlockSpec((1, rows, D), lambda l: (l, 0, 0)),
        out_shape=jax.ShapeDtypeStruct((L, rows, D), BF16),
        scratch_shapes=[pltpu.VMEM((D // LANES, rows, LANES), F32)],
        compiler_params=_params(("arbitrary",), VMEM_LIMIT_BYTES),
        name="peer_fold",
    )(peer_subkeys, peer_wq)


def _inproj_kernel(x_ref, g_ref, sc_ref, sh_ref, w_ref, o_ref):
    x = x_ref[...]
    ms = jnp.mean(x * x, axis=-1, keepdims=True)
    y = x * lax.rsqrt(ms + EPS) * g_ref[...]
    h = y * (1.0 + sc_ref[0]) + sh_ref[0]
    o_ref[...] = jnp.dot(h.astype(BF16), w_ref[...], preferred_element_type=F32).astype(BF16)


def _in_proj(xf, g, sc, sh, wp, T, tm=512):
    N, D = xf.shape
    B = sc.shape[0]
    ncol = 3
    tn = PROJ_W // ncol
    return pl.pallas_call(
        _inproj_kernel,
        grid=(ncol, N // tm),
        in_specs=[
            pl.BlockSpec((tm, D), lambda j, i: (i, 0)),
            pl.BlockSpec((1, D), lambda j, i: (0, 0)),
            pl.BlockSpec((1, 1, D), lambda j, i: ((i * tm) // T, 0, 0)),
            pl.BlockSpec((1, 1, D), lambda j, i: ((i * tm) // T, 0, 0)),
            pl.BlockSpec((D, tn), lambda j, i: (0, j)),
        ],
        out_specs=pl.BlockSpec((tm, tn), lambda j, i: (i, j)),
        out_shape=jax.ShapeDtypeStruct((N, PROJ_W), BF16),
        compiler_params=_params(("arbitrary", "arbitrary")),
        name="in_proj",
    )(xf, g.reshape(1, D), sc.reshape(B, 1, D), sh.reshape(B, 1, D), wp)


_SRC = dict(zip(("q", "k", "v", "glr", "gr", "sq", "sk", "sv", "ga", "gb"),
                np.cumsum([0, 512, 512, 1024, GLA_RANK, 1024, 1024, 128, 128, 1024])))
W_IN_SLABS = ((COL_Q, _SRC["q"], 512), (COL_K, _SRC["k"], 512), (COL_V, _SRC["v"], 1024),
              (COL_GR, _SRC["gr"], 1024), (COL_SQ, _SRC["sq"], 1024), (COL_GA, _SRC["ga"], 1024),
              (COL_GB, _SRC["gb"], 1024), (COL_SK, _SRC["sk"], 128), (COL_SV, _SRC["sv"], 128),
              (COL_GLR, _SRC["glr"], GLA_RANK))


def _pack_w_in_kernel(w_ref, o_ref):
    o_ref[:, COL_GLR:] = jnp.zeros((o_ref.shape[0], PROJ_W - COL_GLR), BF16)
    for dst, src, width in W_IN_SLABS:
        o_ref[:, dst:dst + width] = w_ref[:, int(src):int(src) + width].astype(BF16)


def _pack_w_in(w_in, layer, tr=128):
    _, rows, cols = w_in.shape
    return pl.pallas_call(
        _pack_w_in_kernel,
        grid=(rows // tr,),
        in_specs=[pl.BlockSpec((None, tr, cols), lambda i: (layer, i, 0))],
        out_specs=pl.BlockSpec((tr, PROJ_W), lambda i: (i, 0)),
        out_shape=jax.ShapeDtypeStruct((rows, PROJ_W), BF16),
        compiler_params=_params(("arbitrary",)),
        name="pack_w_in",
    )(w_in)


def _gla_kernel(q_ref, k_ref, v_ref, r_ref, glr_ref, w2_ref, b2_ref, ng_ref, o_ref, st_ref, *, nchunk):
    @pl.when(pl.program_id(1) == 0)
    def _():
        st_ref[...] = jnp.zeros_like(st_ref)

    C = GLA_CHUNK
    dk, dv = GLA_DK, GLA_DV
    row = lax.broadcasted_iota(I32, (C, C), 0)
    col = lax.broadcasted_iota(I32, (C, C), 1)
    tri = col <= row
    tri_b = tri.astype(F32).astype(BF16)
    w2 = w2_ref[...]
    b2 = b2_ref[...]
    ng = ng_ref[...]
    for ci in range(nchunk):
        sl = pl.ds(ci * C, C)
        z = jnp.dot(glr_ref[sl, :], w2, preferred_element_type=F32) + b2
        log_a = (jnp.minimum(z, 0.0) - jnp.log(1.0 + jnp.exp(-jnp.abs(z)))) * (1.0 / GLA_TAU)
        la_hi = log_a.astype(BF16)
        la_lo = (log_a - la_hi.astype(F32)).astype(BF16)
        b = (jnp.dot(tri_b, la_hi, preferred_element_type=F32)
             + jnp.dot(tri_b, la_lo, preferred_element_type=F32))
        b_last = b[C - 1:C, :]
        q = q_ref[sl, :].astype(F32) * (dk ** -0.5)
        k = k_ref[sl, :].astype(F32)
        q_dec = (q * jnp.exp(b)).astype(BF16)
        k_inv = (k * jnp.exp(-b)).astype(BF16)
        k_tail = (k * jnp.exp(b_last - b)).astype(BF16)
        decay = jnp.exp(b_last)
        for h in range(GLA_HEADS):
            kc = slice(h * dk, (h + 1) * dk)
            vc = slice(h * dv, (h + 1) * dv)
            v = v_ref[sl, vc]
            attn = lax.dot_general(q_dec[:, kc], k_inv[:, kc], NT_DIMS, preferred_element_type=F32)
            attn = jnp.where(tri, attn, 0.0).astype(BF16)
            st = st_ref[h]
            o = (jnp.dot(attn, v, preferred_element_type=F32)
                 + lax.dot_general(q_dec[:, kc], st.astype(BF16), NT_DIMS, preferred_element_type=F32))
            st_ref[h] = st * decay[:, kc] + lax.dot_general(v, k_tail[:, kc], TN_DIMS,
                                                            preferred_element_type=F32)
            on = o * lax.rsqrt(jnp.mean(o * o, axis=-1, keepdims=True) + EPS) * ng
            r = r_ref[sl, vc].astype(F32)
            o_ref[sl, vc] = (on * (r * jax.nn.sigmoid(r))).astype(BF16)


def _gla(proj, w2p, b2, ng, tc=256):
    B, T, _ = proj.shape
    H = GLA_HEADS
    kw, vw = H * GLA_DK, H * GLA_DV
    return pl.pallas_call(
        functools.partial(_gla_kernel, nchunk=tc // GLA_CHUNK),
        grid=(B, T // tc),
        in_specs=[
            pl.BlockSpec((None, tc, kw), lambda b, c: (b, c, COL_Q // kw)),
            pl.BlockSpec((None, tc, kw), lambda b, c: (b, c, COL_K // kw)),
            pl.BlockSpec((None, tc, vw), lambda b, c: (b, c, COL_V // vw)),
            pl.BlockSpec((None, tc, vw), lambda b, c: (b, c, COL_GR // vw)),
            pl.BlockSpec((None, tc, LANES), lambda b, c: (b, c, COL_GLR // LANES)),
            pl.BlockSpec((LANES, kw), lambda b, c: (0, 0)),
            pl.BlockSpec((1, kw), lambda b, c: (0, 0)),
            pl.BlockSpec((1, GLA_DV), lambda b, c: (0, 0)),
        ],
        out_specs=pl.BlockSpec((None, tc, vw), lambda b, c: (b, c, 0)),
        out_shape=jax.ShapeDtypeStruct((B, T, vw), BF16),
        scratch_shapes=[pltpu.VMEM((H, GLA_DV, GLA_DK), F32)],
        compiler_params=_params(("arbitrary", "arbitrary")),
        name="gla",
    )(proj, proj, proj, proj, proj, w2p, b2, ng)


def _t5_bucket(dist):
    max_exact = N_BUCKETS // 2
    d = np.maximum(dist, 1).astype(np.float32)
    large = max_exact + (np.log(d / max_exact) / np.log(MAX_DISTANCE / max_exact)
                         * (N_BUCKETS - max_exact)).astype(np.int32)
    large = np.minimum(large, N_BUCKETS - 1)
    return np.where(dist < max_exact, dist, large).astype(np.int32)


NEG_BIG = -1e30


def _swa_bias(rel_bias):
    blk = SWA_BLOCK
    qi = np.arange(blk)[:, None]
    sj = np.arange(2 * blk)[None, :]
    dist = blk + qi - sj
    band = (dist >= 0) & (dist < blk)
    bucket = _t5_bucket(np.clip(dist, 0, None))
    onehot = jnp.asarray(bucket.reshape(-1, 1) == np.arange(N_BUCKETS)[None, :], F32)
    bias = jnp.dot(onehot, rel_bias.astype(F32), precision=HIGHEST).T.reshape(-1, blk, 2 * blk)
    masks = np.stack([band, band & (sj >= blk)])[:, None]
    return jnp.where(jnp.asarray(masks), bias[None], NEG_BIG)


def _segment_sums(x, seg):
    hi = x.astype(BF16)
    lo = (x - hi.astype(F32)).astype(BF16)
    return (jnp.dot(hi, seg, preferred_element_type=F32) + jnp.dot(lo, seg, preferred_element_type=F32))


def _head_rms_scale(x, seg, seg_t):
    inv = lax.rsqrt(_segment_sums(x * x, seg) * (1.0 / SWA_HD) + EPS)
    return _segment_sums(inv, seg_t)


SWA_ROWS = 32


def _swa_kernel(sink_ref, q_ref, kp_ref, kc_ref, vp_ref, vc_ref, bias_ref, qg_ref, kg_ref, seg_ref,
                segt_ref, o_ref):
    blk = SWA_BLOCK
    hd = SWA_HD
    group = SWA_HEADS // SWA_KV_HEADS
    kvw = SWA_KV_HEADS * hd
    seg = seg_ref[...]
    seg_t = segt_ref[...]
    q = q_ref[...].astype(F32)
    qn = (q * _head_rms_scale(q, seg, seg_t) * qg_ref[...] * (hd ** -0.5)).astype(BF16)
    k2 = jnp.concatenate([kp_ref[...], kc_ref[...]], axis=0).astype(F32)
    kn = (k2 * _head_rms_scale(k2, seg[:kvw], seg_t[:, :kvw]) * kg_ref[...]).astype(BF16)
    v2 = jnp.concatenate([vp_ref[...], vc_ref[...]], axis=0)
    lane = lax.broadcasted_iota(I32, (2 * blk, kvw), 1)
    outs = []
    for kh in range(SWA_KV_HEADS):
        kk = kn[:, kh * hd:(kh + 1) * hd]
        vsh = v2 if kh == 0 else jnp.concatenate([v2[:, kh * hd:], v2[:, :kh * hd]], axis=1)
        vv = jnp.where(lane < hd, vsh, jnp.ones_like(vsh))
        qs = jnp.concatenate([qn[:, (kh * group + g) * hd:(kh * group + g + 1) * hd]
                              for g in range(group)], axis=0)
        logits = lax.dot_general(qs, kk, NT_DIMS, preferred_element_type=F32)
        logits = logits + bias_ref[kh * group:(kh + 1) * group].reshape(group * blk, 2 * blk)
        for g in range(group):
            sink = sink_ref[kh * group + g]
            parts = []
            for r0 in range(0, blk, SWA_ROWS):
                lg = logits[g * blk + r0:g * blk + r0 + SWA_ROWS]
                m = jnp.maximum(jnp.max(lg, axis=-1, keepdims=True), sink)
                p = jnp.exp(lg - m).astype(BF16)
                pv = jnp.dot(p, vv, preferred_element_type=F32)
                parts.append(pv[:, :hd] / (pv[:, hd:hd + 1] + jnp.exp(sink - m)))
            outs.append(jnp.concatenate(parts, axis=0))
    o_ref[...] = jnp.concatenate(outs, axis=-1).astype(BF16)


def _swa(proj, bias, qg, kg, sinks):
    B, T, _ = proj.shape
    blk = SWA_BLOCK
    qw = SWA_HEADS * SWA_HD
    kvw = SWA_KV_HEADS * SWA_HD
    prev = lambda b, i: jnp.maximum(i - 1, 0)
    seg = (np.arange(qw)[:, None] // SWA_HD == np.arange(LANES)[None, :])
    seg = jnp.asarray(seg, BF16)
    return pl.pallas_call(
        _swa_kernel,
        grid=(B, T // blk),
        in_specs=[
            pl.BlockSpec(memory_space=pltpu.SMEM),
            pl.BlockSpec((None, blk, qw), lambda b, i: (b, i, COL_SQ // qw)),
            pl.BlockSpec((None, blk, kvw), lambda b, i: (b, prev(b, i), COL_SK // kvw)),
            pl.BlockSpec((None, blk, kvw), lambda b, i: (b, i, COL_SK // kvw)),
            pl.BlockSpec((None, blk, kvw), lambda b, i: (b, prev(b, i), COL_SV // kvw)),
            pl.BlockSpec((None, blk, kvw), lambda b, i: (b, i, COL_SV // kvw)),
            pl.BlockSpec((None, SWA_HEADS, blk, 2 * blk), lambda b, i: (jnp.where(i == 0, 1, 0), 0, 0, 0)),
            pl.BlockSpec((1, qw), lambda b, i: (0, 0)),
            pl.BlockSpec((1, kvw), lambda b, i: (0, 0)),
            pl.BlockSpec((qw, LANES), lambda b, i: (0, 0)),
            pl.BlockSpec((LANES, qw), lambda b, i: (0, 0)),
        ],
        out_specs=pl.BlockSpec((None, blk, qw), lambda b, i: (b, i, 0)),
        out_shape=jax.ShapeDtypeStruct((B, T, qw), BF16),
        compiler_params=_params(("arbitrary", "arbitrary")),
        name="swa",
    )(sinks, proj, proj, proj, proj, proj, bias, jnp.tile(qg, (1, SWA_HEADS)),
      jnp.tile(kg, (1, SWA_KV_HEADS)), seg, seg.T)


def _store_token_tiles(ref, x):
    tokens = x.shape[0]
    for q in range(SUBLANES):
        ref[pl.ds(q, tokens, stride=SUBLANES), :] = x[:, q * LANES:(q + 1) * LANES]


def _load_token_tiles(ref, tokens):
    return jnp.concatenate([ref[pl.ds(q, tokens, stride=SUBLANES), :] for q in range(SUBLANES)], axis=1)


def _merge_kernel(x_ref, ga_ref, gb_ref, go_ref, so_ref, wa_ref, wb_ref, wo_ref, g1_ref,
                  n2_ref, sc2_ref, sh2_ref, we_ref, x1_ref, h2_ref, st_ref):
    ya = jnp.dot(go_ref[...], wa_ref[...], preferred_element_type=F32)
    yb = jnp.dot(so_ref[...], wb_ref[...], preferred_element_type=F32)
    m = jax.nn.sigmoid(ga_ref[...].astype(F32)) * ya + jax.nn.sigmoid(gb_ref[...].astype(F32)) * yb
    mixed = jnp.dot(m.astype(BF16), wo_ref[...], preferred_element_type=F32)
    x1 = x_ref[...] + g1_ref[0] * mixed
    y = x1 * lax.rsqrt(jnp.mean(x1 * x1, axis=-1, keepdims=True) + EPS) * n2_ref[...]
    h2 = y * (1.0 + sc2_ref[0]) + sh2_ref[0]
    _store_token_tiles(x1_ref, x1)
    _store_token_tiles(h2_ref, h2)
    st_ref[...] = lax.dot_general(we_ref[...], h2.astype(BF16), NT_DIMS, preferred_element_type=F32)


def _merge(xf, proj2, gla_o, swa_o, wa, wb, wo, g1, n2, sc2, sh2, weff_t, T, tm=256):
    N, D = xf.shape
    B = g1.shape[0]
    SW = weff_t.shape[0]
    bat = lambda i: ((i * tm) // T, 0, 0)
    full = lambda i: (0, 0)
    return pl.pallas_call(
        _merge_kernel,
        grid=(N // tm,),
        in_specs=[
            pl.BlockSpec((tm, D), lambda i: (i, 0)),
            pl.BlockSpec((tm, D), lambda i: (i, COL_GA // D)),
            pl.BlockSpec((tm, D), lambda i: (i, COL_GB // D)),
            pl.BlockSpec((tm, D), lambda i: (i, 0)),
            pl.BlockSpec((tm, D), lambda i: (i, 0)),
            pl.BlockSpec((D, D), full),
            pl.BlockSpec((D, D), full),
            pl.BlockSpec((D, D), full),
            pl.BlockSpec((1, 1, D), bat),
            pl.BlockSpec((1, D), full),
            pl.BlockSpec((1, 1, D), bat),
            pl.BlockSpec((1, 1, D), bat),
            pl.BlockSpec((SW, D), full),
        ],
        out_specs=[
            pl.BlockSpec((tm * SUBLANES, LANES), lambda i: (i, 0)),
            pl.BlockSpec((tm * SUBLANES, LANES), lambda i: (i, 0)),
            pl.BlockSpec((SW, tm), lambda i: (0, i)),
        ],
        out_shape=[
            jax.ShapeDtypeStruct((N * SUBLANES, LANES), F32),
            jax.ShapeDtypeStruct((N * SUBLANES, LANES), F32),
            jax.ShapeDtypeStruct((SW, N), F32),
        ],
        compiler_params=_params(("arbitrary",), VMEM_LIMIT_BYTES),
        name="merge",
    )(xf, proj2, proj2, gla_o, swa_o, wa, wb, wo, g1.reshape(B, 1, D), n2.reshape(1, D),
      sc2.reshape(B, 1, D), sh2.reshape(B, 1, D), weff_t)


def _tree(op, xs):
    xs = list(xs)
    while len(xs) > 1:
        xs = [op(xs[i], xs[i + 1]) for i in range(0, len(xs) - 1, 2)] + ([xs[-1]] if len(xs) % 2 else [])
    return xs[0]


EXTRACT_CHAINS = 4


def _extract_best(ref, n):
    lanes = min(EXTRACT_CHAINS, n)
    best = [ref[i] for i in range(lanes)]
    for i in range(lanes, n):
        best[i % lanes] = jnp.maximum(best[i % lanes], ref[i])
    m = _tree(jnp.maximum, best)
    first = [jnp.where(ref[i] == m, i, n) for i in range(lanes)]
    for i in range(lanes, n):
        first[i % lanes] = jnp.minimum(first[i % lanes], jnp.where(ref[i] == m, i, n))
    pos = _tree(jnp.minimum, first)
    for i in range(n):
        ref[i] = jnp.where(pos == i, -jnp.inf, ref[i])
    return m, pos


def _ordered(a, b):
    (va, ia), (vb, ib) = a, b
    keep = jnp.logical_or(va > vb, jnp.logical_and(va == vb, ia < ib))
    first = (jnp.where(keep, va, vb), jnp.where(keep, ia, ib))
    second = (jnp.where(keep, vb, va), jnp.where(keep, ib, ia))
    return first, second


def _bitonic_merge(xs):
    xs = list(xs)
    j = len(xs) // 2
    while j >= 1:
        for i in range(len(xs)):
            if i & j == 0:
                xs[i], xs[i | j] = _ordered(xs[i], xs[i | j])
        j //= 2
    return xs


def _bitonic_sort(xs):
    xs = list(xs)
    k = 2
    while k <= len(xs):
        j = k // 2
        while j >= 1:
            for i in range(len(xs)):
                if i & j == 0:
                    first, second = _ordered(xs[i], xs[i | j])
                    xs[i], xs[i | j] = (first, second) if i & k == 0 else (second, first)
            j //= 2
        k *= 2
    return xs


def _merge_best(xs, ys):
    n = len(xs)
    return _bitonic_merge([_ordered(xs[i], ys[n - 1 - i])[0] for i in range(n)])


TOPK_PAIRS = tuple((a, b) for a in range(PEER_TOPK) for b in range(PEER_TOPK)
                   if (a + 1) * (b + 1) <= PEER_TOPK)


def _topk_kernel(s_ref, e_ref, g_ref, sv_ref, si_ref, v_ref, i_ref, cand_ref, cidx_ref, best_ref, row_ref):
    K = PEER_TOPK
    nk = PEER_NKEYS
    H = PEER_HEADS
    G = 2 * H
    tm = s_ref.shape[1]
    nblocks = nk // K
    for hb in range(2):
        rows = slice(hb * H, (hb + 1) * H)
        for blk in range(nblocks):
            keys = [(s_ref[(blk * K + i) * G + hb * H:(blk * K + i) * G + (hb + 1) * H, :],
                     jnp.full((H, tm), blk * K + i, I32)) for i in range(K)]
            for i, (v, ix) in enumerate(_bitonic_sort(keys)):
                sv_ref[blk, i, rows, :] = v
                si_ref[blk, i, rows, :] = ix
        step = 1
        while step < nblocks:
            for blk in range(0, nblocks, 2 * step):
                xs = [(sv_ref[blk, i, rows, :], si_ref[blk, i, rows, :]) for i in range(K)]
                ys = [(sv_ref[blk + step, i, rows, :], si_ref[blk + step, i, rows, :]) for i in range(K)]
                for i, (v, ix) in enumerate(_merge_best(xs, ys)):
                    sv_ref[blk, i, rows, :] = v
                    si_ref[blk, i, rows, :] = ix
            step *= 2
    v_ref[...] = sv_ref[0]
    i_ref[...] = si_ref[0]

    for ci, (a, b) in enumerate(TOPK_PAIRS):
        cand_ref[ci] = v_ref[a, 0:H, :] + v_ref[b, H:2 * H, :]
        cidx_ref[ci] = (i_ref[a, 0:H, :] * nk + i_ref[b, H:2 * H, :]) * WORDS_PER_EXPERT
    ncand = len(TOPK_PAIRS)

    def stage2(k, _):
        m, pos = _extract_best(cand_ref, ncand)
        row = _tree(jnp.maximum, [jnp.where(pos == ci, cidx_ref[ci], -1) for ci in range(ncand)])
        best_ref[k] = m
        row_ref[k] = row
        return 0

    lax.fori_loop(0, K, stage2, 0)
    best = best_ref[...]
    ex = jnp.exp(best - best[0:1])
    gates = ex / jnp.sum(ex, axis=0, keepdims=True)
    g_ref[...] = gates.reshape(K * H, tm).T
    e_ref[...] = row_ref[...].reshape(K * H, tm).astype(F32).T.astype(I32)


def _topk(s_t, tm=128):
    SW, N = s_t.shape
    K = PEER_TOPK
    G = 2 * PEER_HEADS
    return pl.pallas_call(
        _topk_kernel,
        grid=(N // tm,),
        in_specs=[pl.BlockSpec((SW, tm), lambda i: (0, i))],
        out_specs=[
            pl.BlockSpec((tm, PEER_SLOTS), lambda i: (i, 0)),
            pl.BlockSpec((tm, PEER_SLOTS), lambda i: (i, 0)),
        ],
        out_shape=[
            jax.ShapeDtypeStruct((N, PEER_SLOTS), I32),
            jax.ShapeDtypeStruct((N, PEER_SLOTS), F32),
        ],
        scratch_shapes=[pltpu.VMEM((PEER_NKEYS // K, K, G, tm), F32),
                        pltpu.VMEM((PEER_NKEYS // K, K, G, tm), I32),
                        pltpu.VMEM((K, G, tm), F32),
                        pltpu.VMEM((K, G, tm), I32),
                        pltpu.VMEM((len(TOPK_PAIRS), PEER_HEADS, tm), F32),
                        pltpu.VMEM((len(TOPK_PAIRS), PEER_HEADS, tm), I32),
                        pltpu.VMEM((K, PEER_HEADS, tm), F32),
                        pltpu.VMEM((K, PEER_HEADS, tm), I32)],
        compiler_params=_params(("arbitrary",)),
        name="peer_topk",
    )(s_t)


WORDS_PER_EXPERT = SUBLANES // 2


TILE_BLOCK = tuple((r % 2) * WORDS_PER_EXPERT + r // 2 for r in range(SUBLANES))
BLOCK_ROW = tuple(TILE_BLOCK.index(q) for q in range(SUBLANES))


def _bf16_bits(x):
    return lax.bitcast_convert_type(x.astype(BF16).astype(F32), jnp.uint32)


def _table_kernel(t_ref, o_ref):
    rows = t_ref.shape[0]
    for s in range(WORDS_PER_EXPERT):
        lo = _bf16_bits(t_ref[:, s * LANES:(s + 1) * LANES])
        hi = _bf16_bits(t_ref[:, (s + WORDS_PER_EXPERT) * LANES:(s + WORDS_PER_EXPERT + 1) * LANES])
        word = lax.shift_right_logical(lo, jnp.uint32(16)) | (hi & jnp.uint32(0xFFFF0000))
        o_ref[pl.ds(s, rows, stride=WORDS_PER_EXPERT), :] = lax.bitcast_convert_type(word, I32)


def _table(tables, layer, te=2048):
    _, n, d = tables.shape
    return pl.pallas_call(
        _table_kernel,
        grid=(n // te,),
        in_specs=[pl.BlockSpec((None, te, d), lambda i: (layer, i, 0))],
        out_specs=pl.BlockSpec((te * WORDS_PER_EXPERT, LANES), lambda i: (i, 0)),
        out_shape=jax.ShapeDtypeStruct((n * WORDS_PER_EXPERT, LANES), I32),
        compiler_params=_params(("arbitrary",)),
        name="peer_table",
    )(tables)


def _expert_tile(tab_ref, e4):
    words = tab_ref[pl.ds(pl.multiple_of(e4, WORDS_PER_EXPERT), WORDS_PER_EXPERT), :]
    return pltpu.bitcast(words, BF16)


def _to_tile_rows(x):
    return jnp.concatenate([x[q:q + 1, :] for q in TILE_BLOCK], axis=0)


def _from_tile_rows(x):
    return jnp.concatenate([x[r:r + 1, :] for r in BLOCK_ROW], axis=0)


def _gelu_tanh(x):
    return 0.5 * x * (1.0 + jnp.tanh(np.sqrt(2.0 / np.pi) * (x + 0.044715 * (x * x * x))))


RING_SLOTS = 2
SLOT_TOKENS = 32
SLAB_TOKENS = 256


def _for_each_token(e_ref, next_ref, idx_ref, sem, first, count, token_body):
    group0 = first // SLOT_TOKENS
    ngroups = count // SLOT_TOKENS
    niter = ngroups // RING_SLOTS
    step = pl.program_id(0)
    next_group0 = jnp.minimum(step + 1, pl.num_programs(0) - 1) * ngroups

    def fetch(src_ref, q, slot):
        return pltpu.make_async_copy(src_ref.at[q], idx_ref.at[slot], sem.at[slot])

    @pl.when(step == 0)
    def _():
        for slot in range(RING_SLOTS):
            fetch(e_ref, group0 + slot, slot).start()

    def ring(j, _):
        for slot in range(RING_SLOTS):
            q = j * RING_SLOTS + slot
            fetch(e_ref, group0 + q, slot).wait()
            for tt in range(SLOT_TOKENS):
                token_body(first + q * SLOT_TOKENS + tt, lambda k, slot=slot, tt=tt: idx_ref[slot, tt, k])

            @pl.when(j < niter - 1)
            def _():
                fetch(e_ref, group0 + q + RING_SLOTS, slot).start()

            @pl.when(j == niter - 1)
            def _():
                fetch(next_ref, next_group0 + slot, slot).start()
        return 0

    lax.fori_loop(0, niter, ring, 0)

    @pl.when(step == pl.num_programs(0) - 1)
    def _():
        for slot in range(RING_SLOTS):
            fetch(next_ref, next_group0 + slot, slot).wait()


PAIR_GROUP = LANES // SUBLANES
SLOT_GROUPS = PEER_SLOTS // PAIR_GROUP


def _slot_tiles(tab_ref, row, j):
    return jnp.concatenate([_expert_tile(tab_ref, row(j * PAIR_GROUP + kk)) for kk in range(PAIR_GROUP)],
                           axis=0)


def _spread_matrices():
    lane = np.arange(LANES)
    return np.stack([lane[:, None] == PAIR_GROUP * j + lane[None, :] // SUBLANES
                     for j in range(SLOT_GROUPS)])


def _peer_u_kernel(e_ref, next_ref, h_ref, g_ref, tab_ref, sel_ref, o_ref, slab_ref, idx_ref, sem, *, tb):
    groups_per_token = PEER_SLOTS // SUBLANES
    sel = sel_ref[...]
    rio = lax.broadcasted_iota(I32, (LANES, LANES), 0)
    lio = lax.broadcasted_iota(I32, (LANES, LANES), 1)
    diag = (rio == lio)[None]
    rows = SUBLANES * PEER_SLOTS

    def sub_block(sb, _):
        first = pl.multiple_of(sb * SLAB_TOKENS, SLAB_TOKENS)

        def token(t, row):
            hb = _to_tile_rows(h_ref[t]).astype(BF16)
            for j in range(groups_per_token):
                prods = jnp.concatenate([_expert_tile(tab_ref, row(j * SUBLANES + qn)) * hb
                                         for qn in range(SUBLANES)], axis=0)
                dst = pl.multiple_of((t - first) * PEER_SLOTS + j * SUBLANES, SUBLANES)
                slab_ref[pl.ds(dst, SUBLANES), :] = jnp.dot(sel, prods, preferred_element_type=F32)

        _for_each_token(e_ref, next_ref, idx_ref, sem, first, SLAB_TOKENS, token)
        for gi in range(SLAB_TOKENS // SUBLANES):
            tok = pl.ds(pl.multiple_of(first + gi * SUBLANES, SUBLANES), SUBLANES)
            rs = jnp.sum(slab_ref[gi * rows:(gi + 1) * rows, :], axis=-1, keepdims=True)
            rs = rs.reshape(SUBLANES, PEER_SLOTS, 1)
            a = jnp.sum(jnp.where(diag, rs, 0.0), axis=1)
            o_ref[tok, :] = g_ref[tok, :] * _gelu_tanh(a)
        return 0

    lax.fori_loop(0, tb // SLAB_TOKENS, sub_block, 0)


def _peer_u(e4, h3, gates, tab, tb=256):
    N = h3.shape[0]
    assert tb == SLAB_TOKENS and SLAB_TOKENS % (RING_SLOTS * SLOT_TOKENS) == 0
    sel = np.arange(SUBLANES)[:, None] == np.arange(SUBLANES * SUBLANES)[None, :] // SUBLANES
    e4q = e4.reshape(N // SLOT_TOKENS, SLOT_TOKENS, PEER_SLOTS)
    return pl.pallas_call(
        functools.partial(_peer_u_kernel, tb=tb),
        grid=(N // tb,),
        in_specs=[
            pl.BlockSpec((tb // SLOT_TOKENS, SLOT_TOKENS, PEER_SLOTS), lambda i: (i, 0, 0)),
            pl.BlockSpec(memory_space=pl.ANY),
            pl.BlockSpec((tb, SUBLANES, LANES), lambda i: (i, 0, 0)),
            pl.BlockSpec((tb, PEER_SLOTS), lambda i: (i, 0)),
            pl.BlockSpec(tab.shape, lambda i: (0, 0), pipeline_mode=pl.Buffered(1)),
            pl.BlockSpec(sel.shape, lambda i: (0, 0)),
        ],
        out_specs=pl.BlockSpec((tb, PEER_SLOTS), lambda i: (i, 0)),
        out_shape=jax.ShapeDtypeStruct((N, PEER_SLOTS), F32),
        scratch_shapes=[pltpu.VMEM((SLAB_TOKENS * PEER_SLOTS, LANES), F32),
                        pltpu.SMEM((RING_SLOTS, SLOT_TOKENS, PEER_SLOTS), I32),
                        pltpu.SemaphoreType.DMA((RING_SLOTS,))],
        compiler_params=_params(("arbitrary",), VMEM_LIMIT_BYTES),
        name="peer_u",
    )(e4q, e4q, h3, gates, tab, jnp.asarray(sel, BF16))


def _peer_v_kernel(e_ref, next_ref, c_ref, x1_ref, g2_ref, tab_ref, spread_ref, o_ref, hi_ref, lo_ref, out_ref, idx_ref,
                   sem, *, tb):
    c = c_ref[...]
    c_hi = c.astype(BF16)
    c_lo = (c - c_hi.astype(F32)).astype(BF16)
    for j in range(SLOT_GROUPS):
        rows = pl.ds(j, tb, stride=SLOT_GROUPS)
        hi_ref[rows, :] = jnp.dot(c_hi, spread_ref[j], preferred_element_type=F32)
        lo_ref[rows, :] = jnp.dot(c_lo, spread_ref[j], preferred_element_type=F32)
    rio = lax.broadcasted_iota(I32, (SUBLANES, LANES), 0)
    lio = lax.broadcasted_iota(I32, (SUBLANES, LANES), 1)
    own_row = (lio & (SUBLANES - 1)) == rio

    def token(t, row):
        base = pl.multiple_of(t * SLOT_GROUPS, SLOT_GROUPS)
        hi = hi_ref[pl.ds(base, SLOT_GROUPS), :]
        lo = lo_ref[pl.ds(base, SLOT_GROUPS), :]
        acc = jnp.zeros((SUBLANES, LANES), F32)
        for j in range(SLOT_GROUPS):
            lhs = jnp.concatenate(
                [jnp.where(own_row, jnp.broadcast_to(part[j:j + 1, :], (SUBLANES, LANES)), 0.0).astype(BF16)
                 for part in (hi, lo)], axis=0)
            out = jnp.dot(lhs, _slot_tiles(tab_ref, row, j), preferred_element_type=F32)
            acc = acc + (out[:SUBLANES] + out[SUBLANES:])
        tile = pl.ds(pl.multiple_of(t * SUBLANES, SUBLANES), SUBLANES)
        out_ref[tile, :] = x1_ref[t] + g2_ref[0] * _from_tile_rows(acc)

    _for_each_token(e_ref, next_ref, idx_ref, sem, 0, tb, token)
    o_ref[...] = _load_token_tiles(out_ref, tb)


def _peer_v(e4, coef, x1_3, g2_3, tab, T, tb=512):
    N = x1_3.shape[0]
    assert tb % (RING_SLOTS * SLOT_TOKENS) == 0 and tb % SUBLANES == 0
    e4q = e4.reshape(N // SLOT_TOKENS, SLOT_TOKENS, PEER_SLOTS)
    return pl.pallas_call(
        functools.partial(_peer_v_kernel, tb=tb),
        grid=(N // tb,),
        in_specs=[
            pl.BlockSpec((tb // SLOT_TOKENS, SLOT_TOKENS, PEER_SLOTS), lambda i: (i, 0, 0)),
            pl.BlockSpec(memory_space=pl.ANY),
            pl.BlockSpec((tb, PEER_SLOTS), lambda i: (i, 0)),
            pl.BlockSpec((tb, SUBLANES, LANES), lambda i: (i, 0, 0)),
            pl.BlockSpec((1, SUBLANES, LANES), lambda i: ((i * tb) // T, 0, 0)),
            pl.BlockSpec(tab.shape, lambda i: (0, 0), pipeline_mode=pl.Buffered(1)),
            pl.BlockSpec((SLOT_GROUPS, LANES, LANES), lambda i: (0, 0, 0)),
        ],
        out_specs=pl.BlockSpec((tb, SUBLANES * LANES), lambda i: (i, 0)),
        out_shape=jax.ShapeDtypeStruct((N, SUBLANES * LANES), F32),
        scratch_shapes=[pltpu.VMEM((tb * SLOT_GROUPS, LANES), F32),
                        pltpu.VMEM((tb * SLOT_GROUPS, LANES), F32),
                        pltpu.VMEM((tb * SUBLANES, LANES), F32),
                        pltpu.SMEM((RING_SLOTS, SLOT_TOKENS, PEER_SLOTS), I32),
                        pltpu.SemaphoreType.DMA((RING_SLOTS,))],
        compiler_params=_params(("arbitrary",), VMEM_LIMIT_BYTES),
        name="peer_v",
    )(e4q, e4q, coef, x1_3, g2_3, tab,
      jnp.asarray(_spread_matrices(), BF16))


def kernel(x, c, w_ada, b_ada, norm1_g, w_in, gla_gate_w2, gla_gate_b, gla_norm_g, swa_qnorm_g,
           swa_knorm_g, swa_sinks, rel_bias, w_up_a, w_up_b, w_out, norm2_g, peer_wq, peer_subkeys,
           peer_u, peer_v):
    B, T, D = x.shape
    N = B * T
    L = w_ada.shape[0]
    mod = _adaln(c, w_ada, b_ada)
    weff_t = _fold_peer_keys(peer_wq, peer_subkeys)
    bias = _swa_bias(rel_bias)
    xf = x.reshape(N, D)
    for l in range(L):
        sh1, sc1, g1, sh2, sc2, g2 = [mod[l, :, i * D:(i + 1) * D] for i in range(6)]
        proj = _in_proj(xf, norm1_g[l], sc1, sh1, _pack_w_in(w_in, l), T)
        proj3 = proj.reshape(B, T, PROJ_W)
        w2p = jnp.zeros((LANES, GLA_HEADS * GLA_DK), BF16).at[:GLA_RANK].set(gla_gate_w2[l].astype(BF16))
        gla_o = _gla(proj3, w2p, gla_gate_b[l].reshape(1, -1), gla_norm_g[l].reshape(1, -1))
        swa_o = _swa(proj3, bias, swa_qnorm_g[l].reshape(1, -1), swa_knorm_g[l].reshape(1, -1),
                     swa_sinks[l])
        x1, h2, s_t = _merge(xf, proj, gla_o.reshape(N, -1), swa_o.reshape(N, -1),
                             w_up_a[l].astype(BF16), w_up_b[l].astype(BF16), w_out[l].astype(BF16),
                             g1, norm2_g[l], sc2, sh2, weff_t[l], T)
        e4, gates = _topk(s_t)
        coef = _peer_u(e4, h2.reshape(N, SUBLANES, LANES), gates, _table(peer_u, l))
        g2_3 = g2.reshape(B, SUBLANES, LANES)
        xf = _peer_v(e4, coef, x1.reshape(N, SUBLANES, LANES), g2_3, _table(peer_v, l), T)
    return xf.reshape(B, T, D)
```
